```python
import jax, jax.numpy as jnp
from jax import lax
import numpy as np

D_MODEL = 2048
BATCH = 2
SEQ = 8192
DEPTH = 2

CHUNK = 64
Q_BLOCK = 128
RWKV_HEADS = 16
RWKV_HEAD_DIM = 64
RWKV_DIM = RWKV_HEADS * RWKV_HEAD_DIM
W_LORA = 64
A_LORA = 64
G_LORA = 128
RWKV_IN = 3 * RWKV_DIM + W_LORA + A_LORA + G_LORA
MLA_HEADS = 8
Q_LORA = 512
KV_LORA = 512
QK_NOPE = 128
QK_ROPE = 64
V_HEAD = 128
MLA_IN = Q_LORA + KV_LORA + QK_ROPE
MLA_DIM = MLA_HEADS * V_HEAD
ROPE_THETA = 10000.0
MIX_IN = RWKV_IN + MLA_IN
MIX_OUT = RWKV_DIM + MLA_DIM
CONV_WIDTH = 31
CONV_DIM = D_MODEL
D_FF = 5632
N_EXPERTS = 8
TOP_K = 2
D_FF_EXPERT = 7168
MOE_BLOCK = 128
NORM_EPS = 1e-6
LN_EPS = 1e-5
GN_EPS = RWKV_HEAD_DIM * 1e-5
MAX_POS_OFFSET = 4096

kernel_name = "hybrid_rwkv7_mla_conformer_moe"


def rms_norm(x, g, eps=NORM_EPS):
    xf = x.astype(jnp.float32)
    y = xf * lax.rsqrt(jnp.mean(xf * xf, axis=-1, keepdims=True) + eps)
    return (y * g.astype(jnp.float32)).astype(x.dtype)


def layer_norm(x, g, b, eps=LN_EPS):
    xf = x.astype(jnp.float32)
    mu = jnp.mean(xf, axis=-1, keepdims=True)
    var = jnp.mean(jnp.square(xf - mu), axis=-1, keepdims=True)
    y = (xf - mu) * lax.rsqrt(var + eps)
    return (y * g.astype(jnp.float32) + b.astype(jnp.float32)).astype(x.dtype)


def token_shift(x):
    return jnp.pad(x[:, :-1], ((0, 0), (1, 0), (0, 0)))


def rope_tables(positions):
    inv = ROPE_THETA ** (-jnp.arange(0, QK_ROPE, 2, dtype=jnp.float32) / QK_ROPE)
    ang = positions.astype(jnp.float32)[..., None] * inv
    return jnp.cos(ang), jnp.sin(ang)


def apply_rope(x, cos, sin):
    x1, x2 = jnp.split(x, 2, axis=-1)
    c = cos.astype(x.dtype)
    s = sin.astype(x.dtype)
    return jnp.concatenate([x1 * c - x2 * s, x1 * s + x2 * c], axis=-1)


def rwkv7_group(p, w0, w_up, a0, a_up, g_up, k_k, k_a, r_k, gn_w, gn_b):
    B, S, _ = p.shape
    f32 = jnp.float32
    i1, i2, i3 = RWKV_DIM, 2 * RWKV_DIM, 3 * RWKV_DIM
    r, k, v, xw, xa, xg = jnp.split(p, [i1, i2, i3, i3 + W_LORA, i3 + W_LORA + A_LORA], axis=-1)
    w = -jax.nn.softplus(-(w0 + jnp.tanh(xw) @ w_up)) - 0.5
    decay = jnp.exp(-jnp.exp(w.astype(f32)))
    a = jax.nn.sigmoid(a0 + xa @ a_up)
    g = jax.nn.sigmoid(xg) @ g_up

    def heads(t):
        return t.reshape(B, S, RWKV_HEADS, RWKV_HEAD_DIM)

    kk = heads(k * k_k).astype(f32)
    kk = kk / jnp.maximum(jnp.sqrt(jnp.sum(kk * kk, axis=-1, keepdims=True)), 1e-12)
    k = k * (1.0 + (a - 1.0) * k_a)
    rh, kh, vh, ah = heads(r), heads(k), heads(v), heads(a)

    def seq_major(t):
        return jnp.transpose(t.astype(f32), (1, 0, 2, 3))

    xs = (seq_major(rh), seq_major(heads(decay)), seq_major(kh), seq_major(vh),
          seq_major(kk), seq_major(ah))

    def step(state, inp):
        r_t, w_t, k_t, v_t, kk_t, a_t = inp
        sa = jnp.einsum('bhvk,bhk->bhv', state, -kk_t)
        state = (state * w_t[:, :, None, :]
                 + sa[..., None] * (kk_t * a_t)[:, :, None, :]
                 + v_t[..., None] * k_t[:, :, None, :])
        return state, jnp.einsum('bhvk,bhk->bhv', state, r_t)

    state0 = jnp.zeros((B, RWKV_HEADS, RWKV_HEAD_DIM, RWKV_HEAD_DIM), f32)
    _, y = lax.scan(step, state0, xs)
    y = jnp.transpose(y, (1, 0, 2, 3))
    mu = jnp.mean(y, axis=-1, keepdims=True)
    var = jnp.mean(jnp.square(y - mu), axis=-1, keepdims=True)
    y = (y - mu) * lax.rsqrt(var + GN_EPS)
    y = y * gn_w.reshape(RWKV_HEADS, RWKV_HEAD_DIM) + gn_b.reshape(RWKV_HEADS, RWKV_HEAD_DIM)
    y = y.astype(p.dtype)
    bonus = jnp.sum(rh * kh * r_k, axis=-1, keepdims=True) * vh
    return (y + bonus).reshape(B, S, RWKV_DIM) * g


def mla_group(q_lat, kv_lat, k_rope_raw, cos, sin, q_norm, w_q_up, kv_norm, w_kv_up):
    B, S, _ = q_lat.shape
    q = (rms_norm(q_lat, q_norm) @ w_q_up).reshape(B, S, MLA_HEADS, QK_NOPE + QK_ROPE)
    q_nope, q_rope = q[..., :QK_NOPE], q[..., QK_NOPE:]
    q_rope = apply_rope(q_rope, cos[:, :, None, :], sin[:, :, None, :])
    kv = (rms_norm(kv_lat, kv_norm) @ w_kv_up).reshape(B, S, MLA_HEADS, QK_NOPE + V_HEAD)
    k_nope, v = kv[..., :QK_NOPE], kv[..., QK_NOPE:]
    k_rope = apply_rope(k_rope_raw, cos, sin)
    scale = (QK_NOPE + QK_ROPE) ** -0.5
    n_blk = S // Q_BLOCK
    qn_blocks = jnp.transpose(q_nope.reshape(B, n_blk, Q_BLOCK, MLA_HEADS, QK_NOPE), (1, 0, 2, 3, 4))
    qr_blocks = jnp.transpose(q_rope.reshape(B, n_blk, Q_BLOCK, MLA_HEADS, QK_ROPE), (1, 0, 2, 3, 4))
    key_chunk = jnp.arange(S) // CHUNK

    def attend_block(args):
        qn, qr, blk = args
        s = (jnp.einsum('bqhd,bkhd->bhqk', qn, k_nope)
             + jnp.einsum('bqhd,bkd->bhqk', qr, k_rope)).astype(jnp.float32) * scale
        q_chunk = (blk * Q_BLOCK + jnp.arange(Q_BLOCK)) // CHUNK
        mask = key_chunk[None, :] <= q_chunk[:, None]
        s = jnp.where(mask[None, None], s, -jnp.inf)
        pr = jax.nn.softmax(s, axis=-1).astype(v.dtype)
        return jnp.einsum('bhqk,bkhd->bqhd', pr, v)

    out = lax.map(attend_block, (qn_blocks, qr_blocks, jnp.arange(n_blk)))
    return jnp.transpose(out, (1, 0, 2, 3, 4)).reshape(B, S, MLA_DIM)


def hybrid_mixer(h, cos, sin, w_in, shift_mu, w0, w_up, a0, a_up, g_up, k_k, k_a, r_k,
                 gn_w, gn_b, q_norm, w_q_up, kv_norm, w_kv_up, w_out):
    p = h @ w_in
    p_rwkv, q_lat, kv_lat, k_rope_raw = jnp.split(
        p, [RWKV_IN, RWKV_IN + Q_LORA, RWKV_IN + Q_LORA + KV_LORA], axis=-1)
    p_rwkv = p_rwkv + (token_shift(p_rwkv) - p_rwkv) * shift_mu
    y_rwkv = rwkv7_group(p_rwkv, w0, w_up, a0, a_up, g_up, k_k, k_a, r_k, gn_w, gn_b)
    y_mla = mla_group(q_lat, kv_lat, k_rope_raw, cos, sin, q_norm, w_q_up, kv_norm, w_kv_up)
    return jnp.concatenate([y_rwkv, y_mla], axis=-1) @ w_out


def conformer_conv(h, pw1_w, pw1_b, dw_w, dw_b, ln_g, ln_b, pw2_w, pw2_b):
    u = h @ pw1_w + pw1_b
    u = u[..., :CONV_DIM] * jax.nn.sigmoid(u[..., CONV_DIM:])
    u = lax.conv_general_dilated(
        u, dw_w[:, None, :].astype(u.dtype), window_strides=(1,),
        padding=((CONV_WIDTH - 1, 0),), dimension_numbers=('NWC', 'WIO', 'NWC'),
        feature_group_count=CONV_DIM) + dw_b
    u = jax.nn.silu(layer_norm(u, ln_g, ln_b))
    return u @ pw2_w + pw2_b


def dense_swiglu(h, w1, w3, w2):
    return (jax.nn.silu(h @ w1) * (h @ w3)) @ w2


def moe_swiglu(h, router, w1, w3, w2):
    B, S, D = h.shape
    n = B * S
    hf = h.reshape(n, D)
    logits = (hf @ router).astype(jnp.float32)
    top_logits, top_idx = lax.top_k(logits, TOP_K)
    gates = jax.nn.softmax(top_logits, axis=-1)
    flat_e = top_idx.reshape(-1).astype(jnp.int32)
    flat_tok = jnp.repeat(jnp.arange(n, dtype=jnp.int32), TOP_K)
    flat_gate = gates.reshape(-1)
    order = jnp.argsort(flat_e)
    sorted_e = flat_e[order]
    counts = jnp.bincount(flat_e, length=N_EXPERTS)
    padded = (counts + MOE_BLOCK - 1) // MOE_BLOCK * MOE_BLOCK
    start = jnp.cumsum(counts) - counts
    pad_end = jnp.cumsum(padded)
    pad_start = pad_end - padded
    dest = pad_start[sorted_e] + jnp.arange(n * TOP_K, dtype=jnp.int32) - start[sorted_e]
    cap = n * TOP_K + N_EXPERTS * MOE_BLOCK
    n_blocks = cap // MOE_BLOCK
    slot_tok = jnp.full((cap,), n, jnp.int32).at[dest].set(flat_tok[order])
    slot_gate = jnp.zeros((cap,), jnp.float32).at[dest].set(flat_gate[order])
    block_start = jnp.arange(n_blocks) * MOE_BLOCK
    block_expert = jnp.minimum(jnp.sum(block_start[:, None] >= pad_end[None, :], axis=1),
                               N_EXPERTS - 1)
    h_pad = jnp.concatenate([hf, jnp.zeros((1, D), hf.dtype)], axis=0)
    xb = h_pad[slot_tok].reshape(n_blocks, MOE_BLOCK, D)

    def expert_block(args):
        xblk, e = args
        return (jax.nn.silu(xblk @ w1[e]) * (xblk @ w3[e])) @ w2[e]

    yb = lax.map(expert_block, (xb, block_expert)).reshape(cap, D)
    y = jnp.zeros((n + 1, D), h.dtype).at[slot_tok].add(yb * slot_gate[:, None].astype(h.dtype))
    return y[:n].reshape(B, S, D)


def setup_inputs(seed: int = 0) -> dict:
    key = jax.random.key(seed)
    ks = iter(jax.random.split(key, 64))
    f32 = jnp.float32

    def dense(shape, fan_in, s=1.0):
        return jax.random.normal(next(ks), shape, f32) * (s * fan_in ** -0.5)

    def gain(shape):
        return 1.0 + 0.05 * jax.random.normal(next(ks), shape, f32)

    def small(shape, s=0.02):
        return s * jax.random.normal(next(ks), shape, f32)

    x = jax.random.normal(next(ks), (BATCH, SEQ, D_MODEL), f32)
    offset = jax.random.randint(next(ks), (BATCH, 1), 0, MAX_POS_OFFSET, dtype=jnp.int32)
    positions = (offset + jnp.arange(SEQ, dtype=jnp.int32)[None, :]).astype(jnp.int32)
    return {
        "x": x,
        "positions": positions,
        "l0_mix_norm": gain((D_MODEL,)),
        "l0_w_in": dense((D_MODEL, MIX_IN), D_MODEL),
        "l0_shift_mu": jax.random.uniform(next(ks), (RWKV_IN,), f32),
        "l0_w0": jax.random.uniform(next(ks), (RWKV_DIM,), f32, -5.0, 0.0),
        "l0_w_up": dense((W_LORA, RWKV_DIM), W_LORA, 0.5),
        "l0_a0": small((RWKV_DIM,), 0.5),
        "l0_a_up": dense((A_LORA, RWKV_DIM), A_LORA, 0.5),
        "l0_g_up": dense((G_LORA, RWKV_DIM), G_LORA),
        "l0_k_k": 0.85 + small((RWKV_DIM,), 0.05),
        "l0_k_a": gain((RWKV_DIM,)),
        "l0_r_k": small((RWKV_HEADS, RWKV_HEAD_DIM), 0.1),
        "l0_gn_w": gain((RWKV_DIM,)),
        "l0_gn_b": small((RWKV_DIM,)),
        "l0_q_norm": gain((Q_LORA,)),
        "l0_w_q_up": dense((Q_LORA, MLA_HEADS * (QK_NOPE + QK_ROPE)), Q_LORA),
        "l0_kv_norm": gain((KV_LORA,)),
        "l0_w_kv_up": dense((KV_LORA, MLA_HEADS * (QK_NOPE + V_HEAD)), KV_LORA),
        "l0_w_out": dense((MIX_OUT, D_MODEL), MIX_OUT),
        "l0_ffn_norm": gain((D_MODEL,)),
        "l0_ffn_w1": dense((D_MODEL, D_FF), D_MODEL),
        "l0_ffn_w3": dense((D_MODEL, D_FF), D_MODEL),
        "l0_ffn_w2": dense((D_FF, D_MODEL), D_FF),
        "l1_mix_norm": gain((D_MODEL,)),
        "l1_pw1_w": dense((D_MODEL, 2 * CONV_DIM), D_MODEL),
        "l1_pw1_b": small((2 * CONV_DIM,)),
        "l1_dw_w": dense((CONV_WIDTH, CONV_DIM), CONV_WIDTH),
        "l1_dw_b": small((CONV_DIM,)),
        "l1_ln_g": gain((CONV_DIM,)),
        "l1_ln_b": small((CONV_DIM,)),
        "l1_pw2_w": dense((CONV_DIM, D_MODEL), CONV_DIM),
        "l1_pw2_b": small((D_MODEL,)),
        "l1_ffn_norm": gain((D_MODEL,)),
        "l1_router": dense((D_MODEL, N_EXPERTS), D_MODEL),
        "l1_exp_w1": dense((N_EXPERTS, D_MODEL, D_FF_EXPERT), D_MODEL),
        "l1_exp_w3": dense((N_EXPERTS, D_MODEL, D_FF_EXPERT), D_MODEL),
        "l1_exp_w2": dense((N_EXPERTS, D_FF_EXPERT, D_MODEL), D_FF_EXPERT),
        "final_norm": gain((D_MODEL,)),
    }


def reference(x, positions,
              l0_mix_norm, l0_w_in, l0_shift_mu, l0_w0, l0_w_up, l0_a0, l0_a_up, l0_g_up,
              l0_k_k, l0_k_a, l0_r_k, l0_gn_w, l0_gn_b, l0_q_norm, l0_w_q_up, l0_kv_norm,
              l0_w_kv_up, l0_w_out,
              l0_ffn_norm, l0_ffn_w1, l0_ffn_w3, l0_ffn_w2,
              l1_mix_norm, l1_pw1_w, l1_pw1_b, l1_dw_w, l1_dw_b, l1_ln_g, l1_ln_b,
              l1_pw2_w, l1_pw2_b,
              l1_ffn_norm, l1_router, l1_exp_w1, l1_exp_w3, l1_exp_w2,
              final_norm):
    cos, sin = rope_tables(positions)
    layers = [
        (hybrid_mixer,
         (cos, sin, l0_w_in, l0_shift_mu, l0_w0, l0_w_up, l0_a0, l0_a_up, l0_g_up, l0_k_k,
          l0_k_a, l0_r_k, l0_gn_w, l0_gn_b, l0_q_norm, l0_w_q_up, l0_kv_norm, l0_w_kv_up,
          l0_w_out),
         l0_mix_norm,
         dense_swiglu, (l0_ffn_w1, l0_ffn_w3, l0_ffn_w2), l0_ffn_norm),
        (conformer_conv,
         (l1_pw1_w, l1_pw1_b, l1_dw_w, l1_dw_b, l1_ln_g, l1_ln_b, l1_pw2_w, l1_pw2_b),
         l1_mix_norm,
         moe_swiglu, (l1_router, l1_exp_w1, l1_exp_w3, l1_exp_w2), l1_ffn_norm),
    ]
    for layer in range(DEPTH):
        mixer, mix_args, mix_norm, ffn, ffn_args, ffn_norm = layers[layer]
        x = x + mixer(rms_norm(x, mix_norm), *mix_args)
        x = x + ffn(rms_norm(x, ffn_norm), *ffn_args)
    return rms_norm(x, final_norm)
```

```python
import functools

import numpy as np
import jax
import jax.numpy as jnp
from jax import lax
from jax.experimental import pallas as pl
from jax.experimental.pallas import tpu as pltpu

F32 = jnp.float32
BF16 = jnp.bfloat16

D_MODEL = 2048
CHUNK = 64
HEAD = 64
N_HEADS = 16
RWKV_DIM = N_HEADS * HEAD
W_LORA, A_LORA, G_LORA = 64, 64, 128
RWKV_IN = 3 * RWKV_DIM + W_LORA + A_LORA + G_LORA
MLA_HEADS = 8
Q_LORA = KV_LORA = 512
QK_NOPE, QK_ROPE, V_HEAD = 128, 64, 128
QK_CAT = 2 * QK_NOPE
ROPE_THETA = 10000.0
CONV_WIDTH = 31
CONV_HALO = 32
N_EXPERTS = 8
NORM_EPS = 1e-6
LN_EPS = 1e-5
GN_EPS = HEAD * 1e-5
NEG_BIG = -1e30

LANES = 128
VMEM_LIMIT = 56 * 1024 * 1024

HEADS_PER_GROUP = 2
GROUP_W = HEADS_PER_GROUP * HEAD
MOE_TM = 1024
MOE_SUB = 256
MOE_TF = 256
GATHER_ROWS = 256


def _cparams(*sem):
    return pltpu.CompilerParams(dimension_semantics=sem, vmem_limit_bytes=VMEM_LIMIT)


def _tile(n, pref, align=8):
    if n <= pref:
        return n
    t = (pref // align) * align
    while t > align and n % t:
        t -= align
    assert n % t == 0, (n, pref)
    return t


def _dot(a, b):
    return jnp.dot(a, b, preferred_element_type=F32)


def _dot_nt(a, b):
    return lax.dot_general(a, b, (((1,), (1,)), ((), ())), preferred_element_type=F32)


def _split3(x):
    hi = x.astype(BF16)
    r1 = x - hi.astype(F32)
    mid = r1.astype(BF16)
    lo = (r1 - mid.astype(F32)).astype(BF16)
    return hi, mid, lo


def _dot_lx(a, b_exact):
    hi, mid, lo = _split3(a)
    return _dot(hi, b_exact) + (_dot(mid, b_exact) + _dot(lo, b_exact))


def _dot_xr(a_exact, b):
    hi, mid, lo = _split3(b)
    return _dot(a_exact, hi) + (_dot(a_exact, mid) + _dot(a_exact, lo))


def _dot_hp(a, b):
    ah, am, _ = _split3(a)
    bh, bm, _ = _split3(b)
    return _dot(ah, bh) + (_dot(ah, bm) + _dot(am, bh))


def _sigmoid(x):
    return 1.0 / (1.0 + jnp.exp(-x))


def _silu(x):
    return x * _sigmoid(x)


def _softplus(x):
    return jnp.maximum(x, 0.0) + jnp.log(1.0 + jnp.exp(-jnp.abs(x)))


def _rms(x, g):
    return x * lax.rsqrt(jnp.mean(x * x, axis=-1, keepdims=True) + NORM_EPS) * g


def _rmsnorm_kernel(x_ref, g_ref, o_ref):
    o_ref[...] = _rms(x_ref[...], g_ref[...]).astype(o_ref.dtype)


def _rmsnorm_call(x, g):
    m, d = x.shape
    tm = _tile(m, 512)
    return pl.pallas_call(
        _rmsnorm_kernel,
        grid=(m // tm,),
        in_specs=[pl.BlockSpec((tm, d), lambda i: (i, 0)), pl.BlockSpec((1, d), lambda i: (0, 0))],
        out_specs=pl.BlockSpec((tm, d), lambda i: (i, 0)),
        out_shape=jax.ShapeDtypeStruct((m, d), BF16),
        compiler_params=_cparams("parallel"),
        name="rmsnorm",
    )(x, g.reshape(1, d))


def _matmul_kernel(a_ref, w_ref, o_ref):
    o_ref[...] = _dot(a_ref[...], w_ref[...])


def _matmul_call(a, w, tm_pref=512, tn_pref=1664, name="matmul"):
    m, k = a.shape
    n = w.shape[1]
    tm = _tile(m, tm_pref)
    tn = _tile(n, tn_pref, LANES)
    return pl.pallas_call(
        _matmul_kernel,
        grid=(m // tm, n // tn),
        in_specs=[pl.BlockSpec((tm, k), lambda i, j: (i, 0)), pl.BlockSpec((k, tn), lambda i, j: (0, j))],
        out_specs=pl.BlockSpec((tm, tn), lambda i, j: (i, j)),
        out_shape=jax.ShapeDtypeStruct((m, n), F32),
        compiler_params=_cparams("parallel", "arbitrary"),
        name=name,
    )(a, w)


def _rwkv_prep_kernel(p_ref, prev_ref, mu_ref, lw_ref, w0_ref, a0_ref, gup_ref, kk_ref, ka_ref, rk_ref,
                      tril_ref, ones_ref, seg_ref, segt_ref,
                      at_ref, bt_ref, kt_ref, rt_ref, v_ref, bh_ref, kh_ref, gc_ref, bonus_ref, g_ref,
                      *, tiles_per_seq):
    ts = p_ref.shape[0]
    first = (pl.program_id(0) % tiles_per_seq) == 0
    row0 = lax.broadcasted_iota(jnp.int32, (ts, 1), 0) == 0

    def mixed(c0, c1):
        pc = p_ref[:, c0:c1]
        prev = jnp.where(first, 0.0, prev_ref[7:8, c0:c1])
        sh = jnp.where(row0, prev, pltpu.roll(pc, 1, axis=0))
        return pc + (sh - pc) * mu_ref[:, c0:c1]

    d = RWKV_DIM
    r = mixed(0, d)
    k = mixed(d, 2 * d)
    v = mixed(2 * d, 3 * d)
    xwa = mixed(3 * d, 3 * d + W_LORA + A_LORA)
    xg = mixed(3 * d + W_LORA + A_LORA, RWKV_IN)

    lane = lax.broadcasted_iota(jnp.int32, xwa.shape, 1)
    z = jnp.where(lane < W_LORA, jnp.tanh(xwa), xwa)
    wa = _dot_hp(z, lw_ref[...])
    w = -_softplus(-(w0_ref[...] + wa[:, :d])) - 0.5
    logdecay = -jnp.exp(w)
    a = _sigmoid(a0_ref[...] + wa[:, d:])
    g_ref[...] = _dot_hp(_sigmoid(xg), gup_ref[...])

    seg = seg_ref[...]
    segt = segt_ref[...]

    def head_sum(x):
        return _dot_lx(_dot_lx(x, seg), segt)

    kkr = k * kk_ref[...]
    kk = kkr / jnp.maximum(jnp.sqrt(head_sum(kkr * kkr)), 1e-12)
    k2 = k * (1.0 + (a - 1.0) * ka_ref[...])
    bonus_ref[...] = head_sum(r * k2 * rk_ref[...]) * v

    gcum = _dot_xr(tril_ref[...], logdecay)
    gtot = _dot_xr(ones_ref[...], logdecay)
    g_in = jnp.exp(gcum)
    g_ex = jnp.exp(gcum - logdecay)
    g_inv = jnp.exp(-gcum)
    g_rest = jnp.exp(gtot - gcum)
    beta = kk * a
    at_ref[...] = -kk * g_ex
    bt_ref[...] = beta * g_inv
    kt_ref[...] = k2 * g_inv
    rt_ref[...] = r * g_in
    v_ref[...] = v
    bh_ref[...] = beta * g_rest
    kh_ref[...] = k2 * g_rest
    gc_ref[...] = jnp.exp(gtot)


def _rwkv_prep_call(p, seq, mu, lora_w, w0, a0, g_up, k_k, k_a, r_k):
    m = p.shape[0]
    ts = _tile(seq, 256, CHUNK)
    tiles_per_seq = seq // ts
    d = RWKV_DIM
    idx = np.arange(ts)
    same = (idx[:, None] // CHUNK) == (idx[None, :] // CHUNK)
    tril = jnp.asarray(same & (idx[:, None] >= idx[None, :]), BF16)
    ones = jnp.asarray(same, BF16)
    lane = np.arange(d)
    seg_np = (lane[:, None] // HEAD) == np.arange(LANES)[None, :]
    seg = jnp.asarray(seg_np, BF16)
    segt = jnp.asarray(seg_np.T, BF16)

    def row(x):
        return x.reshape(1, -1).astype(F32)

    full = lambda shape: pl.BlockSpec(shape, lambda i: (0, 0))
    tok = pl.BlockSpec((ts, d), lambda i: (i, 0))
    outs = pl.pallas_call(
        functools.partial(_rwkv_prep_kernel, tiles_per_seq=tiles_per_seq),
        grid=(m // ts,),
        in_specs=[
            pl.BlockSpec((ts, RWKV_IN), lambda i: (i, 0)),
            pl.BlockSpec((8, RWKV_IN), lambda i: (jnp.maximum(i * (ts // 8) - 1, 0), 0)),
            full((1, RWKV_IN)), full((W_LORA + A_LORA, 2 * d)), full((1, d)), full((1, d)),
            full((G_LORA, d)), full((1, d)), full((1, d)), full((1, d)),
            full((ts, ts)), full((ts, ts)), full((d, LANES)), full((LANES, d)),
        ],
        out_specs=[tok] * 10,
        out_shape=[jax.ShapeDtypeStruct((m, d), F32)] * 10,
        compiler_params=_cparams("parallel"),
        name="rwkv_prep",
    )(p, p, row(mu), lora_w, row(w0), row(a0), g_up, row(k_k), row(k_a), row(r_k), tril, ones, seg, segt)
    return outs


def _stack_heads(x):
    lane_head = lax.broadcasted_iota(jnp.int32, x.shape, 1) // HEAD
    return jnp.concatenate([jnp.where(lane_head == h, x, 0.0) for h in range(HEADS_PER_GROUP)], axis=0)


def _unstack_heads(x):
    out = x[0:CHUNK]
    for h in range(1, HEADS_PER_GROUP):
        out = out + x[h * CHUNK:(h + 1) * CHUNK]
    return out


def _rwkv_intra_kernel(at_ref, bt_ref, kt_ref, rt_ref, v_ref, bh_ref, kh_ref, gc_ref,
                       ra_ref, o2_ref, p_ref, q_ref, *, mm):
    w = GROUP_W
    n_chunks = at_ref.shape[0] // CHUNK
    ri = lax.broadcasted_iota(jnp.int32, (w, w), 0)
    ci = lax.broadcasted_iota(jnp.int32, (w, w), 1)
    strict = (ri % CHUNK) > (ci % CHUNK)
    incl = (ri % CHUNK) >= (ci % CHUNK)
    blk16 = (ri // 16) == (ci // 16)
    eye = (ri == ci).astype(F32)

    for c in range(n_chunks):
        rows = slice(c * CHUNK, (c + 1) * CHUNK)
        a_s = _stack_heads(at_ref[rows, :])
        r_s = _stack_heads(rt_ref[rows, :])
        b_s = _stack_heads(bt_ref[rows, :])
        k_s = _stack_heads(kt_ref[rows, :])
        v_s = _stack_heads(v_ref[rows, :])
        bh_s = _stack_heads(bh_ref[rows, :])
        kh_s = _stack_heads(kh_ref[rows, :])

        s = mm(jnp.concatenate([a_s, r_s], axis=0), jnp.concatenate([b_s, k_s], axis=0), nt=True)
        l_ab = jnp.where(strict, s[:w, :w], 0.0)
        l_ak = jnp.where(strict, s[:w, w:], 0.0)
        m_rb = jnp.where(incl, s[w:, :w], 0.0)
        m_rk = jnp.where(incl, s[w:, w:], 0.0)

        dg = jnp.where(blk16, l_ab, 0.0)
        off = l_ab - dg
        td = eye + dg
        pw = dg
        for _ in range(3):
            pw = mm(pw, pw)
            td = td + mm(td, pw)
        n1 = mm(td, off)
        n2 = mm(n1, n1)
        t1 = td + mm(n2, td)
        t = t1 + mm(n1, t1)

        y = mm(t, jnp.concatenate([mm(l_ak, v_s), a_s], axis=1))
        z = mm(m_rb, y)
        ra_ref[rows, :] = _unstack_heads(r_s + z[:, w:])
        o2_ref[rows, :] = _unstack_heads(z[:, :w] + mm(m_rk, v_s))
        z2 = mm(bh_s.T, y)
        p_ref[rows, :] = _unstack_heads(eye * gc_ref[c * CHUNK:c * CHUNK + 1, :] + z2[:, w:])
        q_ref[rows, :] = _unstack_heads(z2[:, :w] + mm(kh_s.T, v_s))


def _mm_bf16(a, b, nt=False):
    a = a.astype(BF16)
    b = b.astype(BF16)
    return _dot_nt(a, b) if nt else _dot(a, b)


def _mm_hp(a, b, nt=False):
    ah, am, _ = _split3(a)
    bh, bm, _ = _split3(b)
    f = _dot_nt if nt else _dot
    return f(ah, bh) + (f(ah, bm) + f(am, bh))


def _rwkv_intra_call(at, bt, kt, rt, v, bh, kh, gc, seq):
    m, d = at.shape
    ts = _tile(seq, 256, CHUNK)
    spec = pl.BlockSpec((ts, GROUP_W), lambda i, j: (i, j))
    return pl.pallas_call(
        functools.partial(_rwkv_intra_kernel, mm=_mm_hp),
        grid=(m // ts, d // GROUP_W),
        in_specs=[spec] * 8,
        out_specs=[spec] * 4,
        out_shape=[jax.ShapeDtypeStruct((m, d), F32)] * 4,
        compiler_params=_cparams("parallel", "parallel"),
        name="rwkv_intra",
    )(at, bt, kt, rt, v, bh, kh, gc)


def _rwkv_seq_kernel(ra_ref, o2_ref, p_ref, q_ref, bonus_ref, g_ref, gnw_ref, gnb_ref, seg_ref, segt_ref,
                     o_ref, h_sc, y_sc, *, mm):
    w = GROUP_W
    n_groups = RWKV_DIM // w
    n_chunks = ra_ref.shape[0] // CHUNK

    @pl.when(pl.program_id(1) == 0)
    def _():
        h_sc[...] = jnp.zeros_like(h_sc)

    def chunk(c, carry):
        r0 = pl.multiple_of(c * CHUNK, CHUNK)
        rows = pl.ds(r0, CHUNK)
        for gi in range(n_groups):
            cols = slice(gi * w, (gi + 1) * w)
            lhs = jnp.concatenate([ra_ref[rows, cols], _stack_heads(p_ref[rows, cols])], axis=0)
            res = mm(lhs, h_sc[gi])
            y_sc[rows, cols] = res[:CHUNK] + o2_ref[rows, cols]
            h_sc[gi] = res[CHUNK:] + _stack_heads(q_ref[rows, cols])
        return carry

    lax.fori_loop(0, n_chunks, chunk, 0)

    y = y_sc[...]
    seg = seg_ref[...]
    segt = segt_ref[...]
    mean = _dot_lx(_dot_lx(y, seg), segt) * (1.0 / HEAD)
    yc = y - mean
    var = _dot_lx(_dot_lx(yc * yc, seg), segt) * (1.0 / HEAD)
    yn = yc * lax.rsqrt(var + GN_EPS) * gnw_ref[...] + gnb_ref[...]
    o_ref[...] = ((yn + bonus_ref[...]) * g_ref[...]).astype(o_ref.dtype)


def _rwkv_seq_call(ra, o2, p, q, bonus, g, gn_w, gn_b, batch, seq):
    m, d = ra.shape
    ts = _tile(seq, 512, CHUNK)
    nt = seq // ts
    lane = np.arange(d)
    seg_np = (lane[:, None] // HEAD) == np.arange(LANES)[None, :]
    seg = jnp.asarray(seg_np, BF16)
    segt = jnp.asarray(seg_np.T, BF16)
    tok = pl.BlockSpec((ts, d), lambda b, i: (b * nt + i, 0))
    full = lambda shape: pl.BlockSpec(shape, lambda b, i: (0, 0))
    return pl.pallas_call(
        functools.partial(_rwkv_seq_kernel, mm=_mm_hp),
        grid=(batch, nt),
        in_specs=[tok] * 6 + [full((1, d)), full((1, d)), full((d, LANES)), full((LANES, d))],
        out_specs=tok,
        out_shape=jax.ShapeDtypeStruct((m, d), BF16),
        scratch_shapes=[pltpu.VMEM((d // GROUP_W, GROUP_W, GROUP_W), F32), pltpu.VMEM((ts, d), F32)],
        compiler_params=_cparams("arbitrary", "arbitrary"),
        name="rwkv_seq",
    )(ra, o2, p, q, bonus, g, gn_w.reshape(1, d), gn_b.reshape(1, d), seg, segt)


def _rope_tab(pos_ref, inv_ref):
    ang = pos_ref[...].astype(F32) * inv_ref[...]
    lane = lax.broadcasted_iota(jnp.int32, ang.shape, 1)
    return jnp.where(lane < QK_ROPE, jnp.cos(ang), jnp.sin(ang))


def _mla_q_kernel(lat_ref, pos_ref, inv_ref, g_ref, w_ref, o_ref, *, scale):
    hn = _rms(lat_ref[...], g_ref[...]).astype(BF16)
    q = _dot(hn, w_ref[...]) * scale
    tab = _rope_tab(pos_ref, inv_ref)
    for h in range(MLA_HEADS):
        c0 = h * QK_CAT
        o_ref[0, h, :, 0:QK_NOPE] = q[:, c0:c0 + QK_NOPE].astype(BF16)
        o_ref[0, h, :, QK_NOPE:QK_CAT] = (q[:, c0 + QK_NOPE:c0 + QK_CAT] * tab).astype(BF16)


def _mla_kv_kernel(lat_ref, kr_ref, pos_ref, inv_ref, g_ref, w_ref, k_ref, v_ref):
    hn = _rms(lat_ref[...], g_ref[...]).astype(BF16)
    kv = _dot(hn, w_ref[...])
    t = kr_ref[...] * _rope_tab(pos_ref, inv_ref)
    k_rope = (t + pltpu.roll(t, QK_ROPE, axis=1)).astype(BF16)
    for h in range(MLA_HEADS):
        c0 = h * (QK_NOPE + V_HEAD)
        k_ref[0, h, :, 0:QK_NOPE] = kv[:, c0:c0 + QK_NOPE].astype(BF16)
        k_ref[0, h, :, QK_NOPE:QK_CAT] = k_rope
        v_ref[0, h] = kv[:, c0 + QK_NOPE:c0 + QK_NOPE + V_HEAD].astype(BF16)


def _mla_proj_calls(p_mla, pos, inv_tab, q_norm, wq, kv_norm, wkv, batch, seq):
    tm = _tile(seq, 512)
    nt = seq // tm
    scale = float((QK_NOPE + QK_ROPE) ** -0.5)
    lat = lambda c: pl.BlockSpec((tm, Q_LORA), lambda b, i: (b * nt + i, c))
    posspec = pl.BlockSpec((tm, 1), lambda b, i: (b * nt + i, 0))
    full = lambda shape: pl.BlockSpec(shape, lambda b, i: (0, 0))
    headed = lambda w: pl.BlockSpec((1, MLA_HEADS, tm, w), lambda b, i: (b, 0, i, 0))
    q = pl.pallas_call(
        functools.partial(_mla_q_kernel, scale=scale),
        grid=(batch, nt),
        in_specs=[lat(0), posspec, full((1, LANES)), full((1, Q_LORA)), full((Q_LORA, MLA_HEADS * QK_CAT))],
        out_specs=headed(QK_CAT),
        out_shape=jax.ShapeDtypeStruct((batch, MLA_HEADS, seq, QK_CAT), BF16),
        compiler_params=_cparams("parallel", "parallel"),
        name="mla_q",
    )(p_mla, pos, inv_tab, q_norm.reshape(1, -1), wq)
    k, v = pl.pallas_call(
        _mla_kv_kernel,
        grid=(batch, nt),
        in_specs=[lat(1), pl.BlockSpec((tm, LANES), lambda b, i: (b * nt + i, 2 * Q_LORA // LANES)),
                  posspec, full((1, LANES)), full((1, KV_LORA)),
                  full((KV_LORA, MLA_HEADS * (QK_NOPE + V_HEAD)))],
        out_specs=[headed(QK_CAT), headed(V_HEAD)],
        out_shape=[jax.ShapeDtypeStruct((batch, MLA_HEADS, seq, QK_CAT), BF16),
                   jax.ShapeDtypeStruct((batch, MLA_HEADS, seq, V_HEAD), BF16)],
        compiler_params=_cparams("parallel", "parallel"),
        name="mla_kv",
    )(p_mla, p_mla, pos, inv_tab, kv_norm.reshape(1, -1), wkv)
    return q, k, v


def _attn_kernel(qi_ref, kj_ref, q_ref, k_ref, v_ref, o_ref, m_sc, l_sc, acc_sc):
    t = pl.program_id(2)
    qi = qi_ref[t]
    kj = kj_ref[t]

    @pl.when(kj == 0)
    def _():
        m_sc[...] = jnp.full_like(m_sc, NEG_BIG)
        l_sc[...] = jnp.zeros_like(l_sc)
        acc_sc[...] = jnp.zeros_like(acc_sc)

    def update(mask):
        s = _dot_nt(q_ref[0, 0], k_ref[0, 0])
        if mask:
            ri = lax.broadcasted_iota(jnp.int32, s.shape, 0) // CHUNK
            ci = lax.broadcasted_iota(jnp.int32, s.shape, 1) // CHUNK
            s = jnp.where(ci <= ri, s, NEG_BIG)
        m_prev = m_sc[...]
        m_new = jnp.maximum(m_prev, jnp.max(s, axis=-1, keepdims=True))
        alpha = jnp.exp(m_prev - m_new)
        p = jnp.exp(s - m_new)
        l_sc[...] = alpha * l_sc[...] + jnp.sum(p, axis=-1, keepdims=True)
        acc_sc[...] = alpha * acc_sc[...] + _dot(p.astype(BF16), v_ref[0, 0])
        m_sc[...] = m_new

    @pl.when(kj < qi)
    def _():
        update(False)

    @pl.when(kj == qi)
    def _():
        update(True)
        o_ref[...] = (acc_sc[...] / l_sc[...]).astype(o_ref.dtype)


def _attn_call(q, k, v, batch, seq):
    tq = _tile(seq, 512, CHUNK)
    nq = seq // tq
    pairs = [(i, j) for i in range(nq) for j in range(i + 1)]
    qi = jnp.asarray([p[0] for p in pairs], jnp.int32)
    kj = jnp.asarray([p[1] for p in pairs], jnp.int32)
    grid_spec = pltpu.PrefetchScalarGridSpec(
        num_scalar_prefetch=2,
        grid=(batch, MLA_HEADS, len(pairs)),
        in_specs=[
            pl.BlockSpec((1, 1, tq, QK_CAT), lambda b, h, t, qi, kj: (b, h, qi[t], 0)),
            pl.BlockSpec((1, 1, tq, QK_CAT), lambda b, h, t, qi, kj: (b, h, kj[t], 0)),
            pl.BlockSpec((1, 1, tq, V_HEAD), lambda b, h, t, qi, kj: (b, h, kj[t], 0)),
        ],
        out_specs=pl.BlockSpec((tq, V_HEAD), lambda b, h, t, qi, kj: (b * nq + qi[t], h)),
        scratch_shapes=[pltpu.VMEM((tq, 1), F32), pltpu.VMEM((tq, 1), F32), pltpu.VMEM((tq, V_HEAD), F32)],
    )
    return pl.pallas_call(
        _attn_kernel,
        grid_spec=grid_spec,
        out_shape=jax.ShapeDtypeStruct((batch * seq, MLA_HEADS * V_HEAD), BF16),
        compiler_params=_cparams("parallel", "parallel", "arbitrary"),
        name="mla_attn",
    )(qi, kj, q, k, v)


def _mix_out_kernel(ya_ref, yb_ref, w_ref, x_ref, g_ref, xo_ref, hn_ref):
    half = ya_ref.shape[1]
    acc = _dot(ya_ref[...], w_ref[0:half, :]) + _dot(yb_ref[...], w_ref[half:, :])
    x = x_ref[...] + acc
    xo_ref[...] = x
    hn_ref[...] = _rms(x, g_ref[...]).astype(hn_ref.dtype)


def _mix_out_call(ya, yb, w, x, g):
    m, d = x.shape
    tm = _tile(m, 512)
    half = ya.shape[1]
    row = pl.BlockSpec((tm, d), lambda i: (i, 0))
    return pl.pallas_call(
        _mix_out_kernel,
        grid=(m // tm,),
        in_specs=[pl.BlockSpec((tm, half), lambda i: (i, 0)), pl.BlockSpec((tm, half), lambda i: (i, 0)),
                  pl.BlockSpec((2 * half, d), lambda i: (0, 0)), row, pl.BlockSpec((1, d), lambda i: (0, 0))],
        out_specs=[row, row],
        out_shape=[jax.ShapeDtypeStruct((m, d), F32), jax.ShapeDtypeStruct((m, d), BF16)],
        compiler_params=_cparams("parallel"),
        name="mix_out",
    )(ya, yb, w, x, g.reshape(1, d))


def _conv_out_kernel(a_ref, w_ref, b_ref, x_ref, g_ref, r_ref, xo_ref, hn_ref, lg_ref):
    x = x_ref[...] + _dot(a_ref[...], w_ref[...]) + b_ref[...]
    xo_ref[...] = x
    hn = _rms(x, g_ref[...])
    hn_ref[...] = hn
    lg_ref[...] = _dot_hp(hn, r_ref[...])


def _conv_out_call(a, w, b, x, g, router_pad):
    m, d = x.shape
    tm = _tile(m, 256)
    row = pl.BlockSpec((tm, d), lambda i: (i, 0))
    full = lambda shape: pl.BlockSpec(shape, lambda i: (0, 0))
    return pl.pallas_call(
        _conv_out_kernel,
        grid=(m // tm,),
        in_specs=[row, full((d, d)), full((1, d)), row, full((1, d)), full((d, LANES))],
        out_specs=[row, row, pl.BlockSpec((tm, LANES), lambda i: (i, 0))],
        out_shape=[jax.ShapeDtypeStruct((m, d), F32), jax.ShapeDtypeStruct((m, d), F32),
                   jax.ShapeDtypeStruct((m, LANES), F32)],
        compiler_params=_cparams("parallel"),
        name="conv_out",
    )(a, w, b.reshape(1, d), x, g.reshape(1, d), router_pad)


def _ffn_kernel(h_ref, w1_ref, w3_ref, w2_ref, x_ref, g_ref, xo_ref, hn_ref):
    f = pl.program_id(1)

    @pl.when(f == 0)
    def _():
        xo_ref[...] = x_ref[...]

    h = h_ref[...]
    gate = (_silu(_dot(h, w1_ref[...])) * _dot(h, w3_ref[...])).astype(BF16)
    xo_ref[...] += _dot(gate, w2_ref[...])

    @pl.when(f == pl.num_programs(1) - 1)
    def _():
        hn_ref[...] = _rms(xo_ref[...], g_ref[...]).astype(hn_ref.dtype)


def _ffn_call(h, w1, w3, w2, x, g):
    m, d = x.shape
    ff = w1.shape[1]
    tm = _tile(m, 512)
    tf = _tile(ff, 512, LANES)
    row = pl.BlockSpec((tm, d), lambda i, f: (i, 0))
    return pl.pallas_call(
        _ffn_kernel,
        grid=(m // tm, ff // tf),
        in_specs=[row, pl.BlockSpec((d, tf), lambda i, f: (0, f)), pl.BlockSpec((d, tf), lambda i, f: (0, f)),
                  pl.BlockSpec((tf, d), lambda i, f: (f, 0)), row, pl.BlockSpec((1, d), lambda i, f: (0, 0))],
        out_specs=[row, row],
        out_shape=[jax.ShapeDtypeStruct((m, d), F32), jax.ShapeDtypeStruct((m, d), BF16)],
        compiler_params=_cparams("parallel", "arbitrary"),
        name="ffn",
    )(h, w1, w3, w2, x, g.reshape(1, d))


def _glu_kernel(h_ref, wa_ref, wb_ref, ba_ref, bb_ref, o_ref):
    h = h_ref[...]
    a = _dot(h, wa_ref[...]) + ba_ref[...]
    b = _dot(h, wb_ref[...]) + bb_ref[...]
    o_ref[...] = a * _sigmoid(b)


def _glu_call(h, w, b):
    m, d = h.shape
    n = w.shape[1] // 2
    tm = _tile(m, 512)
    tn = _tile(n, 512, LANES)
    nj = n // tn
    b2 = b.reshape(1, 2 * n)
    return pl.pallas_call(
        _glu_kernel,
        grid=(m // tm, nj),
        in_specs=[pl.BlockSpec((tm, d), lambda i, j: (i, 0)),
                  pl.BlockSpec((d, tn), lambda i, j: (0, j)), pl.BlockSpec((d, tn), lambda i, j: (0, j + nj)),
                  pl.BlockSpec((1, tn), lambda i, j: (0, j)), pl.BlockSpec((1, tn), lambda i, j: (0, j + nj))],
        out_specs=pl.BlockSpec((tm, tn), lambda i, j: (i, j)),
        out_shape=jax.ShapeDtypeStruct((m, n), F32),
        compiler_params=_cparams("parallel", "arbitrary"),
        name="conv_glu",
    )(h, w, w, b2, b2)


def _dwconv_kernel(u_ref, halo_ref, dw_ref, dwb_ref, lg_ref, lb_ref, o_ref, ext_sc, acc_sc, *, tiles_per_seq):
    ts, d = u_ref.shape
    first = (pl.program_id(0) % tiles_per_seq) == 0
    ext_sc[0:CONV_HALO, :] = jnp.where(first, 0.0, halo_ref[...])
    ext_sc[CONV_HALO:, :] = u_ref[...]
    rc, cc = 64, 256
    base = CONV_HALO - (CONV_WIDTH - 1)
    for c0 in range(0, d, cc):
        for r0 in range(0, ts, rc):
            acc = jnp.zeros((rc, cc), F32) + dwb_ref[:, c0:c0 + cc]
            for j in range(CONV_WIDTH):
                acc = acc + dw_ref[j:j + 1, c0:c0 + cc] * ext_sc[r0 + base + j:r0 + base + j + rc, c0:c0 + cc]
            acc_sc[r0:r0 + rc, c0:c0 + cc] = acc
    y = acc_sc[...]
    mu = jnp.mean(y, axis=-1, keepdims=True)
    yc = y - mu
    var = jnp.mean(yc * yc, axis=-1, keepdims=True)
    yn = yc * lax.rsqrt(var + LN_EPS) * lg_ref[...] + lb_ref[...]
    o_ref[...] = _silu(yn).astype(o_ref.dtype)


def _dwconv_call(u, seq, dw_w, dw_b, ln_g, ln_b):
    m, d = u.shape
    ts = _tile(seq, 256, CONV_HALO)
    tiles_per_seq = seq // ts
    dw_pad = jnp.concatenate([dw_w, jnp.zeros((CONV_HALO - CONV_WIDTH, d), F32)], axis=0)
    full = lambda shape: pl.BlockSpec(shape, lambda i: (0, 0))
    return pl.pallas_call(
        functools.partial(_dwconv_kernel, tiles_per_seq=tiles_per_seq),
        grid=(m // ts,),
        in_specs=[pl.BlockSpec((ts, d), lambda i: (i, 0)),
                  pl.BlockSpec((CONV_HALO, d), lambda i: (jnp.maximum(i * (ts // CONV_HALO) - 1, 0), 0)),
                  full((CONV_HALO, d)), full((1, d)), full((1, d)), full((1, d))],
        out_specs=pl.BlockSpec((ts, d), lambda i: (i, 0)),
        out_shape=jax.ShapeDtypeStruct((m, d), BF16),
        scratch_shapes=[pltpu.VMEM((ts + CONV_HALO, d), F32), pltpu.VMEM((ts, d), F32)],
        compiler_params=_cparams("parallel"),
        name="dwconv",
    )(u, u, dw_pad, dw_b.reshape(1, d), ln_g.reshape(1, d), ln_b.reshape(1, d))


def _route_kernel(lg_ref, tri_ref, o_ref, cnt_ref, carry_sc):
    @pl.when(pl.program_id(0) == 0)
    def _():
        carry_sc[...] = jnp.zeros_like(carry_sc)

    lg = lg_ref[...]
    lane = lax.broadcasted_iota(jnp.int32, lg.shape, 1)
    lg = jnp.where(lane < N_EXPERTS, lg, -jnp.inf)
    m1 = jnp.max(lg, axis=-1, keepdims=True)
    e1 = jnp.min(jnp.where(lg == m1, lane, LANES), axis=-1, keepdims=True)
    lg2 = jnp.where(lane == e1, -jnp.inf, lg)
    m2 = jnp.max(lg2, axis=-1, keepdims=True)
    e2 = jnp.min(jnp.where(lg2 == m2, lane, LANES), axis=-1, keepdims=True)
    ex = jnp.exp(m2 - m1)
    g1 = 1.0 / (1.0 + ex)
    g2 = ex / (1.0 + ex)
    oh1 = (lane == e1).astype(F32)
    oh2 = (lane == e2).astype(F32)
    both = oh1 + oh2
    before = _dot(tri_ref[...], both.astype(BF16)) + carry_sc[...]
    r1 = jnp.sum(before * oh1, axis=-1, keepdims=True)
    r2 = jnp.sum(before * oh2, axis=-1, keepdims=True)
    carry_sc[...] = carry_sc[...] + jnp.sum(both, axis=0, keepdims=True)
    cnt_ref[...] = jnp.broadcast_to(carry_sc[...], cnt_ref.shape)
    out = jnp.where(lane == 0, e1.astype(F32), 0.0)
    out = jnp.where(lane == 1, e2.astype(F32), out)
    out = jnp.where(lane == 2, g1, out)
    out = jnp.where(lane == 3, g2, out)
    out = jnp.where(lane == 4, r1, out)
    out = jnp.where(lane == 5, r2, out)
    o_ref[...] = out


def _route_call(logits):
    m = logits.shape[0]
    tm = _tile(m, 512)
    idx = np.arange(tm)
    tri = jnp.asarray(idx[:, None] > idx[None, :], BF16)
    return pl.pallas_call(
        _route_kernel,
        grid=(m // tm,),
        in_specs=[pl.BlockSpec((tm, LANES), lambda i: (i, 0)), pl.BlockSpec((tm, tm), lambda i: (0, 0))],
        out_specs=[pl.BlockSpec((tm, LANES), lambda i: (i, 0)), pl.BlockSpec((8, LANES), lambda i: (0, 0))],
        out_shape=[jax.ShapeDtypeStruct((m, LANES), F32), jax.ShapeDtypeStruct((8, LANES), F32)],
        scratch_shapes=[pltpu.VMEM((1, LANES), F32)],
        compiler_params=_cparams("arbitrary"),
        name="moe_route",
    )(logits, tri)


def _row_copy(src_ref, dst_ref, sem, src_row, dst_row):
    return pltpu.make_async_copy(src_ref.at[pl.ds(src_row, 1)], dst_ref.at[pl.ds(dst_row, 1)], sem)


def _gather_kernel(tok_ref, src_ref, o_ref, buf_sc, sem):
    rows = o_ref.shape[0]

    def start(r, c):
        _row_copy(src_ref, buf_sc, sem, tok_ref[0, 0, r], r).start()
        return c

    def wait(r, c):
        _row_copy(src_ref, buf_sc, sem, 0, r).wait()
        return c

    lax.fori_loop(0, rows, start, 0)
    lax.fori_loop(0, rows, wait, 0)
    o_ref[...] = buf_sc[...].astype(o_ref.dtype)


def _gather_call(src, slot_tok):
    cap = slot_tok.shape[0]
    d = src.shape[1]
    rows = GATHER_ROWS
    return pl.pallas_call(
        _gather_kernel,
        grid=(cap // rows,),
        in_specs=[pl.BlockSpec((1, 1, rows), lambda i: (i, 0, 0), memory_space=pltpu.SMEM),
                  pl.BlockSpec(memory_space=pl.ANY)],
        out_specs=pl.BlockSpec((rows, d), lambda i: (i, 0)),
        out_shape=jax.ShapeDtypeStruct((cap, d), BF16),
        scratch_shapes=[pltpu.VMEM((rows, d), F32), pltpu.SemaphoreType.DMA(())],
        compiler_params=_cparams("arbitrary"),
        name="moe_gather",
    )(slot_tok.reshape(cap // rows, 1, rows), src)


def _moe_kernel(te_ref, tr_ref, x_ref, w1_ref, w3_ref, w2_ref, o_ref, w1_sc, w3_sc, w2_sc):
    i = pl.program_id(0)
    f = pl.program_id(1)
    nrows = tr_ref[i]

    @pl.when(f == 0)
    def _():
        o_ref[...] = jnp.zeros_like(o_ref)

    @pl.when(nrows > 0)
    def _():
        w1_sc[...] = w1_ref[0].astype(BF16)
        w3_sc[...] = w3_ref[0].astype(BF16)
        w2_sc[...] = w2_ref[0].astype(BF16)

    for s in range(x_ref.shape[0] // MOE_SUB):
        @pl.when(s * MOE_SUB < nrows)
        def _():
            rows = slice(s * MOE_SUB, (s + 1) * MOE_SUB)
            x = x_ref[rows, :]
            gate = (_silu(_dot(x, w1_sc[...])) * _dot(x, w3_sc[...])).astype(BF16)
            o_ref[rows, :] += _dot(gate, w2_sc[...])


def _moe_call(xs, tile_expert, tile_rows, w1, w3, w2):
    cap, d = xs.shape
    ff = w1.shape[2]
    tm = MOE_TM
    tf = MOE_TF
    nf = ff // tf

    def fidx(i, f, te, tr):
        return jnp.where(tr[i] > 0, f, nf - 1)

    grid_spec = pltpu.PrefetchScalarGridSpec(
        num_scalar_prefetch=2,
        grid=(cap // tm, nf),
        in_specs=[
            pl.BlockSpec((tm, d), lambda i, f, te, tr: (i, 0)),
            pl.BlockSpec((1, d, tf), lambda i, f, te, tr: (te[i], 0, fidx(i, f, te, tr))),
            pl.BlockSpec((1, d, tf), lambda i, f, te, tr: (te[i], 0, fidx(i, f, te, tr))),
            pl.BlockSpec((1, tf, d), lambda i, f, te, tr: (te[i], fidx(i, f, te, tr), 0)),
        ],
        out_specs=pl.BlockSpec((tm, d), lambda i, f, te, tr: (i, 0)),
        scratch_shapes=[pltpu.VMEM((d, tf), BF16), pltpu.VMEM((d, tf), BF16), pltpu.VMEM((tf, d), BF16)],
    )
    return pl.pallas_call(
        _moe_kernel,
        grid_spec=grid_spec,
        out_shape=jax.ShapeDtypeStruct((cap, d), F32),
        compiler_params=_cparams("parallel", "arbitrary"),
        name="moe_experts",
    )(tile_expert, tile_rows, xs, w1, w3, w2)


def _combine_kernel(pos_ref, y_ref, x_ref, rt_ref, g_ref, o_ref, buf_sc, sem):
    rows = o_ref.shape[0]

    def start(r, c):
        _row_copy(y_ref, buf_sc.at[0], sem, pos_ref[0, 0, r], r).start()
        _row_copy(y_ref, buf_sc.at[1], sem, pos_ref[0, 1, r], r).start()
        return c

    def wait(r, c):
        _row_copy(y_ref, buf_sc.at[0], sem, 0, r).wait()
        _row_copy(y_ref, buf_sc.at[1], sem, 0, r).wait()
        return c

    lax.fori_loop(0, rows, start, 0)
    lax.fori_loop(0, rows, wait, 0)
    rt = rt_ref[...]
    x = x_ref[...] + rt[:, 2:3] * buf_sc[0] + rt[:, 3:4] * buf_sc[1]
    o_ref[...] = _rms(x, g_ref[...])


def _combine_call(pos, yb, x, route, g):
    m, d = x.shape
    rows = GATHER_ROWS if m % GATHER_ROWS == 0 else m
    return pl.pallas_call(
        _combine_kernel,
        grid=(m // rows,),
        in_specs=[pl.BlockSpec((1, 2, rows), lambda i: (i, 0, 0), memory_space=pltpu.SMEM),
                  pl.BlockSpec(memory_space=pl.ANY),
                  pl.BlockSpec((rows, d), lambda i: (i, 0)),
                  pl.BlockSpec((rows, LANES), lambda i: (i, 0)),
                  pl.BlockSpec((1, d), lambda i: (0, 0))],
        out_specs=pl.BlockSpec((rows, d), lambda i: (i, 0)),
        out_shape=jax.ShapeDtypeStruct((m, d), F32),
        scratch_shapes=[pltpu.VMEM((2, rows, d), F32), pltpu.SemaphoreType.DMA(())],
        compiler_params=_cparams("arbitrary"),
        name="moe_combine",
    )(pos.reshape(m // rows, rows, 2).transpose(0, 2, 1), yb, x, route, g.reshape(1, d))


def _rot_cols(w):
    half = QK_ROPE // 2
    return jnp.concatenate([-w[..., half:], w[..., :half]], axis=-1)


def _mixer_weights(w_in, w_up, a_up, w_q_up):
    d = RWKV_DIM
    w_rwkv = w_in[:, :RWKV_IN]
    w_lat = w_in[:, RWKV_IN:RWKV_IN + Q_LORA + KV_LORA]
    w_kr = w_in[:, RWKV_IN + Q_LORA + KV_LORA:]
    w_mla = jnp.concatenate([w_lat, w_kr, _rot_cols(w_kr)], axis=1)
    lora = jnp.zeros((W_LORA + A_LORA, 2 * d), F32)
    lora = lora.at[:W_LORA, :d].set(w_up).at[W_LORA:, d:].set(a_up)
    wq = w_q_up.reshape(Q_LORA, MLA_HEADS, QK_NOPE + QK_ROPE)
    wq_rope = wq[..., QK_NOPE:]
    wq = jnp.concatenate([wq, _rot_cols(wq_rope)], axis=-1).reshape(Q_LORA, MLA_HEADS * QK_CAT)
    return w_rwkv.astype(BF16), w_mla.astype(BF16), lora, wq.astype(BF16)


def kernel(x, positions, l0_mix_norm, l0_w_in, l0_shift_mu, l0_w0, l0_w_up, l0_a0, l0_a_up, l0_g_up, l0_k_k, l0_k_a, l0_r_k, l0_gn_w, l0_gn_b, l0_q_norm, l0_w_q_up, l0_kv_norm, l0_w_kv_up, l0_w_out, l0_ffn_norm, l0_ffn_w1, l0_ffn_w3, l0_ffn_w2, l1_mix_norm, l1_pw1_w, l1_pw1_b, l1_dw_w, l1_dw_b, l1_ln_g, l1_ln_b, l1_pw2_w, l1_pw2_b, l1_ffn_norm, l1_router, l1_exp_w1, l1_exp_w3, l1_exp_w2, final_norm):
    batch, seq, d = x.shape
    m = batch * seq
    x0 = x.reshape(m, d)
    pos = positions.reshape(m, 1)
    inv = ROPE_THETA ** (-jnp.arange(0, QK_ROPE, 2, dtype=F32) / QK_ROPE)
    inv_tab = jnp.tile(inv, LANES // inv.shape[0]).reshape(1, LANES)

    w_rwkv, w_mla, lora_w, wq = _mixer_weights(l0_w_in, l0_w_up, l0_a_up, l0_w_q_up)

    hn0 = _rmsnorm_call(x0, l0_mix_norm)
    p_rwkv = _matmul_call(hn0, w_rwkv, name="in_proj_rwkv")
    p_mla = _matmul_call(hn0, w_mla, name="in_proj_mla")

    prep = _rwkv_prep_call(p_rwkv, seq, l0_shift_mu, lora_w, l0_w0, l0_a0, l0_g_up, l0_k_k, l0_k_a,
                           l0_r_k.reshape(-1))
    at, bt, kt, rt, v, bh, kh, gc, bonus, gate = prep
    ra, o2, pm, qm = _rwkv_intra_call(at, bt, kt, rt, v, bh, kh, gc, seq)
    y_rwkv = _rwkv_seq_call(ra, o2, pm, qm, bonus, gate, l0_gn_w, l0_gn_b, batch, seq)

    q, k, vv = _mla_proj_calls(p_mla, pos, inv_tab, l0_q_norm, wq, l0_kv_norm, l0_w_kv_up.astype(BF16),
                               batch, seq)
    y_mla = _attn_call(q, k, vv, batch, seq)

    x1, hn1 = _mix_out_call(y_rwkv, y_mla, l0_w_out.astype(BF16), x0, l0_ffn_norm)

    x2, hn2 = _ffn_call(hn1, l0_ffn_w1.astype(BF16), l0_ffn_w3.astype(BF16), l0_ffn_w2.astype(BF16),
                        x1, l1_mix_norm)

    u = _glu_call(hn2, l1_pw1_w.astype(BF16), l1_pw1_b)
    sc = _dwconv_call(u, seq, l1_dw_w, l1_dw_b, l1_ln_g, l1_ln_b)
    router_pad = jnp.zeros((d, LANES), F32).at[:, :N_EXPERTS].set(l1_router)
    x3, hn3, logits = _conv_out_call(sc, l1_pw2_w.astype(BF16), l1_pw2_b, x2, l1_ffn_norm, router_pad)

    route, counts = _route_call(logits)
    e = route[:, 0:2].astype(jnp.int32)
    rank = route[:, 4:6].astype(jnp.int32)
    cnt = counts[0, :N_EXPERTS].astype(jnp.int32)
    padded = (cnt + MOE_TM - 1) // MOE_TM * MOE_TM
    pad_end = jnp.cumsum(padded)
    pad_start = pad_end - padded
    pos_slot = pad_start[e] + rank
    cap = (2 * m + N_EXPERTS * MOE_TM + MOE_TM - 1) // MOE_TM * MOE_TM
    tok = jnp.broadcast_to(jnp.arange(m, dtype=jnp.int32)[:, None], (m, 2))
    slot_tok = jnp.zeros((cap,), jnp.int32).at[pos_slot.reshape(-1)].set(tok.reshape(-1))
    tile_start = jnp.arange(cap // MOE_TM, dtype=jnp.int32) * MOE_TM
    tile_expert = jnp.minimum(jnp.sum(tile_start[:, None] >= pad_end[None, :], axis=1), N_EXPERTS - 1)
    used = tile_start < pad_end[-1]
    last_used = jnp.max(jnp.where(used, tile_expert, 0))
    tile_expert = jnp.where(used, tile_expert, last_used).astype(jnp.int32)
    tile_rows = jnp.clip(pad_start[tile_expert] + cnt[tile_expert] - tile_start, 0, MOE_TM)
    tile_rows = jnp.where(used, tile_rows, 0).astype(jnp.int32)

    xs = _gather_call(hn3, slot_tok)
    yb = _moe_call(xs, tile_expert, tile_rows, l1_exp_w1, l1_exp_w3, l1_exp_w2)
    out = _combine_call(pos_slot, yb, x3, route, final_norm)
    return out.reshape(batch, seq, d)
```

```python
import functools

import numpy as np
import jax
import jax.numpy as jnp
from jax import lax
from jax.experimental import pallas as pl
from jax.experimental.pallas import tpu as pltpu

F32 = jnp.float32
BF16 = jnp.bfloat16

D_MODEL = 2048
CHUNK = 64
HEAD = 64
N_HEADS = 16
RWKV_DIM = N_HEADS * HEAD
W_LORA, A_LORA, G_LORA = 64, 64, 128
RWKV_IN = 3 * RWKV_DIM + W_LORA + A_LORA + G_LORA
MLA_HEADS = 8
Q_LORA = KV_LORA = 512
QK_NOPE, QK_ROPE, V_HEAD = 128, 64, 128
QK_CAT = 2 * QK_NOPE
ROPE_THETA = 10000.0
CONV_WIDTH = 31
CONV_HALO = 32
N_EXPERTS = 8
NORM_EPS = 1e-6
LN_EPS = 1e-5
GN_EPS = HEAD * 1e-5
NEG_BIG = -1e30

LANES = 128
SUBLANES = 8
VMEM_LIMIT = 56 * 1024 * 1024

HEADS_PER_GROUP = 2
GROUP_W = HEADS_PER_GROUP * HEAD
MOE_TM = 1024
MOE_SUB = 256
MOE_TF = 256
GATHER_ROWS = 256
ATTN_HEADS_PER_STEP = 4
LOG2_E = 1.4426950408889634


def _cparams(*sem):
    return pltpu.CompilerParams(dimension_semantics=sem, vmem_limit_bytes=VMEM_LIMIT)


def _tile(n, pref, align=8):
    if n <= pref:
        return n
    t = (pref // align) * align
    while t > align and n % t:
        t -= align
    assert n % t == 0, (n, pref)
    return t


def _dot(a, b):
    return jnp.dot(a, b, preferred_element_type=F32)


def _dot_nt(a, b):
    return lax.dot_general(a, b, (((1,), (1,)), ((), ())), preferred_element_type=F32)


def _split3(x):
    hi = x.astype(BF16)
    r1 = x - hi.astype(F32)
    mid = r1.astype(BF16)
    lo = (r1 - mid.astype(F32)).astype(BF16)
    return hi, mid, lo


def _dot_lx(a, b_exact):
    hi, mid, lo = _split3(a)
    return _dot(hi, b_exact) + (_dot(mid, b_exact) + _dot(lo, b_exact))


def _dot_xr(a_exact, b):
    hi, mid, lo = _split3(b)
    return _dot(a_exact, hi) + (_dot(a_exact, mid) + _dot(a_exact, lo))


def _dot_hp(a, b):
    ah, am, _ = _split3(a)
    bh, bm, _ = _split3(b)
    return _dot(ah, bh) + (_dot(ah, bm) + _dot(am, bh))


def _sigmoid(x):
    return 1.0 / (1.0 + jnp.exp(-x))


def _silu(x):
    return x * _sigmoid(x)


def _softplus(x):
    return jnp.maximum(x, 0.0) + jnp.log(1.0 + jnp.exp(-jnp.abs(x)))


def _rms(x, g):
    return x * lax.rsqrt(jnp.mean(x * x, axis=-1, keepdims=True) + NORM_EPS) * g


def _rmsnorm_kernel(x_ref, g_ref, o_ref):
    o_ref[...] = _rms(x_ref[...], g_ref[...]).astype(o_ref.dtype)


def _rmsnorm_call(x, g):
    m, d = x.shape
    tm = _tile(m, 512)
    return pl.pallas_call(
        _rmsnorm_kernel,
        grid=(m // tm,),
        in_specs=[pl.BlockSpec((tm, d), lambda i: (i, 0)), pl.BlockSpec((1, d), lambda i: (0, 0))],
        out_specs=pl.BlockSpec((tm, d), lambda i: (i, 0)),
        out_shape=jax.ShapeDtypeStruct((m, d), BF16),
        compiler_params=_cparams("parallel"),
        name="rmsnorm",
    )(x, g.reshape(1, d))


def _matmul_kernel(a_ref, w_ref, o_ref):
    o_ref[...] = _dot(a_ref[...], w_ref[...])


def _matmul_call(a, w, tm_pref=512, tn_pref=1664, name="matmul"):
    m, k = a.shape
    n = w.shape[1]
    tm = _tile(m, tm_pref)
    tn = _tile(n, tn_pref, LANES)
    return pl.pallas_call(
        _matmul_kernel,
        grid=(m // tm, n // tn),
        in_specs=[pl.BlockSpec((tm, k), lambda i, j: (i, 0)), pl.BlockSpec((k, tn), lambda i, j: (0, j))],
        out_specs=pl.BlockSpec((tm, tn), lambda i, j: (i, j)),
        out_shape=jax.ShapeDtypeStruct((m, n), F32),
        compiler_params=_cparams("parallel", "arbitrary"),
        name=name,
    )(a, w)


def _rwkv_prep_kernel(p_ref, prev_ref, mu_ref, lw_ref, w0_ref, a0_ref, gup_ref, kk_ref, ka_ref, rk_ref,
                      tril_ref, ones_ref, seg_ref, segt_ref,
                      at_ref, bt_ref, kt_ref, rt_ref, v_ref, bh_ref, kh_ref, gc_ref, bonus_ref, g_ref,
                      *, tiles_per_seq):
    ts = p_ref.shape[0]
    first = (pl.program_id(0) % tiles_per_seq) == 0
    row0 = lax.broadcasted_iota(jnp.int32, (ts, 1), 0) == 0

    def mixed(c0, c1):
        pc = p_ref[:, c0:c1]
        prev = jnp.where(first, 0.0, prev_ref[7:8, c0:c1])
        sh = jnp.where(row0, prev, pltpu.roll(pc, 1, axis=0))
        return pc + (sh - pc) * mu_ref[:, c0:c1]

    d = RWKV_DIM
    r = mixed(0, d)
    k = mixed(d, 2 * d)
    v = mixed(2 * d, 3 * d)
    xwa = mixed(3 * d, 3 * d + W_LORA + A_LORA)
    xg = mixed(3 * d + W_LORA + A_LORA, RWKV_IN)

    lane = lax.broadcasted_iota(jnp.int32, xwa.shape, 1)
    z = jnp.where(lane < W_LORA, jnp.tanh(xwa), xwa)
    wa = _dot_hp(z, lw_ref[...])
    w = -_softplus(-(w0_ref[...] + wa[:, :d])) - 0.5
    logdecay = -jnp.exp(w)
    a = _sigmoid(a0_ref[...] + wa[:, d:])
    g_ref[...] = _dot_hp(_sigmoid(xg), gup_ref[...])

    seg = seg_ref[...]
    segt = segt_ref[...]

    def head_sum(x):
        return _dot_lx(_dot_lx(x, seg), segt)

    kkr = k * kk_ref[...]
    kk = kkr / jnp.maximum(jnp.sqrt(head_sum(kkr * kkr)), 1e-12)
    k2 = k * (1.0 + (a - 1.0) * ka_ref[...])
    bonus_ref[...] = head_sum(r * k2 * rk_ref[...]) * v

    gcum = _dot_xr(tril_ref[...], logdecay)
    gtot = _dot_xr(ones_ref[...], logdecay)
    g_in = jnp.exp(gcum)
    g_ex = jnp.exp(gcum - logdecay)
    g_inv = jnp.exp(-gcum)
    g_rest = jnp.exp(gtot - gcum)
    beta = kk * a
    at_ref[...] = -kk * g_ex
    bt_ref[...] = beta * g_inv
    kt_ref[...] = k2 * g_inv
    rt_ref[...] = r * g_in
    v_ref[...] = v
    bh_ref[...] = beta * g_rest
    kh_ref[...] = k2 * g_rest
    gc_ref[...] = jnp.exp(gtot)


def _rwkv_prep_call(p, seq, mu, lora_w, w0, a0, g_up, k_k, k_a, r_k):
    m = p.shape[0]
    ts = _tile(seq, 256, CHUNK)
    tiles_per_seq = seq // ts
    d = RWKV_DIM
    idx = np.arange(ts)
    same = (idx[:, None] // CHUNK) == (idx[None, :] // CHUNK)
    tril = jnp.asarray(same & (idx[:, None] >= idx[None, :]), BF16)
    ones = jnp.asarray(same, BF16)
    lane = np.arange(d)
    seg_np = (lane[:, None] // HEAD) == np.arange(LANES)[None, :]
    seg = jnp.asarray(seg_np, BF16)
    segt = jnp.asarray(seg_np.T, BF16)

    def row(x):
        return x.reshape(1, -1).astype(F32)

    full = lambda shape: pl.BlockSpec(shape, lambda i: (0, 0))
    tok = pl.BlockSpec((ts, d), lambda i: (i, 0))
    outs = pl.pallas_call(
        functools.partial(_rwkv_prep_kernel, tiles_per_seq=tiles_per_seq),
        grid=(m // ts,),
        in_specs=[
            pl.BlockSpec((ts, RWKV_IN), lambda i: (i, 0)),
            pl.BlockSpec((8, RWKV_IN), lambda i: (jnp.maximum(i * (ts // 8) - 1, 0), 0)),
            full((1, RWKV_IN)), full((W_LORA + A_LORA, 2 * d)), full((1, d)), full((1, d)),
            full((G_LORA, d)), full((1, d)), full((1, d)), full((1, d)),
            full((ts, ts)), full((ts, ts)), full((d, LANES)), full((LANES, d)),
        ],
        out_specs=[tok] * 10,
        out_shape=[jax.ShapeDtypeStruct((m, d), F32)] * 10,
        compiler_params=_cparams("parallel"),
        name="rwkv_prep",
    )(p, p, row(mu), lora_w, row(w0), row(a0), g_up, row(k_k), row(k_a), row(r_k), tril, ones, seg, segt)
    return outs


def _stack_heads(x):
    lane_head = lax.broadcasted_iota(jnp.int32, x.shape, 1) // HEAD
    return jnp.concatenate([jnp.where(lane_head == h, x, 0.0) for h in range(HEADS_PER_GROUP)], axis=0)


def _unstack_heads(x):
    out = x[0:CHUNK]
    for h in range(1, HEADS_PER_GROUP):
        out = out + x[h * CHUNK:(h + 1) * CHUNK]
    return out


def _rwkv_intra_kernel(at_ref, bt_ref, kt_ref, rt_ref, v_ref, bh_ref, kh_ref, gc_ref,
                       ra_ref, o2_ref, p_ref, q_ref, *, mm):
    w = GROUP_W
    n_chunks = at_ref.shape[0] // CHUNK
    ri = lax.broadcasted_iota(jnp.int32, (w, w), 0)
    ci = lax.broadcasted_iota(jnp.int32, (w, w), 1)
    strict = (ri % CHUNK) > (ci % CHUNK)
    incl = (ri % CHUNK) >= (ci % CHUNK)
    blk16 = (ri // 16) == (ci // 16)
    eye = (ri == ci).astype(F32)

    chunks = [slice(c * CHUNK, (c + 1) * CHUNK) for c in range(n_chunks)]

    def each(f, *lists):
        return [f(*args) for args in zip(*lists)]

    def stacked(ref):
        return [_stack_heads(ref[rows, :]) for rows in chunks]

    a_s, r_s, b_s, k_s, v_s = stacked(at_ref), stacked(rt_ref), stacked(bt_ref), stacked(kt_ref), stacked(v_ref)
    s = each(lambda a, r, b, k: mm(jnp.concatenate([a, r], axis=0), jnp.concatenate([b, k], axis=0), nt=True),
             a_s, r_s, b_s, k_s)
    l_ab = [jnp.where(strict, x[:w, :w], 0.0) for x in s]
    l_ak = [jnp.where(strict, x[:w, w:], 0.0) for x in s]
    m_rb = [jnp.where(incl, x[w:, :w], 0.0) for x in s]
    m_rk = [jnp.where(incl, x[w:, w:], 0.0) for x in s]

    dg = [jnp.where(blk16, x, 0.0) for x in l_ab]
    off = each(lambda x, y: x - y, l_ab, dg)
    td = [eye + x for x in dg]
    pw = dg
    for _ in range(3):
        pw = each(mm, pw, pw)
        td = each(lambda x, y: x + mm(x, y), td, pw)
    n1 = each(mm, td, off)
    n2 = each(mm, n1, n1)
    t1 = each(lambda x, y: x + mm(y, x), td, n2)
    t = each(lambda x, y: x + mm(y, x), t1, n1)

    lv = each(mm, l_ak, v_s)
    y = each(lambda tt, x, a: mm(tt, jnp.concatenate([x, a], axis=1)), t, lv, a_s)
    z = each(mm, m_rb, y)
    mv = each(mm, m_rk, v_s)
    z2 = each(lambda ref_rows, yy: mm(_stack_heads(bh_ref[ref_rows, :]).T, yy), chunks, y)
    kv = each(lambda ref_rows, vv: mm(_stack_heads(kh_ref[ref_rows, :]).T, vv), chunks, v_s)
    for c, rows in enumerate(chunks):
        ra_ref[rows, :] = _unstack_heads(r_s[c] + z[c][:, w:])
        o2_ref[rows, :] = _unstack_heads(z[c][:, :w] + mv[c])
        p_ref[rows, :] = _unstack_heads(eye * gc_ref[c * CHUNK:c * CHUNK + 1, :] + z2[c][:, w:])
        q_ref[rows, :] = _unstack_heads(z2[c][:, :w] + kv[c])


def _mm_bf16(a, b, nt=False):
    a = a.astype(BF16)
    b = b.astype(BF16)
    return _dot_nt(a, b) if nt else _dot(a, b)


def _mm_hp(a, b, nt=False):
    ah, am, _ = _split3(a)
    bh, bm, _ = _split3(b)
    f = _dot_nt if nt else _dot
    return f(ah, bh) + (f(ah, bm) + f(am, bh))


def _rwkv_intra_call(at, bt, kt, rt, v, bh, kh, gc, seq):
    m, d = at.shape
    ts = _tile(seq, 512, CHUNK)
    spec = pl.BlockSpec((ts, GROUP_W), lambda i, j: (i, j))
    return pl.pallas_call(
        functools.partial(_rwkv_intra_kernel, mm=_mm_bf16),
        grid=(m // ts, d // GROUP_W),
        in_specs=[spec] * 8,
        out_specs=[spec] * 4,
        out_shape=[jax.ShapeDtypeStruct((m, d), F32)] * 4,
        compiler_params=_cparams("parallel", "parallel"),
        name="rwkv_intra",
    )(at, bt, kt, rt, v, bh, kh, gc)


def _rwkv_seq_kernel(ra_ref, o2_ref, p_ref, q_ref, bonus_ref, g_ref, gnw_ref, gnb_ref, seg_ref, segt_ref,
                     o_ref, h_sc, y_sc, *, mm):
    w = GROUP_W
    n_groups = RWKV_DIM // w
    n_chunks = ra_ref.shape[0] // CHUNK

    @pl.when(pl.program_id(1) == 0)
    def _():
        h_sc[...] = jnp.zeros_like(h_sc)

    def chunk(c, carry):
        r0 = pl.multiple_of(c * CHUNK, CHUNK)
        rows = pl.ds(r0, CHUNK)
        for gi in range(n_groups):
            cols = slice(gi * w, (gi + 1) * w)
            lhs = jnp.concatenate([ra_ref[rows, cols], _stack_heads(p_ref[rows, cols])], axis=0)
            res = mm(lhs, h_sc[gi])
            y_sc[rows, cols] = res[:CHUNK] + o2_ref[rows, cols]
            h_sc[gi] = res[CHUNK:] + _stack_heads(q_ref[rows, cols])
        return carry

    lax.fori_loop(0, n_chunks, chunk, 0)

    y = y_sc[...]
    seg = seg_ref[...]
    segt = segt_ref[...]
    mean = _dot_lx(_dot_lx(y, seg), segt) * (1.0 / HEAD)
    yc = y - mean
    var = _dot_lx(_dot_lx(yc * yc, seg), segt) * (1.0 / HEAD)
    yn = yc * lax.rsqrt(var + GN_EPS) * gnw_ref[...] + gnb_ref[...]
    o_ref[...] = ((yn + bonus_ref[...]) * g_ref[...]).astype(o_ref.dtype)


def _rwkv_seq_call(ra, o2, p, q, bonus, g, gn_w, gn_b, batch, seq):
    m, d = ra.shape
    ts = _tile(seq, 512, CHUNK)
    nt = seq // ts
    lane = np.arange(d)
    seg_np = (lane[:, None] // HEAD) == np.arange(LANES)[None, :]
    seg = jnp.asarray(seg_np, BF16)
    segt = jnp.asarray(seg_np.T, BF16)
    tok = pl.BlockSpec((ts, d), lambda b, i: (b * nt + i, 0))
    full = lambda shape: pl.BlockSpec(shape, lambda b, i: (0, 0))
    return pl.pallas_call(
        functools.partial(_rwkv_seq_kernel, mm=_mm_bf16),
        grid=(batch, nt),
        in_specs=[tok] * 6 + [full((1, d)), full((1, d)), full((d, LANES)), full((LANES, d))],
        out_specs=tok,
        out_shape=jax.ShapeDtypeStruct((m, d), BF16),
        scratch_shapes=[pltpu.VMEM((d // GROUP_W, GROUP_W, GROUP_W), F32), pltpu.VMEM((ts, d), F32)],
        compiler_params=_cparams("arbitrary", "arbitrary"),
        name="rwkv_seq",
    )(ra, o2, p, q, bonus, g, gn_w.reshape(1, d), gn_b.reshape(1, d), seg, segt)


def _rope_tab(pos_ref, inv_ref):
    ang = pos_ref[...].astype(F32) * inv_ref[...]
    lane = lax.broadcasted_iota(jnp.int32, ang.shape, 1)
    return jnp.where(lane < QK_ROPE, jnp.cos(ang), jnp.sin(ang))


def _mla_q_kernel(lat_ref, pos_ref, inv_ref, g_ref, w_ref, o_ref, *, scale):
    hn = _rms(lat_ref[...], g_ref[...]).astype(BF16)
    q = _dot(hn, w_ref[...]) * scale
    tab = _rope_tab(pos_ref, inv_ref)
    for h in range(MLA_HEADS):
        c0 = h * QK_CAT
        o_ref[0, h, :, 0:QK_NOPE] = q[:, c0:c0 + QK_NOPE].astype(BF16)
        o_ref[0, h, :, QK_NOPE:QK_CAT] = (q[:, c0 + QK_NOPE:c0 + QK_CAT] * tab).astype(BF16)


def _mla_kv_kernel(lat_ref, kr_ref, pos_ref, inv_ref, g_ref, w_ref, k_ref, v_ref):
    hn = _rms(lat_ref[...], g_ref[...]).astype(BF16)
    kv = _dot(hn, w_ref[...])
    t = kr_ref[...] * _rope_tab(pos_ref, inv_ref)
    k_rope = (t + pltpu.roll(t, QK_ROPE, axis=1)).astype(BF16)
    for h in range(MLA_HEADS):
        c0 = h * (QK_NOPE + V_HEAD)
        k_ref[0, h, :, 0:QK_NOPE] = kv[:, c0:c0 + QK_NOPE].astype(BF16)
        k_ref[0, h, :, QK_NOPE:QK_CAT] = k_rope
        v_ref[0, h] = kv[:, c0 + QK_NOPE:c0 + QK_NOPE + V_HEAD].astype(BF16)


def _mla_proj_calls(p_mla, pos, inv_tab, q_norm, wq, kv_norm, wkv, batch, seq):
    tm = _tile(seq, 512)
    nt = seq // tm
    scale = float((QK_NOPE + QK_ROPE) ** -0.5) * LOG2_E
    lat = lambda c: pl.BlockSpec((tm, Q_LORA), lambda b, i: (b * nt + i, c))
    posspec = pl.BlockSpec((tm, 1), lambda b, i: (b * nt + i, 0))
    full = lambda shape: pl.BlockSpec(shape, lambda b, i: (0, 0))
    headed = lambda w: pl.BlockSpec((1, MLA_HEADS, tm, w), lambda b, i: (b, 0, i, 0))
    q = pl.pallas_call(
        functools.partial(_mla_q_kernel, scale=scale),
        grid=(batch, nt),
        in_specs=[lat(0), posspec, full((1, LANES)), full((1, Q_LORA)), full((Q_LORA, MLA_HEADS * QK_CAT))],
        out_specs=headed(QK_CAT),
        out_shape=jax.ShapeDtypeStruct((batch, MLA_HEADS, seq, QK_CAT), BF16),
        compiler_params=_cparams("parallel", "parallel"),
        name="mla_q",
    )(p_mla, pos, inv_tab, q_norm.reshape(1, -1), wq)
    k, v = pl.pallas_call(
        _mla_kv_kernel,
        grid=(batch, nt),
        in_specs=[lat(1), pl.BlockSpec((tm, LANES), lambda b, i: (b * nt + i, 2 * Q_LORA // LANES)),
                  posspec, full((1, LANES)), full((1, KV_LORA)),
                  full((KV_LORA, MLA_HEADS * (QK_NOPE + V_HEAD)))],
        out_specs=[headed(QK_CAT), headed(V_HEAD)],
        out_shape=[jax.ShapeDtypeStruct((batch, MLA_HEADS, seq, QK_CAT), BF16),
                   jax.ShapeDtypeStruct((batch, MLA_HEADS, seq, V_HEAD), BF16)],
        compiler_params=_cparams("parallel", "parallel"),
        name="mla_kv",
    )(p_mla, p_mla, pos, inv_tab, kv_norm.reshape(1, -1), wkv)
    return q, k, v


def _attn_kernel(qi_ref, kj_ref, q_ref, k_ref, v_ref, o_ref, m_sc, l_sc, acc_sc):
    t = pl.program_id(2)
    qi = qi_ref[t]
    kj = kj_ref[t]
    n_heads = q_ref.shape[1]

    @pl.when(kj == 0)
    def _():
        m_sc[...] = jnp.full_like(m_sc, NEG_BIG)
        l_sc[...] = jnp.zeros_like(l_sc)
        acc_sc[...] = jnp.zeros_like(acc_sc)

    heads = range(n_heads)

    def update(mask):
        s = [_dot_nt(q_ref[0, h], k_ref[0, h]) for h in heads]
        if mask:
            ri = lax.broadcasted_iota(jnp.int32, s[0].shape, 0) // CHUNK
            ci = lax.broadcasted_iota(jnp.int32, s[0].shape, 1) // CHUNK
            s = [jnp.where(ci <= ri, x, NEG_BIG) for x in s]
        m_prev = [m_sc[h] for h in heads]
        m_new = [jnp.maximum(m_prev[h], jnp.max(s[h], axis=-1, keepdims=True)) for h in heads]
        alpha = [jnp.exp2(m_prev[h] - m_new[h]) for h in heads]
        p = [jnp.exp2(s[h] - m_new[h]) for h in heads]
        pv = [_dot(p[h].astype(BF16), v_ref[0, h]) for h in heads]
        for h in heads:
            l_sc[h] = alpha[h] * l_sc[h] + jnp.sum(p[h], axis=-1, keepdims=True)
            acc_sc[h] = alpha[h] * acc_sc[h] + pv[h]
            m_sc[h] = m_new[h]

    @pl.when(kj < qi)
    def _():
        update(False)

    @pl.when(kj == qi)
    def _():
        update(True)
        for h in heads:
            o_ref[:, h * V_HEAD:(h + 1) * V_HEAD] = (acc_sc[h] / l_sc[h]).astype(o_ref.dtype)


def _attn_call(q, k, v, batch, seq):
    tq = _tile(seq, 512, CHUNK)
    nq = seq // tq
    hg = ATTN_HEADS_PER_STEP
    pairs = [(i, j) for i in range(nq) for j in range(i + 1)]
    qi = jnp.asarray([p[0] for p in pairs], jnp.int32)
    kj = jnp.asarray([p[1] for p in pairs], jnp.int32)
    grid_spec = pltpu.PrefetchScalarGridSpec(
        num_scalar_prefetch=2,
        grid=(batch, MLA_HEADS // hg, len(pairs)),
        in_specs=[
            pl.BlockSpec((1, hg, tq, QK_CAT), lambda b, h, t, qi, kj: (b, h, qi[t], 0)),
            pl.BlockSpec((1, hg, tq, QK_CAT), lambda b, h, t, qi, kj: (b, h, kj[t], 0)),
            pl.BlockSpec((1, hg, tq, V_HEAD), lambda b, h, t, qi, kj: (b, h, kj[t], 0)),
        ],
        out_specs=pl.BlockSpec((tq, hg * V_HEAD), lambda b, h, t, qi, kj: (b * nq + qi[t], h)),
        scratch_shapes=[pltpu.VMEM((hg, tq, 1), F32), pltpu.VMEM((hg, tq, 1), F32),
                        pltpu.VMEM((hg, tq, V_HEAD), F32)],
    )
    return pl.pallas_call(
        _attn_kernel,
        grid_spec=grid_spec,
        out_shape=jax.ShapeDtypeStruct((batch * seq, MLA_HEADS * V_HEAD), BF16),
        compiler_params=_cparams("parallel", "parallel", "arbitrary"),
        name="mla_attn",
    )(qi, kj, q, k, v)


def _mix_out_kernel(ya_ref, yb_ref, w_ref, x_ref, g_ref, xo_ref, hn_ref):
    half = ya_ref.shape[1]
    acc = _dot(ya_ref[...], w_ref[0:half, :]) + _dot(yb_ref[...], w_ref[half:, :])
    x = x_ref[...] + acc
    xo_ref[...] = x
    hn_ref[...] = _rms(x, g_ref[...]).astype(hn_ref.dtype)


def _mix_out_call(ya, yb, w, x, g):
    m, d = x.shape
    tm = _tile(m, 512)
    half = ya.shape[1]
    row = pl.BlockSpec((tm, d), lambda i: (i, 0))
    return pl.pallas_call(
        _mix_out_kernel,
        grid=(m // tm,),
        in_specs=[pl.BlockSpec((tm, half), lambda i: (i, 0)), pl.BlockSpec((tm, half), lambda i: (i, 0)),
                  pl.BlockSpec((2 * half, d), lambda i: (0, 0)), row, pl.BlockSpec((1, d), lambda i: (0, 0))],
        out_specs=[row, row],
        out_shape=[jax.ShapeDtypeStruct((m, d), F32), jax.ShapeDtypeStruct((m, d), BF16)],
        compiler_params=_cparams("parallel"),
        name="mix_out",
    )(ya, yb, w, x, g.reshape(1, d))


def _conv_out_kernel(a_ref, w_ref, b_ref, x_ref, g_ref, r_ref, xo_ref, hn_ref, lg_ref):
    x = x_ref[...] + _dot(a_ref[...], w_ref[...]) + b_ref[...]
    xo_ref[...] = x
    hn = _rms(x, g_ref[...])
    hn_ref[...] = hn
    lg_ref[...] = _dot_hp(hn, r_ref[...])


def _conv_out_call(a, w, b, x, g, router_pad):
    m, d = x.shape
    tm = _tile(m, 256)
    row = pl.BlockSpec((tm, d), lambda i: (i, 0))
    full = lambda shape: pl.BlockSpec(shape, lambda i: (0, 0))
    return pl.pallas_call(
        _conv_out_kernel,
        grid=(m // tm,),
        in_specs=[row, full((d, d)), full((1, d)), row, full((1, d)), full((d, LANES))],
        out_specs=[row, row, pl.BlockSpec((tm, LANES), lambda i: (i, 0))],
        out_shape=[jax.ShapeDtypeStruct((m, d), F32), jax.ShapeDtypeStruct((m, d), F32),
                   jax.ShapeDtypeStruct((m, LANES), F32)],
        compiler_params=_cparams("parallel"),
        name="conv_out",
    )(a, w, b.reshape(1, d), x, g.reshape(1, d), router_pad)


def _ffn_kernel(h_ref, w1_ref, w3_ref, w2_ref, x_ref, g_ref, xo_ref, hn_ref):
    f = pl.program_id(1)

    @pl.when(f == 0)
    def _():
        xo_ref[...] = x_ref[...]

    h = h_ref[...]
    gate = (_silu(_dot(h, w1_ref[...])) * _dot(h, w3_ref[...])).astype(BF16)
    xo_ref[...] += _dot(gate, w2_ref[...])

    @pl.when(f == pl.num_programs(1) - 1)
    def _():
        hn_ref[...] = _rms(xo_ref[...], g_ref[...]).astype(hn_ref.dtype)


def _ffn_call(h, w1, w3, w2, x, g):
    m, d = x.shape
    ff = w1.shape[1]
    tm = _tile(m, 512)
    tf = _tile(ff, 512, LANES)
    row = pl.BlockSpec((tm, d), lambda i, f: (i, 0))
    return pl.pallas_call(
        _ffn_kernel,
        grid=(m // tm, ff // tf),
        in_specs=[row, pl.BlockSpec((d, tf), lambda i, f: (0, f)), pl.BlockSpec((d, tf), lambda i, f: (0, f)),
                  pl.BlockSpec((tf, d), lambda i, f: (f, 0)), row, pl.BlockSpec((1, d), lambda i, f: (0, 0))],
        out_specs=[row, row],
        out_shape=[jax.ShapeDtypeStruct((m, d), F32), jax.ShapeDtypeStruct((m, d), BF16)],
        compiler_params=_cparams("parallel", "arbitrary"),
        name="ffn",
    )(h, w1, w3, w2, x, g.reshape(1, d))


def _glu_kernel(h_ref, wa_ref, wb_ref, ba_ref, bb_ref, o_ref):
    h = h_ref[...]
    a = _dot(h, wa_ref[...]) + ba_ref[...]
    b = _dot(h, wb_ref[...]) + bb_ref[...]
    o_ref[...] = a * _sigmoid(b)


def _glu_call(h, w, b):
    m, d = h.shape
    n = w.shape[1] // 2
    tm = _tile(m, 512)
    tn = _tile(n, 512, LANES)
    nj = n // tn
    b2 = b.reshape(1, 2 * n)
    return pl.pallas_call(
        _glu_kernel,
        grid=(m // tm, nj),
        in_specs=[pl.BlockSpec((tm, d), lambda i, j: (i, 0)),
                  pl.BlockSpec((d, tn), lambda i, j: (0, j)), pl.BlockSpec((d, tn), lambda i, j: (0, j + nj)),
                  pl.BlockSpec((1, tn), lambda i, j: (0, j)), pl.BlockSpec((1, tn), lambda i, j: (0, j + nj))],
        out_specs=pl.BlockSpec((tm, tn), lambda i, j: (i, j)),
        out_shape=jax.ShapeDtypeStruct((m, n), F32),
        compiler_params=_cparams("parallel", "arbitrary"),
        name="conv_glu",
    )(h, w, w, b2, b2)


def _dwconv_kernel(u_ref, halo_ref, dw_ref, dwb_ref, lg_ref, lb_ref, o_ref, ext_sc, acc_sc, *, tiles_per_seq):
    ts, d = u_ref.shape
    first = (pl.program_id(0) % tiles_per_seq) == 0
    ext_sc[0, 0:CONV_HALO, :] = jnp.where(first, 0.0, halo_ref[...])
    ext_sc[0, CONV_HALO:, :] = u_ref[...]
    n_shift = ts + CONV_HALO - SUBLANES
    for b in range(1, SUBLANES):
        ext_sc[b, 0:n_shift, :] = ext_sc[0, b:b + n_shift, :]
    rc, cc = 64, 256
    base = CONV_HALO - (CONV_WIDTH - 1)
    for c0 in range(0, d, cc):
        for r0 in range(0, ts, rc):
            acc = jnp.zeros((rc, cc), F32) + dwb_ref[:, c0:c0 + cc]
            for j in range(CONV_WIDTH):
                a8, b = divmod(base + j, SUBLANES)
                rows = slice(r0 + a8 * SUBLANES, r0 + a8 * SUBLANES + rc)
                acc = acc + dw_ref[j:j + 1, c0:c0 + cc] * ext_sc[b, rows, c0:c0 + cc]
            acc_sc[r0:r0 + rc, c0:c0 + cc] = acc
    y = acc_sc[...]
    mu = jnp.mean(y, axis=-1, keepdims=True)
    yc = y - mu
    var = jnp.mean(yc * yc, axis=-1, keepdims=True)
    yn = yc * lax.rsqrt(var + LN_EPS) * lg_ref[...] + lb_ref[...]
    o_ref[...] = _silu(yn).astype(o_ref.dtype)


def _dwconv_call(u, seq, dw_w, dw_b, ln_g, ln_b):
    m, d = u.shape
    ts = _tile(seq, 256, CONV_HALO)
    tiles_per_seq = seq // ts
    dw_pad = jnp.concatenate([dw_w, jnp.zeros((CONV_HALO - CONV_WIDTH, d), F32)], axis=0)
    full = lambda shape: pl.BlockSpec(shape, lambda i: (0, 0))
    return pl.pallas_call(
        functools.partial(_dwconv_kernel, tiles_per_seq=tiles_per_seq),
        grid=(m // ts,),
        in_specs=[pl.BlockSpec((ts, d), lambda i: (i, 0)),
                  pl.BlockSpec((CONV_HALO, d), lambda i: (jnp.maximum(i * (ts // CONV_HALO) - 1, 0), 0)),
                  full((CONV_HALO, d)), full((1, d)), full((1, d)), full((1, d))],
        out_specs=pl.BlockSpec((ts, d), lambda i: (i, 0)),
        out_shape=jax.ShapeDtypeStruct((m, d), BF16),
        scratch_shapes=[pltpu.VMEM((SUBLANES, ts + CONV_HALO, d), F32), pltpu.VMEM((ts, d), F32)],
        compiler_params=_cparams("parallel"),
        name="dwconv",
    )(u, u, dw_pad, dw_b.reshape(1, d), ln_g.reshape(1, d), ln_b.reshape(1, d))


def _route_kernel(lg_ref, tri_ref, o_ref, cnt_ref, carry_sc):
    @pl.when(pl.program_id(0) == 0)
    def _():
        carry_sc[...] = jnp.zeros_like(carry_sc)

    lg = lg_ref[...]
    lane = lax.broadcasted_iota(jnp.int32, lg.shape, 1)
    lg = jnp.where(lane < N_EXPERTS, lg, -jnp.inf)
    m1 = jnp.max(lg, axis=-1, keepdims=True)
    e1 = jnp.min(jnp.where(lg == m1, lane, LANES), axis=-1, keepdims=True)
    lg2 = jnp.where(lane == e1, -jnp.inf, lg)
    m2 = jnp.max(lg2, axis=-1, keepdims=True)
    e2 = jnp.min(jnp.where(lg2 == m2, lane, LANES), axis=-1, keepdims=True)
    ex = jnp.exp(m2 - m1)
    g1 = 1.0 / (1.0 + ex)
    g2 = ex / (1.0 + ex)
    oh1 = (lane == e1).astype(F32)
    oh2 = (lane == e2).astype(F32)
    both = oh1 + oh2
    before = _dot(tri_ref[...], both.astype(BF16)) + carry_sc[...]
    r1 = jnp.sum(before * oh1, axis=-1, keepdims=True)
    r2 = jnp.sum(before * oh2, axis=-1, keepdims=True)
    carry_sc[...] = carry_sc[...] + jnp.sum(both, axis=0, keepdims=True)
    cnt_ref[...] = jnp.broadcast_to(carry_sc[...], cnt_ref.shape)
    out = jnp.where(lane == 0, e1.astype(F32), 0.0)
    out = jnp.where(lane == 1, e2.astype(F32), out)
    out = jnp.where(lane == 2, g1, out)
    out = jnp.where(lane == 3, g2, out)
    out = jnp.where(lane == 4, r1, out)
    out = jnp.where(lane == 5, r2, out)
    o_ref[...] = out


def _route_call(logits):
    m = logits.shape[0]
    tm = _tile(m, 512)
    idx = np.arange(tm)
    tri = jnp.asarray(idx[:, None] > idx[None, :], BF16)
    return pl.pallas_call(
        _route_kernel,
        grid=(m // tm,),
        in_specs=[pl.BlockSpec((tm, LANES), lambda i: (i, 0)), pl.BlockSpec((tm, tm), lambda i: (0, 0))],
        out_specs=[pl.BlockSpec((tm, LANES), lambda i: (i, 0)), pl.BlockSpec((8, LANES), lambda i: (0, 0))],
        out_shape=[jax.ShapeDtypeStruct((m, LANES), F32), jax.ShapeDtypeStruct((8, LANES), F32)],
        scratch_shapes=[pltpu.VMEM((1, LANES), F32)],
        compiler_params=_cparams("arbitrary"),
        name="moe_route",
    )(logits, tri)


def _row_copy(src_ref, dst_ref, sem, src_row, dst_row):
    return pltpu.make_async_copy(src_ref.at[pl.ds(src_row, 1)], dst_ref.at[pl.ds(dst_row, 1)], sem)


def _gather_kernel(used_ref, tok_ref, src_ref, o_ref, buf_sc, sem):
    rows = o_ref.shape[0]
    used = used_ref[pl.program_id(0)] > 0

    def start(r, c):
        _row_copy(src_ref, buf_sc, sem, tok_ref[0, 0, r], r).start()
        return c

    def wait(r, c):
        _row_copy(src_ref, buf_sc, sem, 0, r).wait()
        return c

    @pl.when(used)
    def _():
        lax.fori_loop(0, rows, start, 0)
        lax.fori_loop(0, rows, wait, 0)
        o_ref[...] = buf_sc[...].astype(o_ref.dtype)

    @pl.when(jnp.logical_not(used))
    def _():
        o_ref[...] = jnp.zeros_like(o_ref)


def _gather_call(src, slot_tok, block_used):
    cap = slot_tok.shape[0]
    d = src.shape[1]
    rows = GATHER_ROWS
    grid_spec = pltpu.PrefetchScalarGridSpec(
        num_scalar_prefetch=1,
        grid=(cap // rows,),
        in_specs=[pl.BlockSpec((1, 1, rows), lambda i, u: (i, 0, 0), memory_space=pltpu.SMEM),
                  pl.BlockSpec(memory_space=pl.ANY)],
        out_specs=pl.BlockSpec((rows, d), lambda i, u: (i, 0)),
        scratch_shapes=[pltpu.VMEM((rows, d), F32), pltpu.SemaphoreType.DMA(())],
    )
    return pl.pallas_call(
        _gather_kernel,
        grid_spec=grid_spec,
        out_shape=jax.ShapeDtypeStruct((cap, d), BF16),
        compiler_params=_cparams("arbitrary"),
        name="moe_gather",
    )(block_used, slot_tok.reshape(cap // rows, 1, rows), src)


def _moe_kernel(te_ref, tr_ref, x_ref, w1_ref, w3_ref, w2_ref, o_ref, w1_sc, w3_sc, w2_sc):
    i = pl.program_id(0)
    f = pl.program_id(1)
    nrows = tr_ref[i]

    @pl.when(f == 0)
    def _():
        o_ref[...] = jnp.zeros_like(o_ref)

    tm = x_ref.shape[0]

    def cast_weights():
        w1_sc[...] = w1_ref[0].astype(BF16)
        w3_sc[...] = w3_ref[0].astype(BF16)
        w2_sc[...] = w2_ref[0].astype(BF16)

    def swiglu(rows):
        x = x_ref[rows, :]
        gate = (_silu(_dot(x, w1_sc[...])) * _dot(x, w3_sc[...])).astype(BF16)
        o_ref[rows, :] += _dot(gate, w2_sc[...])

    @pl.when(nrows == tm)
    def _():
        cast_weights()
        swiglu(slice(0, tm))

    @pl.when(jnp.logical_and(nrows > 0, nrows < tm))
    def _():
        cast_weights()
        for s in range(tm // MOE_SUB):
            @pl.when(s * MOE_SUB < nrows)
            def _():
                swiglu(slice(s * MOE_SUB, (s + 1) * MOE_SUB))


def _moe_call(xs, tile_expert, tile_rows, w1, w3, w2):
    cap, d = xs.shape
    ff = w1.shape[2]
    tm = MOE_TM
    tf = MOE_TF
    nf = ff // tf

    def fidx(i, f, te, tr):
        return jnp.where(tr[i] > 0, f, nf - 1)

    grid_spec = pltpu.PrefetchScalarGridSpec(
        num_scalar_prefetch=2,
        grid=(cap // tm, nf),
        in_specs=[
            pl.BlockSpec((tm, d), lambda i, f, te, tr: (i, 0)),
            pl.BlockSpec((1, d, tf), lambda i, f, te, tr: (te[i], 0, fidx(i, f, te, tr))),
            pl.BlockSpec((1, d, tf), lambda i, f, te, tr: (te[i], 0, fidx(i, f, te, tr))),
            pl.BlockSpec((1, tf, d), lambda i, f, te, tr: (te[i], fidx(i, f, te, tr), 0)),
        ],
        out_specs=pl.BlockSpec((tm, d), lambda i, f, te, tr: (i, 0)),
        scratch_shapes=[pltpu.VMEM((d, tf), BF16), pltpu.VMEM((d, tf), BF16), pltpu.VMEM((tf, d), BF16)],
    )
    return pl.pallas_call(
        _moe_kernel,
        grid_spec=grid_spec,
        out_shape=jax.ShapeDtypeStruct((cap, d), F32),
        compiler_params=_cparams("parallel", "arbitrary"),
        name="moe_experts",
    )(tile_expert, tile_rows, xs, w1, w3, w2)


def _combine_kernel(pos_ref, y_ref, x_ref, rt_ref, g_ref, o_ref, buf_sc, sem):
    rows = o_ref.shape[0]

    def start(r, c):
        _row_copy(y_ref, buf_sc.at[0], sem, pos_ref[0, 0, r], r).start()
        _row_copy(y_ref, buf_sc.at[1], sem, pos_ref[0, 1, r], r).start()
        return c

    def wait(r, c):
        _row_copy(y_ref, buf_sc.at[0], sem, 0, r).wait()
        _row_copy(y_ref, buf_sc.at[1], sem, 0, r).wait()
        return c

    lax.fori_loop(0, rows, start, 0)
    lax.fori_loop(0, rows, wait, 0)
    rt = rt_ref[...]
    x = x_ref[...] + rt[:, 2:3] * buf_sc[0] + rt[:, 3:4] * buf_sc[1]
    o_ref[...] = _rms(x, g_ref[...])


def _combine_call(pos, yb, x, route, g):
    m, d = x.shape
    rows = GATHER_ROWS if m % GATHER_ROWS == 0 else m
    return pl.pallas_call(
        _combine_kernel,
        grid=(m // rows,),
        in_specs=[pl.BlockSpec((1, 2, rows), lambda i: (i, 0, 0), memory_space=pltpu.SMEM),
                  pl.BlockSpec(memory_space=pl.ANY),
                  pl.BlockSpec((rows, d), lambda i: (i, 0)),
                  pl.BlockSpec((rows, LANES), lambda i: (i, 0)),
                  pl.BlockSpec((1, d), lambda i: (0, 0))],
        out_specs=pl.BlockSpec((rows, d), lambda i: (i, 0)),
        out_shape=jax.ShapeDtypeStruct((m, d), F32),
        scratch_shapes=[pltpu.VMEM((2, rows, d), F32), pltpu.SemaphoreType.DMA(())],
        compiler_params=_cparams("arbitrary"),
        name="moe_combine",
    )(pos.reshape(m // rows, rows, 2).transpose(0, 2, 1), yb, x, route, g.reshape(1, d))


def _rot_cols(w):
    half = QK_ROPE // 2
    return jnp.concatenate([-w[..., half:], w[..., :half]], axis=-1)


def _mixer_weights(w_in, w_up, a_up, w_q_up):
    d = RWKV_DIM
    w_rwkv = w_in[:, :RWKV_IN]
    w_lat = w_in[:, RWKV_IN:RWKV_IN + Q_LORA + KV_LORA]
    w_kr = w_in[:, RWKV_IN + Q_LORA + KV_LORA:]
    w_mla = jnp.concatenate([w_lat, w_kr, _rot_cols(w_kr)], axis=1)
    lora = jnp.zeros((W_LORA + A_LORA, 2 * d), F32)
    lora = lora.at[:W_LORA, :d].set(w_up).at[W_LORA:, d:].set(a_up)
    wq = w_q_up.reshape(Q_LORA, MLA_HEADS, QK_NOPE + QK_ROPE)
    wq_rope = wq[..., QK_NOPE:]
    wq = jnp.concatenate([wq, _rot_cols(wq_rope)], axis=-1).reshape(Q_LORA, MLA_HEADS * QK_CAT)
    return w_rwkv.astype(BF16), w_mla.astype(BF16), lora, wq.astype(BF16)


def kernel(x, positions, l0_mix_norm, l0_w_in, l0_shift_mu, l0_w0, l0_w_up, l0_a0, l0_a_up, l0_g_up, l0_k_k, l0_k_a, l0_r_k, l0_gn_w, l0_gn_b, l0_q_norm, l0_w_q_up, l0_kv_norm, l0_w_kv_up, l0_w_out, l0_ffn_norm, l0_ffn_w1, l0_ffn_w3, l0_ffn_w2, l1_mix_norm, l1_pw1_w, l1_pw1_b, l1_dw_w, l1_dw_b, l1_ln_g, l1_ln_b, l1_pw2_w, l1_pw2_b, l1_ffn_norm, l1_router, l1_exp_w1, l1_exp_w3, l1_exp_w2, final_norm):
    batch, seq, d = x.shape
    m = batch * seq
    x0 = x.reshape(m, d)
    pos = positions.reshape(m, 1)
    inv = ROPE_THETA ** (-jnp.arange(0, QK_ROPE, 2, dtype=F32) / QK_ROPE)
    inv_tab = jnp.tile(inv, LANES // inv.shape[0]).reshape(1, LANES)

    w_rwkv, w_mla, lora_w, wq = _mixer_weights(l0_w_in, l0_w_up, l0_a_up, l0_w_q_up)

    hn0 = _rmsnorm_call(x0, l0_mix_norm)
    p_rwkv = _matmul_call(hn0, w_rwkv, name="in_proj_rwkv")
    p_mla = _matmul_call(hn0, w_mla, name="in_proj_mla")

    prep = _rwkv_prep_call(p_rwkv, seq, l0_shift_mu, lora_w, l0_w0, l0_a0, l0_g_up, l0_k_k, l0_k_a,
                           l0_r_k.reshape(-1))
    at, bt, kt, rt, v, bh, kh, gc, bonus, gate = prep
    ra, o2, pm, qm = _rwkv_intra_call(at, bt, kt, rt, v, bh, kh, gc, seq)
    y_rwkv = _rwkv_seq_call(ra, o2, pm, qm, bonus, gate, l0_gn_w, l0_gn_b, batch, seq)

    q, k, vv = _mla_proj_calls(p_mla, pos, inv_tab, l0_q_norm, wq, l0_kv_norm, l0_w_kv_up.astype(BF16),
                               batch, seq)
    y_mla = _attn_call(q, k, vv, batch, seq)

    x1, hn1 = _mix_out_call(y_rwkv, y_mla, l0_w_out.astype(BF16), x0, l0_ffn_norm)

    x2, hn2 = _ffn_call(hn1, l0_ffn_w1.astype(BF16), l0_ffn_w3.astype(BF16), l0_ffn_w2.astype(BF16),
                        x1, l1_mix_norm)

    u = _glu_call(hn2, l1_pw1_w.astype(BF16), l1_pw1_b)
    sc = _dwconv_call(u, seq, l1_dw_w, l1_dw_b, l1_ln_g, l1_ln_b)
    router_pad = jnp.zeros((d, LANES), F32).at[:, :N_EXPERTS].set(l1_router)
    x3, hn3, logits = _conv_out_call(sc, l1_pw2_w.astype(BF16), l1_pw2_b, x2, l1_ffn_norm, router_pad)

    route, counts = _route_call(logits)
    e = route[:, 0:2].astype(jnp.int32)
    rank = route[:, 4:6].astype(jnp.int32)
    cnt = counts[0, :N_EXPERTS].astype(jnp.int32)
    padded = (cnt + MOE_TM - 1) // MOE_TM * MOE_TM
    pad_end = jnp.cumsum(padded)
    pad_start = pad_end - padded
    pos_slot = pad_start[e] + rank
    cap = (2 * m + N_EXPERTS * MOE_TM + MOE_TM - 1) // MOE_TM * MOE_TM
    tok = jnp.broadcast_to(jnp.arange(m, dtype=jnp.int32)[:, None], (m, 2))
    slot_tok = jnp.zeros((cap,), jnp.int32).at[pos_slot.reshape(-1)].set(tok.reshape(-1))
    tile_start = jnp.arange(cap // MOE_TM, dtype=jnp.int32) * MOE_TM
    tile_expert = jnp.minimum(jnp.sum(tile_start[:, None] >= pad_end[None, :], axis=1), N_EXPERTS - 1)
    used = tile_start < pad_end[-1]
    last_used = jnp.max(jnp.where(used, tile_expert, 0))
    tile_expert = jnp.where(used, tile_expert, last_used).astype(jnp.int32)
    tile_rows = jnp.clip(pad_start[tile_expert] + cnt[tile_expert] - tile_start, 0, MOE_TM)
    tile_rows = jnp.where(used, tile_rows, 0).astype(jnp.int32)

    per_tile = MOE_TM // GATHER_ROWS
    block_off = jnp.tile(jnp.arange(per_tile, dtype=jnp.int32) * GATHER_ROWS, cap // MOE_TM)
    block_used = (jnp.repeat(tile_rows, per_tile) > block_off).astype(jnp.int32)
    xs = _gather_call(hn3, slot_tok, block_used)
    yb = _moe_call(xs, tile_expert, tile_rows, l1_exp_w1, l1_exp_w3, l1_exp_w2)
    out = _combine_call(pos_slot, yb, x3, route, final_norm)
    return out.reshape(batch, seq, d)
```

```python
import functools

import numpy as np
import jax
import jax.numpy as jnp
from jax import lax
from jax.experimental import pallas as pl
from jax.experimental.pallas import tpu as pltpu

F32 = jnp.float32
BF16 = jnp.bfloat16

D_MODEL = 2048
CHUNK = 64
HEAD = 64
N_HEADS = 16
RWKV_DIM = N_HEADS * HEAD
W_LORA, A_LORA, G_LORA = 64, 64, 128
RWKV_IN = 3 * RWKV_DIM + W_LORA + A_LORA + G_LORA
MLA_HEADS = 8
Q_LORA = KV_LORA = 512
QK_NOPE, QK_ROPE, V_HEAD = 128, 64, 128
QK_CAT = 2 * QK_NOPE
V_EXT = 2 * V_HEAD
ROPE_THETA = 10000.0
CONV_WIDTH = 31
CONV_HALO = 32
N_EXPERTS = 8
NORM_EPS = 1e-6
LN_EPS = 1e-5
GN_EPS = HEAD * 1e-5
NEG_BIG = -1e30

LANES = 128
SUBLANES = 8
VMEM_LIMIT = 56 * 1024 * 1024

HEADS_PER_GROUP = 2
GROUP_W = HEADS_PER_GROUP * HEAD
MOE_TM = 1024
MOE_SUB = 512
MOE_TF = 256
GATHER_ROWS = 256
SLABS = D_MODEL // LANES
ATTN_HEADS_PER_STEP = 4
LOG2_E = 1.4426950408889634


def _cparams(*sem):
    return pltpu.CompilerParams(dimension_semantics=sem, vmem_limit_bytes=VMEM_LIMIT)


def _tile(n, pref, align=8):
    if n <= pref:
        return n
    t = (pref // align) * align
    while t > align and n % t:
        t -= align
    assert n % t == 0, (n, pref)
    return t


def _dot(a, b):
    return jnp.dot(a, b, preferred_element_type=F32)


def _dot_nt(a, b):
    return lax.dot_general(a, b, (((1,), (1,)), ((), ())), preferred_element_type=F32)


def _split3(x):
    hi = x.astype(BF16)
    r1 = x - hi.astype(F32)
    mid = r1.astype(BF16)
    lo = (r1 - mid.astype(F32)).astype(BF16)
    return hi, mid, lo


def _dot_lx(a, b_exact):
    hi, mid, lo = _split3(a)
    return _dot(hi, b_exact) + (_dot(mid, b_exact) + _dot(lo, b_exact))


def _dot_xr(a_exact, b):
    hi, mid, lo = _split3(b)
    return _dot(a_exact, hi) + (_dot(a_exact, mid) + _dot(a_exact, lo))


def _dot_hp(a, b):
    ah, am, _ = _split3(a)
    bh, bm, _ = _split3(b)
    return _dot(ah, bh) + (_dot(ah, bm) + _dot(am, bh))


def _sigmoid(x):
    return 1.0 / (1.0 + jnp.exp(-x))


def _silu(x):
    return x * _sigmoid(x)


def _softplus(x):
    return jnp.maximum(x, 0.0) + jnp.log(1.0 + jnp.exp(-jnp.abs(x)))


def _rms(x, g):
    return x * lax.rsqrt(jnp.mean(x * x, axis=-1, keepdims=True) + NORM_EPS) * g


def _rmsnorm_kernel(x_ref, g_ref, o_ref):
    o_ref[...] = _rms(x_ref[...], g_ref[...]).astype(o_ref.dtype)


def _rmsnorm_call(x, g):
    m, d = x.shape
    tm = _tile(m, 512)
    return pl.pallas_call(
        _rmsnorm_kernel,
        grid=(m // tm,),
        in_specs=[pl.BlockSpec((tm, d), lambda i: (i, 0)), pl.BlockSpec((1, d), lambda i: (0, 0))],
        out_specs=pl.BlockSpec((tm, d), lambda i: (i, 0)),
        out_shape=jax.ShapeDtypeStruct((m, d), BF16),
        compiler_params=_cparams("parallel"),
        name="rmsnorm",
    )(x, g.reshape(1, d))


def _matmul_kernel(a_ref, w_ref, o_ref):
    o_ref[...] = _dot(a_ref[...], w_ref[...])


def _matmul_call(a, w, tm_pref=512, tn_pref=1664, name="matmul"):
    m, k = a.shape
    n = w.shape[1]
    tm = _tile(m, tm_pref)
    tn = _tile(n, tn_pref, LANES)
    return pl.pallas_call(
        _matmul_kernel,
        grid=(m // tm, n // tn),
        in_specs=[pl.BlockSpec((tm, k), lambda i, j: (i, 0)), pl.BlockSpec((k, tn), lambda i, j: (0, j))],
        out_specs=pl.BlockSpec((tm, tn), lambda i, j: (i, j)),
        out_shape=jax.ShapeDtypeStruct((m, n), F32),
        compiler_params=_cparams("parallel", "arbitrary"),
        name=name,
    )(a, w)


def _rwkv_prep_kernel(p_ref, prev_ref, mu_ref, lw_ref, w0_ref, a0_ref, gup_ref, kk_ref, ka_ref, rk_ref,
                      tril_ref, ones_ref, seg_ref, segt_ref,
                      at_ref, bt_ref, kt_ref, rt_ref, v_ref, bh_ref, kh_ref, gc_ref, bonus_ref, g_ref,
                      *, tiles_per_seq):
    ts = p_ref.shape[0]
    first = (pl.program_id(0) % tiles_per_seq) == 0
    row0 = lax.broadcasted_iota(jnp.int32, (ts, 1), 0) == 0

    def mixed(c0, c1):
        pc = p_ref[:, c0:c1]
        prev = jnp.where(first, 0.0, prev_ref[7:8, c0:c1])
        sh = jnp.where(row0, prev, pltpu.roll(pc, 1, axis=0))
        return pc + (sh - pc) * mu_ref[:, c0:c1]

    d = RWKV_DIM
    r = mixed(0, d)
    k = mixed(d, 2 * d)
    v = mixed(2 * d, 3 * d)
    xwa = mixed(3 * d, 3 * d + W_LORA + A_LORA)
    xg = mixed(3 * d + W_LORA + A_LORA, RWKV_IN)

    lane = lax.broadcasted_iota(jnp.int32, xwa.shape, 1)
    z = jnp.where(lane < W_LORA, jnp.tanh(xwa), xwa)
    wa = _dot_hp(z, lw_ref[...])
    w = -_softplus(-(w0_ref[...] + wa[:, :d])) - 0.5
    logdecay = -jnp.exp(w)
    a = _sigmoid(a0_ref[...] + wa[:, d:])
    g_ref[...] = _dot_hp(_sigmoid(xg), gup_ref[...])

    seg = seg_ref[...]
    segt = segt_ref[...]

    def head_sum(x):
        return _dot_lx(_dot_lx(x, seg), segt)

    kkr = k * kk_ref[...]
    kk = kkr / jnp.maximum(jnp.sqrt(head_sum(kkr * kkr)), 1e-12)
    k2 = k * (1.0 + (a - 1.0) * ka_ref[...])
    bonus_ref[...] = head_sum(r * k2 * rk_ref[...]) * v

    gcum = _dot_xr(tril_ref[...], logdecay)
    gtot = _dot_xr(ones_ref[...], logdecay)
    g_in = jnp.exp(gcum)
    g_ex = jnp.exp(gcum - logdecay)
    g_inv = jnp.exp(-gcum)
    g_rest = jnp.exp(gtot - gcum)
    beta = kk * a
    at_ref[...] = (-kk * g_ex).astype(at_ref.dtype)
    bt_ref[...] = (beta * g_inv).astype(bt_ref.dtype)
    kt_ref[...] = (k2 * g_inv).astype(kt_ref.dtype)
    rt_ref[...] = (r * g_in).astype(rt_ref.dtype)
    v_ref[...] = v.astype(v_ref.dtype)
    bh_ref[...] = beta * g_rest
    kh_ref[...] = k2 * g_rest
    gc_ref[...] = jnp.exp(gtot)


def _rwkv_prep_call(p, seq, mu, lora_w, w0, a0, g_up, k_k, k_a, r_k):
    m = p.shape[0]
    ts = _tile(seq, 256, CHUNK)
    tiles_per_seq = seq // ts
    d = RWKV_DIM
    idx = np.arange(ts)
    same = (idx[:, None] // CHUNK) == (idx[None, :] // CHUNK)
    tril = jnp.asarray(same & (idx[:, None] >= idx[None, :]), BF16)
    ones = jnp.asarray(same, BF16)
    lane = np.arange(d)
    seg_np = (lane[:, None] // HEAD) == np.arange(LANES)[None, :]
    seg = jnp.asarray(seg_np, BF16)
    segt = jnp.asarray(seg_np.T, BF16)

    def row(x):
        return x.reshape(1, -1).astype(F32)

    full = lambda shape: pl.BlockSpec(shape, lambda i: (0, 0))
    tok = pl.BlockSpec((ts, d), lambda i: (i, 0))
    outs = pl.pallas_call(
        functools.partial(_rwkv_prep_kernel, tiles_per_seq=tiles_per_seq),
        grid=(m // ts,),
        in_specs=[
            pl.BlockSpec((ts, RWKV_IN), lambda i: (i, 0)),
            pl.BlockSpec((8, RWKV_IN), lambda i: (jnp.maximum(i * (ts // 8) - 1, 0), 0)),
            full((1, RWKV_IN)), full((W_LORA + A_LORA, 2 * d)), full((1, d)), full((1, d)),
            full((G_LORA, d)), full((1, d)), full((1, d)), full((1, d)),
            full((ts, ts)), full((ts, ts)), full((d, LANES)), full((LANES, d)),
        ],
        out_specs=[tok] * 10,
        out_shape=[jax.ShapeDtypeStruct((m, d), BF16)] * 5 + [jax.ShapeDtypeStruct((m, d), F32)] * 5,
        compiler_params=_cparams("parallel"),
        name="rwkv_prep",
    )(p, p, row(mu), lora_w, row(w0), row(a0), g_up, row(k_k), row(k_a), row(r_k), tril, ones, seg, segt)
    return outs


def _stack_heads(x):
    lane_head = lax.broadcasted_iota(jnp.int32, x.shape, 1) // HEAD
    return jnp.concatenate([jnp.where(lane_head == h, x, 0.0) for h in range(HEADS_PER_GROUP)], axis=0)


def _unstack_heads(x):
    out = x[0:CHUNK]
    for h in range(1, HEADS_PER_GROUP):
        out = out + x[h * CHUNK:(h + 1) * CHUNK]
    return out


def _rwkv_intra_kernel(at_ref, bt_ref, kt_ref, rt_ref, v_ref, bh_ref, kh_ref, gc_ref,
                       ra_ref, o2_ref, p_ref, q_ref, *, mm):
    w = GROUP_W
    n_chunks = at_ref.shape[0] // CHUNK
    ri = lax.broadcasted_iota(jnp.int32, (w, w), 0)
    ci = lax.broadcasted_iota(jnp.int32, (w, w), 1)
    strict = (ri % CHUNK) > (ci % CHUNK)
    incl = (ri % CHUNK) >= (ci % CHUNK)
    blk16 = (ri // 16) == (ci // 16)
    eye = (ri == ci).astype(F32)

    chunks = [slice(c * CHUNK, (c + 1) * CHUNK) for c in range(n_chunks)]

    def each(f, *lists):
        return [f(*args) for args in zip(*lists)]

    def stacked(ref):
        return [_stack_heads(ref[rows, :]) for rows in chunks]

    a_s, r_s, b_s, k_s, v_s = stacked(at_ref), stacked(rt_ref), stacked(bt_ref), stacked(kt_ref), stacked(v_ref)
    s = each(lambda a, r, b, k: mm(jnp.concatenate([a, r], axis=0), jnp.concatenate([b, k], axis=0), nt=True),
             a_s, r_s, b_s, k_s)
    l_ab = [jnp.where(strict, x[:w, :w], 0.0) for x in s]
    l_ak = [jnp.where(strict, x[:w, w:], 0.0) for x in s]
    m_rb = [jnp.where(incl, x[w:, :w], 0.0) for x in s]
    m_rk = [jnp.where(incl, x[w:, w:], 0.0) for x in s]

    dg = [jnp.where(blk16, x, 0.0) for x in l_ab]
    off = each(lambda x, y: x - y, l_ab, dg)
    td = [eye + x for x in dg]
    pw = dg
    for _ in range(3):
        pw = each(mm, pw, pw)
        td = each(lambda x, y: x + mm(x, y), td, pw)
    n1 = each(mm, td, off)
    n2 = each(mm, n1, n1)
    t1 = each(lambda x, y: x + mm(y, x), td, n2)
    t = each(lambda x, y: x + mm(y, x), t1, n1)

    lv = each(mm, l_ak, v_s)
    y = each(lambda tt, x, a: mm(tt, jnp.concatenate([x, a], axis=1)), t, lv, a_s)
    z = each(mm, m_rb, y)
    mv = each(mm, m_rk, v_s)
    z2 = each(lambda ref_rows, yy: mm(_stack_heads(bh_ref[ref_rows, :]).T, yy), chunks, y)
    kv = each(lambda ref_rows, vv: mm(_stack_heads(kh_ref[ref_rows, :]).T, vv), chunks, v_s)
    for c, rows in enumerate(chunks):
        ra_ref[rows, :] = _unstack_heads(r_s[c] + z[c][:, w:]).astype(ra_ref.dtype)
        o2_ref[rows, :] = _unstack_heads(z[c][:, :w] + mv[c])
        p_ref[rows, :] = _unstack_heads(eye * gc_ref[c * CHUNK:c * CHUNK + 1, :]
                                        + z2[c][:, w:]).astype(p_ref.dtype)
        q_ref[rows, :] = _unstack_heads(z2[c][:, :w] + kv[c])


def _mm_bf16(a, b, nt=False):
    a = a.astype(BF16)
    b = b.astype(BF16)
    return _dot_nt(a, b) if nt else _dot(a, b)


def _mm_hp(a, b, nt=False):
    ah, am, _ = _split3(a)
    bh, bm, _ = _split3(b)
    f = _dot_nt if nt else _dot
    return f(ah, bh) + (f(ah, bm) + f(am, bh))


def _rwkv_intra_call(at, bt, kt, rt, v, bh, kh, gc, seq):
    m, d = at.shape
    ts = _tile(seq, 512, CHUNK)
    spec = pl.BlockSpec((ts, GROUP_W), lambda i, j: (i, j))
    return pl.pallas_call(
        functools.partial(_rwkv_intra_kernel, mm=_mm_bf16),
        grid=(m // ts, d // GROUP_W),
        in_specs=[spec] * 8,
        out_specs=[spec] * 4,
        out_shape=[jax.ShapeDtypeStruct((m, d), dt) for dt in (BF16, F32, BF16, F32)],
        compiler_params=_cparams("parallel", "parallel"),
        name="rwkv_intra",
    )(at, bt, kt, rt, v, bh, kh, gc)


def _rwkv_seq_kernel(ra_ref, o2_ref, p_ref, q_ref, bonus_ref, g_ref, gnw_ref, gnb_ref, seg_ref, segt_ref,
                     o_ref, h_sc, y_sc, *, mm):
    w = GROUP_W
    n_groups = RWKV_DIM // w
    n_chunks = ra_ref.shape[0] // CHUNK

    @pl.when(pl.program_id(1) == 0)
    def _():
        h_sc[...] = jnp.zeros_like(h_sc)

    def chunk(c, carry):
        r0 = pl.multiple_of(c * CHUNK, CHUNK)
        rows = pl.ds(r0, CHUNK)
        for gi in range(n_groups):
            cols = slice(gi * w, (gi + 1) * w)
            lhs = jnp.concatenate([ra_ref[rows, cols], _stack_heads(p_ref[rows, cols])], axis=0)
            res = mm(lhs, h_sc[gi])
            y_sc[rows, cols] = res[:CHUNK] + o2_ref[rows, cols]
            h_sc[gi] = res[CHUNK:] + _stack_heads(q_ref[rows, cols])
        return carry

    lax.fori_loop(0, n_chunks, chunk, 0)

    y = y_sc[...]
    seg = seg_ref[...]
    segt = segt_ref[...]
    mean = _dot_lx(_dot_lx(y, seg), segt) * (1.0 / HEAD)
    yc = y - mean
    var = _dot_lx(_dot_lx(yc * yc, seg), segt) * (1.0 / HEAD)
    yn = yc * lax.rsqrt(var + GN_EPS) * gnw_ref[...] + gnb_ref[...]
    o_ref[...] = ((yn + bonus_ref[...]) * g_ref[...]).astype(o_ref.dtype)


def _rwkv_seq_call(ra, o2, p, q, bonus, g, gn_w, gn_b, batch, seq):
    m, d = ra.shape
    ts = _tile(seq, 512, CHUNK)
    nt = seq // ts
    lane = np.arange(d)
    seg_np = (lane[:, None] // HEAD) == np.arange(LANES)[None, :]
    seg = jnp.asarray(seg_np, BF16)
    segt = jnp.asarray(seg_np.T, BF16)
    tok = pl.BlockSpec((ts, d), lambda b, i: (b * nt + i, 0))
    full = lambda shape: pl.BlockSpec(shape, lambda b, i: (0, 0))
    return pl.pallas_call(
        functools.partial(_rwkv_seq_kernel, mm=_mm_bf16),
        grid=(batch, nt),
        in_specs=[tok] * 6 + [full((1, d)), full((1, d)), full((d, LANES)), full((LANES, d))],
        out_specs=tok,
        out_shape=jax.ShapeDtypeStruct((m, d), BF16),
        scratch_shapes=[pltpu.VMEM((d // GROUP_W, GROUP_W, GROUP_W), F32), pltpu.VMEM((ts, d), F32)],
        compiler_params=_cparams("arbitrary", "arbitrary"),
        name="rwkv_seq",
    )(ra, o2, p, q, bonus, g, gn_w.reshape(1, d), gn_b.reshape(1, d), seg, segt)


def _rope_tab(pos_ref, inv_ref):
    ang = pos_ref[...].astype(F32) * inv_ref[...]
    lane = lax.broadcasted_iota(jnp.int32, ang.shape, 1)
    return jnp.where(lane < QK_ROPE, jnp.cos(ang), jnp.sin(ang))


def _mla_q_kernel(lat_ref, pos_ref, inv_ref, g_ref, w_ref, o_ref, *, scale):
    hn = _rms(lat_ref[...], g_ref[...]).astype(BF16)
    q = _dot(hn, w_ref[...]) * scale
    tab = _rope_tab(pos_ref, inv_ref)
    for h in range(MLA_HEADS):
        c0 = h * QK_CAT
        o_ref[0, h, :, 0:QK_NOPE] = q[:, c0:c0 + QK_NOPE].astype(BF16)
        o_ref[0, h, :, QK_NOPE:QK_CAT] = (q[:, c0 + QK_NOPE:c0 + QK_CAT] * tab).astype(BF16)


def _mla_kv_kernel(lat_ref, kr_ref, pos_ref, inv_ref, g_ref, w_ref, k_ref, v_ref):
    hn = _rms(lat_ref[...], g_ref[...]).astype(BF16)
    kv = _dot(hn, w_ref[...])
    t = kr_ref[...] * _rope_tab(pos_ref, inv_ref)
    k_rope = (t + pltpu.roll(t, QK_ROPE, axis=1)).astype(BF16)
    for h in range(MLA_HEADS):
        c0 = h * (QK_NOPE + V_HEAD)
        k_ref[0, h, :, 0:QK_NOPE] = kv[:, c0:c0 + QK_NOPE].astype(BF16)
        k_ref[0, h, :, QK_NOPE:QK_CAT] = k_rope
        v_ref[0, h, :, 0:V_HEAD] = kv[:, c0 + QK_NOPE:c0 + QK_NOPE + V_HEAD].astype(BF16)
        v_ref[0, h, :, V_HEAD:V_EXT] = jnp.ones((kv.shape[0], V_EXT - V_HEAD), BF16)


def _mla_proj_calls(p_mla, pos, inv_tab, q_norm, wq, kv_norm, wkv, batch, seq):
    tm = _tile(seq, 512)
    nt = seq // tm
    scale = float((QK_NOPE + QK_ROPE) ** -0.5) * LOG2_E
    lat = lambda c: pl.BlockSpec((tm, Q_LORA), lambda b, i: (b * nt + i, c))
    posspec = pl.BlockSpec((tm, 1), lambda b, i: (b * nt + i, 0))
    full = lambda shape: pl.BlockSpec(shape, lambda b, i: (0, 0))
    headed = lambda w: pl.BlockSpec((1, MLA_HEADS, tm, w), lambda b, i: (b, 0, i, 0))
    q = pl.pallas_call(
        functools.partial(_mla_q_kernel, scale=scale),
        grid=(batch, nt),
        in_specs=[lat(0), posspec, full((1, LANES)), full((1, Q_LORA)), full((Q_LORA, MLA_HEADS * QK_CAT))],
        out_specs=headed(QK_CAT),
        out_shape=jax.ShapeDtypeStruct((batch, MLA_HEADS, seq, QK_CAT), BF16),
        compiler_params=_cparams("parallel", "parallel"),
        name="mla_q",
    )(p_mla, pos, inv_tab, q_norm.reshape(1, -1), wq)
    k, v = pl.pallas_call(
        _mla_kv_kernel,
        grid=(batch, nt),
        in_specs=[lat(1), pl.BlockSpec((tm, LANES), lambda b, i: (b * nt + i, 2 * Q_LORA // LANES)),
                  posspec, full((1, LANES)), full((1, KV_LORA)),
                  full((KV_LORA, MLA_HEADS * (QK_NOPE + V_HEAD)))],
        out_specs=[headed(QK_CAT), headed(V_EXT)],
        out_shape=[jax.ShapeDtypeStruct((batch, MLA_HEADS, seq, QK_CAT), BF16),
                   jax.ShapeDtypeStruct((batch, MLA_HEADS, seq, V_EXT), BF16)],
        compiler_params=_cparams("parallel", "parallel"),
        name="mla_kv",
    )(p_mla, p_mla, pos, inv_tab, kv_norm.reshape(1, -1), wkv)
    return q, k, v


def _attn_kernel(qi_ref, kj_ref, q_ref, k_ref, v_ref, o_ref, m_sc, acc_sc):
    t = pl.program_id(2)
    qi = qi_ref[t]
    kj = kj_ref[t]
    n_heads = q_ref.shape[1]
    tk = k_ref.shape[2]

    @pl.when(kj == 0)
    def _():
        m_sc[...] = jnp.full_like(m_sc, NEG_BIG)
        acc_sc[...] = jnp.zeros_like(acc_sc)

    def update(heads, mask):
        s = [_dot_nt(q_ref[0, h], k_ref[0, h]) for h in heads]
        if mask:
            ri = lax.broadcasted_iota(jnp.int32, s[0].shape, 0) // CHUNK
            ci = lax.broadcasted_iota(jnp.int32, s[0].shape, 1) // CHUNK
            s = [jnp.where(ci <= ri, x, NEG_BIG) for x in s]
        for i, h in enumerate(heads):
            m_prev = m_sc[h]
            m_new = jnp.maximum(m_prev, jnp.max(s[i], axis=-1, keepdims=True))
            alpha = jnp.exp2(m_prev - m_new)
            p = jnp.exp2(s[i] - jnp.concatenate([m_new] * (tk // LANES), axis=1))
            pv = _dot(p.astype(BF16), v_ref[0, h])
            acc_sc[h] = jnp.concatenate([alpha] * (V_EXT // LANES), axis=1) * acc_sc[h] + pv
            m_sc[h] = m_new

    pairs = [tuple(range(h, min(h + 2, n_heads))) for h in range(0, n_heads, 2)]

    @pl.when(kj < qi)
    def _():
        for hp in pairs:
            update(hp, False)

    @pl.when(kj == qi)
    def _():
        for hp in pairs:
            update(hp, True)
        for h in range(n_heads):
            acc = acc_sc[h]
            o_ref[:, h * V_HEAD:(h + 1) * V_HEAD] = (acc[:, :V_HEAD] / acc[:, V_HEAD:]).astype(o_ref.dtype)


def _attn_call(q, k, v, batch, seq):
    tq = _tile(seq, 512, CHUNK)
    nq = seq // tq
    hg = ATTN_HEADS_PER_STEP
    pairs = [(i, j) for i in range(nq) for j in range(i + 1)]
    qi = jnp.asarray([p[0] for p in pairs], jnp.int32)
    kj = jnp.asarray([p[1] for p in pairs], jnp.int32)
    grid_spec = pltpu.PrefetchScalarGridSpec(
        num_scalar_prefetch=2,
        grid=(batch, MLA_HEADS // hg, len(pairs)),
        in_specs=[
            pl.BlockSpec((1, hg, tq, QK_CAT), lambda b, h, t, qi, kj: (b, h, qi[t], 0)),
            pl.BlockSpec((1, hg, tq, QK_CAT), lambda b, h, t, qi, kj: (b, h, kj[t], 0)),
            pl.BlockSpec((1, hg, tq, V_EXT), lambda b, h, t, qi, kj: (b, h, kj[t], 0)),
        ],
        out_specs=pl.BlockSpec((tq, hg * V_HEAD), lambda b, h, t, qi, kj: (b * nq + qi[t], h)),
        scratch_shapes=[pltpu.VMEM((hg, tq, LANES), F32), pltpu.VMEM((hg, tq, V_EXT), F32)],
    )
    return pl.pallas_call(
        _attn_kernel,
        grid_spec=grid_spec,
        out_shape=jax.ShapeDtypeStruct((batch * seq, MLA_HEADS * V_HEAD), BF16),
        compiler_params=_cparams("parallel", "parallel", "arbitrary"),
        name="mla_attn",
    )(qi, kj, q, k, v)


def _mix_out_kernel(ya_ref, yb_ref, w_ref, x_ref, g_ref, xo_ref, hn_ref):
    half = ya_ref.shape[1]
    acc = _dot(ya_ref[...], w_ref[0:half, :]) + _dot(yb_ref[...], w_ref[half:, :])
    x = x_ref[...] + acc
    xo_ref[...] = x
    hn_ref[...] = _rms(x, g_ref[...]).astype(hn_ref.dtype)


def _mix_out_call(ya, yb, w, x, g):
    m, d = x.shape
    tm = _tile(m, 512)
    half = ya.shape[1]
    row = pl.BlockSpec((tm, d), lambda i: (i, 0))
    return pl.pallas_call(
        _mix_out_kernel,
        grid=(m // tm,),
        in_specs=[pl.BlockSpec((tm, half), lambda i: (i, 0)), pl.BlockSpec((tm, half), lambda i: (i, 0)),
                  pl.BlockSpec((2 * half, d), lambda i: (0, 0)), row, pl.BlockSpec((1, d), lambda i: (0, 0))],
        out_specs=[row, row],
        out_shape=[jax.ShapeDtypeStruct((m, d), F32), jax.ShapeDtypeStruct((m, d), BF16)],
        compiler_params=_cparams("parallel"),
        name="mix_out",
    )(ya, yb, w, x, g.reshape(1, d))


def _conv_out_kernel(a_ref, w_ref, b_ref, x_ref, g_ref, r_ref, xo_ref, hn_ref, lg_ref):
    x = x_ref[...] + _dot(a_ref[...], w_ref[...]) + b_ref[...]
    xo_ref[...] = x
    hn = _rms(x, g_ref[...])
    tm = x.shape[0]
    for j in range(SLABS):
        hn_ref[pl.ds(j, tm, stride=SLABS), :] = hn[:, j * LANES:(j + 1) * LANES]
    lg_ref[...] = _dot_hp(hn, r_ref[...])


def _conv_out_call(a, w, b, x, g, router_pad):
    m, d = x.shape
    tm = _tile(m, 256)
    row = pl.BlockSpec((tm, d), lambda i: (i, 0))
    full = lambda shape: pl.BlockSpec(shape, lambda i: (0, 0))
    return pl.pallas_call(
        _conv_out_kernel,
        grid=(m // tm,),
        in_specs=[row, full((d, d)), full((1, d)), row, full((1, d)), full((d, LANES))],
        out_specs=[row, pl.BlockSpec((tm * SLABS, LANES), lambda i: (i, 0)),
                   pl.BlockSpec((tm, LANES), lambda i: (i, 0))],
        out_shape=[jax.ShapeDtypeStruct((m, d), F32), jax.ShapeDtypeStruct((m * SLABS, LANES), F32),
                   jax.ShapeDtypeStruct((m, LANES), F32)],
        compiler_params=_cparams("parallel"),
        name="conv_out",
    )(a, w, b.reshape(1, d), x, g.reshape(1, d), router_pad)


def _ffn_kernel(h_ref, w1_ref, w3_ref, w2_ref, x_ref, g_ref, xo_ref, hn_ref):
    f = pl.program_id(1)

    @pl.when(f == 0)
    def _():
        xo_ref[...] = x_ref[...]

    h = h_ref[...]
    gate = (_silu(_dot(h, w1_ref[...])) * _dot(h, w3_ref[...])).astype(BF16)
    xo_ref[...] += _dot(gate, w2_ref[...])

    @pl.when(f == pl.num_programs(1) - 1)
    def _():
        hn_ref[...] = _rms(xo_ref[...], g_ref[...]).astype(hn_ref.dtype)


def _ffn_call(h, w1, w3, w2, x, g):
    m, d = x.shape
    ff = w1.shape[1]
    tm = _tile(m, 512)
    tf = _tile(ff, 512, LANES)
    row = pl.BlockSpec((tm, d), lambda i, f: (i, 0))
    return pl.pallas_call(
        _ffn_kernel,
        grid=(m // tm, ff // tf),
        in_specs=[row, pl.BlockSpec((d, tf), lambda i, f: (0, f)), pl.BlockSpec((d, tf), lambda i, f: (0, f)),
                  pl.BlockSpec((tf, d), lambda i, f: (f, 0)), row, pl.BlockSpec((1, d), lambda i, f: (0, 0))],
        out_specs=[row, row],
        out_shape=[jax.ShapeDtypeStruct((m, d), F32), jax.ShapeDtypeStruct((m, d), BF16)],
        compiler_params=_cparams("parallel", "arbitrary"),
        name="ffn",
    )(h, w1, w3, w2, x, g.reshape(1, d))


def _glu_kernel(h_ref, wa_ref, wb_ref, ba_ref, bb_ref, o_ref):
    h = h_ref[...]
    a = _dot(h, wa_ref[...]) + ba_ref[...]
    b = _dot(h, wb_ref[...]) + bb_ref[...]
    o_ref[...] = a * _sigmoid(b)


def _glu_call(h, w, b):
    m, d = h.shape
    n = w.shape[1] // 2
    tm = _tile(m, 512)
    tn = _tile(n, 512, LANES)
    nj = n // tn
    b2 = b.reshape(1, 2 * n)
    return pl.pallas_call(
        _glu_kernel,
        grid=(m // tm, nj),
        in_specs=[pl.BlockSpec((tm, d), lambda i, j: (i, 0)),
                  pl.BlockSpec((d, tn), lambda i, j: (0, j)), pl.BlockSpec((d, tn), lambda i, j: (0, j + nj)),
                  pl.BlockSpec((1, tn), lambda i, j: (0, j)), pl.BlockSpec((1, tn), lambda i, j: (0, j + nj))],
        out_specs=pl.BlockSpec((tm, tn), lambda i, j: (i, j)),
        out_shape=jax.ShapeDtypeStruct((m, n), F32),
        compiler_params=_cparams("parallel", "arbitrary"),
        name="conv_glu",
    )(h, w, w, b2, b2)


def _dwconv_kernel(u_ref, halo_ref, dw_ref, dwb_ref, lg_ref, lb_ref, o_ref, ext_sc, acc_sc, *, tiles_per_seq):
    ts, d = u_ref.shape
    first = (pl.program_id(0) % tiles_per_seq) == 0
    ext_sc[0, 0:CONV_HALO, :] = jnp.where(first, 0.0, halo_ref[...])
    ext_sc[0, CONV_HALO:, :] = u_ref[...]
    n_shift = ts + CONV_HALO - SUBLANES
    for b in range(1, SUBLANES):
        ext_sc[b, 0:n_shift, :] = ext_sc[0, b:b + n_shift, :]
    rc, cc = 64, 256
    base = CONV_HALO - (CONV_WIDTH - 1)
    for c0 in range(0, d, cc):
        for r0 in range(0, ts, rc):
            acc = jnp.zeros((rc, cc), F32) + dwb_ref[:, c0:c0 + cc]
            for j in range(CONV_WIDTH):
                a8, b = divmod(base + j, SUBLANES)
                rows = slice(r0 + a8 * SUBLANES, r0 + a8 * SUBLANES + rc)
                acc = acc + dw_ref[j:j + 1, c0:c0 + cc] * ext_sc[b, rows, c0:c0 + cc]
            acc_sc[r0:r0 + rc, c0:c0 + cc] = acc
    y = acc_sc[...]
    mu = jnp.mean(y, axis=-1, keepdims=True)
    yc = y - mu
    var = jnp.mean(yc * yc, axis=-1, keepdims=True)
    yn = yc * lax.rsqrt(var + LN_EPS) * lg_ref[...] + lb_ref[...]
    o_ref[...] = _silu(yn).astype(o_ref.dtype)


def _dwconv_call(u, seq, dw_w, dw_b, ln_g, ln_b):
    m, d = u.shape
    ts = _tile(seq, 256, CONV_HALO)
    tiles_per_seq = seq // ts
    dw_pad = jnp.concatenate([dw_w, jnp.zeros((CONV_HALO - CONV_WIDTH, d), F32)], axis=0)
    full = lambda shape: pl.BlockSpec(shape, lambda i: (0, 0))
    return pl.pallas_call(
        functools.partial(_dwconv_kernel, tiles_per_seq=tiles_per_seq),
        grid=(m // ts,),
        in_specs=[pl.BlockSpec((ts, d), lambda i: (i, 0)),
                  pl.BlockSpec((CONV_HALO, d), lambda i: (jnp.maximum(i * (ts // CONV_HALO) - 1, 0), 0)),
                  full((CONV_HALO, d)), full((1, d)), full((1, d)), full((1, d))],
        out_specs=pl.BlockSpec((ts, d), lambda i: (i, 0)),
        out_shape=jax.ShapeDtypeStruct((m, d), BF16),
        scratch_shapes=[pltpu.VMEM((SUBLANES, ts + CONV_HALO, d), F32), pltpu.VMEM((ts, d), F32)],
        compiler_params=_cparams("parallel"),
        name="dwconv",
    )(u, u, dw_pad, dw_b.reshape(1, d), ln_g.reshape(1, d), ln_b.reshape(1, d))


def _route_kernel(lg_ref, tri_ref, o_ref, cnt_ref, carry_sc):
    @pl.when(pl.program_id(0) == 0)
    def _():
        carry_sc[...] = jnp.zeros_like(carry_sc)

    lg = lg_ref[...]
    lane = lax.broadcasted_iota(jnp.int32, lg.shape, 1)
    lg = jnp.where(lane < N_EXPERTS, lg, -jnp.inf)
    m1 = jnp.max(lg, axis=-1, keepdims=True)
    e1 = jnp.min(jnp.where(lg == m1, lane, LANES), axis=-1, keepdims=True)
    lg2 = jnp.where(lane == e1, -jnp.inf, lg)
    m2 = jnp.max(lg2, axis=-1, keepdims=True)
    e2 = jnp.min(jnp.where(lg2 == m2, lane, LANES), axis=-1, keepdims=True)
    ex = jnp.exp(m2 - m1)
    g1 = 1.0 / (1.0 + ex)
    g2 = ex / (1.0 + ex)
    oh1 = (lane == e1).astype(F32)
    oh2 = (lane == e2).astype(F32)
    both = oh1 + oh2
    before = _dot(tri_ref[...], both.astype(BF16)) + carry_sc[...]
    r1 = jnp.sum(before * oh1, axis=-1, keepdims=True)
    r2 = jnp.sum(before * oh2, axis=-1, keepdims=True)
    carry_sc[...] = carry_sc[...] + jnp.sum(both, axis=0, keepdims=True)
    cnt_ref[...] = jnp.broadcast_to(carry_sc[...], cnt_ref.shape)
    out = jnp.where(lane == 0, e1.astype(F32), 0.0)
    out = jnp.where(lane == 1, e2.astype(F32), out)
    out = jnp.where(lane == 2, g1, out)
    out = jnp.where(lane == 3, g2, out)
    out = jnp.where(lane == 4, r1, out)
    out = jnp.where(lane == 5, r2, out)
    o_ref[...] = out


def _route_call(logits):
    m = logits.shape[0]
    tm = _tile(m, 512)
    idx = np.arange(tm)
    tri = jnp.asarray(idx[:, None] > idx[None, :], BF16)
    return pl.pallas_call(
        _route_kernel,
        grid=(m // tm,),
        in_specs=[pl.BlockSpec((tm, LANES), lambda i: (i, 0)), pl.BlockSpec((tm, tm), lambda i: (0, 0))],
        out_specs=[pl.BlockSpec((tm, LANES), lambda i: (i, 0)), pl.BlockSpec((8, LANES), lambda i: (0, 0))],
        out_shape=[jax.ShapeDtypeStruct((m, LANES), F32), jax.ShapeDtypeStruct((8, LANES), F32)],
        scratch_shapes=[pltpu.VMEM((1, LANES), F32)],
        compiler_params=_cparams("arbitrary"),
        name="moe_route",
    )(logits, tri)


def _slab_copy(src_ref, dst_ref, sem, src_slab, dst_row):
    src = src_ref.at[pl.ds(pl.multiple_of(src_slab, SLABS), SLABS)]
    dst = dst_ref.at[pl.ds(pl.multiple_of(dst_row * SLABS, SLABS), SLABS)]
    return pltpu.make_async_copy(src, dst, sem)


def _slab_cols(ref, j, rows):
    return ref[pl.ds(j, rows, stride=SLABS), :]


def _gather_kernel(used_ref, tok_ref, src_ref, o_ref, buf_sc, sem):
    rows = o_ref.shape[0]
    used = used_ref[pl.program_id(0)] > 0

    def start(r, c):
        _slab_copy(src_ref, buf_sc, sem, tok_ref[0, 0, r], r).start()
        return c

    def wait(r, c):
        _slab_copy(src_ref, buf_sc, sem, 0, r).wait()
        return c

    @pl.when(used)
    def _():
        lax.fori_loop(0, rows, start, 0, unroll=8)
        lax.fori_loop(0, rows, wait, 0, unroll=8)
        for j in range(SLABS):
            o_ref[:, j * LANES:(j + 1) * LANES] = _slab_cols(buf_sc, j, rows).astype(o_ref.dtype)

    @pl.when(jnp.logical_not(used))
    def _():
        o_ref[...] = jnp.zeros_like(o_ref)


def _gather_call(src_slabs, slot_slab, block_used):
    cap = slot_slab.shape[0]
    d = SLABS * LANES
    rows = GATHER_ROWS
    grid_spec = pltpu.PrefetchScalarGridSpec(
        num_scalar_prefetch=1,
        grid=(cap // rows,),
        in_specs=[pl.BlockSpec((1, 1, rows), lambda i, u: (i, 0, 0), memory_space=pltpu.SMEM),
                  pl.BlockSpec(memory_space=pl.ANY)],
        out_specs=pl.BlockSpec((rows, d), lambda i, u: (i, 0)),
        scratch_shapes=[pltpu.VMEM((rows * SLABS, LANES), F32), pltpu.SemaphoreType.DMA(())],
    )
    return pl.pallas_call(
        _gather_kernel,
        grid_spec=grid_spec,
        out_shape=jax.ShapeDtypeStruct((cap, d), BF16),
        compiler_params=_cparams("arbitrary"),
        name="moe_gather",
    )(block_used, slot_slab.reshape(cap // rows, 1, rows), src_slabs)


def _moe_kernel(te_ref, tr_ref, x_ref, w1_ref, w3_ref, w2_ref, o_ref, w1_sc, w3_sc, w2_sc):
    i = pl.program_id(0)
    f = pl.program_id(1)
    nrows = tr_ref[i]
    tm = x_ref.shape[0]

    @pl.when(f == 0)
    def _():
        o_ref[...] = jnp.zeros_like(o_ref)

    def cast_weights():
        w1_sc[...] = w1_ref[0].astype(BF16)
        w3_sc[...] = w3_ref[0].astype(BF16)
        w2_sc[...] = w2_ref[0].astype(BF16)

    def swiglu(rows):
        x = x_ref[rows, :]
        gate = (_silu(_dot(x, w1_sc[...])) * _dot(x, w3_sc[...])).astype(BF16)
        o_ref[rows, :] += _dot(gate, w2_sc[...])

    @pl.when(nrows == tm)
    def _():
        cast_weights()
        swiglu(slice(0, tm))

    @pl.when(jnp.logical_and(nrows > 0, nrows < tm))
    def _():
        cast_weights()
        for s in range(tm // MOE_SUB):
            @pl.when(s * MOE_SUB < nrows)
            def _():
                swiglu(slice(s * MOE_SUB, (s + 1) * MOE_SUB))


def _moe_call(xs, tile_expert, tile_rows, w1, w3, w2):
    cap, d = xs.shape
    ff = w1.shape[2]
    tm = MOE_TM
    tf = MOE_TF
    nf = ff // tf

    def fidx(i, f, te, tr):
        return jnp.where(tr[i] > 0, f, nf - 1)

    grid_spec = pltpu.PrefetchScalarGridSpec(
        num_scalar_prefetch=2,
        grid=(cap // tm, nf),
        in_specs=[
            pl.BlockSpec((tm, d), lambda i, f, te, tr: (i, 0)),
            pl.BlockSpec((1, d, tf), lambda i, f, te, tr: (te[i], 0, fidx(i, f, te, tr))),
            pl.BlockSpec((1, d, tf), lambda i, f, te, tr: (te[i], 0, fidx(i, f, te, tr))),
            pl.BlockSpec((1, tf, d), lambda i, f, te, tr: (te[i], fidx(i, f, te, tr), 0)),
        ],
        out_specs=pl.BlockSpec((tm, d), lambda i, f, te, tr: (i, 0)),
        scratch_shapes=[pltpu.VMEM((d, tf), BF16), pltpu.VMEM((d, tf), BF16), pltpu.VMEM((tf, d), BF16)],
    )
    return pl.pallas_call(
        _moe_kernel,
        grid_spec=grid_spec,
        out_shape=jax.ShapeDtypeStruct((cap, d), F32),
        compiler_params=_cparams("parallel", "arbitrary"),
        name="moe_experts",
    )(tile_expert, tile_rows, xs, w1, w3, w2)


def _row_copy(src_ref, dst_ref, sem, src_row, dst_row):
    return pltpu.make_async_copy(src_ref.at[pl.ds(src_row, 1)], dst_ref.at[pl.ds(dst_row, 1)], sem)


def _combine_kernel(pos_ref, y_ref, x_ref, rt_ref, g_ref, o_ref, buf_sc, sem):
    rows = o_ref.shape[0]

    def start(r, c):
        _row_copy(y_ref, buf_sc.at[0], sem, pos_ref[0, 0, r], r).start()
        _row_copy(y_ref, buf_sc.at[1], sem, pos_ref[0, 1, r], r).start()
        return c

    def wait(r, c):
        _row_copy(y_ref, buf_sc.at[0], sem, 0, r).wait()
        _row_copy(y_ref, buf_sc.at[1], sem, 0, r).wait()
        return c

    lax.fori_loop(0, rows, start, 0, unroll=8)
    lax.fori_loop(0, rows, wait, 0, unroll=8)
    rt = rt_ref[...]
    x = x_ref[...] + rt[:, 2:3] * buf_sc[0] + rt[:, 3:4] * buf_sc[1]
    o_ref[...] = _rms(x, g_ref[...])


def _combine_call(pos, yb, x, route, g):
    m, d = x.shape
    rows = GATHER_ROWS if m % GATHER_ROWS == 0 else m
    return pl.pallas_call(
        _combine_kernel,
        grid=(m // rows,),
        in_specs=[pl.BlockSpec((1, 2, rows), lambda i: (i, 0, 0), memory_space=pltpu.SMEM),
                  pl.BlockSpec(memory_space=pl.ANY),
                  pl.BlockSpec((rows, d), lambda i: (i, 0)),
                  pl.BlockSpec((rows, LANES), lambda i: (i, 0)),
                  pl.BlockSpec((1, d), lambda i: (0, 0))],
        out_specs=pl.BlockSpec((rows, d), lambda i: (i, 0)),
        out_shape=jax.ShapeDtypeStruct((m, d), F32),
        scratch_shapes=[pltpu.VMEM((2, rows, d), F32), pltpu.SemaphoreType.DMA(())],
        compiler_params=_cparams("arbitrary"),
        name="moe_combine",
    )(pos.reshape(m // rows, rows, 2).transpose(0, 2, 1), yb, x, route, g.reshape(1, d))


def _rot_cols(w):
    half = QK_ROPE // 2
    return jnp.concatenate([-w[..., half:], w[..., :half]], axis=-1)


def _mixer_weights(w_in, w_up, a_up, w_q_up):
    d = RWKV_DIM
    w_rwkv = w_in[:, :RWKV_IN]
    w_lat = w_in[:, RWKV_IN:RWKV_IN + Q_LORA + KV_LORA]
    w_kr = w_in[:, RWKV_IN + Q_LORA + KV_LORA:]
    w_mla = jnp.concatenate([w_lat, w_kr, _rot_cols(w_kr)], axis=1)
    lora = jnp.zeros((W_LORA + A_LORA, 2 * d), F32)
    lora = lora.at[:W_LORA, :d].set(w_up).at[W_LORA:, d:].set(a_up)
    wq = w_q_up.reshape(Q_LORA, MLA_HEADS, QK_NOPE + QK_ROPE)
    wq_rope = wq[..., QK_NOPE:]
    wq = jnp.concatenate([wq, _rot_cols(wq_rope)], axis=-1).reshape(Q_LORA, MLA_HEADS * QK_CAT)
    return w_rwkv.astype(BF16), w_mla.astype(BF16), lora, wq.astype(BF16)


def kernel(x, positions, l0_mix_norm, l0_w_in, l0_shift_mu, l0_w0, l0_w_up, l0_a0, l0_a_up, l0_g_up, l0_k_k, l0_k_a, l0_r_k, l0_gn_w, l0_gn_b, l0_q_norm, l0_w_q_up, l0_kv_norm, l0_w_kv_up, l0_w_out, l0_ffn_norm, l0_ffn_w1, l0_ffn_w3, l0_ffn_w2, l1_mix_norm, l1_pw1_w, l1_pw1_b, l1_dw_w, l1_dw_b, l1_ln_g, l1_ln_b, l1_pw2_w, l1_pw2_b, l1_ffn_norm, l1_router, l1_exp_w1, l1_exp_w3, l1_exp_w2, final_norm):
    batch, seq, d = x.shape
    m = batch * seq
    x0 = x.reshape(m, d)
    pos = positions.reshape(m, 1)
    inv = ROPE_THETA ** (-jnp.arange(0, QK_ROPE, 2, dtype=F32) / QK_ROPE)
    inv_tab = jnp.tile(inv, LANES // inv.shape[0]).reshape(1, LANES)

    w_rwkv, w_mla, lora_w, wq = _mixer_weights(l0_w_in, l0_w_up, l0_a_up, l0_w_q_up)

    hn0 = _rmsnorm_call(x0, l0_mix_norm)
    p_rwkv = _matmul_call(hn0, w_rwkv, name="in_proj_rwkv")
    p_mla = _matmul_call(hn0, w_mla, name="in_proj_mla")

    prep = _rwkv_prep_call(p_rwkv, seq, l0_shift_mu, lora_w, l0_w0, l0_a0, l0_g_up, l0_k_k, l0_k_a,
                           l0_r_k.reshape(-1))
    at, bt, kt, rt, v, bh, kh, gc, bonus, gate = prep
    ra, o2, pm, qm = _rwkv_intra_call(at, bt, kt, rt, v, bh, kh, gc, seq)
    y_rwkv = _rwkv_seq_call(ra, o2, pm, qm, bonus, gate, l0_gn_w, l0_gn_b, batch, seq)

    q, k, vv = _mla_proj_calls(p_mla, pos, inv_tab, l0_q_norm, wq, l0_kv_norm, l0_w_kv_up.astype(BF16),
                               batch, seq)
    y_mla = _attn_call(q, k, vv, batch, seq)

    x1, hn1 = _mix_out_call(y_rwkv, y_mla, l0_w_out.astype(BF16), x0, l0_ffn_norm)

    x2, hn2 = _ffn_call(hn1, l0_ffn_w1.astype(BF16), l0_ffn_w3.astype(BF16), l0_ffn_w2.astype(BF16),
                        x1, l1_mix_norm)

    u = _glu_call(hn2, l1_pw1_w.astype(BF16), l1_pw1_b)
    sc = _dwconv_call(u, seq, l1_dw_w, l1_dw_b, l1_ln_g, l1_ln_b)
    router_pad = jnp.zeros((d, LANES), F32).at[:, :N_EXPERTS].set(l1_router)
    x3, hn3, logits = _conv_out_call(sc, l1_pw2_w.astype(BF16), l1_pw2_b, x2, l1_ffn_norm, router_pad)

    route, counts = _route_call(logits)
    e = route[:, 0:2].astype(jnp.int32)
    rank = route[:, 4:6].astype(jnp.int32)
    cnt = counts[0, :N_EXPERTS].astype(jnp.int32)
    padded = (cnt + MOE_TM - 1) // MOE_TM * MOE_TM
    pad_end = jnp.cumsum(padded)
    pad_start = pad_end - padded
    pos_slot = pad_start[e] + rank
    cap = (2 * m + N_EXPERTS * MOE_TM + MOE_TM - 1) // MOE_TM * MOE_TM
    tok = jnp.broadcast_to(jnp.arange(m, dtype=jnp.int32)[:, None], (m, 2))
    slot_slab = jnp.zeros((cap,), jnp.int32).at[pos_slot.reshape(-1)].set(tok.reshape(-1) * SLABS)
    tile_start = jnp.arange(cap // MOE_TM, dtype=jnp.int32) * MOE_TM
    tile_expert = jnp.minimum(jnp.sum(tile_start[:, None] >= pad_end[None, :], axis=1), N_EXPERTS - 1)
    used = tile_start < pad_end[-1]
    last_used = jnp.max(jnp.where(used, tile_expert, 0))
    tile_expert = jnp.where(used, tile_expert, last_used).astype(jnp.int32)
    tile_rows = jnp.clip(pad_start[tile_expert] + cnt[tile_expert] - tile_start, 0, MOE_TM)
    tile_rows = jnp.where(used, tile_rows, 0).astype(jnp.int32)

    per_tile = MOE_TM // GATHER_ROWS
    block_off = jnp.tile(jnp.arange(per_tile, dtype=jnp.int32) * GATHER_ROWS, cap // MOE_TM)
    block_used = (jnp.repeat(tile_rows, per_tile) > block_off).astype(jnp.int32)
    xs = _gather_call(hn3, slot_slab, block_used)
    yb = _moe_call(xs, tile_expert, tile_rows, l1_exp_w1, l1_exp_w3, l1_exp_w2)
    out = _combine_call(pos_slot, yb, x3, route, final_norm)
    return out.reshape(batch, seq, d)
```

```python
import functools

import numpy as np
import jax
import jax.numpy as jnp
from jax import lax
from jax.experimental import pallas as pl
from jax.experimental.pallas import tpu as pltpu

F32 = jnp.float32
BF16 = jnp.bfloat16

D_MODEL = 2048
CHUNK = 64
HEAD = 64
N_HEADS = 16
RWKV_DIM = N_HEADS * HEAD
W_LORA, A_LORA, G_LORA = 64, 64, 128
RWKV_IN = 3 * RWKV_DIM + W_LORA + A_LORA + G_LORA
MLA_HEADS = 8
Q_LORA = KV_LORA = 512
QK_NOPE, QK_ROPE, V_HEAD = 128, 64, 128
QK_CAT = 2 * QK_NOPE
V_EXT = 2 * V_HEAD
ROPE_THETA = 10000.0
CONV_WIDTH = 31
CONV_HALO = 32
N_EXPERTS = 8
NORM_EPS = 1e-6
LN_EPS = 1e-5
GN_EPS = HEAD * 1e-5
NEG_BIG = -1e30

LANES = 128
SUBLANES = 8
VMEM_LIMIT = 56 * 1024 * 1024

HEADS_PER_GROUP = 2
GROUP_W = HEADS_PER_GROUP * HEAD
MOE_TM = 1024
MOE_SUB = 512
MOE_TF = 512
EXPERT_CAST_BLOCKS = 512
GATHER_ROWS = 256
SLABS = D_MODEL // LANES
ATTN_HEADS_PER_STEP = 4
LOG2_E = 1.4426950408889634


def _cparams(*sem):
    return pltpu.CompilerParams(dimension_semantics=sem, vmem_limit_bytes=VMEM_LIMIT)


def _tile(n, pref, align=8):
    if n <= pref:
        return n
    t = (pref // align) * align
    while t > align and n % t:
        t -= align
    assert n % t == 0, (n, pref)
    return t


def _dot(a, b):
    return jnp.dot(a, b, preferred_element_type=F32)


def _dot_nt(a, b):
    return lax.dot_general(a, b, (((1,), (1,)), ((), ())), preferred_element_type=F32)


def _split3(x):
    hi = x.astype(BF16)
    r1 = x - hi.astype(F32)
    mid = r1.astype(BF16)
    lo = (r1 - mid.astype(F32)).astype(BF16)
    return hi, mid, lo


def _dot_lx(a, b_exact):
    hi, mid, lo = _split3(a)
    return _dot(hi, b_exact) + (_dot(mid, b_exact) + _dot(lo, b_exact))


def _dot_xr(a_exact, b):
    hi, mid, lo = _split3(b)
    return _dot(a_exact, hi) + (_dot(a_exact, mid) + _dot(a_exact, lo))


def _dot_hp(a, b):
    ah, am, _ = _split3(a)
    bh, bm, _ = _split3(b)
    return _dot(ah, bh) + (_dot(ah, bm) + _dot(am, bh))


def _sigmoid(x):
    return 1.0 / (1.0 + jnp.exp(-x))


def _silu(x):
    return x * _sigmoid(x)


def _softplus(x):
    return jnp.maximum(x, 0.0) + jnp.log(1.0 + jnp.exp(-jnp.abs(x)))


def _rms(x, g):
    return x * lax.rsqrt(jnp.mean(x * x, axis=-1, keepdims=True) + NORM_EPS) * g


def _rmsnorm_kernel(x_ref, g_ref, o_ref):
    o_ref[...] = _rms(x_ref[...], g_ref[...]).astype(o_ref.dtype)


def _rmsnorm_call(x, g):
    m, d = x.shape
    tm = _tile(m, 512)
    return pl.pallas_call(
        _rmsnorm_kernel,
        grid=(m // tm,),
        in_specs=[pl.BlockSpec((tm, d), lambda i: (i, 0)), pl.BlockSpec((1, d), lambda i: (0, 0))],
        out_specs=pl.BlockSpec((tm, d), lambda i: (i, 0)),
        out_shape=jax.ShapeDtypeStruct((m, d), BF16),
        compiler_params=_cparams("parallel"),
        name="rmsnorm",
    )(x, g.reshape(1, d))


def _col_tiles(w, tn):
    k, n = w.shape
    return w.reshape(k, n // tn, tn).transpose(1, 0, 2)


def _matmul_kernel(a_ref, w_ref, o_ref):
    o_ref[...] = _dot(a_ref[...], w_ref[0])


def _matmul_call(a, w, tm_pref=1024, tn_pref=1664, name="matmul"):
    m, k = a.shape
    n = w.shape[1]
    tm = _tile(m, tm_pref)
    tn = _tile(n, tn_pref, LANES)
    return pl.pallas_call(
        _matmul_kernel,
        grid=(m // tm, n // tn),
        in_specs=[pl.BlockSpec((tm, k), lambda i, j: (i, 0)), pl.BlockSpec((1, k, tn), lambda i, j: (j, 0, 0))],
        out_specs=pl.BlockSpec((tm, tn), lambda i, j: (i, j)),
        out_shape=jax.ShapeDtypeStruct((m, n), F32),
        compiler_params=_cparams("parallel", "arbitrary"),
        name=name,
    )(a, _col_tiles(w, tn))


def _rwkv_prep_kernel(p_ref, prev_ref, mu_ref, lw_ref, w0_ref, a0_ref, gup_ref, kk_ref, ka_ref, rk_ref,
                      tril_ref, ones_ref, seg_ref, segt_ref,
                      at_ref, bt_ref, kt_ref, rt_ref, v_ref, bh_ref, kh_ref, gc_ref, bonus_ref, g_ref,
                      *, tiles_per_seq):
    ts = p_ref.shape[0]
    first = (pl.program_id(0) % tiles_per_seq) == 0
    row0 = lax.broadcasted_iota(jnp.int32, (ts, 1), 0) == 0

    def mixed(c0, c1):
        pc = p_ref[:, c0:c1]
        prev = jnp.where(first, 0.0, prev_ref[7:8, c0:c1])
        sh = jnp.where(row0, prev, pltpu.roll(pc, 1, axis=0))
        return pc + (sh - pc) * mu_ref[:, c0:c1]

    d = RWKV_DIM
    r = mixed(0, d)
    k = mixed(d, 2 * d)
    v = mixed(2 * d, 3 * d)
    xwa = mixed(3 * d, 3 * d + W_LORA + A_LORA)
    xg = mixed(3 * d + W_LORA + A_LORA, RWKV_IN)

    lane = lax.broadcasted_iota(jnp.int32, xwa.shape, 1)
    z = jnp.where(lane < W_LORA, jnp.tanh(xwa), xwa)
    wa = _dot_hp(z, lw_ref[...])
    w = -_softplus(-(w0_ref[...] + wa[:, :d])) - 0.5
    logdecay = -jnp.exp(w)
    a = _sigmoid(a0_ref[...] + wa[:, d:])
    g_ref[...] = _dot_hp(_sigmoid(xg), gup_ref[...])

    seg = seg_ref[...]
    segt = segt_ref[...]

    def head_sum(x):
        return _dot_lx(_dot_lx(x, seg), segt)

    kkr = k * kk_ref[...]
    kk = kkr / jnp.maximum(jnp.sqrt(head_sum(kkr * kkr)), 1e-12)
    k2 = k * (1.0 + (a - 1.0) * ka_ref[...])
    bonus_ref[...] = head_sum(r * k2 * rk_ref[...]) * v

    gcum = _dot_xr(tril_ref[...], logdecay)
    gtot = _dot_xr(ones_ref[...], logdecay)
    g_in = jnp.exp(gcum)
    g_ex = jnp.exp(gcum - logdecay)
    g_inv = jnp.exp(-gcum)
    g_rest = jnp.exp(gtot - gcum)
    beta = kk * a
    at_ref[...] = (-kk * g_ex).astype(at_ref.dtype)
    bt_ref[...] = (beta * g_inv).astype(bt_ref.dtype)
    kt_ref[...] = (k2 * g_inv).astype(kt_ref.dtype)
    rt_ref[...] = (r * g_in).astype(rt_ref.dtype)
    v_ref[...] = v.astype(v_ref.dtype)
    bh_ref[...] = beta * g_rest
    kh_ref[...] = k2 * g_rest
    gc_ref[...] = jnp.exp(gtot)


def _rwkv_prep_call(p, seq, mu, lora_w, w0, a0, g_up, k_k, k_a, r_k):
    m = p.shape[0]
    ts = _tile(seq, 256, CHUNK)
    tiles_per_seq = seq // ts
    d = RWKV_DIM
    idx = np.arange(ts)
    same = (idx[:, None] // CHUNK) == (idx[None, :] // CHUNK)
    tril = jnp.asarray(same & (idx[:, None] >= idx[None, :]), BF16)
    ones = jnp.asarray(same, BF16)
    lane = np.arange(d)
    seg_np = (lane[:, None] // HEAD) == np.arange(LANES)[None, :]
    seg = jnp.asarray(seg_np, BF16)
    segt = jnp.asarray(seg_np.T, BF16)

    def row(x):
        return x.reshape(1, -1).astype(F32)

    full = lambda shape: pl.BlockSpec(shape, lambda i: (0, 0))
    tok = pl.BlockSpec((ts, d), lambda i: (i, 0))
    outs = pl.pallas_call(
        functools.partial(_rwkv_prep_kernel, tiles_per_seq=tiles_per_seq),
        grid=(m // ts,),
        in_specs=[
            pl.BlockSpec((ts, RWKV_IN), lambda i: (i, 0)),
            pl.BlockSpec((8, RWKV_IN), lambda i: (jnp.maximum(i * (ts // 8) - 1, 0), 0)),
            full((1, RWKV_IN)), full((W_LORA + A_LORA, 2 * d)), full((1, d)), full((1, d)),
            full((G_LORA, d)), full((1, d)), full((1, d)), full((1, d)),
            full((ts, ts)), full((ts, ts)), full((d, LANES)), full((LANES, d)),
        ],
        out_specs=[tok] * 10,
        out_shape=[jax.ShapeDtypeStruct((m, d), BF16)] * 5 + [jax.ShapeDtypeStruct((m, d), F32)] * 5,
        compiler_params=_cparams("parallel"),
        name="rwkv_prep",
    )(p, p, row(mu), lora_w, row(w0), row(a0), g_up, row(k_k), row(k_a), row(r_k), tril, ones, seg, segt)
    return outs


def _stack_heads(x):
    lane_head = lax.broadcasted_iota(jnp.int32, x.shape, 1) // HEAD
    return jnp.concatenate([jnp.where(lane_head == h, x, 0.0) for h in range(HEADS_PER_GROUP)], axis=0)


def _unstack_heads(x):
    out = x[0:CHUNK]
    for h in range(1, HEADS_PER_GROUP):
        out = out + x[h * CHUNK:(h + 1) * CHUNK]
    return out


def _rwkv_intra_kernel(at_ref, bt_ref, kt_ref, rt_ref, v_ref, bh_ref, kh_ref, gc_ref,
                       ra_ref, o2_ref, p_ref, q_ref, *, mm):
    w = GROUP_W
    n_chunks = at_ref.shape[0] // CHUNK
    ri = lax.broadcasted_iota(jnp.int32, (w, w), 0)
    ci = lax.broadcasted_iota(jnp.int32, (w, w), 1)
    strict = (ri % CHUNK) > (ci % CHUNK)
    incl = (ri % CHUNK) >= (ci % CHUNK)
    blk16 = (ri // 16) == (ci // 16)
    eye = (ri == ci).astype(F32)

    chunks = [slice(c * CHUNK, (c + 1) * CHUNK) for c in range(n_chunks)]

    def each(f, *lists):
        return [f(*args) for args in zip(*lists)]

    def stacked(ref):
        return [_stack_heads(ref[rows, :]) for rows in chunks]

    a_s, r_s, b_s, k_s, v_s = stacked(at_ref), stacked(rt_ref), stacked(bt_ref), stacked(kt_ref), stacked(v_ref)
    s = each(lambda a, r, b, k: mm(jnp.concatenate([a, r], axis=0), jnp.concatenate([b, k], axis=0), nt=True),
             a_s, r_s, b_s, k_s)
    l_ab = [jnp.where(strict, x[:w, :w], 0.0) for x in s]
    l_ak = [jnp.where(strict, x[:w, w:], 0.0) for x in s]
    m_rb = [jnp.where(incl, x[w:, :w], 0.0) for x in s]
    m_rk = [jnp.where(incl, x[w:, w:], 0.0) for x in s]

    dg = [jnp.where(blk16, x, 0.0) for x in l_ab]
    off = each(lambda x, y: x - y, l_ab, dg)
    td = [eye + x for x in dg]
    pw = dg
    for _ in range(3):
        pw = each(mm, pw, pw)
        td = each(lambda x, y: x + mm(x, y), td, pw)
    n1 = each(mm, td, off)
    n2 = each(mm, n1, n1)
    t1 = each(lambda x, y: x + mm(y, x), td, n2)
    t = each(lambda x, y: x + mm(y, x), t1, n1)

    lv = each(mm, l_ak, v_s)
    y = each(lambda tt, x, a: mm(tt, jnp.concatenate([x, a], axis=1)), t, lv, a_s)
    z = each(mm, m_rb, y)
    mv = each(mm, m_rk, v_s)
    z2 = each(lambda ref_rows, yy: mm(_stack_heads(bh_ref[ref_rows, :]).T, yy), chunks, y)
    kv = each(lambda ref_rows, vv: mm(_stack_heads(kh_ref[ref_rows, :]).T, vv), chunks, v_s)
    for c, rows in enumerate(chunks):
        ra_ref[rows, :] = _unstack_heads(r_s[c] + z[c][:, w:]).astype(ra_ref.dtype)
        o2_ref[rows, :] = _unstack_heads(z[c][:, :w] + mv[c])
        p_ref[rows, :] = _unstack_heads(eye * gc_ref[c * CHUNK:c * CHUNK + 1, :]
                                        + z2[c][:, w:]).astype(p_ref.dtype)
        q_ref[rows, :] = _unstack_heads(z2[c][:, :w] + kv[c])


def _mm_bf16(a, b, nt=False):
    a = a.astype(BF16)
    b = b.astype(BF16)
    return _dot_nt(a, b) if nt else _dot(a, b)


def _mm_hp(a, b, nt=False):
    ah, am, _ = _split3(a)
    bh, bm, _ = _split3(b)
    f = _dot_nt if nt else _dot
    return f(ah, bh) + (f(ah, bm) + f(am, bh))


def _rwkv_intra_call(at, bt, kt, rt, v, bh, kh, gc, seq):
    m, d = at.shape
    ts = _tile(seq, 512, CHUNK)
    spec = pl.BlockSpec((ts, GROUP_W), lambda i, j: (i, j))
    return pl.pallas_call(
        functools.partial(_rwkv_intra_kernel, mm=_mm_bf16),
        grid=(m // ts, d // GROUP_W),
        in_specs=[spec] * 8,
        out_specs=[spec] * 4,
        out_shape=[jax.ShapeDtypeStruct((m, d), dt) for dt in (BF16, F32, BF16, F32)],
        compiler_params=_cparams("parallel", "parallel"),
        name="rwkv_intra",
    )(at, bt, kt, rt, v, bh, kh, gc)


def _rwkv_seq_kernel(ra_ref, o2_ref, p_ref, q_ref, bonus_ref, g_ref, gnw_ref, gnb_ref, seg_ref, segt_ref,
                     o_ref, h_sc, y_sc, *, mm):
    w = GROUP_W
    n_groups = RWKV_DIM // w
    n_chunks = ra_ref.shape[0] // CHUNK

    @pl.when(pl.program_id(1) == 0)
    def _():
        h_sc[...] = jnp.zeros_like(h_sc)

    def chunk(c, carry):
        r0 = pl.multiple_of(c * CHUNK, CHUNK)
        rows = pl.ds(r0, CHUNK)
        for gi in range(n_groups):
            cols = slice(gi * w, (gi + 1) * w)
            lhs = jnp.concatenate([ra_ref[rows, cols], _stack_heads(p_ref[rows, cols])], axis=0)
            res = mm(lhs, h_sc[gi])
            y_sc[rows, cols] = res[:CHUNK] + o2_ref[rows, cols]
            h_sc[gi] = res[CHUNK:] + _stack_heads(q_ref[rows, cols])
        return carry

    lax.fori_loop(0, n_chunks, chunk, 0)

    y = y_sc[...]
    seg = seg_ref[...]
    segt = segt_ref[...]
    mean = _dot_lx(_dot_lx(y, seg), segt) * (1.0 / HEAD)
    yc = y - mean
    var = _dot_lx(_dot_lx(yc * yc, seg), segt) * (1.0 / HEAD)
    yn = yc * lax.rsqrt(var + GN_EPS) * gnw_ref[...] + gnb_ref[...]
    o_ref[...] = ((yn + bonus_ref[...]) * g_ref[...]).astype(o_ref.dtype)


def _rwkv_seq_call(ra, o2, p, q, bonus, g, gn_w, gn_b, batch, seq):
    m, d = ra.shape
    ts = _tile(seq, 512, CHUNK)
    nt = seq // ts
    lane = np.arange(d)
    seg_np = (lane[:, None] // HEAD) == np.arange(LANES)[None, :]
    seg = jnp.asarray(seg_np, BF16)
    segt = jnp.asarray(seg_np.T, BF16)
    tok = pl.BlockSpec((ts, d), lambda b, i: (b * nt + i, 0))
    full = lambda shape: pl.BlockSpec(shape, lambda b, i: (0, 0))
    return pl.pallas_call(
        functools.partial(_rwkv_seq_kernel, mm=_mm_bf16),
        grid=(batch, nt),
        in_specs=[tok] * 6 + [full((1, d)), full((1, d)), full((d, LANES)), full((LANES, d))],
        out_specs=tok,
        out_shape=jax.ShapeDtypeStruct((m, d), BF16),
        scratch_shapes=[pltpu.VMEM((d // GROUP_W, GROUP_W, GROUP_W), F32), pltpu.VMEM((ts, d), F32)],
        compiler_params=_cparams("arbitrary", "arbitrary"),
        name="rwkv_seq",
    )(ra, o2, p, q, bonus, g, gn_w.reshape(1, d), gn_b.reshape(1, d), seg, segt)


def _rope_tab(pos_ref, inv_ref):
    ang = pos_ref[...].astype(F32) * inv_ref[...]
    lane = lax.broadcasted_iota(jnp.int32, ang.shape, 1)
    return jnp.where(lane < QK_ROPE, jnp.cos(ang), jnp.sin(ang))


def _mla_q_kernel(lat_ref, pos_ref, inv_ref, g_ref, w_ref, o_ref, *, scale):
    hn = _rms(lat_ref[...], g_ref[...]).astype(BF16)
    q = _dot(hn, w_ref[...]) * scale
    tab = _rope_tab(pos_ref, inv_ref)
    for h in range(MLA_HEADS):
        c0 = h * QK_CAT
        o_ref[0, h, :, 0:QK_NOPE] = q[:, c0:c0 + QK_NOPE].astype(BF16)
        o_ref[0, h, :, QK_NOPE:QK_CAT] = (q[:, c0 + QK_NOPE:c0 + QK_CAT] * tab).astype(BF16)


def _mla_kv_kernel(lat_ref, kr_ref, pos_ref, inv_ref, g_ref, w_ref, k_ref, v_ref):
    hn = _rms(lat_ref[...], g_ref[...]).astype(BF16)
    kv = _dot(hn, w_ref[...])
    t = kr_ref[...] * _rope_tab(pos_ref, inv_ref)
    k_rope = (t + pltpu.roll(t, QK_ROPE, axis=1)).astype(BF16)
    for h in range(MLA_HEADS):
        c0 = h * (QK_NOPE + V_HEAD)
        k_ref[0, h, :, 0:QK_NOPE] = kv[:, c0:c0 + QK_NOPE].astype(BF16)
        k_ref[0, h, :, QK_NOPE:QK_CAT] = k_rope
        v_ref[0, h, :, 0:V_HEAD] = kv[:, c0 + QK_NOPE:c0 + QK_NOPE + V_HEAD].astype(BF16)
        v_ref[0, h, :, V_HEAD:V_EXT] = jnp.ones((kv.shape[0], V_EXT - V_HEAD), BF16)


def _mla_proj_calls(p_mla, pos, inv_tab, q_norm, wq, kv_norm, wkv, batch, seq):
    tm = _tile(seq, 512)
    nt = seq // tm
    scale = float((QK_NOPE + QK_ROPE) ** -0.5) * LOG2_E
    lat = lambda c: pl.BlockSpec((tm, Q_LORA), lambda b, i: (b * nt + i, c))
    posspec = pl.BlockSpec((tm, 1), lambda b, i: (b * nt + i, 0))
    full = lambda shape: pl.BlockSpec(shape, lambda b, i: (0, 0))
    headed = lambda w: pl.BlockSpec((1, MLA_HEADS, tm, w), lambda b, i: (b, 0, i, 0))
    q = pl.pallas_call(
        functools.partial(_mla_q_kernel, scale=scale),
        grid=(batch, nt),
        in_specs=[lat(0), posspec, full((1, LANES)), full((1, Q_LORA)), full((Q_LORA, MLA_HEADS * QK_CAT))],
        out_specs=headed(QK_CAT),
        out_shape=jax.ShapeDtypeStruct((batch, MLA_HEADS, seq, QK_CAT), BF16),
        compiler_params=_cparams("parallel", "parallel"),
        name="mla_q",
    )(p_mla, pos, inv_tab, q_norm.reshape(1, -1), wq)
    k, v = pl.pallas_call(
        _mla_kv_kernel,
        grid=(batch, nt),
        in_specs=[lat(1), pl.BlockSpec((tm, LANES), lambda b, i: (b * nt + i, 2 * Q_LORA // LANES)),
                  posspec, full((1, LANES)), full((1, KV_LORA)),
                  full((KV_LORA, MLA_HEADS * (QK_NOPE + V_HEAD)))],
        out_specs=[headed(QK_CAT), headed(V_EXT)],
        out_shape=[jax.ShapeDtypeStruct((batch, MLA_HEADS, seq, QK_CAT), BF16),
                   jax.ShapeDtypeStruct((batch, MLA_HEADS, seq, V_EXT), BF16)],
        compiler_params=_cparams("parallel", "parallel"),
        name="mla_kv",
    )(p_mla, p_mla, pos, inv_tab, kv_norm.reshape(1, -1), wkv)
    return q, k, v


def _cast_expert_blocks(w1_ref, w3_ref, w2_ref, w1o_ref, w3o_ref, w2o_ref):
    for f in range(w1o_ref.shape[1]):
        cols = slice(f * MOE_TF, (f + 1) * MOE_TF)
        w1o_ref[0, f] = w1_ref[:, cols].astype(BF16)
        w3o_ref[0, f] = w3_ref[:, cols].astype(BF16)
    w2o_ref[...] = w2_ref[...].astype(BF16)


def _attn_cast_kernel(qi_ref, kj_ref, q_ref, k_ref, v_ref, w1_ref, w3_ref, w2_ref,
                      o_ref, w1o_ref, w3o_ref, w2o_ref, m_sc, acc_sc):
    _cast_expert_blocks(w1_ref, w3_ref, w2_ref, w1o_ref, w3o_ref, w2o_ref)
    _attn_kernel(qi_ref, kj_ref, q_ref, k_ref, v_ref, o_ref, m_sc, acc_sc)


def _attn_kernel(qi_ref, kj_ref, q_ref, k_ref, v_ref, o_ref, m_sc, acc_sc):
    t = pl.program_id(2)
    qi = qi_ref[t]
    kj = kj_ref[t]
    n_heads = q_ref.shape[1]
    tk = k_ref.shape[2]

    @pl.when(kj == 0)
    def _():
        m_sc[...] = jnp.full_like(m_sc, NEG_BIG)
        acc_sc[...] = jnp.zeros_like(acc_sc)

    def update(heads, mask):
        s = [_dot_nt(q_ref[0, h], k_ref[0, h]) for h in heads]
        if mask:
            ri = lax.broadcasted_iota(jnp.int32, s[0].shape, 0) // CHUNK
            ci = lax.broadcasted_iota(jnp.int32, s[0].shape, 1) // CHUNK
            s = [jnp.where(ci <= ri, x, NEG_BIG) for x in s]
        for i, h in enumerate(heads):
            m_prev = m_sc[h]
            m_new = jnp.maximum(m_prev, jnp.max(s[i], axis=-1, keepdims=True))
            alpha = jnp.exp2(m_prev - m_new)
            p = jnp.exp2(s[i] - jnp.concatenate([m_new] * (tk // LANES), axis=1))
            pv = _dot(p.astype(BF16), v_ref[0, h])
            acc_sc[h] = jnp.concatenate([alpha] * (V_EXT // LANES), axis=1) * acc_sc[h] + pv
            m_sc[h] = m_new

    pairs = [tuple(range(h, min(h + 2, n_heads))) for h in range(0, n_heads, 2)]

    @pl.when(kj < qi)
    def _():
        for hp in pairs:
            update(hp, False)

    @pl.when(kj == qi)
    def _():
        for hp in pairs:
            update(hp, True)
        for h in range(n_heads):
            acc = acc_sc[h]
            o_ref[:, h * V_HEAD:(h + 1) * V_HEAD] = (acc[:, :V_HEAD] / acc[:, V_HEAD:]).astype(o_ref.dtype)


def _expert_cast_specs(w1, w3, w2, block_of):
    n_exp, d, ff = w1.shape
    nf = ff // MOE_TF
    rows13 = n_exp * d // EXPERT_CAST_BLOCKS
    rows2 = n_exp * ff // EXPERT_CAST_BLOCKS
    per_exp = d // rows13
    in_specs = [
        pl.BlockSpec((rows13, ff), lambda *a: (block_of(*a), 0)),
        pl.BlockSpec((rows13, ff), lambda *a: (block_of(*a), 0)),
        pl.BlockSpec((rows2, d), lambda *a: (block_of(*a), 0)),
    ]
    tiled = pl.BlockSpec((1, nf, rows13, MOE_TF), lambda *a: (block_of(*a) // per_exp, 0, block_of(*a) % per_exp, 0))
    out_specs = [tiled, tiled, pl.BlockSpec((rows2, d), lambda *a: (block_of(*a), 0))]
    out_shape = [jax.ShapeDtypeStruct((n_exp, nf, d, MOE_TF), BF16)] * 2 + [jax.ShapeDtypeStruct((n_exp * ff, d), BF16)]
    args = (w1.reshape(n_exp * d, ff), w3.reshape(n_exp * d, ff), w2.reshape(n_exp * ff, d))
    return in_specs, out_specs, out_shape, args


def _expert_cast_call(w1, w3, w2):
    in_specs, out_specs, out_shape, args = _expert_cast_specs(w1, w3, w2, lambda i: i)
    return pl.pallas_call(
        _cast_expert_blocks,
        grid=(EXPERT_CAST_BLOCKS,),
        in_specs=in_specs,
        out_specs=out_specs,
        out_shape=out_shape,
        compiler_params=_cparams("parallel"),
        name="expert_cast",
    )(*args)


def _attn_call(q, k, v, batch, seq, expert_weights):
    tq = _tile(seq, 512, CHUNK)
    nq = seq // tq
    hg = ATTN_HEADS_PER_STEP
    n_hg = MLA_HEADS // hg
    pairs = [(i, j) for i in range(nq) for j in range(i + 1)]
    qi = jnp.asarray([p[0] for p in pairs], jnp.int32)
    kj = jnp.asarray([p[1] for p in pairs], jnp.int32)
    in_specs = [
        pl.BlockSpec((1, hg, tq, QK_CAT), lambda b, h, t, qi, kj: (b, h, qi[t], 0)),
        pl.BlockSpec((1, hg, tq, QK_CAT), lambda b, h, t, qi, kj: (b, h, kj[t], 0)),
        pl.BlockSpec((1, hg, tq, V_EXT), lambda b, h, t, qi, kj: (b, h, kj[t], 0)),
    ]
    out_specs = [pl.BlockSpec((tq, hg * V_HEAD), lambda b, h, t, qi, kj: (b * nq + qi[t], h))]
    out_shape = [jax.ShapeDtypeStruct((batch * seq, MLA_HEADS * V_HEAD), BF16)]
    args = (qi, kj, q, k, v)
    fused = batch * n_hg * len(pairs) >= EXPERT_CAST_BLOCKS
    if fused:
        def block_of(b, h, t, qi, kj):
            return jnp.minimum((b * n_hg + h) * len(pairs) + t, EXPERT_CAST_BLOCKS - 1)

        c_in, c_out, c_shape, c_args = _expert_cast_specs(*expert_weights, block_of)
        in_specs, out_specs, out_shape, args = in_specs + c_in, out_specs + c_out, out_shape + c_shape, args + c_args
    grid_spec = pltpu.PrefetchScalarGridSpec(
        num_scalar_prefetch=2,
        grid=(batch, n_hg, len(pairs)),
        in_specs=in_specs,
        out_specs=out_specs,
        scratch_shapes=[pltpu.VMEM((hg, tq, LANES), F32), pltpu.VMEM((hg, tq, V_EXT), F32)],
    )
    outs = pl.pallas_call(
        _attn_cast_kernel if fused else _attn_kernel,
        grid_spec=grid_spec,
        out_shape=out_shape,
        compiler_params=_cparams("arbitrary", "arbitrary", "arbitrary"),
        name="mla_attn",
    )(*args)
    if fused:
        return outs[0], tuple(outs[1:])
    return outs[0], tuple(_expert_cast_call(*expert_weights))


def _mix_out_kernel(ya_ref, yb_ref, w_ref, x_ref, g_ref, xo_ref, hn_ref):
    half = ya_ref.shape[1]
    acc = _dot(ya_ref[...], w_ref[0:half, :]) + _dot(yb_ref[...], w_ref[half:, :])
    x = x_ref[...] + acc
    xo_ref[...] = x
    hn_ref[...] = _rms(x, g_ref[...]).astype(hn_ref.dtype)


def _mix_out_call(ya, yb, w, x, g):
    m, d = x.shape
    tm = _tile(m, 512)
    half = ya.shape[1]
    row = pl.BlockSpec((tm, d), lambda i: (i, 0))
    return pl.pallas_call(
        _mix_out_kernel,
        grid=(m // tm,),
        in_specs=[pl.BlockSpec((tm, half), lambda i: (i, 0)), pl.BlockSpec((tm, half), lambda i: (i, 0)),
                  pl.BlockSpec((2 * half, d), lambda i: (0, 0)), row, pl.BlockSpec((1, d), lambda i: (0, 0))],
        out_specs=[row, row],
        out_shape=[jax.ShapeDtypeStruct((m, d), F32), jax.ShapeDtypeStruct((m, d), BF16)],
        compiler_params=_cparams("parallel"),
        name="mix_out",
    )(ya, yb, w, x, g.reshape(1, d))


def _conv_out_kernel(a_ref, w_ref, b_ref, x_ref, g_ref, r_ref, xo_ref, hn_ref, lg_ref):
    x = x_ref[...] + _dot(a_ref[...], w_ref[...]) + b_ref[...]
    xo_ref[...] = x
    hn = _rms(x, g_ref[...])
    tm = x.shape[0]
    for j in range(SLABS):
        hn_ref[pl.ds(j, tm, stride=SLABS), :] = hn[:, j * LANES:(j + 1) * LANES]
    lg_ref[...] = _dot_hp(hn, r_ref[...])


def _conv_out_call(a, w, b, x, g, router_pad):
    m, d = x.shape
    tm = _tile(m, 256)
    row = pl.BlockSpec((tm, d), lambda i: (i, 0))
    full = lambda shape: pl.BlockSpec(shape, lambda i: (0, 0))
    return pl.pallas_call(
        _conv_out_kernel,
        grid=(m // tm,),
        in_specs=[row, full((d, d)), full((1, d)), row, full((1, d)), full((d, LANES))],
        out_specs=[row, pl.BlockSpec((tm * SLABS, LANES), lambda i: (i, 0)),
                   pl.BlockSpec((tm, LANES), lambda i: (i, 0))],
        out_shape=[jax.ShapeDtypeStruct((m, d), F32), jax.ShapeDtypeStruct((m * SLABS, LANES), F32),
                   jax.ShapeDtypeStruct((m, LANES), F32)],
        compiler_params=_cparams("parallel"),
        name="conv_out",
    )(a, w, b.reshape(1, d), x, g.reshape(1, d), router_pad)


def _ffn_kernel(h_ref, w1_ref, w3_ref, w2_ref, x_ref, g_ref, xo_ref, hn_ref):
    f = pl.program_id(1)

    @pl.when(f == 0)
    def _():
        xo_ref[...] = x_ref[...]

    h = h_ref[...]
    gate = (_silu(_dot(h, w1_ref[0])) * _dot(h, w3_ref[0])).astype(BF16)
    xo_ref[...] += _dot(gate, w2_ref[...])

    @pl.when(f == pl.num_programs(1) - 1)
    def _():
        hn_ref[...] = _rms(xo_ref[...], g_ref[...]).astype(hn_ref.dtype)


def _ffn_call(h, w1, w3, w2, x, g):
    m, d = x.shape
    ff = w1.shape[1]
    tm = _tile(m, 512)
    tf = _tile(ff, 512, LANES)
    row = pl.BlockSpec((tm, d), lambda i, f: (i, 0))
    wtile = pl.BlockSpec((1, d, tf), lambda i, f: (f, 0, 0))
    w1, w3 = _col_tiles(w1, tf), _col_tiles(w3, tf)
    return pl.pallas_call(
        _ffn_kernel,
        grid=(m // tm, ff // tf),
        in_specs=[row, wtile, wtile,
                  pl.BlockSpec((tf, d), lambda i, f: (f, 0)), row, pl.BlockSpec((1, d), lambda i, f: (0, 0))],
        out_specs=[row, row],
        out_shape=[jax.ShapeDtypeStruct((m, d), F32), jax.ShapeDtypeStruct((m, d), BF16)],
        compiler_params=_cparams("parallel", "arbitrary"),
        name="ffn",
    )(h, w1, w3, w2, x, g.reshape(1, d))


def _glu_kernel(h_ref, wa_ref, wb_ref, ba_ref, bb_ref, o_ref):
    h = h_ref[...]
    a = _dot(h, wa_ref[0]) + ba_ref[...]
    b = _dot(h, wb_ref[0]) + bb_ref[...]
    o_ref[...] = a * _sigmoid(b)


def _glu_call(h, w, b):
    m, d = h.shape
    n = w.shape[1] // 2
    tm = _tile(m, 1024)
    tn = _tile(n, 512, LANES)
    nj = n // tn
    b2 = b.reshape(1, 2 * n)
    wt = _col_tiles(w, tn)
    return pl.pallas_call(
        _glu_kernel,
        grid=(m // tm, nj),
        in_specs=[pl.BlockSpec((tm, d), lambda i, j: (i, 0)),
                  pl.BlockSpec((1, d, tn), lambda i, j: (j, 0, 0)), pl.BlockSpec((1, d, tn), lambda i, j: (j + nj, 0, 0)),
                  pl.BlockSpec((1, tn), lambda i, j: (0, j)), pl.BlockSpec((1, tn), lambda i, j: (0, j + nj))],
        out_specs=pl.BlockSpec((tm, tn), lambda i, j: (i, j)),
        out_shape=jax.ShapeDtypeStruct((m, n), F32),
        compiler_params=_cparams("parallel", "arbitrary"),
        name="conv_glu",
    )(h, wt, wt, b2, b2)


def _dwconv_kernel(u_ref, halo_ref, dw_ref, dwb_ref, lg_ref, lb_ref, o_ref, ext_sc, acc_sc, *, tiles_per_seq):
    ts, d = u_ref.shape
    first = (pl.program_id(0) % tiles_per_seq) == 0
    ext_sc[0, 0:CONV_HALO, :] = jnp.where(first, 0.0, halo_ref[...])
    ext_sc[0, CONV_HALO:, :] = u_ref[...]
    n_shift = ts + CONV_HALO - SUBLANES
    for b in range(1, SUBLANES):
        ext_sc[b, 0:n_shift, :] = ext_sc[0, b:b + n_shift, :]
    rc, cc = 64, 256
    base = CONV_HALO - (CONV_WIDTH - 1)
    for c0 in range(0, d, cc):
        for r0 in range(0, ts, rc):
            acc = jnp.zeros((rc, cc), F32) + dwb_ref[:, c0:c0 + cc]
            for j in range(CONV_WIDTH):
                a8, b = divmod(base + j, SUBLANES)
                rows = slice(r0 + a8 * SUBLANES, r0 + a8 * SUBLANES + rc)
                acc = acc + dw_ref[j:j + 1, c0:c0 + cc] * ext_sc[b, rows, c0:c0 + cc]
            acc_sc[r0:r0 + rc, c0:c0 + cc] = acc
    y = acc_sc[...]
    mu = jnp.mean(y, axis=-1, keepdims=True)
    yc = y - mu
    var = jnp.mean(yc * yc, axis=-1, keepdims=True)
    yn = yc * lax.rsqrt(var + LN_EPS) * lg_ref[...] + lb_ref[...]
    o_ref[...] = _silu(yn).astype(o_ref.dtype)


def _dwconv_call(u, seq, dw_w, dw_b, ln_g, ln_b):
    m, d = u.shape
    ts = _tile(seq, 256, CONV_HALO)
    tiles_per_seq = seq // ts
    dw_pad = jnp.concatenate([dw_w, jnp.zeros((CONV_HALO - CONV_WIDTH, d), F32)], axis=0)
    full = lambda shape: pl.BlockSpec(shape, lambda i: (0, 0))
    return pl.pallas_call(
        functools.partial(_dwconv_kernel, tiles_per_seq=tiles_per_seq),
        grid=(m // ts,),
        in_specs=[pl.BlockSpec((ts, d), lambda i: (i, 0)),
                  pl.BlockSpec((CONV_HALO, d), lambda i: (jnp.maximum(i * (ts // CONV_HALO) - 1, 0), 0)),
                  full((CONV_HALO, d)), full((1, d)), full((1, d)), full((1, d))],
        out_specs=pl.BlockSpec((ts, d), lambda i: (i, 0)),
        out_shape=jax.ShapeDtypeStruct((m, d), BF16),
        scratch_shapes=[pltpu.VMEM((SUBLANES, ts + CONV_HALO, d), F32), pltpu.VMEM((ts, d), F32)],
        compiler_params=_cparams("parallel"),
        name="dwconv",
    )(u, u, dw_pad, dw_b.reshape(1, d), ln_g.reshape(1, d), ln_b.reshape(1, d))


def _route_kernel(lg_ref, tri_ref, o_ref, cnt_ref, carry_sc):
    @pl.when(pl.program_id(0) == 0)
    def _():
        carry_sc[...] = jnp.zeros_like(carry_sc)

    lg = lg_ref[...]
    lane = lax.broadcasted_iota(jnp.int32, lg.shape, 1)
    lg = jnp.where(lane < N_EXPERTS, lg, -jnp.inf)
    m1 = jnp.max(lg, axis=-1, keepdims=True)
    e1 = jnp.min(jnp.where(lg == m1, lane, LANES), axis=-1, keepdims=True)
    lg2 = jnp.where(lane == e1, -jnp.inf, lg)
    m2 = jnp.max(lg2, axis=-1, keepdims=True)
    e2 = jnp.min(jnp.where(lg2 == m2, lane, LANES), axis=-1, keepdims=True)
    ex = jnp.exp(m2 - m1)
    g1 = 1.0 / (1.0 + ex)
    g2 = ex / (1.0 + ex)
    oh1 = (lane == e1).astype(F32)
    oh2 = (lane == e2).astype(F32)
    both = oh1 + oh2
    before = _dot(tri_ref[...], both.astype(BF16)) + carry_sc[...]
    r1 = jnp.sum(before * oh1, axis=-1, keepdims=True)
    r2 = jnp.sum(before * oh2, axis=-1, keepdims=True)
    carry_sc[...] = carry_sc[...] + jnp.sum(both, axis=0, keepdims=True)
    cnt_ref[...] = jnp.broadcast_to(carry_sc[...], cnt_ref.shape)
    out = jnp.where(lane == 0, e1.astype(F32), 0.0)
    out = jnp.where(lane == 1, e2.astype(F32), out)
    out = jnp.where(lane == 2, g1, out)
    out = jnp.where(lane == 3, g2, out)
    out = jnp.where(lane == 4, r1, out)
    out = jnp.where(lane == 5, r2, out)
    o_ref[...] = out


def _route_call(logits):
    m = logits.shape[0]
    tm = _tile(m, 512)
    idx = np.arange(tm)
    tri = jnp.asarray(idx[:, None] > idx[None, :], BF16)
    return pl.pallas_call(
        _route_kernel,
        grid=(m // tm,),
        in_specs=[pl.BlockSpec((tm, LANES), lambda i: (i, 0)), pl.BlockSpec((tm, tm), lambda i: (0, 0))],
        out_specs=[pl.BlockSpec((tm, LANES), lambda i: (i, 0)), pl.BlockSpec((8, LANES), lambda i: (0, 0))],
        out_shape=[jax.ShapeDtypeStruct((m, LANES), F32), jax.ShapeDtypeStruct((8, LANES), F32)],
        scratch_shapes=[pltpu.VMEM((1, LANES), F32)],
        compiler_params=_cparams("arbitrary"),
        name="moe_route",
    )(logits, tri)


def _slab_copy(src_ref, dst_ref, sem, src_slab, dst_row):
    src = src_ref.at[pl.ds(pl.multiple_of(src_slab, SLABS), SLABS)]
    dst = dst_ref.at[pl.ds(pl.multiple_of(dst_row * SLABS, SLABS), SLABS)]
    return pltpu.make_async_copy(src, dst, sem)


def _slab_cols(ref, j, rows):
    return ref[pl.ds(j, rows, stride=SLABS), :]


def _gather_kernel(used_ref, tok_ref, src_ref, o_ref, buf_sc, sem):
    rows = o_ref.shape[0]
    used = used_ref[pl.program_id(0)] > 0

    def start(r, c):
        _slab_copy(src_ref, buf_sc, sem, tok_ref[0, 0, r], r).start()
        return c

    def wait(r, c):
        _slab_copy(src_ref, buf_sc, sem, 0, r).wait()
        return c

    @pl.when(used)
    def _():
        lax.fori_loop(0, rows, start, 0, unroll=8)
        lax.fori_loop(0, rows, wait, 0, unroll=8)
        for j in range(SLABS):
            o_ref[:, j * LANES:(j + 1) * LANES] = _slab_cols(buf_sc, j, rows).astype(o_ref.dtype)

    @pl.when(jnp.logical_not(used))
    def _():
        o_ref[...] = jnp.zeros_like(o_ref)


def _gather_call(src_slabs, slot_slab, block_used):
    cap = slot_slab.shape[0]
    d = SLABS * LANES
    rows = GATHER_ROWS
    grid_spec = pltpu.PrefetchScalarGridSpec(
        num_scalar_prefetch=1,
        grid=(cap // rows,),
        in_specs=[pl.BlockSpec((1, 1, rows), lambda i, u: (i, 0, 0), memory_space=pltpu.SMEM),
                  pl.BlockSpec(memory_space=pl.ANY)],
        out_specs=pl.BlockSpec((rows, d), lambda i, u: (i, 0)),
        scratch_shapes=[pltpu.VMEM((rows * SLABS, LANES), F32), pltpu.SemaphoreType.DMA(())],
    )
    return pl.pallas_call(
        _gather_kernel,
        grid_spec=grid_spec,
        out_shape=jax.ShapeDtypeStruct((cap, d), BF16),
        compiler_params=_cparams("arbitrary"),
        name="moe_gather",
    )(block_used, slot_slab.reshape(cap // rows, 1, rows), src_slabs)


def _moe_kernel(te_ref, tr_ref, x_ref, w1_ref, w3_ref, w2_ref, o_ref):
    i = pl.program_id(0)
    f = pl.program_id(1)
    nrows = tr_ref[i]
    tm = x_ref.shape[0]

    @pl.when(f == 0)
    def _():
        o_ref[...] = jnp.zeros_like(o_ref)

    def swiglu(rows):
        x = x_ref[rows, :]
        gate = (_silu(_dot(x, w1_ref[0, 0])) * _dot(x, w3_ref[0, 0])).astype(BF16)
        o_ref[rows, :] += _dot(gate, w2_ref[0])

    @pl.when(nrows == tm)
    def _():
        swiglu(slice(0, tm))

    @pl.when(nrows < tm)
    def _():
        for s in range(tm // MOE_SUB):
            @pl.when(s * MOE_SUB < nrows)
            def _():
                swiglu(slice(s * MOE_SUB, (s + 1) * MOE_SUB))


def _moe_call(xs, tile_expert, tile_rows, w1t, w3t, w2b):
    cap, d = xs.shape
    n_exp, nf, _, tf = w1t.shape
    tm = MOE_TM

    def fidx(i, f, te, tr):
        return jnp.where(tr[i] > 0, f, nf - 1)

    grid_spec = pltpu.PrefetchScalarGridSpec(
        num_scalar_prefetch=2,
        grid=(cap // tm, nf),
        in_specs=[
            pl.BlockSpec((tm, d), lambda i, f, te, tr: (i, 0)),
            pl.BlockSpec((1, 1, d, tf), lambda i, f, te, tr: (te[i], fidx(i, f, te, tr), 0, 0)),
            pl.BlockSpec((1, 1, d, tf), lambda i, f, te, tr: (te[i], fidx(i, f, te, tr), 0, 0)),
            pl.BlockSpec((1, tf, d), lambda i, f, te, tr: (te[i], fidx(i, f, te, tr), 0)),
        ],
        out_specs=pl.BlockSpec((tm, d), lambda i, f, te, tr: (i, 0)),
    )
    return pl.pallas_call(
        _moe_kernel,
        grid_spec=grid_spec,
        out_shape=jax.ShapeDtypeStruct((cap, d), F32),
        compiler_params=_cparams("parallel", "arbitrary"),
        name="moe_experts",
    )(tile_expert, tile_rows, xs, w1t, w3t, w2b.reshape(n_exp, nf * tf, d))


def _row_copy(src_ref, dst_ref, sem, src_row, dst_row):
    return pltpu.make_async_copy(src_ref.at[pl.ds(src_row, 1)], dst_ref.at[pl.ds(dst_row, 1)], sem)


def _combine_kernel(pos_ref, y_ref, x_ref, rt_ref, g_ref, o_ref, buf_sc, sem):
    rows = o_ref.shape[0]

    def start(r, c):
        _row_copy(y_ref, buf_sc.at[0], sem, pos_ref[0, 0, r], r).start()
        _row_copy(y_ref, buf_sc.at[1], sem, pos_ref[0, 1, r], r).start()
        return c

    def wait(r, c):
        _row_copy(y_ref, buf_sc.at[0], sem, 0, r).wait()
        _row_copy(y_ref, buf_sc.at[1], sem, 0, r).wait()
        return c

    lax.fori_loop(0, rows, start, 0, unroll=8)
    lax.fori_loop(0, rows, wait, 0, unroll=8)
    rt = rt_ref[...]
    x = x_ref[...] + rt[:, 2:3] * buf_sc[0] + rt[:, 3:4] * buf_sc[1]
    o_ref[...] = _rms(x, g_ref[...])


def _combine_call(pos, yb, x, route, g):
    m, d = x.shape
    rows = GATHER_ROWS if m % GATHER_ROWS == 0 else m
    return pl.pallas_call(
        _combine_kernel,
        grid=(m // rows,),
        in_specs=[pl.BlockSpec((1, 2, rows), lambda i: (i, 0, 0), memory_space=pltpu.SMEM),
                  pl.BlockSpec(memory_space=pl.ANY),
                  pl.BlockSpec((rows, d), lambda i: (i, 0)),
                  pl.BlockSpec((rows, LANES), lambda i: (i, 0)),
                  pl.BlockSpec((1, d), lambda i: (0, 0))],
        out_specs=pl.BlockSpec((rows, d), lambda i: (i, 0)),
        out_shape=jax.ShapeDtypeStruct((m, d), F32),
        scratch_shapes=[pltpu.VMEM((2, rows, d), F32), pltpu.SemaphoreType.DMA(())],
        compiler_params=_cparams("arbitrary"),
        name="moe_combine",
    )(pos.reshape(m // rows, rows, 2).transpose(0, 2, 1), yb, x, route, g.reshape(1, d))


def _rot_cols(w):
    half = QK_ROPE // 2
    return jnp.concatenate([-w[..., half:], w[..., :half]], axis=-1)


def _mixer_weights(w_in, w_up, a_up, w_q_up):
    d = RWKV_DIM
    w_rwkv = w_in[:, :RWKV_IN]
    w_lat = w_in[:, RWKV_IN:RWKV_IN + Q_LORA + KV_LORA]
    w_kr = w_in[:, RWKV_IN + Q_LORA + KV_LORA:]
    w_mla = jnp.concatenate([w_lat, w_kr, _rot_cols(w_kr)], axis=1)
    lora = jnp.zeros((W_LORA + A_LORA, 2 * d), F32)
    lora = lora.at[:W_LORA, :d].set(w_up).at[W_LORA:, d:].set(a_up)
    wq = w_q_up.reshape(Q_LORA, MLA_HEADS, QK_NOPE + QK_ROPE)
    wq_rope = wq[..., QK_NOPE:]
    wq = jnp.concatenate([wq, _rot_cols(wq_rope)], axis=-1).reshape(Q_LORA, MLA_HEADS * QK_CAT)
    return w_rwkv.astype(BF16), w_mla.astype(BF16), lora, wq.astype(BF16)


def kernel(x, positions, l0_mix_norm, l0_w_in, l0_shift_mu, l0_w0, l0_w_up, l0_a0, l0_a_up, l0_g_up, l0_k_k, l0_k_a, l0_r_k, l0_gn_w, l0_gn_b, l0_q_norm, l0_w_q_up, l0_kv_norm, l0_w_kv_up, l0_w_out, l0_ffn_norm, l0_ffn_w1, l0_ffn_w3, l0_ffn_w2, l1_mix_norm, l1_pw1_w, l1_pw1_b, l1_dw_w, l1_dw_b, l1_ln_g, l1_ln_b, l1_pw2_w, l1_pw2_b, l1_ffn_norm, l1_router, l1_exp_w1, l1_exp_w3, l1_exp_w2, final_norm):
    batch, seq, d = x.shape
    m = batch * seq
    x0 = x.reshape(m, d)
    pos = positions.reshape(m, 1)
    inv = ROPE_THETA ** (-jnp.arange(0, QK_ROPE, 2, dtype=F32) / QK_ROPE)
    inv_tab = jnp.tile(inv, LANES // inv.shape[0]).reshape(1, LANES)

    w_rwkv, w_mla, lora_w, wq = _mixer_weights(l0_w_in, l0_w_up, l0_a_up, l0_w_q_up)

    hn0 = _rmsnorm_call(x0, l0_mix_norm)
    p_rwkv = _matmul_call(hn0, w_rwkv, name="in_proj_rwkv")
    p_mla = _matmul_call(hn0, w_mla, name="in_proj_mla")

    prep = _rwkv_prep_call(p_rwkv, seq, l0_shift_mu, lora_w, l0_w0, l0_a0, l0_g_up, l0_k_k, l0_k_a,
                           l0_r_k.reshape(-1))
    at, bt, kt, rt, v, bh, kh, gc, bonus, gate = prep
    ra, o2, pm, qm = _rwkv_intra_call(at, bt, kt, rt, v, bh, kh, gc, seq)
    y_rwkv = _rwkv_seq_call(ra, o2, pm, qm, bonus, gate, l0_gn_w, l0_gn_b, batch, seq)

    q, k, vv = _mla_proj_calls(p_mla, pos, inv_tab, l0_q_norm, wq, l0_kv_norm, l0_w_kv_up.astype(BF16),
                               batch, seq)
    y_mla, expert_w = _attn_call(q, k, vv, batch, seq, (l1_exp_w1, l1_exp_w3, l1_exp_w2))

    x1, hn1 = _mix_out_call(y_rwkv, y_mla, l0_w_out.astype(BF16), x0, l0_ffn_norm)

    x2, hn2 = _ffn_call(hn1, l0_ffn_w1.astype(BF16), l0_ffn_w3.astype(BF16), l0_ffn_w2.astype(BF16),
                        x1, l1_mix_norm)

    u = _glu_call(hn2, l1_pw1_w.astype(BF16), l1_pw1_b)
    sc = _dwconv_call(u, seq, l1_dw_w, l1_dw_b, l1_ln_g, l1_ln_b)
    router_pad = jnp.zeros((d, LANES), F32).at[:, :N_EXPERTS].set(l1_router)
    x3, hn3, logits = _conv_out_call(sc, l1_pw2_w.astype(BF16), l1_pw2_b, x2, l1_ffn_norm, router_pad)

    route, counts = _route_call(logits)
    e = route[:, 0:2].astype(jnp.int32)
    rank = route[:, 4:6].astype(jnp.int32)
    cnt = counts[0, :N_EXPERTS].astype(jnp.int32)
    padded = (cnt + MOE_TM - 1) // MOE_TM * MOE_TM
    pad_end = jnp.cumsum(padded)
    pad_start = pad_end - padded
    pos_slot = pad_start[e] + rank
    cap = (2 * m + N_EXPERTS * MOE_TM + MOE_TM - 1) // MOE_TM * MOE_TM
    tok = jnp.broadcast_to(jnp.arange(m, dtype=jnp.int32)[:, None], (m, 2))
    slot_slab = jnp.zeros((cap,), jnp.int32).at[pos_slot.reshape(-1)].set(tok.reshape(-1) * SLABS)
    tile_start = jnp.arange(cap // MOE_TM, dtype=jnp.int32) * MOE_TM
    tile_expert = jnp.minimum(jnp.sum(tile_start[:, None] >= pad_end[None, :], axis=1), N_EXPERTS - 1)
    used = tile_start < pad_end[-1]
    last_used = jnp.max(jnp.where(used, tile_expert, 0))
    tile_expert = jnp.where(used, tile_expert, last_used).astype(jnp.int32)
    tile_rows = jnp.clip(pad_start[tile_expert] + cnt[tile_expert] - tile_start, 0, MOE_TM)
    tile_rows = jnp.where(used, tile_rows, 0).astype(jnp.int32)

    per_tile = MOE_TM // GATHER_ROWS
    block_off = jnp.tile(jnp.arange(per_tile, dtype=jnp.int32) * GATHER_ROWS, cap // MOE_TM)
    block_used = (jnp.repeat(tile_rows, per_tile) > block_off).astype(jnp.int32)
    xs = _gather_call(hn3, slot_slab, block_used)
    yb = _moe_call(xs, tile_expert, tile_rows, *expert_w)
    out = _combine_call(pos_slot, yb, x3, route, final_norm)
    return out.reshape(batch, seq, d)
```

```python
import functools

import numpy as np
import jax
import jax.numpy as jnp
from jax import lax
from jax.experimental import pallas as pl
from jax.experimental.pallas import tpu as pltpu

F32 = jnp.float32
BF16 = jnp.bfloat16

D_MODEL = 2048
CHUNK = 64
HEAD = 64
N_HEADS = 16
RWKV_DIM = N_HEADS * HEAD
W_LORA, A_LORA, G_LORA = 64, 64, 128
RWKV_IN = 3 * RWKV_DIM + W_LORA + A_LORA + G_LORA
MLA_HEADS = 8
Q_LORA = KV_LORA = 512
QK_NOPE, QK_ROPE, V_HEAD = 128, 64, 128
QK_CAT = 2 * QK_NOPE
V_EXT = 2 * V_HEAD
ROPE_THETA = 10000.0
CONV_WIDTH = 31
CONV_HALO = 32
N_EXPERTS = 8
NORM_EPS = 1e-6
LN_EPS = 1e-5
GN_EPS = HEAD * 1e-5
NEG_BIG = -1e30

LANES = 128
SUBLANES = 8
VMEM_LIMIT = 56 * 1024 * 1024

HEADS_PER_GROUP = 2
GROUP_W = HEADS_PER_GROUP * HEAD
MOE_TM = 1024
MOE_SUB = 512
MOE_TF = 512
EXPERT_CAST_BLOCKS = 512
GATHER_ROWS = 256
SLABS = D_MODEL // LANES
ATTN_HEADS_PER_STEP = 4
LOG2_E = 1.4426950408889634


def _cparams(*sem):
    return pltpu.CompilerParams(dimension_semantics=sem, vmem_limit_bytes=VMEM_LIMIT)


def _tile(n, pref, align=8):
    if n <= pref:
        return n
    t = (pref // align) * align
    while t > align and n % t:
        t -= align
    assert n % t == 0, (n, pref)
    return t


def _dot(a, b):
    return jnp.dot(a, b, preferred_element_type=F32)


def _dot_nt(a, b):
    return lax.dot_general(a, b, (((1,), (1,)), ((), ())), preferred_element_type=F32)


def _split3(x):
    hi = x.astype(BF16)
    r1 = x - hi.astype(F32)
    mid = r1.astype(BF16)
    lo = (r1 - mid.astype(F32)).astype(BF16)
    return hi, mid, lo


def _dot_lx(a, b_exact):
    hi, mid, lo = _split3(a)
    return _dot(hi, b_exact) + (_dot(mid, b_exact) + _dot(lo, b_exact))


def _dot_xr(a_exact, b):
    hi, mid, lo = _split3(b)
    return _dot(a_exact, hi) + (_dot(a_exact, mid) + _dot(a_exact, lo))


def _dot_hp(a, b):
    ah, am, _ = _split3(a)
    bh, bm, _ = _split3(b)
    return _dot(ah, bh) + (_dot(ah, bm) + _dot(am, bh))


def _sigmoid(x):
    return 1.0 / (1.0 + jnp.exp(-x))


def _silu(x):
    return x * _sigmoid(x)


def _softplus(x):
    return jnp.maximum(x, 0.0) + jnp.log(1.0 + jnp.exp(-jnp.abs(x)))


def _rms(x, g):
    return x * lax.rsqrt(jnp.mean(x * x, axis=-1, keepdims=True) + NORM_EPS) * g


def _rmsnorm_kernel(x_ref, g_ref, o_ref):
    o_ref[...] = _rms(x_ref[...], g_ref[...]).astype(o_ref.dtype)


def _rmsnorm_call(x, g):
    m, d = x.shape
    tm = _tile(m, 512)
    return pl.pallas_call(
        _rmsnorm_kernel,
        grid=(m // tm,),
        in_specs=[pl.BlockSpec((tm, d), lambda i: (i, 0)), pl.BlockSpec((1, d), lambda i: (0, 0))],
        out_specs=pl.BlockSpec((tm, d), lambda i: (i, 0)),
        out_shape=jax.ShapeDtypeStruct((m, d), BF16),
        compiler_params=_cparams("parallel"),
        name="rmsnorm",
    )(x, g.reshape(1, d))


def _matmul_kernel(a_ref, w_ref, o_ref):
    o_ref[...] = _dot(a_ref[...], w_ref[...])


def _matmul_call(a, w, tm_pref=1024, tn_pref=1664, name="matmul"):
    m, k = a.shape
    n = w.shape[1]
    tm = _tile(m, tm_pref)
    tn = _tile(n, tn_pref, LANES)
    return pl.pallas_call(
        _matmul_kernel,
        grid=(m // tm, n // tn),
        in_specs=[pl.BlockSpec((tm, k), lambda i, j: (i, 0)), pl.BlockSpec((k, tn), lambda i, j: (0, j))],
        out_specs=pl.BlockSpec((tm, tn), lambda i, j: (i, j)),
        out_shape=jax.ShapeDtypeStruct((m, n), F32),
        compiler_params=_cparams("parallel", "arbitrary"),
        name=name,
    )(a, w)


def _rwkv_prep_kernel(p_ref, prev_ref, mu_ref, lw_ref, w0_ref, a0_ref, gup_ref, kk_ref, ka_ref, rk_ref,
                      tril_ref, ones_ref, seg_ref, segt_ref,
                      at_ref, bt_ref, kt_ref, rt_ref, v_ref, bh_ref, kh_ref, gc_ref, bonus_ref, g_ref,
                      *, tiles_per_seq):
    ts = p_ref.shape[0]
    first = (pl.program_id(0) % tiles_per_seq) == 0
    row0 = lax.broadcasted_iota(jnp.int32, (ts, 1), 0) == 0

    def mixed(c0, c1):
        pc = p_ref[:, c0:c1]
        prev = jnp.where(first, 0.0, prev_ref[7:8, c0:c1])
        sh = jnp.where(row0, prev, pltpu.roll(pc, 1, axis=0))
        return pc + (sh - pc) * mu_ref[:, c0:c1]

    d = RWKV_DIM
    r = mixed(0, d)
    k = mixed(d, 2 * d)
    v = mixed(2 * d, 3 * d)
    xwa = mixed(3 * d, 3 * d + W_LORA + A_LORA)
    xg = mixed(3 * d + W_LORA + A_LORA, RWKV_IN)

    lane = lax.broadcasted_iota(jnp.int32, xwa.shape, 1)
    z = jnp.where(lane < W_LORA, jnp.tanh(xwa), xwa)
    wa = _dot_hp(z, lw_ref[...])
    w = -_softplus(-(w0_ref[...] + wa[:, :d])) - 0.5
    logdecay = -jnp.exp(w)
    a = _sigmoid(a0_ref[...] + wa[:, d:])
    g_ref[...] = _dot_hp(_sigmoid(xg), gup_ref[...])

    seg = seg_ref[...]
    segt = segt_ref[...]

    def head_sum(x):
        return _dot_lx(_dot_lx(x, seg), segt)

    kkr = k * kk_ref[...]
    kk = kkr / jnp.maximum(jnp.sqrt(head_sum(kkr * kkr)), 1e-12)
    k2 = k * (1.0 + (a - 1.0) * ka_ref[...])
    bonus_ref[...] = head_sum(r * k2 * rk_ref[...]) * v

    gcum = _dot_xr(tril_ref[...], logdecay)
    gtot = _dot_xr(ones_ref[...], logdecay)
    g_in = jnp.exp(gcum)
    g_ex = jnp.exp(gcum - logdecay)
    g_inv = jnp.exp(-gcum)
    g_rest = jnp.exp(gtot - gcum)
    beta = kk * a
    at_ref[...] = (-kk * g_ex).astype(at_ref.dtype)
    bt_ref[...] = (beta * g_inv).astype(bt_ref.dtype)
    kt_ref[...] = (k2 * g_inv).astype(kt_ref.dtype)
    rt_ref[...] = (r * g_in).astype(rt_ref.dtype)
    v_ref[...] = v.astype(v_ref.dtype)
    bh_ref[...] = beta * g_rest
    kh_ref[...] = k2 * g_rest
    gc_ref[...] = jnp.exp(gtot)


def _rwkv_prep_call(p, seq, mu, lora_w, w0, a0, g_up, k_k, k_a, r_k):
    m = p.shape[0]
    ts = _tile(seq, 256, CHUNK)
    tiles_per_seq = seq // ts
    d = RWKV_DIM
    idx = np.arange(ts)
    same = (idx[:, None] // CHUNK) == (idx[None, :] // CHUNK)
    tril = jnp.asarray(same & (idx[:, None] >= idx[None, :]), BF16)
    ones = jnp.asarray(same, BF16)
    lane = np.arange(d)
    seg_np = (lane[:, None] // HEAD) == np.arange(LANES)[None, :]
    seg = jnp.asarray(seg_np, BF16)
    segt = jnp.asarray(seg_np.T, BF16)

    def row(x):
        return x.reshape(1, -1).astype(F32)

    full = lambda shape: pl.BlockSpec(shape, lambda i: (0, 0))
    tok = pl.BlockSpec((ts, d), lambda i: (i, 0))
    outs = pl.pallas_call(
        functools.partial(_rwkv_prep_kernel, tiles_per_seq=tiles_per_seq),
        grid=(m // ts,),
        in_specs=[
            pl.BlockSpec((ts, RWKV_IN), lambda i: (i, 0)),
            pl.BlockSpec((8, RWKV_IN), lambda i: (jnp.maximum(i * (ts // 8) - 1, 0), 0)),
            full((1, RWKV_IN)), full((W_LORA + A_LORA, 2 * d)), full((1, d)), full((1, d)),
            full((G_LORA, d)), full((1, d)), full((1, d)), full((1, d)),
            full((ts, ts)), full((ts, ts)), full((d, LANES)), full((LANES, d)),
        ],
        out_specs=[tok] * 10,
        out_shape=[jax.ShapeDtypeStruct((m, d), BF16)] * 5 + [jax.ShapeDtypeStruct((m, d), F32)] * 5,
        compiler_params=_cparams("parallel"),
        name="rwkv_prep",
    )(p, p, row(mu), lora_w, row(w0), row(a0), g_up, row(k_k), row(k_a), row(r_k), tril, ones, seg, segt)
    return outs


def _stack_heads(x):
    lane_head = lax.broadcasted_iota(jnp.int32, x.shape, 1) // HEAD
    return jnp.concatenate([jnp.where(lane_head == h, x, 0.0) for h in range(HEADS_PER_GROUP)], axis=0)


def _unstack_heads(x):
    out = x[0:CHUNK]
    for h in range(1, HEADS_PER_GROUP):
        out = out + x[h * CHUNK:(h + 1) * CHUNK]
    return out


def _rwkv_intra_kernel(at_ref, bt_ref, kt_ref, rt_ref, v_ref, bh_ref, kh_ref, gc_ref,
                       ra_ref, o2_ref, p_ref, q_ref, *, mm):
    w = GROUP_W
    n_chunks = at_ref.shape[0] // CHUNK
    ri = lax.broadcasted_iota(jnp.int32, (w, w), 0)
    ci = lax.broadcasted_iota(jnp.int32, (w, w), 1)
    strict = (ri % CHUNK) > (ci % CHUNK)
    incl = (ri % CHUNK) >= (ci % CHUNK)
    blk16 = (ri // 16) == (ci // 16)
    eye = (ri == ci).astype(F32)

    chunks = [slice(c * CHUNK, (c + 1) * CHUNK) for c in range(n_chunks)]

    def each(f, *lists):
        return [f(*args) for args in zip(*lists)]

    def stacked(ref):
        return [_stack_heads(ref[rows, :]) for rows in chunks]

    a_s, r_s, b_s, k_s, v_s = stacked(at_ref), stacked(rt_ref), stacked(bt_ref), stacked(kt_ref), stacked(v_ref)
    s = each(lambda a, r, b, k: mm(jnp.concatenate([a, r], axis=0), jnp.concatenate([b, k], axis=0), nt=True),
             a_s, r_s, b_s, k_s)
    l_ab = [jnp.where(strict, x[:w, :w], 0.0) for x in s]
    l_ak = [jnp.where(strict, x[:w, w:], 0.0) for x in s]
    m_rb = [jnp.where(incl, x[w:, :w], 0.0) for x in s]
    m_rk = [jnp.where(incl, x[w:, w:], 0.0) for x in s]

    dg = [jnp.where(blk16, x, 0.0) for x in l_ab]
    off = each(lambda x, y: x - y, l_ab, dg)
    td = [eye + x for x in dg]
    pw = dg
    for _ in range(3):
        pw = each(mm, pw, pw)
        td = each(lambda x, y: x + mm(x, y), td, pw)
    n1 = each(mm, td, off)
    n2 = each(mm, n1, n1)
    t1 = each(lambda x, y: x + mm(y, x), td, n2)
    t = each(lambda x, y: x + mm(y, x), t1, n1)

    lv = each(mm, l_ak, v_s)
    y = each(lambda tt, x, a: mm(tt, jnp.concatenate([x, a], axis=1)), t, lv, a_s)
    z = each(mm, m_rb, y)
    mv = each(mm, m_rk, v_s)
    z2 = each(lambda ref_rows, yy: mm(_stack_heads(bh_ref[ref_rows, :]).T, yy), chunks, y)
    kv = each(lambda ref_rows, vv: mm(_stack_heads(kh_ref[ref_rows, :]).T, vv), chunks, v_s)
    for c, rows in enumerate(chunks):
        ra_ref[rows, :] = _unstack_heads(r_s[c] + z[c][:, w:]).astype(ra_ref.dtype)
        o2_ref[rows, :] = _unstack_heads(z[c][:, :w] + mv[c])
        p_ref[rows, :] = _unstack_heads(eye * gc_ref[c * CHUNK:c * CHUNK + 1, :]
                                        + z2[c][:, w:]).astype(p_ref.dtype)
        q_ref[rows, :] = _unstack_heads(z2[c][:, :w] + kv[c])


def _mm_bf16(a, b, nt=False):
    a = a.astype(BF16)
    b = b.astype(BF16)
    return _dot_nt(a, b) if nt else _dot(a, b)


def _mm_hp(a, b, nt=False):
    ah, am, _ = _split3(a)
    bh, bm, _ = _split3(b)
    f = _dot_nt if nt else _dot
    return f(ah, bh) + (f(ah, bm) + f(am, bh))


def _rwkv_intra_call(at, bt, kt, rt, v, bh, kh, gc, seq):
    m, d = at.shape
    ts = _tile(seq, 512, CHUNK)
    spec = pl.BlockSpec((ts, GROUP_W), lambda i, j: (i, j))
    return pl.pallas_call(
        functools.partial(_rwkv_intra_kernel, mm=_mm_bf16),
        grid=(m // ts, d // GROUP_W),
        in_specs=[spec] * 8,
        out_specs=[spec] * 4,
        out_shape=[jax.ShapeDtypeStruct((m, d), dt) for dt in (BF16, F32, BF16, F32)],
        compiler_params=_cparams("parallel", "parallel"),
        name="rwkv_intra",
    )(at, bt, kt, rt, v, bh, kh, gc)


def _rwkv_seq_kernel(ra_ref, o2_ref, p_ref, q_ref, bonus_ref, g_ref, gnw_ref, gnb_ref, seg_ref, segt_ref,
                     o_ref, h_sc, y_sc, *, mm):
    w = GROUP_W
    n_groups = RWKV_DIM // w
    n_chunks = ra_ref.shape[0] // CHUNK

    @pl.when(pl.program_id(1) == 0)
    def _():
        h_sc[...] = jnp.zeros_like(h_sc)

    def chunk(c, carry):
        r0 = pl.multiple_of(c * CHUNK, CHUNK)
        rows = pl.ds(r0, CHUNK)
        for gi in range(n_groups):
            cols = slice(gi * w, (gi + 1) * w)
            lhs = jnp.concatenate([ra_ref[rows, cols], _stack_heads(p_ref[rows, cols])], axis=0)
            res = mm(lhs, h_sc[gi])
            y_sc[rows, cols] = res[:CHUNK] + o2_ref[rows, cols]
            h_sc[gi] = res[CHUNK:] + _stack_heads(q_ref[rows, cols])
        return carry

    lax.fori_loop(0, n_chunks, chunk, 0)

    y = y_sc[...]
    seg = seg_ref[...]
    segt = segt_ref[...]
    mean = _dot_lx(_dot_lx(y, seg), segt) * (1.0 / HEAD)
    yc = y - mean
    var = _dot_lx(_dot_lx(yc * yc, seg), segt) * (1.0 / HEAD)
    yn = yc * lax.rsqrt(var + GN_EPS) * gnw_ref[...] + gnb_ref[...]
    o_ref[...] = ((yn + bonus_ref[...]) * g_ref[...]).astype(o_ref.dtype)


def _rwkv_seq_call(ra, o2, p, q, bonus, g, gn_w, gn_b, batch, seq):
    m, d = ra.shape
    ts = _tile(seq, 512, CHUNK)
    nt = seq // ts
    lane = np.arange(d)
    seg_np = (lane[:, None] // HEAD) == np.arange(LANES)[None, :]
    seg = jnp.asarray(seg_np, BF16)
    segt = jnp.asarray(seg_np.T, BF16)
    tok = pl.BlockSpec((ts, d), lambda b, i: (b * nt + i, 0))
    full = lambda shape: pl.BlockSpec(shape, lambda b, i: (0, 0))
    return pl.pallas_call(
        functools.partial(_rwkv_seq_kernel, mm=_mm_bf16),
        grid=(batch, nt),
        in_specs=[tok] * 6 + [full((1, d)), full((1, d)), full((d, LANES)), full((LANES, d))],
        out_specs=tok,
        out_shape=jax.ShapeDtypeStruct((m, d), BF16),
        scratch_shapes=[pltpu.VMEM((d // GROUP_W, GROUP_W, GROUP_W), F32), pltpu.VMEM((ts, d), F32)],
        compiler_params=_cparams("arbitrary", "arbitrary"),
        name="rwkv_seq",
    )(ra, o2, p, q, bonus, g, gn_w.reshape(1, d), gn_b.reshape(1, d), seg, segt)


def _rope_tab(pos_ref, inv_ref):
    ang = pos_ref[...].astype(F32) * inv_ref[...]
    lane = lax.broadcasted_iota(jnp.int32, ang.shape, 1)
    return jnp.where(lane < QK_ROPE, jnp.cos(ang), jnp.sin(ang))


def _mla_q_kernel(lat_ref, pos_ref, inv_ref, g_ref, w_ref, o_ref, *, scale):
    hn = _rms(lat_ref[...], g_ref[...]).astype(BF16)
    q = _dot(hn, w_ref[...]) * scale
    tab = _rope_tab(pos_ref, inv_ref)
    for h in range(MLA_HEADS):
        c0 = h * QK_CAT
        o_ref[0, h, :, 0:QK_NOPE] = q[:, c0:c0 + QK_NOPE].astype(BF16)
        o_ref[0, h, :, QK_NOPE:QK_CAT] = (q[:, c0 + QK_NOPE:c0 + QK_CAT] * tab).astype(BF16)


def _mla_kv_kernel(lat_ref, kr_ref, pos_ref, inv_ref, g_ref, w_ref, k_ref, v_ref):
    hn = _rms(lat_ref[...], g_ref[...]).astype(BF16)
    kv = _dot(hn, w_ref[...])
    t = kr_ref[...] * _rope_tab(pos_ref, inv_ref)
    k_rope = (t + pltpu.roll(t, QK_ROPE, axis=1)).astype(BF16)
    for h in range(MLA_HEADS):
        c0 = h * (QK_NOPE + V_HEAD)
        k_ref[0, h, :, 0:QK_NOPE] = kv[:, c0:c0 + QK_NOPE].astype(BF16)
        k_ref[0, h, :, QK_NOPE:QK_CAT] = k_rope
        v_ref[0, h, :, 0:V_HEAD] = kv[:, c0 + QK_NOPE:c0 + QK_NOPE + V_HEAD].astype(BF16)
        v_ref[0, h, :, V_HEAD:V_EXT] = jnp.ones((kv.shape[0], V_EXT - V_HEAD), BF16)


def _mla_proj_calls(p_mla, pos, inv_tab, q_norm, wq, kv_norm, wkv, batch, seq):
    tm = _tile(seq, 512)
    nt = seq // tm
    scale = float((QK_NOPE + QK_ROPE) ** -0.5) * LOG2_E
    lat = lambda c: pl.BlockSpec((tm, Q_LORA), lambda b, i: (b * nt + i, c))
    posspec = pl.BlockSpec((tm, 1), lambda b, i: (b * nt + i, 0))
    full = lambda shape: pl.BlockSpec(shape, lambda b, i: (0, 0))
    headed = lambda w: pl.BlockSpec((1, MLA_HEADS, tm, w), lambda b, i: (b, 0, i, 0))
    q = pl.pallas_call(
        functools.partial(_mla_q_kernel, scale=scale),
        grid=(batch, nt),
        in_specs=[lat(0), posspec, full((1, LANES)), full((1, Q_LORA)), full((Q_LORA, MLA_HEADS * QK_CAT))],
        out_specs=headed(QK_CAT),
        out_shape=jax.ShapeDtypeStruct((batch, MLA_HEADS, seq, QK_CAT), BF16),
        compiler_params=_cparams("parallel", "parallel"),
        name="mla_q",
    )(p_mla, pos, inv_tab, q_norm.reshape(1, -1), wq)
    k, v = pl.pallas_call(
        _mla_kv_kernel,
        grid=(batch, nt),
        in_specs=[lat(1), pl.BlockSpec((tm, LANES), lambda b, i: (b * nt + i, 2 * Q_LORA // LANES)),
                  posspec, full((1, LANES)), full((1, KV_LORA)),
                  full((KV_LORA, MLA_HEADS * (QK_NOPE + V_HEAD)))],
        out_specs=[headed(QK_CAT), headed(V_EXT)],
        out_shape=[jax.ShapeDtypeStruct((batch, MLA_HEADS, seq, QK_CAT), BF16),
                   jax.ShapeDtypeStruct((batch, MLA_HEADS, seq, V_EXT), BF16)],
        compiler_params=_cparams("parallel", "parallel"),
        name="mla_kv",
    )(p_mla, p_mla, pos, inv_tab, kv_norm.reshape(1, -1), wkv)
    return q, k, v


def _cast_expert_blocks(w1_ref, w3_ref, w1o_ref, w3o_ref):
    for f in range(w1o_ref.shape[1]):
        cols = slice(f * MOE_TF, (f + 1) * MOE_TF)
        w1o_ref[0, f] = w1_ref[:, cols].astype(BF16)
        w3o_ref[0, f] = w3_ref[:, cols].astype(BF16)


def _attn_cast_kernel(qi_ref, kj_ref, q_ref, k_ref, v_ref, w1_ref, w3_ref,
                      o_ref, w1o_ref, w3o_ref, m_sc, acc_sc):
    _cast_expert_blocks(w1_ref, w3_ref, w1o_ref, w3o_ref)
    _attn_kernel(qi_ref, kj_ref, q_ref, k_ref, v_ref, o_ref, m_sc, acc_sc)


def _attn_kernel(qi_ref, kj_ref, q_ref, k_ref, v_ref, o_ref, m_sc, acc_sc):
    t = pl.program_id(2)
    qi = qi_ref[t]
    kj = kj_ref[t]
    n_heads = q_ref.shape[1]
    tk = k_ref.shape[2]

    @pl.when(kj == 0)
    def _():
        m_sc[...] = jnp.full_like(m_sc, NEG_BIG)
        acc_sc[...] = jnp.zeros_like(acc_sc)

    def update(heads, mask):
        s = [_dot_nt(q_ref[0, h], k_ref[0, h]) for h in heads]
        if mask:
            ri = lax.broadcasted_iota(jnp.int32, s[0].shape, 0) // CHUNK
            ci = lax.broadcasted_iota(jnp.int32, s[0].shape, 1) // CHUNK
            s = [jnp.where(ci <= ri, x, NEG_BIG) for x in s]
        for i, h in enumerate(heads):
            m_prev = m_sc[h]
            m_new = jnp.maximum(m_prev, jnp.max(s[i], axis=-1, keepdims=True))
            alpha = jnp.exp2(m_prev - m_new)
            p = jnp.exp2(s[i] - jnp.concatenate([m_new] * (tk // LANES), axis=1))
            pv = _dot(p.astype(BF16), v_ref[0, h])
            acc_sc[h] = jnp.concatenate([alpha] * (V_EXT // LANES), axis=1) * acc_sc[h] + pv
            m_sc[h] = m_new

    pairs = [tuple(range(h, min(h + 2, n_heads))) for h in range(0, n_heads, 2)]

    @pl.when(kj < qi)
    def _():
        for hp in pairs:
            update(hp, False)

    @pl.when(kj == qi)
    def _():
        for hp in pairs:
            update(hp, True)
        for h in range(n_heads):
            acc = acc_sc[h]
            o_ref[:, h * V_HEAD:(h + 1) * V_HEAD] = (acc[:, :V_HEAD] / acc[:, V_HEAD:]).astype(o_ref.dtype)


def _expert_cast_specs(w1, w3, block_of):
    n_exp, d, ff = w1.shape
    nf = ff // MOE_TF
    rows = n_exp * d // EXPERT_CAST_BLOCKS
    per_exp = d // rows
    in_specs = [pl.BlockSpec((rows, ff), lambda *a: (block_of(*a), 0))] * 2
    out_specs = [pl.BlockSpec((1, nf, rows, MOE_TF),
                              lambda *a: (block_of(*a) // per_exp, 0, block_of(*a) % per_exp, 0))] * 2
    out_shape = [jax.ShapeDtypeStruct((n_exp, nf, d, MOE_TF), BF16)] * 2
    args = (w1.reshape(n_exp * d, ff), w3.reshape(n_exp * d, ff))
    return in_specs, out_specs, out_shape, args


def _expert_cast_call(w1, w3):
    in_specs, out_specs, out_shape, args = _expert_cast_specs(w1, w3, lambda i: i)
    return pl.pallas_call(
        _cast_expert_blocks,
        grid=(EXPERT_CAST_BLOCKS,),
        in_specs=in_specs,
        out_specs=out_specs,
        out_shape=out_shape,
        compiler_params=_cparams("parallel"),
        name="expert_cast",
    )(*args)


def _attn_call(q, k, v, batch, seq, expert_weights):
    tq = _tile(seq, 512, CHUNK)
    nq = seq // tq
    hg = ATTN_HEADS_PER_STEP
    n_hg = MLA_HEADS // hg
    pairs = [(i, j) for i in range(nq) for j in range(i + 1)]
    qi = jnp.asarray([p[0] for p in pairs], jnp.int32)
    kj = jnp.asarray([p[1] for p in pairs], jnp.int32)
    in_specs = [
        pl.BlockSpec((1, hg, tq, QK_CAT), lambda b, h, t, qi, kj: (b, h, qi[t], 0)),
        pl.BlockSpec((1, hg, tq, QK_CAT), lambda b, h, t, qi, kj: (b, h, kj[t], 0)),
        pl.BlockSpec((1, hg, tq, V_EXT), lambda b, h, t, qi, kj: (b, h, kj[t], 0)),
    ]
    out_specs = [pl.BlockSpec((tq, hg * V_HEAD), lambda b, h, t, qi, kj: (b * nq + qi[t], h))]
    out_shape = [jax.ShapeDtypeStruct((batch * seq, MLA_HEADS * V_HEAD), BF16)]
    args = (qi, kj, q, k, v)
    fused = batch * n_hg * len(pairs) >= EXPERT_CAST_BLOCKS
    if fused:
        def block_of(b, h, t, qi, kj):
            return jnp.minimum((b * n_hg + h) * len(pairs) + t, EXPERT_CAST_BLOCKS - 1)

        c_in, c_out, c_shape, c_args = _expert_cast_specs(*expert_weights, block_of)
        in_specs, out_specs, out_shape, args = in_specs + c_in, out_specs + c_out, out_shape + c_shape, args + c_args
    grid_spec = pltpu.PrefetchScalarGridSpec(
        num_scalar_prefetch=2,
        grid=(batch, n_hg, len(pairs)),
        in_specs=in_specs,
        out_specs=out_specs,
        scratch_shapes=[pltpu.VMEM((hg, tq, LANES), F32), pltpu.VMEM((hg, tq, V_EXT), F32)],
    )
    outs = pl.pallas_call(
        _attn_cast_kernel if fused else _attn_kernel,
        grid_spec=grid_spec,
        out_shape=out_shape,
        compiler_params=_cparams("arbitrary", "arbitrary", "arbitrary"),
        name="mla_attn",
    )(*args)
    if fused:
        return outs[0], tuple(outs[1:])
    return outs[0], tuple(_expert_cast_call(*expert_weights))


def _mix_out_kernel(ya_ref, yb_ref, w_ref, x_ref, g_ref, xo_ref, hn_ref):
    half = ya_ref.shape[1]
    acc = _dot(ya_ref[...], w_ref[0:half, :]) + _dot(yb_ref[...], w_ref[half:, :])
    x = x_ref[...] + acc
    xo_ref[...] = x
    hn_ref[...] = _rms(x, g_ref[...]).astype(hn_ref.dtype)


def _mix_out_call(ya, yb, w, x, g):
    m, d = x.shape
    tm = _tile(m, 512)
    half = ya.shape[1]
    row = pl.BlockSpec((tm, d), lambda i: (i, 0))
    return pl.pallas_call(
        _mix_out_kernel,
        grid=(m // tm,),
        in_specs=[pl.BlockSpec((tm, half), lambda i: (i, 0)), pl.BlockSpec((tm, half), lambda i: (i, 0)),
                  pl.BlockSpec((2 * half, d), lambda i: (0, 0)), row, pl.BlockSpec((1, d), lambda i: (0, 0))],
        out_specs=[row, row],
        out_shape=[jax.ShapeDtypeStruct((m, d), F32), jax.ShapeDtypeStruct((m, d), BF16)],
        compiler_params=_cparams("parallel"),
        name="mix_out",
    )(ya, yb, w, x, g.reshape(1, d))


def _conv_out_kernel(a_ref, w_ref, b_ref, x_ref, g_ref, r_ref, xo_ref, hn_ref, lg_ref):
    x = x_ref[...] + _dot(a_ref[...], w_ref[...]) + b_ref[...]
    xo_ref[...] = x
    hn = _rms(x, g_ref[...])
    tm = x.shape[0]
    for j in range(SLABS):
        hn_ref[pl.ds(j, tm, stride=SLABS), :] = hn[:, j * LANES:(j + 1) * LANES]
    lg_ref[...] = _dot_hp(hn, r_ref[...])


def _conv_out_call(a, w, b, x, g, router_pad):
    m, d = x.shape
    tm = _tile(m, 256)
    row = pl.BlockSpec((tm, d), lambda i: (i, 0))
    full = lambda shape: pl.BlockSpec(shape, lambda i: (0, 0))
    return pl.pallas_call(
        _conv_out_kernel,
        grid=(m // tm,),
        in_specs=[row, full((d, d)), full((1, d)), row, full((1, d)), full((d, LANES))],
        out_specs=[row, pl.BlockSpec((tm * SLABS, LANES), lambda i: (i, 0)),
                   pl.BlockSpec((tm, LANES), lambda i: (i, 0))],
        out_shape=[jax.ShapeDtypeStruct((m, d), F32), jax.ShapeDtypeStruct((m * SLABS, LANES), F32),
                   jax.ShapeDtypeStruct((m, LANES), F32)],
        compiler_params=_cparams("parallel"),
        name="conv_out",
    )(a, w, b.reshape(1, d), x, g.reshape(1, d), router_pad)


def _ffn_kernel(h_ref, w1_ref, w3_ref, w2_ref, x_ref, g_ref, xo_ref, hn_ref):
    f = pl.program_id(1)

    @pl.when(f == 0)
    def _():
        xo_ref[...] = x_ref[...]

    h = h_ref[...]
    gate = (_silu(_dot(h, w1_ref[...])) * _dot(h, w3_ref[...])).astype(BF16)
    xo_ref[...] += _dot(gate, w2_ref[...])

    @pl.when(f == pl.num_programs(1) - 1)
    def _():
        hn_ref[...] = _rms(xo_ref[...], g_ref[...]).astype(hn_ref.dtype)


def _ffn_call(h, w1, w3, w2, x, g):
    m, d = x.shape
    ff = w1.shape[1]
    tm = _tile(m, 512)
    tf = _tile(ff, 512, LANES)
    row = pl.BlockSpec((tm, d), lambda i, f: (i, 0))
    wtile = pl.BlockSpec((d, tf), lambda i, f: (0, f))
    return pl.pallas_call(
        _ffn_kernel,
        grid=(m // tm, ff // tf),
        in_specs=[row, wtile, wtile,
                  pl.BlockSpec((tf, d), lambda i, f: (f, 0)), row, pl.BlockSpec((1, d), lambda i, f: (0, 0))],
        out_specs=[row, row],
        out_shape=[jax.ShapeDtypeStruct((m, d), F32), jax.ShapeDtypeStruct((m, d), BF16)],
        compiler_params=_cparams("parallel", "arbitrary"),
        name="ffn",
    )(h, w1, w3, w2, x, g.reshape(1, d))


def _glu_kernel(h_ref, wa_ref, wb_ref, ba_ref, bb_ref, o_ref):
    h = h_ref[...]
    a = _dot(h, wa_ref[...]) + ba_ref[...]
    b = _dot(h, wb_ref[...]) + bb_ref[...]
    o_ref[...] = a * _sigmoid(b)


def _glu_call(h, w, b):
    m, d = h.shape
    n = w.shape[1] // 2
    tm = _tile(m, 1024)
    tn = _tile(n, 512, LANES)
    nj = n // tn
    b2 = b.reshape(1, 2 * n)
    return pl.pallas_call(
        _glu_kernel,
        grid=(m // tm, nj),
        in_specs=[pl.BlockSpec((tm, d), lambda i, j: (i, 0)),
                  pl.BlockSpec((d, tn), lambda i, j: (0, j)), pl.BlockSpec((d, tn), lambda i, j: (0, j + nj)),
                  pl.BlockSpec((1, tn), lambda i, j: (0, j)), pl.BlockSpec((1, tn), lambda i, j: (0, j + nj))],
        out_specs=pl.BlockSpec((tm, tn), lambda i, j: (i, j)),
        out_shape=jax.ShapeDtypeStruct((m, n), F32),
        compiler_params=_cparams("parallel", "arbitrary"),
        name="conv_glu",
    )(h, w, w, b2, b2)


def _dwconv_kernel(u_ref, halo_ref, dw_ref, dwb_ref, lg_ref, lb_ref, o_ref, ext_sc, acc_sc, *, tiles_per_seq):
    ts, d = u_ref.shape
    first = (pl.program_id(0) % tiles_per_seq) == 0
    ext_sc[0, 0:CONV_HALO, :] = jnp.where(first, 0.0, halo_ref[...])
    ext_sc[0, CONV_HALO:, :] = u_ref[...]
    n_shift = ts + CONV_HALO - SUBLANES
    for b in range(1, SUBLANES):
        ext_sc[b, 0:n_shift, :] = ext_sc[0, b:b + n_shift, :]
    rc, cc = 64, 256
    base = CONV_HALO - (CONV_WIDTH - 1)
    for c0 in range(0, d, cc):
        for r0 in range(0, ts, rc):
            acc = jnp.zeros((rc, cc), F32) + dwb_ref[:, c0:c0 + cc]
            for j in range(CONV_WIDTH):
                a8, b = divmod(base + j, SUBLANES)
                rows = slice(r0 + a8 * SUBLANES, r0 + a8 * SUBLANES + rc)
                acc = acc + dw_ref[j:j + 1, c0:c0 + cc] * ext_sc[b, rows, c0:c0 + cc]
            acc_sc[r0:r0 + rc, c0:c0 + cc] = acc
    y = acc_sc[...]
    mu = jnp.mean(y, axis=-1, keepdims=True)
    yc = y - mu
    var = jnp.mean(yc * yc, axis=-1, keepdims=True)
    yn = yc * lax.rsqrt(var + LN_EPS) * lg_ref[...] + lb_ref[...]
    o_ref[...] = _silu(yn).astype(o_ref.dtype)


def _dwconv_call(u, seq, dw_w, dw_b, ln_g, ln_b):
    m, d = u.shape
    ts = _tile(seq, 256, CONV_HALO)
    tiles_per_seq = seq // ts
    dw_pad = jnp.concatenate([dw_w, jnp.zeros((CONV_HALO - CONV_WIDTH, d), F32)], axis=0)
    full = lambda shape: pl.BlockSpec(shape, lambda i: (0, 0))
    return pl.pallas_call(
        functools.partial(_dwconv_kernel, tiles_per_seq=tiles_per_seq),
        grid=(m // ts,),
        in_specs=[pl.BlockSpec((ts, d), lambda i: (i, 0)),
                  pl.BlockSpec((CONV_HALO, d), lambda i: (jnp.maximum(i * (ts // CONV_HALO) - 1, 0), 0)),
                  full((CONV_HALO, d)), full((1, d)), full((1, d)), full((1, d))],
        out_specs=pl.BlockSpec((ts, d), lambda i: (i, 0)),
        out_shape=jax.ShapeDtypeStruct((m, d), BF16),
        scratch_shapes=[pltpu.VMEM((SUBLANES, ts + CONV_HALO, d), F32), pltpu.VMEM((ts, d), F32)],
        compiler_params=_cparams("parallel"),
        name="dwconv",
    )(u, u, dw_pad, dw_b.reshape(1, d), ln_g.reshape(1, d), ln_b.reshape(1, d))


def _route_kernel(lg_ref, tri_ref, o_ref, cnt_ref, carry_sc):
    @pl.when(pl.program_id(0) == 0)
    def _():
        carry_sc[...] = jnp.zeros_like(carry_sc)

    lg = lg_ref[...]
    lane = lax.broadcasted_iota(jnp.int32, lg.shape, 1)
    lg = jnp.where(lane < N_EXPERTS, lg, -jnp.inf)
    m1 = jnp.max(lg, axis=-1, keepdims=True)
    e1 = jnp.min(jnp.where(lg == m1, lane, LANES), axis=-1, keepdims=True)
    lg2 = jnp.where(lane == e1, -jnp.inf, lg)
    m2 = jnp.max(lg2, axis=-1, keepdims=True)
    e2 = jnp.min(jnp.where(lg2 == m2, lane, LANES), axis=-1, keepdims=True)
    ex = jnp.exp(m2 - m1)
    g1 = 1.0 / (1.0 + ex)
    g2 = ex / (1.0 + ex)
    oh1 = (lane == e1).astype(F32)
    oh2 = (lane == e2).astype(F32)
    both = oh1 + oh2
    before = _dot(tri_ref[...], both.astype(BF16)) + carry_sc[...]
    r1 = jnp.sum(before * oh1, axis=-1, keepdims=True)
    r2 = jnp.sum(before * oh2, axis=-1, keepdims=True)
    carry_sc[...] = carry_sc[...] + jnp.sum(both, axis=0, keepdims=True)
    cnt_ref[...] = jnp.broadcast_to(carry_sc[...], cnt_ref.shape)
    out = jnp.where(lane == 0, e1.astype(F32), 0.0)
    out = jnp.where(lane == 1, e2.astype(F32), out)
    out = jnp.where(lane == 2, g1, out)
    out = jnp.where(lane == 3, g2, out)
    out = jnp.where(lane == 4, r1, out)
    out = jnp.where(lane == 5, r2, out)
    o_ref[...] = out


def _route_call(logits):
    m = logits.shape[0]
    tm = _tile(m, 512)
    idx = np.arange(tm)
    tri = jnp.asarray(idx[:, None] > idx[None, :], BF16)
    return pl.pallas_call(
        _route_kernel,
        grid=(m // tm,),
        in_specs=[pl.BlockSpec((tm, LANES), lambda i: (i, 0)), pl.BlockSpec((tm, tm), lambda i: (0, 0))],
        out_specs=[pl.BlockSpec((tm, LANES), lambda i: (i, 0)), pl.BlockSpec((8, LANES), lambda i: (0, 0))],
        out_shape=[jax.ShapeDtypeStruct((m, LANES), F32), jax.ShapeDtypeStruct((8, LANES), F32)],
        scratch_shapes=[pltpu.VMEM((1, LANES), F32)],
        compiler_params=_cparams("arbitrary"),
        name="moe_route",
    )(logits, tri)


def _slab_copy(src_ref, dst_ref, sem, src_slab, dst_row):
    src = src_ref.at[pl.ds(pl.multiple_of(src_slab, SLABS), SLABS)]
    dst = dst_ref.at[pl.ds(pl.multiple_of(dst_row * SLABS, SLABS), SLABS)]
    return pltpu.make_async_copy(src, dst, sem)


def _slab_cols(ref, j, rows):
    return ref[pl.ds(j, rows, stride=SLABS), :]


def _gather_kernel(used_ref, tok_ref, src_ref, o_ref, buf_sc, sem):
    rows = o_ref.shape[0]
    used = used_ref[pl.program_id(0)] > 0

    def start(r2, c):
        for k in range(2):
            r = 2 * r2 + k
            _slab_copy(src_ref, buf_sc, sem, tok_ref[0, 0, r], r).start(priority=k)
        return c

    def wait(r, c):
        _slab_copy(src_ref, buf_sc, sem, 0, r).wait()
        return c

    @pl.when(used)
    def _():
        lax.fori_loop(0, rows // 2, start, 0, unroll=4)
        lax.fori_loop(0, rows, wait, 0, unroll=8)
        for j in range(SLABS):
            o_ref[:, j * LANES:(j + 1) * LANES] = _slab_cols(buf_sc, j, rows).astype(o_ref.dtype)

    @pl.when(jnp.logical_not(used))
    def _():
        o_ref[...] = jnp.zeros_like(o_ref)


def _gather_call(src_slabs, slot_slab, block_used):
    cap = slot_slab.shape[0]
    d = SLABS * LANES
    rows = GATHER_ROWS
    grid_spec = pltpu.PrefetchScalarGridSpec(
        num_scalar_prefetch=1,
        grid=(cap // rows,),
        in_specs=[pl.BlockSpec((1, 1, rows), lambda i, u: (i, 0, 0), memory_space=pltpu.SMEM),
                  pl.BlockSpec(memory_space=pl.ANY)],
        out_specs=pl.BlockSpec((rows, d), lambda i, u: (i, 0)),
        scratch_shapes=[pltpu.VMEM((rows * SLABS, LANES), F32), pltpu.SemaphoreType.DMA(())],
    )
    return pl.pallas_call(
        _gather_kernel,
        grid_spec=grid_spec,
        out_shape=jax.ShapeDtypeStruct((cap, d), BF16),
        compiler_params=_cparams("arbitrary"),
        name="moe_gather",
    )(block_used, slot_slab.reshape(cap // rows, 1, rows), src_slabs)


def _moe_kernel(te_ref, tr_ref, x_ref, w1_ref, w3_ref, w2_ref, o_ref, w2_sc):
    i = pl.program_id(0)
    f = pl.program_id(1)
    nrows = tr_ref[i]
    tm = x_ref.shape[0]

    @pl.when(f == 0)
    def _():
        o_ref[...] = jnp.zeros_like(o_ref)

    def swiglu(rows):
        x = x_ref[rows, :]
        gate = (_silu(_dot(x, w1_ref[0, 0])) * _dot(x, w3_ref[0, 0])).astype(BF16)
        o_ref[rows, :] += _dot(gate, w2_sc[...])

    @pl.when(nrows == tm)
    def _():
        w2_sc[...] = w2_ref[0].astype(BF16)
        swiglu(slice(0, tm))

    @pl.when(jnp.logical_and(nrows > 0, nrows < tm))
    def _():
        w2_sc[...] = w2_ref[0].astype(BF16)
        for s in range(tm // MOE_SUB):
            @pl.when(s * MOE_SUB < nrows)
            def _():
                swiglu(slice(s * MOE_SUB, (s + 1) * MOE_SUB))


def _moe_call(xs, tile_expert, tile_rows, w1t, w3t, w2):
    cap, d = xs.shape
    n_exp, nf, _, tf = w1t.shape
    tm = MOE_TM

    def fidx(i, f, te, tr):
        return jnp.where(tr[i] > 0, f, nf - 1)

    grid_spec = pltpu.PrefetchScalarGridSpec(
        num_scalar_prefetch=2,
        grid=(cap // tm, nf),
        in_specs=[
            pl.BlockSpec((tm, d), lambda i, f, te, tr: (i, 0)),
            pl.BlockSpec((1, 1, d, tf), lambda i, f, te, tr: (te[i], fidx(i, f, te, tr), 0, 0)),
            pl.BlockSpec((1, 1, d, tf), lambda i, f, te, tr: (te[i], fidx(i, f, te, tr), 0, 0)),
            pl.BlockSpec((1, tf, d), lambda i, f, te, tr: (te[i], fidx(i, f, te, tr), 0)),
        ],
        out_specs=pl.BlockSpec((tm, d), lambda i, f, te, tr: (i, 0)),
        scratch_shapes=[pltpu.VMEM((tf, d), BF16)],
    )
    return pl.pallas_call(
        _moe_kernel,
        grid_spec=grid_spec,
        out_shape=jax.ShapeDtypeStruct((cap, d), F32),
        compiler_params=_cparams("parallel", "arbitrary"),
        name="moe_experts",
    )(tile_expert, tile_rows, xs, w1t, w3t, w2)


def _row_copy(src_ref, dst_ref, sem, src_row, dst_row):
    return pltpu.make_async_copy(src_ref.at[pl.ds(src_row, 1)], dst_ref.at[pl.ds(dst_row, 1)], sem)


def _combine_kernel(pos_ref, y_ref, x_ref, rt_ref, g_ref, o_ref, buf_sc, sem):
    rows = o_ref.shape[0]

    def start(r, c):
        _row_copy(y_ref, buf_sc.at[0], sem, pos_ref[0, 0, r], r).start(priority=0)
        _row_copy(y_ref, buf_sc.at[1], sem, pos_ref[0, 1, r], r).start(priority=1)
        return c

    def wait(r, c):
        _row_copy(y_ref, buf_sc.at[0], sem, 0, r).wait()
        _row_copy(y_ref, buf_sc.at[1], sem, 0, r).wait()
        return c

    lax.fori_loop(0, rows, start, 0, unroll=8)
    lax.fori_loop(0, rows, wait, 0, unroll=8)
    rt = rt_ref[...]
    x = x_ref[...] + rt[:, 2:3] * buf_sc[0] + rt[:, 3:4] * buf_sc[1]
    o_ref[...] = _rms(x, g_ref[...])


def _combine_call(pos, yb, x, route, g):
    m, d = x.shape
    rows = GATHER_ROWS if m % GATHER_ROWS == 0 else m
    return pl.pallas_call(
        _combine_kernel,
        grid=(m // rows,),
        in_specs=[pl.BlockSpec((1, 2, rows), lambda i: (i, 0, 0), memory_space=pltpu.SMEM),
                  pl.BlockSpec(memory_space=pl.ANY),
                  pl.BlockSpec((rows, d), lambda i: (i, 0)),
                  pl.BlockSpec((rows, LANES), lambda i: (i, 0)),
                  pl.BlockSpec((1, d), lambda i: (0, 0))],
        out_specs=pl.BlockSpec((rows, d), lambda i: (i, 0)),
        out_shape=jax.ShapeDtypeStruct((m, d), F32),
        scratch_shapes=[pltpu.VMEM((2, rows, d), F32), pltpu.SemaphoreType.DMA(())],
        compiler_params=_cparams("arbitrary"),
        name="moe_combine",
    )(pos.reshape(m // rows, rows, 2).transpose(0, 2, 1), yb, x, route, g.reshape(1, d))


def _rot_cols(w):
    half = QK_ROPE // 2
    return jnp.concatenate([-w[..., half:], w[..., :half]], axis=-1)


def _mixer_weights(w_in, w_up, a_up, w_q_up):
    d = RWKV_DIM
    w_rwkv = w_in[:, :RWKV_IN]
    w_lat = w_in[:, RWKV_IN:RWKV_IN + Q_LORA + KV_LORA]
    w_kr = w_in[:, RWKV_IN + Q_LORA + KV_LORA:]
    w_mla = jnp.concatenate([w_lat, w_kr, _rot_cols(w_kr)], axis=1)
    lora = jnp.zeros((W_LORA + A_LORA, 2 * d), F32)
    lora = lora.at[:W_LORA, :d].set(w_up).at[W_LORA:, d:].set(a_up)
    wq = w_q_up.reshape(Q_LORA, MLA_HEADS, QK_NOPE + QK_ROPE)
    wq_rope = wq[..., QK_NOPE:]
    wq = jnp.concatenate([wq, _rot_cols(wq_rope)], axis=-1).reshape(Q_LORA, MLA_HEADS * QK_CAT)
    return w_rwkv.astype(BF16), w_mla.astype(BF16), lora, wq.astype(BF16)


def kernel(x, positions, l0_mix_norm, l0_w_in, l0_shift_mu, l0_w0, l0_w_up, l0_a0, l0_a_up, l0_g_up, l0_k_k, l0_k_a, l0_r_k, l0_gn_w, l0_gn_b, l0_q_norm, l0_w_q_up, l0_kv_norm, l0_w_kv_up, l0_w_out, l0_ffn_norm, l0_ffn_w1, l0_ffn_w3, l0_ffn_w2, l1_mix_norm, l1_pw1_w, l1_pw1_b, l1_dw_w, l1_dw_b, l1_ln_g, l1_ln_b, l1_pw2_w, l1_pw2_b, l1_ffn_norm, l1_router, l1_exp_w1, l1_exp_w3, l1_exp_w2, final_norm):
    batch, seq, d = x.shape
    m = batch * seq
    x0 = x.reshape(m, d)
    pos = positions.reshape(m, 1)
    inv = ROPE_THETA ** (-jnp.arange(0, QK_ROPE, 2, dtype=F32) / QK_ROPE)
    inv_tab = jnp.tile(inv, LANES // inv.shape[0]).reshape(1, LANES)

    w_rwkv, w_mla, lora_w, wq = _mixer_weights(l0_w_in, l0_w_up, l0_a_up, l0_w_q_up)

    hn0 = _rmsnorm_call(x0, l0_mix_norm)
    p_rwkv = _matmul_call(hn0, w_rwkv, name="in_proj_rwkv")
    p_mla = _matmul_call(hn0, w_mla, name="in_proj_mla")

    prep = _rwkv_prep_call(p_rwkv, seq, l0_shift_mu, lora_w, l0_w0, l0_a0, l0_g_up, l0_k_k, l0_k_a,
                           l0_r_k.reshape(-1))
    at, bt, kt, rt, v, bh, kh, gc, bonus, gate = prep
    ra, o2, pm, qm = _rwkv_intra_call(at, bt, kt, rt, v, bh, kh, gc, seq)
    y_rwkv = _rwkv_seq_call(ra, o2, pm, qm, bonus, gate, l0_gn_w, l0_gn_b, batch, seq)

    q, k, vv = _mla_proj_calls(p_mla, pos, inv_tab, l0_q_norm, wq, l0_kv_norm, l0_w_kv_up.astype(BF16),
                               batch, seq)
    y_mla, expert_w = _attn_call(q, k, vv, batch, seq, (l1_exp_w1, l1_exp_w3))

    x1, hn1 = _mix_out_call(y_rwkv, y_mla, l0_w_out.astype(BF16), x0, l0_ffn_norm)

    x2, hn2 = _ffn_call(hn1, l0_ffn_w1.astype(BF16), l0_ffn_w3.astype(BF16), l0_ffn_w2.astype(BF16),
                        x1, l1_mix_norm)

    u = _glu_call(hn2, l1_pw1_w.astype(BF16), l1_pw1_b)
    sc = _dwconv_call(u, seq, l1_dw_w, l1_dw_b, l1_ln_g, l1_ln_b)
    router_pad = jnp.zeros((d, LANES), F32).at[:, :N_EXPERTS].set(l1_router)
    x3, hn3, logits = _conv_out_call(sc, l1_pw2_w.astype(BF16), l1_pw2_b, x2, l1_ffn_norm, router_pad)

    route, counts = _route_call(logits)
    e = route[:, 0:2].astype(jnp.int32)
    rank = route[:, 4:6].astype(jnp.int32)
    cnt = counts[0, :N_EXPERTS].astype(jnp.int32)
    padded = (cnt + MOE_TM - 1) // MOE_TM * MOE_TM
    pad_end = jnp.cumsum(padded)
    pad_start = pad_end - padded
    pos_slot = pad_start[e] + rank
    cap = (2 * m + N_EXPERTS * MOE_TM + MOE_TM - 1) // MOE_TM * MOE_TM
    tok = jnp.broadcast_to(jnp.arange(m, dtype=jnp.int32)[:, None], (m, 2))
    slot_slab = jnp.zeros((cap,), jnp.int32).at[pos_slot.reshape(-1)].set(tok.reshape(-1) * SLABS)
    tile_start = jnp.arange(cap // MOE_TM, dtype=jnp.int32) * MOE_TM
    tile_expert = jnp.minimum(jnp.sum(tile_start[:, None] >= pad_end[None, :], axis=1), N_EXPERTS - 1)
    used = tile_start < pad_end[-1]
    last_used = jnp.max(jnp.where(used, tile_expert, 0))
    tile_expert = jnp.where(used, tile_expert, last_used).astype(jnp.int32)
    tile_rows = jnp.clip(pad_start[tile_expert] + cnt[tile_expert] - tile_start, 0, MOE_TM)
    tile_rows = jnp.where(used, tile_rows, 0).astype(jnp.int32)

    per_tile = MOE_TM // GATHER_ROWS
    block_off = jnp.tile(jnp.arange(per_tile, dtype=jnp.int32) * GATHER_ROWS, cap // MOE_TM)
    block_used = (jnp.repeat(tile_rows, per_tile) > block_off).astype(jnp.int32)
    xs = _gather_call(hn3, slot_slab, block_used)
    yb = _moe_call(xs, tile_expert, tile_rows, *expert_w, l1_exp_w2)
    out = _combine_call(pos_slot, yb, x3, route, final_norm)
    return out.reshape(batch, seq, d)
```

```python
import functools

import numpy as np
import jax
import jax.numpy as jnp
from jax import lax
from jax.experimental import pallas as pl
from jax.experimental.pallas import tpu as pltpu

F32 = jnp.float32
BF16 = jnp.bfloat16

D_MODEL = 2048
CHUNK = 64
HEAD = 64
N_HEADS = 16
RWKV_DIM = N_HEADS * HEAD
W_LORA, A_LORA, G_LORA = 64, 64, 128
RWKV_IN = 3 * RWKV_DIM + W_LORA + A_LORA + G_LORA
MLA_HEADS = 8
Q_LORA = KV_LORA = 512
QK_NOPE, QK_ROPE, V_HEAD = 128, 64, 128
QK_CAT = 2 * QK_NOPE
V_EXT = 2 * V_HEAD
ROPE_THETA = 10000.0
CONV_WIDTH = 31
CONV_HALO = 32
N_EXPERTS = 8
NORM_EPS = 1e-6
LN_EPS = 1e-5
GN_EPS = HEAD * 1e-5
NEG_BIG = -1e30

LANES = 128
SUBLANES = 8
VMEM_LIMIT = 56 * 1024 * 1024

HEADS_PER_GROUP = 2
GROUP_W = HEADS_PER_GROUP * HEAD
MOE_TM = 1024
MOE_SUB = 512
MOE_TF = 512
EXPERT_CAST_BLOCKS = 512
EXPERT_CAST_MAX_ROWS = 64
GATHER_ROWS = 256
SLABS = D_MODEL // LANES
ATTN_HEADS_PER_STEP = 4
LOG2_E = 1.4426950408889634


def _cparams(*sem):
    return pltpu.CompilerParams(dimension_semantics=sem, vmem_limit_bytes=VMEM_LIMIT)


def _tile(n, pref, align=8):
    if n <= pref:
        return n
    t = (pref // align) * align
    while t > align and n % t:
        t -= align
    assert n % t == 0, (n, pref)
    return t


def _dot(a, b):
    return jnp.dot(a, b, preferred_element_type=F32)


def _dot_nt(a, b):
    return lax.dot_general(a, b, (((1,), (1,)), ((), ())), preferred_element_type=F32)


def _split2(x):
    hi = x.astype(BF16)
    lo = (x - hi.astype(F32)).astype(BF16)
    return hi, lo


def _dot_lx(a, b_exact):
    hi, lo = _split2(a)
    return _dot(hi, b_exact) + _dot(lo, b_exact)


def _dot_hp(a, b):
    ah, al = _split2(a)
    bh, bl = _split2(b)
    return _dot(ah, bh) + (_dot(ah, bl) + _dot(al, bh))


def _sigmoid(x):
    return 1.0 / (1.0 + jnp.exp(-x))


def _silu(x):
    return x * _sigmoid(x)


def _softplus(x):
    return jnp.maximum(x, 0.0) + jnp.log(1.0 + jnp.exp(-jnp.abs(x)))


def _rms(x, g):
    return x * lax.rsqrt(jnp.mean(x * x, axis=-1, keepdims=True) + NORM_EPS) * g


def _rmsnorm_kernel(x_ref, g_ref, o_ref):
    o_ref[...] = _rms(x_ref[...], g_ref[...]).astype(o_ref.dtype)


def _rmsnorm_call(x, g):
    m, d = x.shape
    tm = _tile(m, 512)
    return pl.pallas_call(
        _rmsnorm_kernel,
        grid=(m // tm,),
        in_specs=[pl.BlockSpec((tm, d), lambda i: (i, 0)), pl.BlockSpec((1, d), lambda i: (0, 0))],
        out_specs=pl.BlockSpec((tm, d), lambda i: (i, 0)),
        out_shape=jax.ShapeDtypeStruct((m, d), BF16),
        compiler_params=_cparams("parallel"),
        name="rmsnorm",
    )(x, g.reshape(1, d))


def _matmul_kernel(a_ref, w_ref, o_ref):
    o_ref[...] = _dot(a_ref[...], w_ref[...])


def _matmul_call(a, w, tm_pref=1024, tn_pref=1664, name="matmul"):
    m, k = a.shape
    n = w.shape[1]
    tm = _tile(m, tm_pref)
    tn = _tile(n, tn_pref, LANES)
    return pl.pallas_call(
        _matmul_kernel,
        grid=(m // tm, n // tn),
        in_specs=[pl.BlockSpec((tm, k), lambda i, j: (i, 0)), pl.BlockSpec((k, tn), lambda i, j: (0, j))],
        out_specs=pl.BlockSpec((tm, tn), lambda i, j: (i, j)),
        out_shape=jax.ShapeDtypeStruct((m, n), F32),
        compiler_params=_cparams("parallel", "arbitrary"),
        name=name,
    )(a, w)


def _rwkv_prep_kernel(p_ref, prev_ref, mu_ref, lw_ref, w0_ref, a0_ref, gup_ref, kk_ref, ka_ref, rk_ref,
                      tril_ref, ones_ref, seg_ref, segt_ref,
                      at_ref, bt_ref, kt_ref, rt_ref, v_ref, bh_ref, kh_ref, gc_ref, bonus_ref, g_ref,
                      *, tiles_per_seq):
    ts = p_ref.shape[0]
    first = (pl.program_id(0) % tiles_per_seq) == 0
    row0 = lax.broadcasted_iota(jnp.int32, (ts, 1), 0) == 0

    def mixed(c0, c1):
        pc = p_ref[:, c0:c1]
        prev = jnp.where(first, 0.0, prev_ref[7:8, c0:c1])
        sh = jnp.where(row0, prev, pltpu.roll(pc, 1, axis=0))
        return pc + (sh - pc) * mu_ref[:, c0:c1]

    d = RWKV_DIM
    r = mixed(0, d)
    k = mixed(d, 2 * d)
    v = mixed(2 * d, 3 * d)
    xwa = mixed(3 * d, 3 * d + W_LORA + A_LORA)
    xg = mixed(3 * d + W_LORA + A_LORA, RWKV_IN)

    lane = lax.broadcasted_iota(jnp.int32, xwa.shape, 1)
    z = jnp.where(lane < W_LORA, jnp.tanh(xwa), xwa)
    wa = _dot_hp(z, lw_ref[...])
    w = -_softplus(-(w0_ref[...] + wa[:, :d])) - 0.5
    logdecay = -jnp.exp(w)
    a = _sigmoid(a0_ref[...] + wa[:, d:])
    g_ref[...] = _dot_hp(_sigmoid(xg), gup_ref[...])

    seg = seg_ref[...]
    segt = segt_ref[...]

    def head_sum(x):
        return _dot_lx(_dot_lx(x, seg), segt)

    kkr = k * kk_ref[...]
    kk = kkr / jnp.maximum(jnp.sqrt(head_sum(kkr * kkr)), 1e-12)
    k2 = k * (1.0 + (a - 1.0) * ka_ref[...])
    bonus_ref[...] = head_sum(r * k2 * rk_ref[...]) * v

    ld_hi, ld_lo = _split2(logdecay)
    tril = tril_ref[...]
    ones = ones_ref[...]
    gcum = _dot(tril, ld_hi) + _dot(tril, ld_lo)
    gtot = _dot(ones, ld_hi) + _dot(ones, ld_lo)
    g_in = jnp.exp(gcum)
    g_ex = jnp.exp(gcum - logdecay)
    g_inv = jnp.exp(-gcum)
    g_rest = jnp.exp(gtot - gcum)
    beta = kk * a
    at_ref[...] = (-kk * g_ex).astype(at_ref.dtype)
    bt_ref[...] = (beta * g_inv).astype(bt_ref.dtype)
    kt_ref[...] = (k2 * g_inv).astype(kt_ref.dtype)
    rt_ref[...] = (r * g_in).astype(rt_ref.dtype)
    v_ref[...] = v.astype(v_ref.dtype)
    bh_ref[...] = beta * g_rest
    kh_ref[...] = k2 * g_rest
    gc_ref[...] = jnp.exp(gtot)


def _rwkv_prep_call(p, seq, mu, lora_w, w0, a0, g_up, k_k, k_a, r_k):
    m = p.shape[0]
    ts = _tile(seq, 256, CHUNK)
    tiles_per_seq = seq // ts
    d = RWKV_DIM
    idx = np.arange(ts)
    same = (idx[:, None] // CHUNK) == (idx[None, :] // CHUNK)
    tril = jnp.asarray(same & (idx[:, None] >= idx[None, :]), BF16)
    ones = jnp.asarray(same, BF16)
    lane = np.arange(d)
    seg_np = (lane[:, None] // HEAD) == np.arange(LANES)[None, :]
    seg = jnp.asarray(seg_np, BF16)
    segt = jnp.asarray(seg_np.T, BF16)

    def row(x):
        return x.reshape(1, -1).astype(F32)

    full = lambda shape: pl.BlockSpec(shape, lambda i: (0, 0))
    tok = pl.BlockSpec((ts, d), lambda i: (i, 0))
    outs = pl.pallas_call(
        functools.partial(_rwkv_prep_kernel, tiles_per_seq=tiles_per_seq),
        grid=(m // ts,),
        in_specs=[
            pl.BlockSpec((ts, RWKV_IN), lambda i: (i, 0)),
            pl.BlockSpec((8, RWKV_IN), lambda i: (jnp.maximum(i * (ts // 8) - 1, 0), 0)),
            full((1, RWKV_IN)), full((W_LORA + A_LORA, 2 * d)), full((1, d)), full((1, d)),
            full((G_LORA, d)), full((1, d)), full((1, d)), full((1, d)),
            full((ts, ts)), full((ts, ts)), full((d, LANES)), full((LANES, d)),
        ],
        out_specs=[tok] * 10,
        out_shape=[jax.ShapeDtypeStruct((m, d), BF16)] * 5 + [jax.ShapeDtypeStruct((m, d), F32)] * 5,
        compiler_params=_cparams("parallel"),
        name="rwkv_prep",
    )(p, p, row(mu), lora_w, row(w0), row(a0), g_up, row(k_k), row(k_a), row(r_k), tril, ones, seg, segt)
    return outs


def _stack_heads(x):
    lane_head = lax.broadcasted_iota(jnp.int32, x.shape, 1) // HEAD
    return jnp.concatenate([jnp.where(lane_head == h, x, 0.0) for h in range(HEADS_PER_GROUP)], axis=0)


def _unstack_heads(x):
    out = x[0:CHUNK]
    for h in range(1, HEADS_PER_GROUP):
        out = out + x[h * CHUNK:(h + 1) * CHUNK]
    return out


def _rwkv_intra_kernel(at_ref, bt_ref, kt_ref, rt_ref, v_ref, bh_ref, kh_ref, gc_ref,
                       ra_ref, o2_ref, p_ref, q_ref, *, mm):
    w = GROUP_W
    n_chunks = at_ref.shape[0] // CHUNK
    ri = lax.broadcasted_iota(jnp.int32, (w, w), 0)
    ci = lax.broadcasted_iota(jnp.int32, (w, w), 1)
    strict = (ri % CHUNK) > (ci % CHUNK)
    incl = (ri % CHUNK) >= (ci % CHUNK)
    blk16 = (ri // 16) == (ci // 16)
    eye = (ri == ci).astype(F32)

    chunks = [slice(c * CHUNK, (c + 1) * CHUNK) for c in range(n_chunks)]

    def each(f, *lists):
        return [f(*args) for args in zip(*lists)]

    def stacked(ref):
        return [_stack_heads(ref[rows, :]) for rows in chunks]

    a_s, r_s, b_s, k_s, v_s = stacked(at_ref), stacked(rt_ref), stacked(bt_ref), stacked(kt_ref), stacked(v_ref)
    s = each(lambda a, r, b, k: mm(jnp.concatenate([a, r], axis=0), jnp.concatenate([b, k], axis=0), nt=True),
             a_s, r_s, b_s, k_s)
    l_ab = [jnp.where(strict, x[:w, :w], 0.0) for x in s]
    l_ak = [jnp.where(strict, x[:w, w:], 0.0) for x in s]
    m_rb = [jnp.where(incl, x[w:, :w], 0.0) for x in s]
    m_rk = [jnp.where(incl, x[w:, w:], 0.0) for x in s]

    def mm2(a, b0, b1):
        out = mm(a, jnp.concatenate([b0, b1], axis=1))
        return out[:, :w], out[:, w:]

    dg = [jnp.where(blk16, x, 0.0) for x in l_ab]
    off = each(lambda x, y: x - y, l_ab, dg)
    td = [eye + x for x in dg]
    pw = each(mm, dg, dg)
    for _ in range(2):
        sq_ptd = each(mm2, pw, pw, td)
        td = each(lambda t_, x_: t_ + x_[1], td, sq_ptd)
        pw = [x_[0] for x_ in sq_ptd]
    td = each(lambda t_, p_: t_ + mm(p_, t_), td, pw)
    n1 = each(mm, td, off)
    sq_ntd = each(mm2, n1, n1, td)
    t1 = each(lambda t_, x_: t_ + x_[1], td, sq_ntd)
    t = each(lambda t_, x_: t_ + mm(x_[0], t_), t1, sq_ntd)

    lv = each(mm, l_ak, v_s)
    y = each(lambda tt, x, a: mm(tt, jnp.concatenate([x, a], axis=1)), t, lv, a_s)
    z = each(mm, m_rb, y)
    mv = each(mm, m_rk, v_s)
    z2 = each(lambda ref_rows, yy: mm(_stack_heads(bh_ref[ref_rows, :]).T, yy), chunks, y)
    kv = each(lambda ref_rows, vv: mm(_stack_heads(kh_ref[ref_rows, :]).T, vv), chunks, v_s)
    for c, rows in enumerate(chunks):
        ra_ref[rows, :] = _unstack_heads(r_s[c] + z[c][:, w:]).astype(ra_ref.dtype)
        o2_ref[rows, :] = _unstack_heads(z[c][:, :w] + mv[c])
        p_ref[rows, :] = _unstack_heads(eye * gc_ref[c * CHUNK:c * CHUNK + 1, :]
                                        + z2[c][:, w:]).astype(p_ref.dtype)
        q_ref[rows, :] = _unstack_heads(z2[c][:, :w] + kv[c])


def _mm_bf16(a, b, nt=False):
    a = a.astype(BF16)
    b = b.astype(BF16)
    return _dot_nt(a, b) if nt else _dot(a, b)


def _rwkv_intra_cast_kernel(at_ref, bt_ref, kt_ref, rt_ref, v_ref, bh_ref, kh_ref, gc_ref, w_ref,
                            ra_ref, o2_ref, p_ref, q_ref, wo_ref, *, mm):
    _cast_expert_block(w_ref, wo_ref)
    _rwkv_intra_kernel(at_ref, bt_ref, kt_ref, rt_ref, v_ref, bh_ref, kh_ref, gc_ref,
                       ra_ref, o2_ref, p_ref, q_ref, mm=mm)


def _rwkv_intra_call(at, bt, kt, rt, v, bh, kh, gc, seq, expert_w):
    m, d = at.shape
    ts = _tile(seq, 512, CHUNK)
    n_groups = d // GROUP_W
    steps = (m // ts) * n_groups
    spec = pl.BlockSpec((ts, GROUP_W), lambda i, j: (i, j))
    in_specs, out_specs = [spec] * 8, [spec] * 4
    out_shape = [jax.ShapeDtypeStruct((m, d), dt) for dt in (BF16, F32, BF16, F32)]
    args = (at, bt, kt, rt, v, bh, kh, gc)
    rows_total = expert_w.shape[0] * expert_w.shape[1]
    fused = steps * EXPERT_CAST_MAX_ROWS >= rows_total and rows_total % steps == 0
    if fused:
        c_in, c_out, c_shape, c_arg = _expert_cast_specs(expert_w, steps, lambda i, j: i * n_groups + j)
        in_specs, out_specs, out_shape, args = in_specs + [c_in], out_specs + [c_out], out_shape + [c_shape], args + (c_arg,)
    outs = pl.pallas_call(
        functools.partial(_rwkv_intra_cast_kernel if fused else _rwkv_intra_kernel, mm=_mm_bf16),
        grid=(m // ts, n_groups),
        in_specs=in_specs,
        out_specs=out_specs,
        out_shape=out_shape,
        compiler_params=_cparams("arbitrary", "arbitrary"),
        name="rwkv_intra",
    )(*args)
    if fused:
        return outs[:4], outs[4]
    return outs, _expert_cast_call(expert_w)


def _rwkv_seq_kernel(ra_ref, o2_ref, p_ref, q_ref, bonus_ref, g_ref, gnw_ref, gnb_ref, seg_ref, segt_ref,
                     o_ref, h_sc, y_sc, *, mm):
    w = GROUP_W
    n_groups = RWKV_DIM // w
    n_chunks = ra_ref.shape[0] // CHUNK

    @pl.when(pl.program_id(1) == 0)
    def _():
        h_sc[...] = jnp.zeros_like(h_sc)

    def chunk(c, carry):
        r0 = pl.multiple_of(c * CHUNK, CHUNK)
        rows = pl.ds(r0, CHUNK)
        groups = [slice(gi * w, (gi + 1) * w) for gi in range(n_groups)]
        lhs = [jnp.concatenate([ra_ref[rows, cols], _stack_heads(p_ref[rows, cols])], axis=0) for cols in groups]
        res = [mm(lhs[gi], h_sc[gi]) for gi in range(n_groups)]
        for gi, cols in enumerate(groups):
            y_sc[rows, cols] = res[gi][:CHUNK] + o2_ref[rows, cols]
            h_sc[gi] = res[gi][CHUNK:] + _stack_heads(q_ref[rows, cols])
        return carry

    lax.fori_loop(0, n_chunks, chunk, 0)

    y = y_sc[...]
    seg = seg_ref[...]
    segt = segt_ref[...]
    mean = _dot_lx(_dot_lx(y, seg), segt) * (1.0 / HEAD)
    yc = y - mean
    var = _dot_lx(_dot_lx(yc * yc, seg), segt) * (1.0 / HEAD)
    yn = yc * lax.rsqrt(var + GN_EPS) * gnw_ref[...] + gnb_ref[...]
    o_ref[...] = ((yn + bonus_ref[...]) * g_ref[...]).astype(o_ref.dtype)


def _rwkv_seq_call(ra, o2, p, q, bonus, g, gn_w, gn_b, batch, seq):
    m, d = ra.shape
    ts = _tile(seq, 512, CHUNK)
    nt = seq // ts
    lane = np.arange(d)
    seg_np = (lane[:, None] // HEAD) == np.arange(LANES)[None, :]
    seg = jnp.asarray(seg_np, BF16)
    segt = jnp.asarray(seg_np.T, BF16)
    tok = pl.BlockSpec((ts, d), lambda b, i: (b * nt + i, 0))
    full = lambda shape: pl.BlockSpec(shape, lambda b, i: (0, 0))
    return pl.pallas_call(
        functools.partial(_rwkv_seq_kernel, mm=_mm_bf16),
        grid=(batch, nt),
        in_specs=[tok] * 6 + [full((1, d)), full((1, d)), full((d, LANES)), full((LANES, d))],
        out_specs=tok,
        out_shape=jax.ShapeDtypeStruct((m, d), BF16),
        scratch_shapes=[pltpu.VMEM((d // GROUP_W, GROUP_W, GROUP_W), F32), pltpu.VMEM((ts, d), F32)],
        compiler_params=_cparams("arbitrary", "arbitrary"),
        name="rwkv_seq",
    )(ra, o2, p, q, bonus, g, gn_w.reshape(1, d), gn_b.reshape(1, d), seg, segt)


def _rope_tab(pos_ref, inv_ref):
    ang = pos_ref[...].astype(F32) * inv_ref[...]
    lane = lax.broadcasted_iota(jnp.int32, ang.shape, 1)
    return jnp.where(lane < QK_ROPE, jnp.cos(ang), jnp.sin(ang))


def _mla_q_kernel(lat_ref, pos_ref, inv_ref, g_ref, w_ref, o_ref, *, scale):
    hn = _rms(lat_ref[...], g_ref[...]).astype(BF16)
    q = _dot(hn, w_ref[...]) * scale
    tab = _rope_tab(pos_ref, inv_ref)
    for h in range(MLA_HEADS):
        c0 = h * QK_CAT
        o_ref[0, h, :, 0:QK_NOPE] = q[:, c0:c0 + QK_NOPE].astype(BF16)
        o_ref[0, h, :, QK_NOPE:QK_CAT] = (q[:, c0 + QK_NOPE:c0 + QK_CAT] * tab).astype(BF16)


def _mla_kv_kernel(lat_ref, kr_ref, pos_ref, inv_ref, g_ref, w_ref, k_ref, v_ref):
    hn = _rms(lat_ref[...], g_ref[...]).astype(BF16)
    kv = _dot(hn, w_ref[...])
    t = kr_ref[...] * _rope_tab(pos_ref, inv_ref)
    k_rope = (t + pltpu.roll(t, QK_ROPE, axis=1)).astype(BF16)
    for h in range(MLA_HEADS):
        c0 = h * (QK_NOPE + V_HEAD)
        k_ref[0, h, :, 0:QK_NOPE] = kv[:, c0:c0 + QK_NOPE].astype(BF16)
        k_ref[0, h, :, QK_NOPE:QK_CAT] = k_rope
        v_ref[0, h, :, 0:V_HEAD] = kv[:, c0 + QK_NOPE:c0 + QK_NOPE + V_HEAD].astype(BF16)
        v_ref[0, h, :, V_HEAD:V_EXT] = jnp.ones((kv.shape[0], V_EXT - V_HEAD), BF16)


def _mla_proj_calls(p_mla, pos, inv_tab, q_norm, wq, kv_norm, wkv, batch, seq):
    tm = _tile(seq, 512)
    nt = seq // tm
    scale = float((QK_NOPE + QK_ROPE) ** -0.5) * LOG2_E
    lat = lambda c: pl.BlockSpec((tm, Q_LORA), lambda b, i: (b * nt + i, c))
    posspec = pl.BlockSpec((tm, 1), lambda b, i: (b * nt + i, 0))
    full = lambda shape: pl.BlockSpec(shape, lambda b, i: (0, 0))
    headed = lambda w: pl.BlockSpec((1, MLA_HEADS, tm, w), lambda b, i: (b, 0, i, 0))
    q = pl.pallas_call(
        functools.partial(_mla_q_kernel, scale=scale),
        grid=(batch, nt),
        in_specs=[lat(0), posspec, full((1, LANES)), full((1, Q_LORA)), full((Q_LORA, MLA_HEADS * QK_CAT))],
        out_specs=headed(QK_CAT),
        out_shape=jax.ShapeDtypeStruct((batch, MLA_HEADS, seq, QK_CAT), BF16),
        compiler_params=_cparams("parallel", "parallel"),
        name="mla_q",
    )(p_mla, pos, inv_tab, q_norm.reshape(1, -1), wq)
    k, v = pl.pallas_call(
        _mla_kv_kernel,
        grid=(batch, nt),
        in_specs=[lat(1), pl.BlockSpec((tm, LANES), lambda b, i: (b * nt + i, 2 * Q_LORA // LANES)),
                  posspec, full((1, LANES)), full((1, KV_LORA)),
                  full((KV_LORA, MLA_HEADS * (QK_NOPE + V_HEAD)))],
        out_specs=[headed(QK_CAT), headed(V_EXT)],
        out_shape=[jax.ShapeDtypeStruct((batch, MLA_HEADS, seq, QK_CAT), BF16),
                   jax.ShapeDtypeStruct((batch, MLA_HEADS, seq, V_EXT), BF16)],
        compiler_params=_cparams("parallel", "parallel"),
        name="mla_kv",
    )(p_mla, p_mla, pos, inv_tab, kv_norm.reshape(1, -1), wkv)
    return q, k, v


def _cast_expert_block(w_ref, wo_ref):
    for f in range(wo_ref.shape[1]):
        wo_ref[0, f] = w_ref[:, f * MOE_TF:(f + 1) * MOE_TF].astype(BF16)


def _expert_cast_specs(w, n_blocks, block_of):
    n_exp, d, ff = w.shape
    nf = ff // MOE_TF
    rows = n_exp * d // n_blocks
    per_exp = d // rows
    in_spec = pl.BlockSpec((rows, ff), lambda *a: (block_of(*a), 0))
    out_spec = pl.BlockSpec((1, nf, rows, MOE_TF),
                            lambda *a: (block_of(*a) // per_exp, 0, block_of(*a) % per_exp, 0))
    out_shape = jax.ShapeDtypeStruct((n_exp, nf, d, MOE_TF), BF16)
    return in_spec, out_spec, out_shape, w.reshape(n_exp * d, ff)


def _expert_cast_call(w):
    in_spec, out_spec, out_shape, arg = _expert_cast_specs(w, EXPERT_CAST_BLOCKS, lambda i: i)
    return pl.pallas_call(
        _cast_expert_block,
        grid=(EXPERT_CAST_BLOCKS,),
        in_specs=[in_spec],
        out_specs=out_spec,
        out_shape=out_shape,
        compiler_params=_cparams("parallel"),
        name="expert_cast",
    )(arg)


def _attn_cast_kernel(qi_ref, kj_ref, q_ref, k_ref, v_ref, w_ref, o_ref, wo_ref, m_sc, acc_sc):
    _cast_expert_block(w_ref, wo_ref)
    _attn_kernel(qi_ref, kj_ref, q_ref, k_ref, v_ref, o_ref, m_sc, acc_sc)


def _attn_kernel(qi_ref, kj_ref, q_ref, k_ref, v_ref, o_ref, m_sc, acc_sc):
    t = pl.program_id(2)
    qi = qi_ref[t]
    kj = kj_ref[t]
    n_heads = q_ref.shape[1]
    tk = k_ref.shape[2]

    @pl.when(kj == 0)
    def _():
        m_sc[...] = jnp.full_like(m_sc, NEG_BIG)
        acc_sc[...] = jnp.zeros_like(acc_sc)

    def update(heads, mask):
        s = [_dot_nt(q_ref[0, h], k_ref[0, h]) for h in heads]
        if mask:
            ri = lax.broadcasted_iota(jnp.int32, s[0].shape, 0) // CHUNK
            ci = lax.broadcasted_iota(jnp.int32, s[0].shape, 1) // CHUNK
            s = [jnp.where(ci <= ri, x, NEG_BIG) for x in s]
        for i, h in enumerate(heads):
            m_prev = m_sc[h]
            m_new = jnp.maximum(m_prev, jnp.max(s[i], axis=-1, keepdims=True))
            alpha = jnp.exp2(m_prev - m_new)
            p = jnp.exp2(s[i] - jnp.concatenate([m_new] * (tk // LANES), axis=1))
            pv = _dot(p.astype(BF16), v_ref[0, h])
            acc_sc[h] = jnp.concatenate([alpha] * (V_EXT // LANES), axis=1) * acc_sc[h] + pv
            m_sc[h] = m_new

    pairs = [tuple(range(h, min(h + 2, n_heads))) for h in range(0, n_heads, 2)]

    @pl.when(kj < qi)
    def _():
        for hp in pairs:
            update(hp, False)

    @pl.when(kj == qi)
    def _():
        for hp in pairs:
            update(hp, True)
        for h in range(n_heads):
            acc = acc_sc[h]
            o_ref[:, h * V_HEAD:(h + 1) * V_HEAD] = (acc[:, :V_HEAD] / acc[:, V_HEAD:]).astype(o_ref.dtype)


def _attn_call(q, k, v, batch, seq, expert_w):
    tq = _tile(seq, 512, CHUNK)
    nq = seq // tq
    hg = ATTN_HEADS_PER_STEP
    n_hg = MLA_HEADS // hg
    pairs = [(i, j) for i in range(nq) for j in range(i + 1)]
    qi = jnp.asarray([p[0] for p in pairs], jnp.int32)
    kj = jnp.asarray([p[1] for p in pairs], jnp.int32)
    in_specs = [
        pl.BlockSpec((1, hg, tq, QK_CAT), lambda b, h, t, qi, kj: (b, h, qi[t], 0)),
        pl.BlockSpec((1, hg, tq, QK_CAT), lambda b, h, t, qi, kj: (b, h, kj[t], 0)),
        pl.BlockSpec((1, hg, tq, V_EXT), lambda b, h, t, qi, kj: (b, h, kj[t], 0)),
    ]
    out_specs = [pl.BlockSpec((tq, hg * V_HEAD), lambda b, h, t, qi, kj: (b * nq + qi[t], h))]
    out_shape = [jax.ShapeDtypeStruct((batch * seq, MLA_HEADS * V_HEAD), BF16)]
    args = (qi, kj, q, k, v)
    fused = batch * n_hg * len(pairs) >= EXPERT_CAST_BLOCKS
    if fused:
        def block_of(b, h, t, qi, kj):
            return jnp.minimum((b * n_hg + h) * len(pairs) + t, EXPERT_CAST_BLOCKS - 1)

        c_in, c_out, c_shape, c_arg = _expert_cast_specs(expert_w, EXPERT_CAST_BLOCKS, block_of)
        in_specs, out_specs, out_shape, args = in_specs + [c_in], out_specs + [c_out], out_shape + [c_shape], args + (c_arg,)
    grid_spec = pltpu.PrefetchScalarGridSpec(
        num_scalar_prefetch=2,
        grid=(batch, n_hg, len(pairs)),
        in_specs=in_specs,
        out_specs=out_specs,
        scratch_shapes=[pltpu.VMEM((hg, tq, LANES), F32), pltpu.VMEM((hg, tq, V_EXT), F32)],
    )
    outs = pl.pallas_call(
        _attn_cast_kernel if fused else _attn_kernel,
        grid_spec=grid_spec,
        out_shape=out_shape,
        compiler_params=_cparams("arbitrary", "arbitrary", "arbitrary"),
        name="mla_attn",
    )(*args)
    if fused:
        return outs[0], outs[1]
    return outs[0], _expert_cast_call(expert_w)


def _mix_out_kernel(ya_ref, yb_ref, w_ref, x_ref, g_ref, xo_ref, hn_ref):
    half = ya_ref.shape[1]
    acc = _dot(ya_ref[...], w_ref[0:half, :]) + _dot(yb_ref[...], w_ref[half:, :])
    x = x_ref[...] + acc
    xo_ref[...] = x
    hn_ref[...] = _rms(x, g_ref[...]).astype(hn_ref.dtype)


def _mix_out_call(ya, yb, w, x, g):
    m, d = x.shape
    tm = _tile(m, 512)
    half = ya.shape[1]
    row = pl.BlockSpec((tm, d), lambda i: (i, 0))
    return pl.pallas_call(
        _mix_out_kernel,
        grid=(m // tm,),
        in_specs=[pl.BlockSpec((tm, half), lambda i: (i, 0)), pl.BlockSpec((tm, half), lambda i: (i, 0)),
                  pl.BlockSpec((2 * half, d), lambda i: (0, 0)), row, pl.BlockSpec((1, d), lambda i: (0, 0))],
        out_specs=[row, row],
        out_shape=[jax.ShapeDtypeStruct((m, d), F32), jax.ShapeDtypeStruct((m, d), BF16)],
        compiler_params=_cparams("parallel"),
        name="mix_out",
    )(ya, yb, w, x, g.reshape(1, d))


def _conv_out_kernel(a_ref, w_ref, b_ref, x_ref, g_ref, r_ref, xo_ref, hn_ref, lg_ref):
    x = x_ref[...] + _dot(a_ref[...], w_ref[...]) + b_ref[...]
    xo_ref[...] = x
    hn = _rms(x, g_ref[...])
    tm = x.shape[0]
    for j in range(SLABS):
        hn_ref[pl.ds(j, tm, stride=SLABS), :] = hn[:, j * LANES:(j + 1) * LANES]
    lg_ref[...] = _dot_hp(hn, r_ref[...])


def _conv_out_call(a, w, b, x, g, router_pad):
    m, d = x.shape
    tm = _tile(m, 256)
    row = pl.BlockSpec((tm, d), lambda i: (i, 0))
    full = lambda shape: pl.BlockSpec(shape, lambda i: (0, 0))
    return pl.pallas_call(
        _conv_out_kernel,
        grid=(m // tm,),
        in_specs=[row, full((d, d)), full((1, d)), row, full((1, d)), full((d, LANES))],
        out_specs=[row, pl.BlockSpec((tm * SLABS, LANES), lambda i: (i, 0)),
                   pl.BlockSpec((tm, LANES), lambda i: (i, 0))],
        out_shape=[jax.ShapeDtypeStruct((m, d), F32), jax.ShapeDtypeStruct((m * SLABS, LANES), F32),
                   jax.ShapeDtypeStruct((m, LANES), F32)],
        compiler_params=_cparams("parallel"),
        name="conv_out",
    )(a, w, b.reshape(1, d), x, g.reshape(1, d), router_pad)


def _ffn_kernel(h_ref, w1_ref, w3_ref, w2_ref, x_ref, g_ref, xo_ref, hn_ref):
    f = pl.program_id(1)

    @pl.when(f == 0)
    def _():
        xo_ref[...] = x_ref[...]

    h = h_ref[...]
    gate = (_silu(_dot(h, w1_ref[...])) * _dot(h, w3_ref[...])).astype(BF16)
    xo_ref[...] += _dot(gate, w2_ref[...])

    @pl.when(f == pl.num_programs(1) - 1)
    def _():
        hn_ref[...] = _rms(xo_ref[...], g_ref[...]).astype(hn_ref.dtype)


def _ffn_call(h, w1, w3, w2, x, g):
    m, d = x.shape
    ff = w1.shape[1]
    tm = _tile(m, 512)
    tf = _tile(ff, 512, LANES)
    row = pl.BlockSpec((tm, d), lambda i, f: (i, 0))
    wtile = pl.BlockSpec((d, tf), lambda i, f: (0, f))
    return pl.pallas_call(
        _ffn_kernel,
        grid=(m // tm, ff // tf),
        in_specs=[row, wtile, wtile,
                  pl.BlockSpec((tf, d), lambda i, f: (f, 0)), row, pl.BlockSpec((1, d), lambda i, f: (0, 0))],
        out_specs=[row, row],
        out_shape=[jax.ShapeDtypeStruct((m, d), F32), jax.ShapeDtypeStruct((m, d), BF16)],
        compiler_params=_cparams("parallel", "arbitrary"),
        name="ffn",
    )(h, w1, w3, w2, x, g.reshape(1, d))


def _glu_kernel(h_ref, wa_ref, wb_ref, ba_ref, bb_ref, o_ref):
    h = h_ref[...]
    a = _dot(h, wa_ref[...]) + ba_ref[...]
    b = _dot(h, wb_ref[...]) + bb_ref[...]
    o_ref[...] = a * _sigmoid(b)


def _glu_call(h, w, b):
    m, d = h.shape
    n = w.shape[1] // 2
    tm = _tile(m, 1024)
    tn = _tile(n, 512, LANES)
    nj = n // tn
    b2 = b.reshape(1, 2 * n)
    return pl.pallas_call(
        _glu_kernel,
        grid=(m // tm, nj),
        in_specs=[pl.BlockSpec((tm, d), lambda i, j: (i, 0)),
                  pl.BlockSpec((d, tn), lambda i, j: (0, j)), pl.BlockSpec((d, tn), lambda i, j: (0, j + nj)),
                  pl.BlockSpec((1, tn), lambda i, j: (0, j)), pl.BlockSpec((1, tn), lambda i, j: (0, j + nj))],
        out_specs=pl.BlockSpec((tm, tn), lambda i, j: (i, j)),
        out_shape=jax.ShapeDtypeStruct((m, n), F32),
        compiler_params=_cparams("parallel", "arbitrary"),
        name="conv_glu",
    )(h, w, w, b2, b2)


def _dwconv_kernel(u_ref, halo_ref, dw_ref, dwb_ref, lg_ref, lb_ref, o_ref, ext_sc, acc_sc, *, tiles_per_seq):
    ts, d = u_ref.shape
    first = (pl.program_id(0) % tiles_per_seq) == 0
    ext_sc[0, 0:CONV_HALO, :] = jnp.where(first, 0.0, halo_ref[...])
    ext_sc[0, CONV_HALO:, :] = u_ref[...]
    n_shift = ts + CONV_HALO - SUBLANES
    for b in range(1, SUBLANES):
        ext_sc[b, 0:n_shift, :] = ext_sc[0, b:b + n_shift, :]
    rc, cc = 64, 256
    base = CONV_HALO - (CONV_WIDTH - 1)
    for c0 in range(0, d, cc):
        for r0 in range(0, ts, rc):
            acc = jnp.zeros((rc, cc), F32) + dwb_ref[:, c0:c0 + cc]
            for j in range(CONV_WIDTH):
                a8, b = divmod(base + j, SUBLANES)
                rows = slice(r0 + a8 * SUBLANES, r0 + a8 * SUBLANES + rc)
                acc = acc + dw_ref[j:j + 1, c0:c0 + cc] * ext_sc[b, rows, c0:c0 + cc]
            acc_sc[r0:r0 + rc, c0:c0 + cc] = acc
    y = acc_sc[...]
    mu = jnp.mean(y, axis=-1, keepdims=True)
    yc = y - mu
    var = jnp.mean(yc * yc, axis=-1, keepdims=True)
    yn = yc * lax.rsqrt(var + LN_EPS) * lg_ref[...] + lb_ref[...]
    o_ref[...] = _silu(yn).astype(o_ref.dtype)


def _dwconv_call(u, seq, dw_w, dw_b, ln_g, ln_b):
    m, d = u.shape
    ts = _tile(seq, 256, CONV_HALO)
    tiles_per_seq = seq // ts
    dw_pad = jnp.concatenate([dw_w, jnp.zeros((CONV_HALO - CONV_WIDTH, d), F32)], axis=0)
    full = lambda shape: pl.BlockSpec(shape, lambda i: (0, 0))
    return pl.pallas_call(
        functools.partial(_dwconv_kernel, tiles_per_seq=tiles_per_seq),
        grid=(m // ts,),
        in_specs=[pl.BlockSpec((ts, d), lambda i: (i, 0)),
                  pl.BlockSpec((CONV_HALO, d), lambda i: (jnp.maximum(i * (ts // CONV_HALO) - 1, 0), 0)),
                  full((CONV_HALO, d)), full((1, d)), full((1, d)), full((1, d))],
        out_specs=pl.BlockSpec((ts, d), lambda i: (i, 0)),
        out_shape=jax.ShapeDtypeStruct((m, d), BF16),
        scratch_shapes=[pltpu.VMEM((SUBLANES, ts + CONV_HALO, d), F32), pltpu.VMEM((ts, d), F32)],
        compiler_params=_cparams("parallel"),
        name="dwconv",
    )(u, u, dw_pad, dw_b.reshape(1, d), ln_g.reshape(1, d), ln_b.reshape(1, d))


def _route_kernel(lg_ref, tri_ref, o_ref, cnt_ref, carry_sc):
    @pl.when(pl.program_id(0) == 0)
    def _():
        carry_sc[...] = jnp.zeros_like(carry_sc)

    lg = lg_ref[...]
    lane = lax.broadcasted_iota(jnp.int32, lg.shape, 1)
    lg = jnp.where(lane < N_EXPERTS, lg, -jnp.inf)
    m1 = jnp.max(lg, axis=-1, keepdims=True)
    e1 = jnp.min(jnp.where(lg == m1, lane, LANES), axis=-1, keepdims=True)
    lg2 = jnp.where(lane == e1, -jnp.inf, lg)
    m2 = jnp.max(lg2, axis=-1, keepdims=True)
    e2 = jnp.min(jnp.where(lg2 == m2, lane, LANES), axis=-1, keepdims=True)
    ex = jnp.exp(m2 - m1)
    g1 = 1.0 / (1.0 + ex)
    g2 = ex / (1.0 + ex)
    oh1 = (lane == e1).astype(F32)
    oh2 = (lane == e2).astype(F32)
    both = oh1 + oh2
    before = _dot(tri_ref[...], both.astype(BF16)) + carry_sc[...]
    r1 = jnp.sum(before * oh1, axis=-1, keepdims=True)
    r2 = jnp.sum(before * oh2, axis=-1, keepdims=True)
    carry_sc[...] = carry_sc[...] + jnp.sum(both, axis=0, keepdims=True)
    cnt_ref[...] = jnp.broadcast_to(carry_sc[...], cnt_ref.shape)
    out = jnp.where(lane == 0, e1.astype(F32), 0.0)
    out = jnp.where(lane == 1, e2.astype(F32), out)
    out = jnp.where(lane == 2, g1, out)
    out = jnp.where(lane == 3, g2, out)
    out = jnp.where(lane == 4, r1, out)
    out = jnp.where(lane == 5, r2, out)
    o_ref[...] = out


def _route_call(logits):
    m = logits.shape[0]
    tm = _tile(m, 512)
    idx = np.arange(tm)
    tri = jnp.asarray(idx[:, None] > idx[None, :], BF16)
    return pl.pallas_call(
        _route_kernel,
        grid=(m // tm,),
        in_specs=[pl.BlockSpec((tm, LANES), lambda i: (i, 0)), pl.BlockSpec((tm, tm), lambda i: (0, 0))],
        out_specs=[pl.BlockSpec((tm, LANES), lambda i: (i, 0)), pl.BlockSpec((8, LANES), lambda i: (0, 0))],
        out_shape=[jax.ShapeDtypeStruct((m, LANES), F32), jax.ShapeDtypeStruct((8, LANES), F32)],
        scratch_shapes=[pltpu.VMEM((1, LANES), F32)],
        compiler_params=_cparams("arbitrary"),
        name="moe_route",
    )(logits, tri)


def _slab_copy(src_ref, dst_ref, sem, src_slab, dst_row):
    src = src_ref.at[pl.ds(pl.multiple_of(src_slab, SLABS), SLABS)]
    dst = dst_ref.at[pl.ds(pl.multiple_of(dst_row * SLABS, SLABS), SLABS)]
    return pltpu.make_async_copy(src, dst, sem)


def _slab_cols(ref, j, rows):
    return ref[pl.ds(j, rows, stride=SLABS), :]


def _gather_kernel(used_ref, tok_ref, src_ref, o_ref, buf_sc, sem):
    rows = o_ref.shape[0]
    used = used_ref[pl.program_id(0)] > 0

    def start(r2, c):
        for k in range(2):
            r = 2 * r2 + k
            _slab_copy(src_ref, buf_sc, sem, tok_ref[0, 0, r], r).start(priority=k)
        return c

    def wait(r, c):
        _slab_copy(src_ref, buf_sc, sem, 0, r).wait()
        return c

    @pl.when(used)
    def _():
        lax.fori_loop(0, rows // 2, start, 0, unroll=4)
        lax.fori_loop(0, rows, wait, 0, unroll=8)
        for j in range(SLABS):
            o_ref[:, j * LANES:(j + 1) * LANES] = _slab_cols(buf_sc, j, rows).astype(o_ref.dtype)

    @pl.when(jnp.logical_not(used))
    def _():
        o_ref[...] = jnp.zeros_like(o_ref)


def _gather_call(src_slabs, slot_slab, block_used):
    cap = slot_slab.shape[0]
    d = SLABS * LANES
    rows = GATHER_ROWS
    grid_spec = pltpu.PrefetchScalarGridSpec(
        num_scalar_prefetch=1,
        grid=(cap // rows,),
        in_specs=[pl.BlockSpec((1, 1, rows), lambda i, u: (i, 0, 0), memory_space=pltpu.SMEM),
                  pl.BlockSpec(memory_space=pl.ANY)],
        out_specs=pl.BlockSpec((rows, d), lambda i, u: (i, 0)),
        scratch_shapes=[pltpu.VMEM((rows * SLABS, LANES), F32), pltpu.SemaphoreType.DMA(())],
    )
    return pl.pallas_call(
        _gather_kernel,
        grid_spec=grid_spec,
        out_shape=jax.ShapeDtypeStruct((cap, d), BF16),
        compiler_params=_cparams("arbitrary"),
        name="moe_gather",
    )(block_used, slot_slab.reshape(cap // rows, 1, rows), src_slabs)


def _moe_kernel(te_ref, tr_ref, x_ref, w1_ref, w3_ref, w2_ref, o_ref, w2_sc):
    i = pl.program_id(0)
    f = pl.program_id(1)
    nrows = tr_ref[i]
    tm = x_ref.shape[0]

    @pl.when(f == 0)
    def _():
        o_ref[...] = jnp.zeros_like(o_ref)

    def swiglu(rows):
        x = x_ref[rows, :]
        gate = (_silu(_dot(x, w1_ref[0, 0])) * _dot(x, w3_ref[0, 0])).astype(BF16)
        o_ref[rows, :] += _dot(gate, w2_sc[...])

    @pl.when(nrows == tm)
    def _():
        w2_sc[...] = w2_ref[0].astype(BF16)
        swiglu(slice(0, tm))

    @pl.when(jnp.logical_and(nrows > 0, nrows < tm))
    def _():
        w2_sc[...] = w2_ref[0].astype(BF16)
        for s in range(tm // MOE_SUB):
            @pl.when(s * MOE_SUB < nrows)
            def _():
                swiglu(slice(s * MOE_SUB, (s + 1) * MOE_SUB))


def _moe_call(xs, tile_expert, tile_rows, w1t, w3t, w2):
    cap, d = xs.shape
    n_exp, nf, _, tf = w1t.shape
    tm = MOE_TM

    def fidx(i, f, te, tr):
        return jnp.where(tr[i] > 0, f, nf - 1)

    grid_spec = pltpu.PrefetchScalarGridSpec(
        num_scalar_prefetch=2,
        grid=(cap // tm, nf),
        in_specs=[
            pl.BlockSpec((tm, d), lambda i, f, te, tr: (i, 0)),
            pl.BlockSpec((1, 1, d, tf), lambda i, f, te, tr: (te[i], fidx(i, f, te, tr), 0, 0)),
            pl.BlockSpec((1, 1, d, tf), lambda i, f, te, tr: (te[i], fidx(i, f, te, tr), 0, 0)),
            pl.BlockSpec((1, tf, d), lambda i, f, te, tr: (te[i], fidx(i, f, te, tr), 0)),
        ],
        out_specs=pl.BlockSpec((tm, d), lambda i, f, te, tr: (i, 0)),
        scratch_shapes=[pltpu.VMEM((tf, d), BF16)],
    )
    return pl.pallas_call(
        _moe_kernel,
        grid_spec=grid_spec,
        out_shape=jax.ShapeDtypeStruct((cap, d), F32),
        compiler_params=_cparams("parallel", "arbitrary"),
        name="moe_experts",
    )(tile_expert, tile_rows, xs, w1t, w3t, w2)


def _row_copy(src_ref, dst_ref, sem, src_row, dst_row):
    return pltpu.make_async_copy(src_ref.at[pl.ds(src_row, 1)], dst_ref.at[pl.ds(dst_row, 1)], sem)


def _combine_kernel(pos_ref, y_ref, x_ref, rt_ref, g_ref, o_ref, buf_sc, sem):
    rows = o_ref.shape[0]

    def start(r, c):
        _row_copy(y_ref, buf_sc.at[0], sem, pos_ref[0, 0, r], r).start(priority=0)
        _row_copy(y_ref, buf_sc.at[1], sem, pos_ref[0, 1, r], r).start(priority=1)
        return c

    def wait(r, c):
        _row_copy(y_ref, buf_sc.at[0], sem, 0, r).wait()
        _row_copy(y_ref, buf_sc.at[1], sem, 0, r).wait()
        return c

    lax.fori_loop(0, rows, start, 0, unroll=8)
    lax.fori_loop(0, rows, wait, 0, unroll=8)
    rt = rt_ref[...]
    x = x_ref[...] + rt[:, 2:3] * buf_sc[0] + rt[:, 3:4] * buf_sc[1]
    o_ref[...] = _rms(x, g_ref[...])


def _combine_call(pos, yb, x, route, g):
    m, d = x.shape
    rows = GATHER_ROWS if m % GATHER_ROWS == 0 else m
    return pl.pallas_call(
        _combine_kernel,
        grid=(m // rows,),
        in_specs=[pl.BlockSpec((1, 2, rows), lambda i: (i, 0, 0), memory_space=pltpu.SMEM),
                  pl.BlockSpec(memory_space=pl.ANY),
                  pl.BlockSpec((rows, d), lambda i: (i, 0)),
                  pl.BlockSpec((rows, LANES), lambda i: (i, 0)),
                  pl.BlockSpec((1, d), lambda i: (0, 0))],
        out_specs=pl.BlockSpec((rows, d), lambda i: (i, 0)),
        out_shape=jax.ShapeDtypeStruct((m, d), F32),
        scratch_shapes=[pltpu.VMEM((2, rows, d), F32), pltpu.SemaphoreType.DMA(())],
        compiler_params=_cparams("arbitrary"),
        name="moe_combine",
    )(pos.reshape(m // rows, rows, 2).transpose(0, 2, 1), yb, x, route, g.reshape(1, d))


def _rot_cols(w):
    half = QK_ROPE // 2
    return jnp.concatenate([-w[..., half:], w[..., :half]], axis=-1)


def _mixer_weights(w_in, w_up, a_up, w_q_up):
    d = RWKV_DIM
    w_rwkv = w_in[:, :RWKV_IN]
    w_lat = w_in[:, RWKV_IN:RWKV_IN + Q_LORA + KV_LORA]
    w_kr = w_in[:, RWKV_IN + Q_LORA + KV_LORA:]
    w_mla = jnp.concatenate([w_lat, w_kr, _rot_cols(w_kr)], axis=1)
    lora = jnp.zeros((W_LORA + A_LORA, 2 * d), F32)
    lora = lora.at[:W_LORA, :d].set(w_up).at[W_LORA:, d:].set(a_up)
    wq = w_q_up.reshape(Q_LORA, MLA_HEADS, QK_NOPE + QK_ROPE)
    wq_rope = wq[..., QK_NOPE:]
    wq = jnp.concatenate([wq, _rot_cols(wq_rope)], axis=-1).reshape(Q_LORA, MLA_HEADS * QK_CAT)
    return w_rwkv.astype(BF16), w_mla.astype(BF16), lora, wq.astype(BF16)


def kernel(x, positions, l0_mix_norm, l0_w_in, l0_shift_mu, l0_w0, l0_w_up, l0_a0, l0_a_up, l0_g_up, l0_k_k, l0_k_a, l0_r_k, l0_gn_w, l0_gn_b, l0_q_norm, l0_w_q_up, l0_kv_norm, l0_w_kv_up, l0_w_out, l0_ffn_norm, l0_ffn_w1, l0_ffn_w3, l0_ffn_w2, l1_mix_norm, l1_pw1_w, l1_pw1_b, l1_dw_w, l1_dw_b, l1_ln_g, l1_ln_b, l1_pw2_w, l1_pw2_b, l1_ffn_norm, l1_router, l1_exp_w1, l1_exp_w3, l1_exp_w2, final_norm):
    batch, seq, d = x.shape
    m = batch * seq
    x0 = x.reshape(m, d)
    pos = positions.reshape(m, 1)
    inv = ROPE_THETA ** (-jnp.arange(0, QK_ROPE, 2, dtype=F32) / QK_ROPE)
    inv_tab = jnp.tile(inv, LANES // inv.shape[0]).reshape(1, LANES)

    w_rwkv, w_mla, lora_w, wq = _mixer_weights(l0_w_in, l0_w_up, l0_a_up, l0_w_q_up)

    hn0 = _rmsnorm_call(x0, l0_mix_norm)
    p_rwkv = _matmul_call(hn0, w_rwkv, name="in_proj_rwkv")
    p_mla = _matmul_call(hn0, w_mla, name="in_proj_mla")

    prep = _rwkv_prep_call(p_rwkv, seq, l0_shift_mu, lora_w, l0_w0, l0_a0, l0_g_up, l0_k_k, l0_k_a,
                           l0_r_k.reshape(-1))
    at, bt, kt, rt, v, bh, kh, gc, bonus, gate = prep
    (ra, o2, pm, qm), w3t = _rwkv_intra_call(at, bt, kt, rt, v, bh, kh, gc, seq, l1_exp_w3)
    y_rwkv = _rwkv_seq_call(ra, o2, pm, qm, bonus, gate, l0_gn_w, l0_gn_b, batch, seq)

    q, k, vv = _mla_proj_calls(p_mla, pos, inv_tab, l0_q_norm, wq, l0_kv_norm, l0_w_kv_up.astype(BF16),
                               batch, seq)
    y_mla, w1t = _attn_call(q, k, vv, batch, seq, l1_exp_w1)

    x1, hn1 = _mix_out_call(y_rwkv, y_mla, l0_w_out.astype(BF16), x0, l0_ffn_norm)

    x2, hn2 = _ffn_call(hn1, l0_ffn_w1.astype(BF16), l0_ffn_w3.astype(BF16), l0_ffn_w2.astype(BF16),
                        x1, l1_mix_norm)

    u = _glu_call(hn2, l1_pw1_w.astype(BF16), l1_pw1_b)
    sc = _dwconv_call(u, seq, l1_dw_w, l1_dw_b, l1_ln_g, l1_ln_b)
    router_pad = jnp.zeros((d, LANES), F32).at[:, :N_EXPERTS].set(l1_router)
    x3, hn3, logits = _conv_out_call(sc, l1_pw2_w.astype(BF16), l1_pw2_b, x2, l1_ffn_norm, router_pad)

    route, counts = _route_call(logits)
    e = route[:, 0:2].astype(jnp.int32)
    rank = route[:, 4:6].astype(jnp.int32)
    cnt = counts[0, :N_EXPERTS].astype(jnp.int32)
    padded = (cnt + MOE_TM - 1) // MOE_TM * MOE_TM
    pad_end = jnp.cumsum(padded)
    pad_start = pad_end - padded
    pos_slot = pad_start[e] + rank
    cap = (2 * m + N_EXPERTS * MOE_TM + MOE_TM - 1) // MOE_TM * MOE_TM
    tok = jnp.broadcast_to(jnp.arange(m, dtype=jnp.int32)[:, None], (m, 2))
    slot_slab = jnp.zeros((cap,), jnp.int32).at[pos_slot.reshape(-1)].set(tok.reshape(-1) * SLABS)
    tile_start = jnp.arange(cap // MOE_TM, dtype=jnp.int32) * MOE_TM
    tile_expert = jnp.minimum(jnp.sum(tile_start[:, None] >= pad_end[None, :], axis=1), N_EXPERTS - 1)
    used = tile_start < pad_end[-1]
    last_used = jnp.max(jnp.where(used, tile_expert, 0))
    tile_expert = jnp.where(used, tile_expert, last_used).astype(jnp.int32)
    tile_rows = jnp.clip(pad_start[tile_expert] + cnt[tile_expert] - tile_start, 0, MOE_TM)
    tile_rows = jnp.where(used, tile_rows, 0).astype(jnp.int32)

    per_tile = MOE_TM // GATHER_ROWS
    block_off = jnp.tile(jnp.arange(per_tile, dtype=jnp.int32) * GATHER_ROWS, cap // MOE_TM)
    block_used = (jnp.repeat(tile_rows, per_tile) > block_off).astype(jnp.int32)
    xs = _gather_call(hn3, slot_slab, block_used)
    yb = _moe_call(xs, tile_expert, tile_rows, w1t, w3t, l1_exp_w2)
    out = _combine_call(pos_slot, yb, x3, route, final_norm)
    return out.reshape(batch, seq, d)
```

```python
import functools

import numpy as np
import jax
import jax.numpy as jnp
from jax import lax
from jax.experimental import pallas as pl
from jax.experimental.pallas import tpu as pltpu

F32 = jnp.float32
BF16 = jnp.bfloat16

D_MODEL = 2048
CHUNK = 64
HEAD = 64
N_HEADS = 16
RWKV_DIM = N_HEADS * HEAD
W_LORA, A_LORA, G_LORA = 64, 64, 128
RWKV_IN = 3 * RWKV_DIM + W_LORA + A_LORA + G_LORA
MLA_HEADS = 8
Q_LORA = KV_LORA = 512
QK_NOPE, QK_ROPE, V_HEAD = 128, 64, 128
QK_CAT = 2 * QK_NOPE
V_EXT = 2 * V_HEAD
ROPE_THETA = 10000.0
CONV_WIDTH = 31
CONV_HALO = 32
N_EXPERTS = 8
NORM_EPS = 1e-6
LN_EPS = 1e-5
GN_EPS = HEAD * 1e-5
NEG_BIG = -1e30

LANES = 128
SUBLANES = 8
VMEM_LIMIT = 56 * 1024 * 1024

HEADS_PER_GROUP = 2
GROUP_W = HEADS_PER_GROUP * HEAD
MOE_TM = 1024
MOE_SUB = 512
MOE_TF = 512
EXPERT_CAST_BLOCKS = 512
EXPERT_CAST_MAX_ROWS = 64
GATHER_ROWS = 512
SLABS = D_MODEL // LANES
ATTN_HEADS_PER_STEP = 4
LOG2_E = 1.4426950408889634


def _cparams(*sem):
    return pltpu.CompilerParams(dimension_semantics=sem, vmem_limit_bytes=VMEM_LIMIT)


def _tile(n, pref, align=8):
    if n <= pref:
        return n
    t = (pref // align) * align
    while t > align and n % t:
        t -= align
    assert n % t == 0, (n, pref)
    return t


def _dot(a, b):
    return jnp.dot(a, b, preferred_element_type=F32)


def _dot_nt(a, b):
    return lax.dot_general(a, b, (((1,), (1,)), ((), ())), preferred_element_type=F32)


def _split2(x):
    hi = x.astype(BF16)
    lo = (x - hi.astype(F32)).astype(BF16)
    return hi, lo


def _dot_lx(a, b_exact):
    hi, lo = _split2(a)
    return _dot(hi, b_exact) + _dot(lo, b_exact)


def _dot_hp(a, b):
    ah, al = _split2(a)
    bh, bl = _split2(b)
    return _dot(ah, bh) + (_dot(ah, bl) + _dot(al, bh))


def _sigmoid(x):
    return 1.0 / (1.0 + jnp.exp(-x))


def _silu(x):
    return x * _sigmoid(x)


def _softplus(x):
    return jnp.maximum(x, 0.0) + jnp.log(1.0 + jnp.exp(-jnp.abs(x)))


def _rms(x, g):
    return x * lax.rsqrt(jnp.mean(x * x, axis=-1, keepdims=True) + NORM_EPS) * g


def _rmsnorm_kernel(x_ref, g_ref, o_ref):
    o_ref[...] = _rms(x_ref[...], g_ref[...]).astype(o_ref.dtype)


def _rmsnorm_call(x, g):
    m, d = x.shape
    tm = _tile(m, 512)
    return pl.pallas_call(
        _rmsnorm_kernel,
        grid=(m // tm,),
        in_specs=[pl.BlockSpec((tm, d), lambda i: (i, 0)), pl.BlockSpec((1, d), lambda i: (0, 0))],
        out_specs=pl.BlockSpec((tm, d), lambda i: (i, 0)),
        out_shape=jax.ShapeDtypeStruct((m, d), BF16),
        compiler_params=_cparams("parallel"),
        name="rmsnorm",
    )(x, g.reshape(1, d))


def _matmul_kernel(a_ref, w_ref, o_ref):
    o_ref[...] = _dot(a_ref[...], w_ref[...])


def _matmul_call(a, w, tm_pref=1024, tn_pref=1664, name="matmul"):
    m, k = a.shape
    n = w.shape[1]
    tm = _tile(m, tm_pref)
    tn = _tile(n, tn_pref, LANES)
    return pl.pallas_call(
        _matmul_kernel,
        grid=(m // tm, n // tn),
        in_specs=[pl.BlockSpec((tm, k), lambda i, j: (i, 0)), pl.BlockSpec((k, tn), lambda i, j: (0, j))],
        out_specs=pl.BlockSpec((tm, tn), lambda i, j: (i, j)),
        out_shape=jax.ShapeDtypeStruct((m, n), F32),
        compiler_params=_cparams("parallel", "arbitrary"),
        name=name,
    )(a, w)


def _rwkv_prep_kernel(p_ref, prev_ref, mu_ref, lw_ref, w0_ref, a0_ref, gup_ref, kk_ref, ka_ref, rk_ref,
                      tril_ref, ones_ref, seg_ref, segt_ref,
                      at_ref, bt_ref, kt_ref, rt_ref, v_ref, bh_ref, kh_ref, gc_ref, bonus_ref, g_ref,
                      *, tiles_per_seq):
    ts = p_ref.shape[0]
    first = (pl.program_id(0) % tiles_per_seq) == 0
    row0 = lax.broadcasted_iota(jnp.int32, (ts, 1), 0) == 0

    def mixed(c0, c1):
        pc = p_ref[:, c0:c1]
        prev = jnp.where(first, 0.0, prev_ref[7:8, c0:c1])
        sh = jnp.where(row0, prev, pltpu.roll(pc, 1, axis=0))
        return pc + (sh - pc) * mu_ref[:, c0:c1]

    d = RWKV_DIM
    r = mixed(0, d)
    k = mixed(d, 2 * d)
    v = mixed(2 * d, 3 * d)
    xwa = mixed(3 * d, 3 * d + W_LORA + A_LORA)
    xg = mixed(3 * d + W_LORA + A_LORA, RWKV_IN)

    lane = lax.broadcasted_iota(jnp.int32, xwa.shape, 1)
    z = jnp.where(lane < W_LORA, jnp.tanh(xwa), xwa)
    wa = _dot_hp(z, lw_ref[...])
    w = -_softplus(-(w0_ref[...] + wa[:, :d])) - 0.5
    logdecay = -jnp.exp(w)
    a = _sigmoid(a0_ref[...] + wa[:, d:])
    g_ref[...] = _dot_hp(_sigmoid(xg), gup_ref[...])

    seg = seg_ref[...]
    segt = segt_ref[...]

    def head_sum(x):
        return _dot_lx(_dot_lx(x, seg), segt)

    kkr = k * kk_ref[...]
    kk = kkr / jnp.maximum(jnp.sqrt(head_sum(kkr * kkr)), 1e-12)
    k2 = k * (1.0 + (a - 1.0) * ka_ref[...])
    bonus_ref[...] = head_sum(r * k2 * rk_ref[...]) * v

    ld_hi, ld_lo = _split2(logdecay)
    tril = tril_ref[...]
    ones = ones_ref[...]
    gcum = _dot(tril, ld_hi) + _dot(tril, ld_lo)
    gtot = _dot(ones, ld_hi) + _dot(ones, ld_lo)
    g_in = jnp.exp(gcum)
    g_ex = jnp.exp(gcum - logdecay)
    g_inv = jnp.exp(-gcum)
    g_rest = jnp.exp(gtot - gcum)
    beta = kk * a
    at_ref[...] = (-kk * g_ex).astype(at_ref.dtype)
    bt_ref[...] = (beta * g_inv).astype(bt_ref.dtype)
    kt_ref[...] = (k2 * g_inv).astype(kt_ref.dtype)
    rt_ref[...] = (r * g_in).astype(rt_ref.dtype)
    v_ref[...] = v.astype(v_ref.dtype)
    bh_ref[...] = beta * g_rest
    kh_ref[...] = k2 * g_rest
    gc_ref[...] = jnp.exp(gtot)


def _rwkv_prep_call(p, seq, mu, lora_w, w0, a0, g_up, k_k, k_a, r_k):
    m = p.shape[0]
    ts = _tile(seq, 256, CHUNK)
    tiles_per_seq = seq // ts
    d = RWKV_DIM
    idx = np.arange(ts)
    same = (idx[:, None] // CHUNK) == (idx[None, :] // CHUNK)
    tril = jnp.asarray(same & (idx[:, None] >= idx[None, :]), BF16)
    ones = jnp.asarray(same, BF16)
    lane = np.arange(d)
    seg_np = (lane[:, None] // HEAD) == np.arange(LANES)[None, :]
    seg = jnp.asarray(seg_np, BF16)
    segt = jnp.asarray(seg_np.T, BF16)

    def row(x):
        return x.reshape(1, -1).astype(F32)

    full = lambda shape: pl.BlockSpec(shape, lambda i: (0, 0))
    tok = pl.BlockSpec((ts, d), lambda i: (i, 0))
    outs = pl.pallas_call(
        functools.partial(_rwkv_prep_kernel, tiles_per_seq=tiles_per_seq),
        grid=(m // ts,),
        in_specs=[
            pl.BlockSpec((ts, RWKV_IN), lambda i: (i, 0)),
            pl.BlockSpec((8, RWKV_IN), lambda i: (jnp.maximum(i * (ts // 8) - 1, 0), 0)),
            full((1, RWKV_IN)), full((W_LORA + A_LORA, 2 * d)), full((1, d)), full((1, d)),
            full((G_LORA, d)), full((1, d)), full((1, d)), full((1, d)),
            full((ts, ts)), full((ts, ts)), full((d, LANES)), full((LANES, d)),
        ],
        out_specs=[tok] * 10,
        out_shape=[jax.ShapeDtypeStruct((m, d), BF16)] * 5 + [jax.ShapeDtypeStruct((m, d), F32)] * 5,
        compiler_params=_cparams("parallel"),
        name="rwkv_prep",
    )(p, p, row(mu), lora_w, row(w0), row(a0), g_up, row(k_k), row(k_a), row(r_k), tril, ones, seg, segt)
    return outs


def _stack_heads(x):
    lane_head = lax.broadcasted_iota(jnp.int32, x.shape, 1) // HEAD
    return jnp.concatenate([jnp.where(lane_head == h, x, 0.0) for h in range(HEADS_PER_GROUP)], axis=0)


def _unstack_heads(x):
    out = x[0:CHUNK]
    for h in range(1, HEADS_PER_GROUP):
        out = out + x[h * CHUNK:(h + 1) * CHUNK]
    return out


def _rwkv_intra_kernel(at_ref, bt_ref, kt_ref, rt_ref, v_ref, bh_ref, kh_ref, gc_ref,
                       ra_ref, o2_ref, p_ref, q_ref, *, mm):
    w = GROUP_W
    n_chunks = at_ref.shape[0] // CHUNK
    ri = lax.broadcasted_iota(jnp.int32, (w, w), 0)
    ci = lax.broadcasted_iota(jnp.int32, (w, w), 1)
    strict = (ri % CHUNK) > (ci % CHUNK)
    incl = (ri % CHUNK) >= (ci % CHUNK)
    blk16 = (ri // 16) == (ci // 16)
    eye = (ri == ci).astype(F32)

    chunks = [slice(c * CHUNK, (c + 1) * CHUNK) for c in range(n_chunks)]

    def each(f, *lists):
        return [f(*args) for args in zip(*lists)]

    def stacked(ref):
        return [_stack_heads(ref[rows, :]) for rows in chunks]

    a_s, r_s, b_s, k_s, v_s = stacked(at_ref), stacked(rt_ref), stacked(bt_ref), stacked(kt_ref), stacked(v_ref)
    s = each(lambda a, r, b, k: mm(jnp.concatenate([a, r], axis=0), jnp.concatenate([b, k], axis=0), nt=True),
             a_s, r_s, b_s, k_s)
    l_ab = [jnp.where(strict, x[:w, :w], 0.0) for x in s]
    l_ak = [jnp.where(strict, x[:w, w:], 0.0) for x in s]
    m_rb = [jnp.where(incl, x[w:, :w], 0.0) for x in s]
    m_rk = [jnp.where(incl, x[w:, w:], 0.0) for x in s]

    def mm2(a, b0, b1):
        out = mm(a, jnp.concatenate([b0, b1], axis=1))
        return out[:, :w], out[:, w:]

    dg = [jnp.where(blk16, x, 0.0) for x in l_ab]
    off = each(lambda x, y: x - y, l_ab, dg)
    td = [eye + x for x in dg]
    pw = each(mm, dg, dg)
    for _ in range(2):
        sq_ptd = each(mm2, pw, pw, td)
        td = each(lambda t_, x_: t_ + x_[1], td, sq_ptd)
        pw = [x_[0] for x_ in sq_ptd]
    td = each(lambda t_, p_: t_ + mm(p_, t_), td, pw)
    n1 = each(mm, td, off)
    sq_ntd = each(mm2, n1, n1, td)
    t1 = each(lambda t_, x_: t_ + x_[1], td, sq_ntd)
    t = each(lambda t_, x_: t_ + mm(x_[0], t_), t1, sq_ntd)

    lv = each(mm, l_ak, v_s)
    y = each(lambda tt, x, a: mm(tt, jnp.concatenate([x, a], axis=1)), t, lv, a_s)
    z = each(mm, m_rb, y)
    mv = each(mm, m_rk, v_s)
    z2 = each(lambda ref_rows, yy: mm(_stack_heads(bh_ref[ref_rows, :]).T, yy), chunks, y)
    kv = each(lambda ref_rows, vv: mm(_stack_heads(kh_ref[ref_rows, :]).T, vv), chunks, v_s)
    for c, rows in enumerate(chunks):
        ra_ref[rows, :] = _unstack_heads(r_s[c] + z[c][:, w:]).astype(ra_ref.dtype)
        o2_ref[rows, :] = _unstack_heads(z[c][:, :w] + mv[c])
        p_ref[rows, :] = _unstack_heads(eye * gc_ref[c * CHUNK:c * CHUNK + 1, :]
                                        + z2[c][:, w:]).astype(p_ref.dtype)
        q_ref[rows, :] = _unstack_heads(z2[c][:, :w] + kv[c])


def _mm_bf16(a, b, nt=False):
    a = a.astype(BF16)
    b = b.astype(BF16)
    return _dot_nt(a, b) if nt else _dot(a, b)


def _rwkv_intra_cast_kernel(at_ref, bt_ref, kt_ref, rt_ref, v_ref, bh_ref, kh_ref, gc_ref, w_ref,
                            ra_ref, o2_ref, p_ref, q_ref, wo_ref, *, mm):
    _cast_expert_block(w_ref, wo_ref)
    _rwkv_intra_kernel(at_ref, bt_ref, kt_ref, rt_ref, v_ref, bh_ref, kh_ref, gc_ref,
                       ra_ref, o2_ref, p_ref, q_ref, mm=mm)


def _rwkv_intra_call(at, bt, kt, rt, v, bh, kh, gc, seq, expert_w):
    m, d = at.shape
    ts = _tile(seq, 512, CHUNK)
    n_groups = d // GROUP_W
    steps = (m // ts) * n_groups
    spec = pl.BlockSpec((ts, GROUP_W), lambda i, j: (i, j))
    in_specs, out_specs = [spec] * 8, [spec] * 4
    out_shape = [jax.ShapeDtypeStruct((m, d), dt) for dt in (BF16, F32, BF16, F32)]
    args = (at, bt, kt, rt, v, bh, kh, gc)
    rows_total = expert_w.shape[0] * expert_w.shape[1]
    fused = steps * EXPERT_CAST_MAX_ROWS >= rows_total and rows_total % steps == 0
    if fused:
        c_in, c_out, c_shape, c_arg = _expert_cast_specs(expert_w, steps, lambda i, j: i * n_groups + j)
        in_specs, out_specs, out_shape, args = in_specs + [c_in], out_specs + [c_out], out_shape + [c_shape], args + (c_arg,)
    outs = pl.pallas_call(
        functools.partial(_rwkv_intra_cast_kernel if fused else _rwkv_intra_kernel, mm=_mm_bf16),
        grid=(m // ts, n_groups),
        in_specs=in_specs,
        out_specs=out_specs,
        out_shape=out_shape,
        compiler_params=_cparams("arbitrary", "arbitrary"),
        name="rwkv_intra",
    )(*args)
    if fused:
        return outs[:4], outs[4]
    return outs, _expert_cast_call(expert_w)


def _rwkv_seq_kernel(ra_ref, o2_ref, p_ref, q_ref, bonus_ref, g_ref, gnw_ref, gnb_ref, seg_ref, segt_ref,
                     o_ref, h_sc, y_sc, *, mm):
    w = GROUP_W
    n_groups = RWKV_DIM // w
    n_chunks = ra_ref.shape[0] // CHUNK

    @pl.when(pl.program_id(1) == 0)
    def _():
        h_sc[...] = jnp.zeros_like(h_sc)

    def chunk(c, carry):
        r0 = pl.multiple_of(c * CHUNK, CHUNK)
        rows = pl.ds(r0, CHUNK)
        groups = [slice(gi * w, (gi + 1) * w) for gi in range(n_groups)]
        lhs = [jnp.concatenate([ra_ref[rows, cols], _stack_heads(p_ref[rows, cols])], axis=0) for cols in groups]
        res = [mm(lhs[gi], h_sc[gi]) for gi in range(n_groups)]
        for gi, cols in enumerate(groups):
            y_sc[rows, cols] = res[gi][:CHUNK] + o2_ref[rows, cols]
            h_sc[gi] = res[gi][CHUNK:] + _stack_heads(q_ref[rows, cols])
        return carry

    lax.fori_loop(0, n_chunks, chunk, 0)

    y = y_sc[...]
    seg = seg_ref[...]
    segt = segt_ref[...]
    mean = _dot_lx(_dot_lx(y, seg), segt) * (1.0 / HEAD)
    yc = y - mean
    var = _dot_lx(_dot_lx(yc * yc, seg), segt) * (1.0 / HEAD)
    yn = yc * lax.rsqrt(var + GN_EPS) * gnw_ref[...] + gnb_ref[...]
    o_ref[...] = ((yn + bonus_ref[...]) * g_ref[...]).astype(o_ref.dtype)


def _rwkv_seq_call(ra, o2, p, q, bonus, g, gn_w, gn_b, batch, seq):
    m, d = ra.shape
    ts = _tile(seq, 512, CHUNK)
    nt = seq // ts
    lane = np.arange(d)
    seg_np = (lane[:, None] // HEAD) == np.arange(LANES)[None, :]
    seg = jnp.asarray(seg_np, BF16)
    segt = jnp.asarray(seg_np.T, BF16)
    tok = pl.BlockSpec((ts, d), lambda b, i: (b * nt + i, 0))
    full = lambda shape: pl.BlockSpec(shape, lambda b, i: (0, 0))
    return pl.pallas_call(
        functools.partial(_rwkv_seq_kernel, mm=_mm_bf16),
        grid=(batch, nt),
        in_specs=[tok] * 6 + [full((1, d)), full((1, d)), full((d, LANES)), full((LANES, d))],
        out_specs=tok,
        out_shape=jax.ShapeDtypeStruct((m, d), BF16),
        scratch_shapes=[pltpu.VMEM((d // GROUP_W, GROUP_W, GROUP_W), F32), pltpu.VMEM((ts, d), F32)],
        compiler_params=_cparams("arbitrary", "arbitrary"),
        name="rwkv_seq",
    )(ra, o2, p, q, bonus, g, gn_w.reshape(1, d), gn_b.reshape(1, d), seg, segt)


def _rope_tab(pos_ref, inv_ref):
    ang = pos_ref[...].astype(F32) * inv_ref[...]
    lane = lax.broadcasted_iota(jnp.int32, ang.shape, 1)
    return jnp.where(lane < QK_ROPE, jnp.cos(ang), jnp.sin(ang))


def _mla_q_kernel(lat_ref, pos_ref, inv_ref, g_ref, w_ref, o_ref, *, scale):
    hn = _rms(lat_ref[...], g_ref[...]).astype(BF16)
    q = _dot(hn, w_ref[...]) * scale
    tab = _rope_tab(pos_ref, inv_ref)
    for h in range(MLA_HEADS):
        c0 = h * QK_CAT
        o_ref[0, h, :, 0:QK_NOPE] = q[:, c0:c0 + QK_NOPE].astype(BF16)
        o_ref[0, h, :, QK_NOPE:QK_CAT] = (q[:, c0 + QK_NOPE:c0 + QK_CAT] * tab).astype(BF16)


def _mla_kv_kernel(lat_ref, kr_ref, pos_ref, inv_ref, g_ref, w_ref, k_ref, v_ref):
    hn = _rms(lat_ref[...], g_ref[...]).astype(BF16)
    kv = _dot(hn, w_ref[...])
    t = kr_ref[...] * _rope_tab(pos_ref, inv_ref)
    k_rope = (t + pltpu.roll(t, QK_ROPE, axis=1)).astype(BF16)
    for h in range(MLA_HEADS):
        c0 = h * (QK_NOPE + V_HEAD)
        k_ref[0, h, :, 0:QK_NOPE] = kv[:, c0:c0 + QK_NOPE].astype(BF16)
        k_ref[0, h, :, QK_NOPE:QK_CAT] = k_rope
        v_ref[0, h, :, 0:V_HEAD] = kv[:, c0 + QK_NOPE:c0 + QK_NOPE + V_HEAD].astype(BF16)
        v_ref[0, h, :, V_HEAD:V_EXT] = jnp.ones((kv.shape[0], V_EXT - V_HEAD), BF16)


def _mla_proj_calls(p_mla, pos, inv_tab, q_norm, wq, kv_norm, wkv, batch, seq):
    tm = _tile(seq, 512)
    nt = seq // tm
    scale = float((QK_NOPE + QK_ROPE) ** -0.5) * LOG2_E
    lat = lambda c: pl.BlockSpec((tm, Q_LORA), lambda b, i: (b * nt + i, c))
    posspec = pl.BlockSpec((tm, 1), lambda b, i: (b * nt + i, 0))
    full = lambda shape: pl.BlockSpec(shape, lambda b, i: (0, 0))
    headed = lambda w: pl.BlockSpec((1, MLA_HEADS, tm, w), lambda b, i: (b, 0, i, 0))
    q = pl.pallas_call(
        functools.partial(_mla_q_kernel, scale=scale),
        grid=(batch, nt),
        in_specs=[lat(0), posspec, full((1, LANES)), full((1, Q_LORA)), full((Q_LORA, MLA_HEADS * QK_CAT))],
        out_specs=headed(QK_CAT),
        out_shape=jax.ShapeDtypeStruct((batch, MLA_HEADS, seq, QK_CAT), BF16),
        compiler_params=_cparams("parallel", "parallel"),
        name="mla_q",
    )(p_mla, pos, inv_tab, q_norm.reshape(1, -1), wq)
    k, v = pl.pallas_call(
        _mla_kv_kernel,
        grid=(batch, nt),
        in_specs=[lat(1), pl.BlockSpec((tm, LANES), lambda b, i: (b * nt + i, 2 * Q_LORA // LANES)),
                  posspec, full((1, LANES)), full((1, KV_LORA)),
                  full((KV_LORA, MLA_HEADS * (QK_NOPE + V_HEAD)))],
        out_specs=[headed(QK_CAT), headed(V_EXT)],
        out_shape=[jax.ShapeDtypeStruct((batch, MLA_HEADS, seq, QK_CAT), BF16),
                   jax.ShapeDtypeStruct((batch, MLA_HEADS, seq, V_EXT), BF16)],
        compiler_params=_cparams("parallel", "parallel"),
        name="mla_kv",
    )(p_mla, p_mla, pos, inv_tab, kv_norm.reshape(1, -1), wkv)
    return q, k, v


def _cast_expert_block(w_ref, wo_ref):
    for f in range(wo_ref.shape[1]):
        wo_ref[0, f] = w_ref[:, f * MOE_TF:(f + 1) * MOE_TF].astype(BF16)


def _expert_cast_specs(w, n_blocks, block_of):
    n_exp, d, ff = w.shape
    nf = ff // MOE_TF
    rows = n_exp * d // n_blocks
    per_exp = d // rows
    in_spec = pl.BlockSpec((rows, ff), lambda *a: (block_of(*a), 0))
    out_spec = pl.BlockSpec((1, nf, rows, MOE_TF),
                            lambda *a: (block_of(*a) // per_exp, 0, block_of(*a) % per_exp, 0))
    out_shape = jax.ShapeDtypeStruct((n_exp, nf, d, MOE_TF), BF16)
    return in_spec, out_spec, out_shape, w.reshape(n_exp * d, ff)


def _expert_cast_call(w):
    in_spec, out_spec, out_shape, arg = _expert_cast_specs(w, EXPERT_CAST_BLOCKS, lambda i: i)
    return pl.pallas_call(
        _cast_expert_block,
        grid=(EXPERT_CAST_BLOCKS,),
        in_specs=[in_spec],
        out_specs=out_spec,
        out_shape=out_shape,
        compiler_params=_cparams("parallel"),
        name="expert_cast",
    )(arg)


def _attn_cast_kernel(qi_ref, kj_ref, q_ref, k_ref, v_ref, w_ref, o_ref, wo_ref, m_sc, acc_sc):
    _cast_expert_block(w_ref, wo_ref)
    _attn_kernel(qi_ref, kj_ref, q_ref, k_ref, v_ref, o_ref, m_sc, acc_sc)


def _attn_kernel(qi_ref, kj_ref, q_ref, k_ref, v_ref, o_ref, m_sc, acc_sc):
    t = pl.program_id(2)
    qi = qi_ref[t]
    kj = kj_ref[t]
    n_heads = q_ref.shape[1]
    tq = q_ref.shape[2]
    tk = k_ref.shape[2]
    chunk_gap = kj * (tk // CHUNK) - qi * (tq // CHUNK)

    @pl.when(kj == 0)
    def _():
        m_sc[...] = jnp.full_like(m_sc, NEG_BIG)
        acc_sc[...] = jnp.zeros_like(acc_sc)

    def update(heads, mask):
        s = [_dot_nt(q_ref[0, h], k_ref[0, h]) for h in heads]
        if mask:
            ri = lax.broadcasted_iota(jnp.int32, s[0].shape, 0) // CHUNK
            ci = lax.broadcasted_iota(jnp.int32, s[0].shape, 1) // CHUNK
            visible = ci - ri <= -chunk_gap
            s = [jnp.where(visible, x, NEG_BIG) for x in s]
        for i, h in enumerate(heads):
            m_prev = m_sc[h]
            m_new = jnp.maximum(m_prev, jnp.max(s[i], axis=-1, keepdims=True))
            alpha = jnp.exp2(m_prev - m_new)
            p = jnp.exp2(s[i] - jnp.concatenate([m_new] * (tk // LANES), axis=1))
            pv = _dot(p.astype(BF16), v_ref[0, h])
            acc_sc[h] = jnp.concatenate([alpha] * (V_EXT // LANES), axis=1) * acc_sc[h] + pv
            m_sc[h] = m_new

    pairs = [tuple(range(h, min(h + 2, n_heads))) for h in range(0, n_heads, 2)]

    fully_visible = chunk_gap <= -(tk // CHUNK - 1)

    @pl.when(fully_visible)
    def _():
        for hp in pairs:
            update(hp, False)

    @pl.when(jnp.logical_not(fully_visible))
    def _():
        for hp in pairs:
            update(hp, True)

    @pl.when((kj + 1) * tk == (qi + 1) * tq)
    def _():
        for h in range(n_heads):
            acc = acc_sc[h]
            o_ref[:, h * V_HEAD:(h + 1) * V_HEAD] = (acc[:, :V_HEAD] / acc[:, V_HEAD:]).astype(o_ref.dtype)


def _attn_call(q, k, v, batch, seq, expert_w):
    tk = _tile(seq, 512, CHUNK)
    tq = _tile(seq, 2 * tk, tk)
    nq = seq // tq
    hg = ATTN_HEADS_PER_STEP
    n_hg = MLA_HEADS // hg
    pairs = [(i, j) for i in range(nq) for j in range((i + 1) * tq // tk)]
    qi = jnp.asarray([p[0] for p in pairs], jnp.int32)
    kj = jnp.asarray([p[1] for p in pairs], jnp.int32)
    in_specs = [
        pl.BlockSpec((1, hg, tq, QK_CAT), lambda b, h, t, qi, kj: (b, h, qi[t], 0)),
        pl.BlockSpec((1, hg, tk, QK_CAT), lambda b, h, t, qi, kj: (b, h, kj[t], 0)),
        pl.BlockSpec((1, hg, tk, V_EXT), lambda b, h, t, qi, kj: (b, h, kj[t], 0)),
    ]
    out_specs = [pl.BlockSpec((tq, hg * V_HEAD), lambda b, h, t, qi, kj: (b * nq + qi[t], h))]
    out_shape = [jax.ShapeDtypeStruct((batch * seq, MLA_HEADS * V_HEAD), BF16)]
    args = (qi, kj, q, k, v)
    steps = batch * n_hg * len(pairs)
    n_blocks = 1 << (steps.bit_length() - 1)
    rows_total = expert_w.shape[0] * expert_w.shape[1]
    fused = n_blocks * EXPERT_CAST_MAX_ROWS >= rows_total
    if fused:
        def block_of(b, h, t, qi, kj):
            return jnp.minimum((b * n_hg + h) * len(pairs) + t, n_blocks - 1)

        c_in, c_out, c_shape, c_arg = _expert_cast_specs(expert_w, n_blocks, block_of)
        in_specs, out_specs, out_shape, args = in_specs + [c_in], out_specs + [c_out], out_shape + [c_shape], args + (c_arg,)
    grid_spec = pltpu.PrefetchScalarGridSpec(
        num_scalar_prefetch=2,
        grid=(batch, n_hg, len(pairs)),
        in_specs=in_specs,
        out_specs=out_specs,
        scratch_shapes=[pltpu.VMEM((hg, tq, LANES), F32), pltpu.VMEM((hg, tq, V_EXT), F32)],
    )
    outs = pl.pallas_call(
        _attn_cast_kernel if fused else _attn_kernel,
        grid_spec=grid_spec,
        out_shape=out_shape,
        compiler_params=_cparams("arbitrary", "arbitrary", "arbitrary"),
        name="mla_attn",
    )(*args)
    if fused:
        return outs[0], outs[1]
    return outs[0], _expert_cast_call(expert_w)


def _mix_out_kernel(ya_ref, yb_ref, w_ref, x_ref, g_ref, xo_ref, hn_ref):
    half = ya_ref.shape[1]
    acc = _dot(ya_ref[...], w_ref[0:half, :]) + _dot(yb_ref[...], w_ref[half:, :])
    x = x_ref[...] + acc
    xo_ref[...] = x
    hn_ref[...] = _rms(x, g_ref[...]).astype(hn_ref.dtype)


def _mix_out_call(ya, yb, w, x, g):
    m, d = x.shape
    tm = _tile(m, 512)
    half = ya.shape[1]
    row = pl.BlockSpec((tm, d), lambda i: (i, 0))
    return pl.pallas_call(
        _mix_out_kernel,
        grid=(m // tm,),
        in_specs=[pl.BlockSpec((tm, half), lambda i: (i, 0)), pl.BlockSpec((tm, half), lambda i: (i, 0)),
                  pl.BlockSpec((2 * half, d), lambda i: (0, 0)), row, pl.BlockSpec((1, d), lambda i: (0, 0))],
        out_specs=[row, row],
        out_shape=[jax.ShapeDtypeStruct((m, d), F32), jax.ShapeDtypeStruct((m, d), BF16)],
        compiler_params=_cparams("parallel"),
        name="mix_out",
    )(ya, yb, w, x, g.reshape(1, d))


def _conv_out_kernel(a_ref, w_ref, b_ref, x_ref, g_ref, r_ref, xo_ref, hn_ref, lg_ref):
    x = x_ref[...] + _dot(a_ref[...], w_ref[...]) + b_ref[...]
    xo_ref[...] = x
    hn = _rms(x, g_ref[...])
    tm = x.shape[0]
    for j in range(SLABS):
        hn_ref[pl.ds(j, tm, stride=SLABS), :] = hn[:, j * LANES:(j + 1) * LANES]
    lg_ref[...] = _dot_hp(hn, r_ref[...])


def _conv_out_call(a, w, b, x, g, router_pad):
    m, d = x.shape
    tm = _tile(m, 256)
    row = pl.BlockSpec((tm, d), lambda i: (i, 0))
    full = lambda shape: pl.BlockSpec(shape, lambda i: (0, 0))
    return pl.pallas_call(
        _conv_out_kernel,
        grid=(m // tm,),
        in_specs=[row, full((d, d)), full((1, d)), row, full((1, d)), full((d, LANES))],
        out_specs=[row, pl.BlockSpec((tm * SLABS, LANES), lambda i: (i, 0)),
                   pl.BlockSpec((tm, LANES), lambda i: (i, 0))],
        out_shape=[jax.ShapeDtypeStruct((m, d), F32), jax.ShapeDtypeStruct((m * SLABS, LANES), F32),
                   jax.ShapeDtypeStruct((m, LANES), F32)],
        compiler_params=_cparams("parallel"),
        name="conv_out",
    )(a, w, b.reshape(1, d), x, g.reshape(1, d), router_pad)


def _ffn_kernel(h_ref, w1_ref, w3_ref, w2_ref, x_ref, g_ref, xo_ref, hn_ref):
    f = pl.program_id(1)

    @pl.when(f == 0)
    def _():
        xo_ref[...] = x_ref[...]

    h = h_ref[...]
    gate = (_silu(_dot(h, w1_ref[...])) * _dot(h, w3_ref[...])).astype(BF16)
    xo_ref[...] += _dot(gate, w2_ref[...])

    @pl.when(f == pl.num_programs(1) - 1)
    def _():
        hn_ref[...] = _rms(xo_ref[...], g_ref[...]).astype(hn_ref.dtype)


def _ffn_call(h, w1, w3, w2, x, g):
    m, d = x.shape
    ff = w1.shape[1]
    tm = _tile(m, 512)
    tf = _tile(ff, 512, LANES)
    row = pl.BlockSpec((tm, d), lambda i, f: (i, 0))
    wtile = pl.BlockSpec((d, tf), lambda i, f: (0, f))
    return pl.pallas_call(
        _ffn_kernel,
        grid=(m // tm, ff // tf),
        in_specs=[row, wtile, wtile,
                  pl.BlockSpec((tf, d), lambda i, f: (f, 0)), row, pl.BlockSpec((1, d), lambda i, f: (0, 0))],
        out_specs=[row, row],
        out_shape=[jax.ShapeDtypeStruct((m, d), F32), jax.ShapeDtypeStruct((m, d), BF16)],
        compiler_params=_cparams("parallel", "arbitrary"),
        name="ffn",
    )(h, w1, w3, w2, x, g.reshape(1, d))


def _glu_kernel(h_ref, wa_ref, wb_ref, ba_ref, bb_ref, o_ref):
    h = h_ref[...]
    a = _dot(h, wa_ref[...]) + ba_ref[...]
    b = _dot(h, wb_ref[...]) + bb_ref[...]
    o_ref[...] = a * _sigmoid(b)


def _glu_call(h, w, b):
    m, d = h.shape
    n = w.shape[1] // 2
    tm = _tile(m, 1024)
    tn = _tile(n, 512, LANES)
    nj = n // tn
    b2 = b.reshape(1, 2 * n)
    return pl.pallas_call(
        _glu_kernel,
        grid=(m // tm, nj),
        in_specs=[pl.BlockSpec((tm, d), lambda i, j: (i, 0)),
                  pl.BlockSpec((d, tn), lambda i, j: (0, j)), pl.BlockSpec((d, tn), lambda i, j: (0, j + nj)),
                  pl.BlockSpec((1, tn), lambda i, j: (0, j)), pl.BlockSpec((1, tn), lambda i, j: (0, j + nj))],
        out_specs=pl.BlockSpec((tm, tn), lambda i, j: (i, j)),
        out_shape=jax.ShapeDtypeStruct((m, n), F32),
        compiler_params=_cparams("parallel", "arbitrary"),
        name="conv_glu",
    )(h, w, w, b2, b2)


def _dwconv_kernel(u_ref, halo_ref, dw_ref, dwb_ref, lg_ref, lb_ref, o_ref, ext_sc, acc_sc, *, tiles_per_seq):
    ts, d = u_ref.shape
    first = (pl.program_id(0) % tiles_per_seq) == 0
    ext_sc[0, 0:CONV_HALO, :] = jnp.where(first, 0.0, halo_ref[...])
    ext_sc[0, CONV_HALO:, :] = u_ref[...]
    n_shift = ts + CONV_HALO - SUBLANES
    for b in range(1, SUBLANES):
        ext_sc[b, 0:n_shift, :] = ext_sc[0, b:b + n_shift, :]
    rc, cc = 64, 256
    base = CONV_HALO - (CONV_WIDTH - 1)
    for c0 in range(0, d, cc):
        for r0 in range(0, ts, rc):
            acc = jnp.zeros((rc, cc), F32) + dwb_ref[:, c0:c0 + cc]
            for j in range(CONV_WIDTH):
                a8, b = divmod(base + j, SUBLANES)
                rows = slice(r0 + a8 * SUBLANES, r0 + a8 * SUBLANES + rc)
                acc = acc + dw_ref[j:j + 1, c0:c0 + cc] * ext_sc[b, rows, c0:c0 + cc]
            acc_sc[r0:r0 + rc, c0:c0 + cc] = acc
    y = acc_sc[...]
    mu = jnp.mean(y, axis=-1, keepdims=True)
    yc = y - mu
    var = jnp.mean(yc * yc, axis=-1, keepdims=True)
    yn = yc * lax.rsqrt(var + LN_EPS) * lg_ref[...] + lb_ref[...]
    o_ref[...] = _silu(yn).astype(o_ref.dtype)


def _dwconv_call(u, seq, dw_w, dw_b, ln_g, ln_b):
    m, d = u.shape
    ts = _tile(seq, 256, CONV_HALO)
    tiles_per_seq = seq // ts
    dw_pad = jnp.concatenate([dw_w, jnp.zeros((CONV_HALO - CONV_WIDTH, d), F32)], axis=0)
    full = lambda shape: pl.BlockSpec(shape, lambda i: (0, 0))
    return pl.pallas_call(
        functools.partial(_dwconv_kernel, tiles_per_seq=tiles_per_seq),
        grid=(m // ts,),
        in_specs=[pl.BlockSpec((ts, d), lambda i: (i, 0)),
                  pl.BlockSpec((CONV_HALO, d), lambda i: (jnp.maximum(i * (ts // CONV_HALO) - 1, 0), 0)),
                  full((CONV_HALO, d)), full((1, d)), full((1, d)), full((1, d))],
        out_specs=pl.BlockSpec((ts, d), lambda i: (i, 0)),
        out_shape=jax.ShapeDtypeStruct((m, d), BF16),
        scratch_shapes=[pltpu.VMEM((SUBLANES, ts + CONV_HALO, d), F32), pltpu.VMEM((ts, d), F32)],
        compiler_params=_cparams("parallel"),
        name="dwconv",
    )(u, u, dw_pad, dw_b.reshape(1, d), ln_g.reshape(1, d), ln_b.reshape(1, d))


def _route_kernel(lg_ref, tri_ref, o_ref, cnt_ref, carry_sc):
    @pl.when(pl.program_id(0) == 0)
    def _():
        carry_sc[...] = jnp.zeros_like(carry_sc)

    lg = lg_ref[...]
    lane = lax.broadcasted_iota(jnp.int32, lg.shape, 1)
    lg = jnp.where(lane < N_EXPERTS, lg, -jnp.inf)
    m1 = jnp.max(lg, axis=-1, keepdims=True)
    e1 = jnp.min(jnp.where(lg == m1, lane, LANES), axis=-1, keepdims=True)
    lg2 = jnp.where(lane == e1, -jnp.inf, lg)
    m2 = jnp.max(lg2, axis=-1, keepdims=True)
    e2 = jnp.min(jnp.where(lg2 == m2, lane, LANES), axis=-1, keepdims=True)
    ex = jnp.exp(m2 - m1)
    g1 = 1.0 / (1.0 + ex)
    g2 = ex / (1.0 + ex)
    oh1 = (lane == e1).astype(F32)
    oh2 = (lane == e2).astype(F32)
    both = oh1 + oh2
    before = _dot(tri_ref[...], both.astype(BF16)) + carry_sc[...]
    r1 = jnp.sum(before * oh1, axis=-1, keepdims=True)
    r2 = jnp.sum(before * oh2, axis=-1, keepdims=True)
    carry_sc[...] = carry_sc[...] + jnp.sum(both, axis=0, keepdims=True)
    cnt_ref[...] = jnp.broadcast_to(carry_sc[...], cnt_ref.shape)
    out = jnp.where(lane == 0, e1.astype(F32), 0.0)
    out = jnp.where(lane == 1, e2.astype(F32), out)
    out = jnp.where(lane == 2, g1, out)
    out = jnp.where(lane == 3, g2, out)
    out = jnp.where(lane == 4, r1, out)
    out = jnp.where(lane == 5, r2, out)
    o_ref[...] = out


def _route_call(logits):
    m = logits.shape[0]
    tm = _tile(m, 512)
    idx = np.arange(tm)
    tri = jnp.asarray(idx[:, None] > idx[None, :], BF16)
    return pl.pallas_call(
        _route_kernel,
        grid=(m // tm,),
        in_specs=[pl.BlockSpec((tm, LANES), lambda i: (i, 0)), pl.BlockSpec((tm, tm), lambda i: (0, 0))],
        out_specs=[pl.BlockSpec((tm, LANES), lambda i: (i, 0)), pl.BlockSpec((8, LANES), lambda i: (0, 0))],
        out_shape=[jax.ShapeDtypeStruct((m, LANES), F32), jax.ShapeDtypeStruct((8, LANES), F32)],
        scratch_shapes=[pltpu.VMEM((1, LANES), F32)],
        compiler_params=_cparams("arbitrary"),
        name="moe_route",
    )(logits, tri)


def _slab_copy(src_ref, dst_ref, sem, src_slab, dst_row):
    src = src_ref.at[pl.ds(pl.multiple_of(src_slab, SLABS), SLABS)]
    dst = dst_ref.at[pl.ds(pl.multiple_of(dst_row * SLABS, SLABS), SLABS)]
    return pltpu.make_async_copy(src, dst, sem)


def _slab_cols(ref, j, rows):
    return ref[pl.ds(j, rows, stride=SLABS), :]


def _gather_kernel(used_ref, tok_ref, src_ref, o_ref, buf_sc, sem):
    rows = o_ref.shape[0]
    used = used_ref[pl.program_id(0)] > 0

    def start(r2, c):
        for k in range(2):
            r = 2 * r2 + k
            _slab_copy(src_ref, buf_sc, sem, tok_ref[0, 0, r], r).start(priority=k)
        return c

    def wait(r, c):
        _slab_copy(src_ref, buf_sc, sem, 0, r).wait()
        return c

    @pl.when(used)
    def _():
        lax.fori_loop(0, rows // 2, start, 0, unroll=4)
        lax.fori_loop(0, rows, wait, 0, unroll=8)
        for j in range(SLABS):
            o_ref[:, j * LANES:(j + 1) * LANES] = _slab_cols(buf_sc, j, rows).astype(o_ref.dtype)

    @pl.when(jnp.logical_not(used))
    def _():
        o_ref[...] = jnp.zeros_like(o_ref)


def _gather_call(src_slabs, slot_slab, block_used):
    cap = slot_slab.shape[0]
    d = SLABS * LANES
    rows = GATHER_ROWS
    grid_spec = pltpu.PrefetchScalarGridSpec(
        num_scalar_prefetch=1,
        grid=(cap // rows,),
        in_specs=[pl.BlockSpec((1, 1, rows), lambda i, u: (i, 0, 0), memory_space=pltpu.SMEM),
                  pl.BlockSpec(memory_space=pl.ANY)],
        out_specs=pl.BlockSpec((rows, d), lambda i, u: (i, 0)),
        scratch_shapes=[pltpu.VMEM((rows * SLABS, LANES), F32), pltpu.SemaphoreType.DMA(())],
    )
    return pl.pallas_call(
        _gather_kernel,
        grid_spec=grid_spec,
        out_shape=jax.ShapeDtypeStruct((cap, d), BF16),
        compiler_params=_cparams("arbitrary"),
        name="moe_gather",
    )(block_used, slot_slab.reshape(cap // rows, 1, rows), src_slabs)


def _moe_kernel(te_ref, tr_ref, x_ref, w1_ref, w3_ref, w2_ref, o_ref, w2_sc):
    i = pl.program_id(0)
    f = pl.program_id(1)
    nrows = tr_ref[i]
    tm = x_ref.shape[0]

    @pl.when(f == 0)
    def _():
        o_ref[...] = jnp.zeros_like(o_ref)

    def swiglu(rows):
        x = x_ref[rows, :]
        gate = (_silu(_dot(x, w1_ref[0, 0])) * _dot(x, w3_ref[0, 0])).astype(BF16)
        o_ref[rows, :] += _dot(gate, w2_sc[...])

    @pl.when(nrows == tm)
    def _():
        w2_sc[...] = w2_ref[0].astype(BF16)
        swiglu(slice(0, tm))

    @pl.when(jnp.logical_and(nrows > 0, nrows < tm))
    def _():
        w2_sc[...] = w2_ref[0].astype(BF16)
        for s in range(tm // MOE_SUB):
            @pl.when(s * MOE_SUB < nrows)
            def _():
                swiglu(slice(s * MOE_SUB, (s + 1) * MOE_SUB))


def _moe_call(xs, tile_expert, tile_rows, w1t, w3t, w2):
    cap, d = xs.shape
    n_exp, nf, _, tf = w1t.shape
    tm = MOE_TM

    def fidx(i, f, te, tr):
        return jnp.where(tr[i] > 0, f, nf - 1)

    grid_spec = pltpu.PrefetchScalarGridSpec(
        num_scalar_prefetch=2,
        grid=(cap // tm, nf),
        in_specs=[
            pl.BlockSpec((tm, d), lambda i, f, te, tr: (i, 0)),
            pl.BlockSpec((1, 1, d, tf), lambda i, f, te, tr: (te[i], fidx(i, f, te, tr), 0, 0)),
            pl.BlockSpec((1, 1, d, tf), lambda i, f, te, tr: (te[i], fidx(i, f, te, tr), 0, 0)),
            pl.BlockSpec((1, tf, d), lambda i, f, te, tr: (te[i], fidx(i, f, te, tr), 0)),
        ],
        out_specs=pl.BlockSpec((tm, d), lambda i, f, te, tr: (i, 0)),
        scratch_shapes=[pltpu.VMEM((tf, d), BF16)],
    )
    return pl.pallas_call(
        _moe_kernel,
        grid_spec=grid_spec,
        out_shape=jax.ShapeDtypeStruct((cap, d), F32),
        compiler_params=_cparams("parallel", "arbitrary"),
        name="moe_experts",
    )(tile_expert, tile_rows, xs, w1t, w3t, w2)


def _row_copy(src_ref, dst_ref, sem, src_row, dst_row):
    return pltpu.make_async_copy(src_ref.at[pl.ds(src_row, 1)], dst_ref.at[pl.ds(dst_row, 1)], sem)


def _combine_kernel(pos_ref, y_ref, x_ref, rt_ref, g_ref, o_ref, buf_sc, sem):
    rows = o_ref.shape[0]

    def start(r, c):
        _row_copy(y_ref, buf_sc.at[0], sem, pos_ref[0, 0, r], r).start(priority=0)
        _row_copy(y_ref, buf_sc.at[1], sem, pos_ref[0, 1, r], r).start(priority=1)
        return c

    def wait(r, c):
        _row_copy(y_ref, buf_sc.at[0], sem, 0, r).wait()
        _row_copy(y_ref, buf_sc.at[1], sem, 0, r).wait()
        return c

    lax.fori_loop(0, rows, start, 0, unroll=8)
    lax.fori_loop(0, rows, wait, 0, unroll=8)
    rt = rt_ref[...]
    x = x_ref[...] + rt[:, 2:3] * buf_sc[0] + rt[:, 3:4] * buf_sc[1]
    o_ref[...] = _rms(x, g_ref[...])


def _combine_call(pos, yb, x, route, g):
    m, d = x.shape
    rows = GATHER_ROWS if m % GATHER_ROWS == 0 else m
    return pl.pallas_call(
        _combine_kernel,
        grid=(m // rows,),
        in_specs=[pl.BlockSpec((1, 2, rows), lambda i: (i, 0, 0), memory_space=pltpu.SMEM),
                  pl.BlockSpec(memory_space=pl.ANY),
                  pl.BlockSpec((rows, d), lambda i: (i, 0)),
                  pl.BlockSpec((rows, LANES), lambda i: (i, 0)),
                  pl.BlockSpec((1, d), lambda i: (0, 0))],
        out_specs=pl.BlockSpec((rows, d), lambda i: (i, 0)),
        out_shape=jax.ShapeDtypeStruct((m, d), F32),
        scratch_shapes=[pltpu.VMEM((2, rows, d), F32), pltpu.SemaphoreType.DMA(())],
        compiler_params=_cparams("arbitrary"),
        name="moe_combine",
    )(pos.reshape(m // rows, rows, 2).transpose(0, 2, 1), yb, x, route, g.reshape(1, d))


def _rot_cols(w):
    half = QK_ROPE // 2
    return jnp.concatenate([-w[..., half:], w[..., :half]], axis=-1)


def _mixer_weights(w_in, w_up, a_up, w_q_up):
    d = RWKV_DIM
    w_rwkv = w_in[:, :RWKV_IN]
    w_lat = w_in[:, RWKV_IN:RWKV_IN + Q_LORA + KV_LORA]
    w_kr = w_in[:, RWKV_IN + Q_LORA + KV_LORA:]
    w_mla = jnp.concatenate([w_lat, w_kr, _rot_cols(w_kr)], axis=1)
    lora = jnp.zeros((W_LORA + A_LORA, 2 * d), F32)
    lora = lora.at[:W_LORA, :d].set(w_up).at[W_LORA:, d:].set(a_up)
    wq = w_q_up.reshape(Q_LORA, MLA_HEADS, QK_NOPE + QK_ROPE)
    wq_rope = wq[..., QK_NOPE:]
    wq = jnp.concatenate([wq, _rot_cols(wq_rope)], axis=-1).reshape(Q_LORA, MLA_HEADS * QK_CAT)
    return w_rwkv.astype(BF16), w_mla.astype(BF16), lora, wq.astype(BF16)


def kernel(x, positions, l0_mix_norm, l0_w_in, l0_shift_mu, l0_w0, l0_w_up, l0_a0, l0_a_up, l0_g_up, l0_k_k, l0_k_a, l0_r_k, l0_gn_w, l0_gn_b, l0_q_norm, l0_w_q_up, l0_kv_norm, l0_w_kv_up, l0_w_out, l0_ffn_norm, l0_ffn_w1, l0_ffn_w3, l0_ffn_w2, l1_mix_norm, l1_pw1_w, l1_pw1_b, l1_dw_w, l1_dw_b, l1_ln_g, l1_ln_b, l1_pw2_w, l1_pw2_b, l1_ffn_norm, l1_router, l1_exp_w1, l1_exp_w3, l1_exp_w2, final_norm):
    batch, seq, d = x.shape
    m = batch * seq
    x0 = x.reshape(m, d)
    pos = positions.reshape(m, 1)
    inv = ROPE_THETA ** (-jnp.arange(0, QK_ROPE, 2, dtype=F32) / QK_ROPE)
    inv_tab = jnp.tile(inv, LANES // inv.shape[0]).reshape(1, LANES)

    w_rwkv, w_mla, lora_w, wq = _mixer_weights(l0_w_in, l0_w_up, l0_a_up, l0_w_q_up)

    hn0 = _rmsnorm_call(x0, l0_mix_norm)
    p_rwkv = _matmul_call(hn0, w_rwkv, name="in_proj_rwkv")
    p_mla = _matmul_call(hn0, w_mla, name="in_proj_mla")

    prep = _rwkv_prep_call(p_rwkv, seq, l0_shift_mu, lora_w, l0_w0, l0_a0, l0_g_up, l0_k_k, l0_k_a,
                           l0_r_k.reshape(-1))
    at, bt, kt, rt, v, bh, kh, gc, bonus, gate = prep
    (ra, o2, pm, qm), w3t = _rwkv_intra_call(at, bt, kt, rt, v, bh, kh, gc, seq, l1_exp_w3)
    y_rwkv = _rwkv_seq_call(ra, o2, pm, qm, bonus, gate, l0_gn_w, l0_gn_b, batch, seq)

    q, k, vv = _mla_proj_calls(p_mla, pos, inv_tab, l0_q_norm, wq, l0_kv_norm, l0_w_kv_up.astype(BF16),
                               batch, seq)
    y_mla, w1t = _attn_call(q, k, vv, batch, seq, l1_exp_w1)

    x1, hn1 = _mix_out_call(y_rwkv, y_mla, l0_w_out.astype(BF16), x0, l0_ffn_norm)

    x2, hn2 = _ffn_call(hn1, l0_ffn_w1.astype(BF16), l0_ffn_w3.astype(BF16), l0_ffn_w2.astype(BF16),
                        x1, l1_mix_norm)

    u = _glu_call(hn2, l1_pw1_w.astype(BF16), l1_pw1_b)
    sc = _dwconv_call(u, seq, l1_dw_w, l1_dw_b, l1_ln_g, l1_ln_b)
    router_pad = jnp.zeros((d, LANES), F32).at[:, :N_EXPERTS].set(l1_router)
    x3, hn3, logits = _conv_out_call(sc, l1_pw2_w.astype(BF16), l1_pw2_b, x2, l1_ffn_norm, router_pad)

    route, counts = _route_call(logits)
    e = route[:, 0:2].astype(jnp.int32)
    rank = route[:, 4:6].astype(jnp.int32)
    cnt = counts[0, :N_EXPERTS].astype(jnp.int32)
    padded = (cnt + MOE_TM - 1) // MOE_TM * MOE_TM
    pad_end = jnp.cumsum(padded)
    pad_start = pad_end - padded
    pos_slot = pad_start[e] + rank
    cap = (2 * m + N_EXPERTS * MOE_TM + MOE_TM - 1) // MOE_TM * MOE_TM
    tok = jnp.broadcast_to(jnp.arange(m, dtype=jnp.int32)[:, None], (m, 2))
    slot_slab = jnp.zeros((cap,), jnp.int32).at[pos_slot.reshape(-1)].set(tok.reshape(-1) * SLABS)
    tile_start = jnp.arange(cap // MOE_TM, dtype=jnp.int32) * MOE_TM
    tile_expert = jnp.minimum(jnp.sum(tile_start[:, None] >= pad_end[None, :], axis=1), N_EXPERTS - 1)
    used = tile_start < pad_end[-1]
    last_used = jnp.max(jnp.where(used, tile_expert, 0))
    tile_expert = jnp.where(used, tile_expert, last_used).astype(jnp.int32)
    tile_rows = jnp.clip(pad_start[tile_expert] + cnt[tile_expert] - tile_start, 0, MOE_TM)
    tile_rows = jnp.where(used, tile_rows, 0).astype(jnp.int32)

    per_tile = MOE_TM // GATHER_ROWS
    block_off = jnp.tile(jnp.arange(per_tile, dtype=jnp.int32) * GATHER_ROWS, cap // MOE_TM)
    block_used = (jnp.repeat(tile_rows, per_tile) > block_off).astype(jnp.int32)
    xs = _gather_call(hn3, slot_slab, block_used)
    yb = _moe_call(xs, tile_expert, tile_rows, w1t, w3t, l1_exp_w2)
    out = _combine_call(pos_slot, yb, x3, route, final_norm)
    return out.reshape(batch, seq, d)
```

```python
import functools

import numpy as np
import jax
import jax.numpy as jnp
from jax import lax
from jax.experimental import pallas as pl
from jax.experimental.pallas import tpu as pltpu

F32 = jnp.float32
BF16 = jnp.bfloat16

D_MODEL = 2048
CHUNK = 64
HEAD = 64
N_HEADS = 16
RWKV_DIM = N_HEADS * HEAD
W_LORA, A_LORA, G_LORA = 64, 64, 128
RWKV_IN = 3 * RWKV_DIM + W_LORA + A_LORA + G_LORA
MLA_HEADS = 8
Q_LORA = KV_LORA = 512
QK_NOPE, QK_ROPE, V_HEAD = 128, 64, 128
QK_CAT = 2 * QK_NOPE
V_EXT = 2 * V_HEAD
ROPE_THETA = 10000.0
CONV_WIDTH = 31
CONV_HALO = 32
N_EXPERTS = 8
NORM_EPS = 1e-6
LN_EPS = 1e-5
GN_EPS = HEAD * 1e-5
NEG_BIG = -1e30

LANES = 128
SUBLANES = 8
VMEM_LIMIT = 56 * 1024 * 1024

HEADS_PER_GROUP = 2
GROUP_W = HEADS_PER_GROUP * HEAD
MOE_TM = 1024
MOE_SUB = 512
MOE_TF = 512
EXPERT_CAST_BLOCKS = 512
EXPERT_CAST_MAX_ROWS = 64
GATHER_ROWS = 512
SLABS = D_MODEL // LANES
ATTN_HEADS_PER_STEP = 4
LOG2_E = 1.4426950408889634


def _cparams(*sem):
    return pltpu.CompilerParams(dimension_semantics=sem, vmem_limit_bytes=VMEM_LIMIT)


def _tile(n, pref, align=8):
    if n <= pref:
        return n
    t = (pref // align) * align
    while t > align and n % t:
        t -= align
    assert n % t == 0, (n, pref)
    return t


def _dot(a, b):
    return jnp.dot(a, b, preferred_element_type=F32)


def _dot_nt(a, b):
    return lax.dot_general(a, b, (((1,), (1,)), ((), ())), preferred_element_type=F32)


def _split2(x):
    hi = x.astype(BF16)
    lo = (x - hi.astype(F32)).astype(BF16)
    return hi, lo


def _dot_lx(a, b_exact):
    hi, lo = _split2(a)
    return _dot(hi, b_exact) + _dot(lo, b_exact)


def _dot_hp(a, b):
    ah, al = _split2(a)
    bh, bl = _split2(b)
    return _dot(ah, bh) + (_dot(ah, bl) + _dot(al, bh))


def _sigmoid(x):
    return 1.0 / (1.0 + jnp.exp(-x))


def _silu(x):
    return x * _sigmoid(x)


def _softplus(x):
    return jnp.maximum(x, 0.0) + jnp.log(1.0 + jnp.exp(-jnp.abs(x)))


def _rms(x, g):
    return x * lax.rsqrt(jnp.mean(x * x, axis=-1, keepdims=True) + NORM_EPS) * g


def _rmsnorm_kernel(x_ref, g_ref, o_ref):
    o_ref[...] = _rms(x_ref[...], g_ref[...]).astype(o_ref.dtype)


def _rmsnorm_call(x, g):
    m, d = x.shape
    tm = _tile(m, 512)
    return pl.pallas_call(
        _rmsnorm_kernel,
        grid=(m // tm,),
        in_specs=[pl.BlockSpec((tm, d), lambda i: (i, 0)), pl.BlockSpec((1, d), lambda i: (0, 0))],
        out_specs=pl.BlockSpec((tm, d), lambda i: (i, 0)),
        out_shape=jax.ShapeDtypeStruct((m, d), BF16),
        compiler_params=_cparams("parallel"),
        name="rmsnorm",
    )(x, g.reshape(1, d))


def _matmul_kernel(a_ref, w_ref, o_ref):
    o_ref[...] = _dot(a_ref[...], w_ref[...])


def _matmul_call(a, w, tm_pref=1024, tn_pref=1664, name="matmul"):
    m, k = a.shape
    n = w.shape[1]
    tm = _tile(m, tm_pref)
    tn = _tile(n, tn_pref, LANES)
    return pl.pallas_call(
        _matmul_kernel,
        grid=(m // tm, n // tn),
        in_specs=[pl.BlockSpec((tm, k), lambda i, j: (i, 0)), pl.BlockSpec((k, tn), lambda i, j: (0, j))],
        out_specs=pl.BlockSpec((tm, tn), lambda i, j: (i, j)),
        out_shape=jax.ShapeDtypeStruct((m, n), F32),
        compiler_params=_cparams("parallel", "arbitrary"),
        name=name,
    )(a, w)


def _rwkv_prep_kernel(p_ref, prev_ref, mu_ref, lw_ref, w0_ref, a0_ref, gup_ref, kk_ref, ka_ref, rk_ref,
                      tril_ref, ones_ref, seg_ref, segt_ref,
                      at_ref, bt_ref, kt_ref, rt_ref, v_ref, bh_ref, kh_ref, gc_ref, bonus_ref, g_ref,
                      *, tiles_per_seq):
    ts = p_ref.shape[0]
    first = (pl.program_id(0) % tiles_per_seq) == 0
    row0 = lax.broadcasted_iota(jnp.int32, (ts, 1), 0) == 0

    def mixed(c0, c1):
        pc = p_ref[:, c0:c1]
        prev = jnp.where(first, 0.0, prev_ref[7:8, c0:c1])
        sh = jnp.where(row0, prev, pltpu.roll(pc, 1, axis=0))
        return pc + (sh - pc) * mu_ref[:, c0:c1]

    d = RWKV_DIM
    r = mixed(0, d)
    k = mixed(d, 2 * d)
    v = mixed(2 * d, 3 * d)
    xwa = mixed(3 * d, 3 * d + W_LORA + A_LORA)
    xg = mixed(3 * d + W_LORA + A_LORA, RWKV_IN)

    lane = lax.broadcasted_iota(jnp.int32, xwa.shape, 1)
    z = jnp.where(lane < W_LORA, jnp.tanh(xwa), xwa)
    wa = _dot_hp(z, lw_ref[...])
    w = -_softplus(-(w0_ref[...] + wa[:, :d])) - 0.5
    logdecay = -jnp.exp(w)
    a = _sigmoid(a0_ref[...] + wa[:, d:])
    g_ref[...] = _dot_hp(_sigmoid(xg), gup_ref[...])

    seg = seg_ref[...]
    segt = segt_ref[...]

    def head_sum(x):
        return _dot_lx(_dot_lx(x, seg), segt)

    kkr = k * kk_ref[...]
    kk = kkr / jnp.maximum(jnp.sqrt(head_sum(kkr * kkr)), 1e-12)
    k2 = k * (1.0 + (a - 1.0) * ka_ref[...])
    bonus_ref[...] = head_sum(r * k2 * rk_ref[...]) * v

    ld_hi, ld_lo = _split2(logdecay)
    tril = tril_ref[...]
    ones = ones_ref[...]
    gcum = _dot(tril, ld_hi) + _dot(tril, ld_lo)
    gtot = _dot(ones, ld_hi) + _dot(ones, ld_lo)
    g_in = jnp.exp(gcum)
    g_ex = jnp.exp(gcum - logdecay)
    g_inv = jnp.exp(-gcum)
    g_rest = jnp.exp(gtot - gcum)
    beta = kk * a
    at_ref[...] = (-kk * g_ex).astype(at_ref.dtype)
    bt_ref[...] = (beta * g_inv).astype(bt_ref.dtype)
    kt_ref[...] = (k2 * g_inv).astype(kt_ref.dtype)
    rt_ref[...] = (r * g_in).astype(rt_ref.dtype)
    v_ref[...] = v.astype(v_ref.dtype)
    bh_ref[...] = beta * g_rest
    kh_ref[...] = k2 * g_rest
    gc_ref[...] = jnp.exp(gtot)


def _rwkv_prep_call(p, seq, mu, lora_w, w0, a0, g_up, k_k, k_a, r_k):
    m = p.shape[0]
    ts = _tile(seq, 256, CHUNK)
    tiles_per_seq = seq // ts
    d = RWKV_DIM
    idx = np.arange(ts)
    same = (idx[:, None] // CHUNK) == (idx[None, :] // CHUNK)
    tril = jnp.asarray(same & (idx[:, None] >= idx[None, :]), BF16)
    ones = jnp.asarray(same, BF16)
    lane = np.arange(d)
    seg_np = (lane[:, None] // HEAD) == np.arange(LANES)[None, :]
    seg = jnp.asarray(seg_np, BF16)
    segt = jnp.asarray(seg_np.T, BF16)

    def row(x):
        return x.reshape(1, -1).astype(F32)

    full = lambda shape: pl.BlockSpec(shape, lambda i: (0, 0))
    tok = pl.BlockSpec((ts, d), lambda i: (i, 0))
    outs = pl.pallas_call(
        functools.partial(_rwkv_prep_kernel, tiles_per_seq=tiles_per_seq),
        grid=(m // ts,),
        in_specs=[
            pl.BlockSpec((ts, RWKV_IN), lambda i: (i, 0)),
            pl.BlockSpec((8, RWKV_IN), lambda i: (jnp.maximum(i * (ts // 8) - 1, 0), 0)),
            full((1, RWKV_IN)), full((W_LORA + A_LORA, 2 * d)), full((1, d)), full((1, d)),
            full((G_LORA, d)), full((1, d)), full((1, d)), full((1, d)),
            full((ts, ts)), full((ts, ts)), full((d, LANES)), full((LANES, d)),
        ],
        out_specs=[tok] * 10,
        out_shape=[jax.ShapeDtypeStruct((m, d), BF16)] * 5 + [jax.ShapeDtypeStruct((m, d), F32)] * 5,
        compiler_params=_cparams("parallel"),
        name="rwkv_prep",
    )(p, p, row(mu), lora_w, row(w0), row(a0), g_up, row(k_k), row(k_a), row(r_k), tril, ones, seg, segt)
    return outs


def _stack_heads(x):
    lane_head = lax.broadcasted_iota(jnp.int32, x.shape, 1) // HEAD
    return jnp.concatenate([jnp.where(lane_head == h, x, 0.0) for h in range(HEADS_PER_GROUP)], axis=0)


def _unstack_heads(x):
    out = x[0:CHUNK]
    for h in range(1, HEADS_PER_GROUP):
        out = out + x[h * CHUNK:(h + 1) * CHUNK]
    return out


def _rwkv_intra_kernel(at_ref, bt_ref, kt_ref, rt_ref, v_ref, bh_ref, kh_ref, gc_ref,
                       ra_ref, o2_ref, p_ref, q_ref, *, mm):
    w = GROUP_W
    n_chunks = at_ref.shape[0] // CHUNK
    ri = lax.broadcasted_iota(jnp.int32, (w, w), 0)
    ci = lax.broadcasted_iota(jnp.int32, (w, w), 1)
    strict = (ri % CHUNK) > (ci % CHUNK)
    incl = (ri % CHUNK) >= (ci % CHUNK)
    blk16 = (ri // 16) == (ci // 16)
    eye = (ri == ci).astype(F32)

    chunks = [slice(c * CHUNK, (c + 1) * CHUNK) for c in range(n_chunks)]

    def each(f, *lists):
        return [f(*args) for args in zip(*lists)]

    def stacked(ref):
        return [_stack_heads(ref[rows, :]) for rows in chunks]

    a_s, r_s, b_s, k_s, v_s = stacked(at_ref), stacked(rt_ref), stacked(bt_ref), stacked(kt_ref), stacked(v_ref)
    s = each(lambda a, r, b, k: mm(jnp.concatenate([a, r], axis=0), jnp.concatenate([b, k], axis=0), nt=True),
             a_s, r_s, b_s, k_s)
    l_ab = [jnp.where(strict, x[:w, :w], 0.0) for x in s]
    l_ak = [jnp.where(strict, x[:w, w:], 0.0) for x in s]
    m_rb = [jnp.where(incl, x[w:, :w], 0.0) for x in s]
    m_rk = [jnp.where(incl, x[w:, w:], 0.0) for x in s]

    def mm2(a, b0, b1):
        out = mm(a, jnp.concatenate([b0, b1], axis=1))
        return out[:, :w], out[:, w:]

    dg = [jnp.where(blk16, x, 0.0) for x in l_ab]
    off = each(lambda x, y: x - y, l_ab, dg)
    td = [eye + x for x in dg]
    pw = each(mm, dg, dg)
    for _ in range(2):
        sq_ptd = each(mm2, pw, pw, td)
        td = each(lambda t_, x_: t_ + x_[1], td, sq_ptd)
        pw = [x_[0] for x_ in sq_ptd]
    td = each(lambda t_, p_: t_ + mm(p_, t_), td, pw)
    n1 = each(mm, td, off)
    sq_ntd = each(mm2, n1, n1, td)
    t1 = each(lambda t_, x_: t_ + x_[1], td, sq_ntd)
    t = each(lambda t_, x_: t_ + mm(x_[0], t_), t1, sq_ntd)

    lv = each(mm, l_ak, v_s)
    y = each(lambda tt, x, a: mm(tt, jnp.concatenate([x, a], axis=1)), t, lv, a_s)
    z = each(mm, m_rb, y)
    mv = each(mm, m_rk, v_s)
    z2 = each(lambda ref_rows, yy: mm(_stack_heads(bh_ref[ref_rows, :]).T, yy), chunks, y)
    kv = each(lambda ref_rows, vv: mm(_stack_heads(kh_ref[ref_rows, :]).T, vv), chunks, v_s)
    for c, rows in enumerate(chunks):
        ra_ref[rows, :] = _unstack_heads(r_s[c] + z[c][:, w:]).astype(ra_ref.dtype)
        o2_ref[rows, :] = _unstack_heads(z[c][:, :w] + mv[c])
        p_ref[rows, :] = _unstack_heads(eye * gc_ref[c * CHUNK:c * CHUNK + 1, :]
                                        + z2[c][:, w:]).astype(p_ref.dtype)
        q_ref[rows, :] = _unstack_heads(z2[c][:, :w] + kv[c])


def _mm_bf16(a, b, nt=False):
    a = a.astype(BF16)
    b = b.astype(BF16)
    return _dot_nt(a, b) if nt else _dot(a, b)


def _rwkv_intra_cast_kernel(at_ref, bt_ref, kt_ref, rt_ref, v_ref, bh_ref, kh_ref, gc_ref, w_ref,
                            ra_ref, o2_ref, p_ref, q_ref, wo_ref, *, mm):
    _cast_expert_block(w_ref, wo_ref)
    _rwkv_intra_kernel(at_ref, bt_ref, kt_ref, rt_ref, v_ref, bh_ref, kh_ref, gc_ref,
                       ra_ref, o2_ref, p_ref, q_ref, mm=mm)


def _rwkv_intra_call(at, bt, kt, rt, v, bh, kh, gc, seq, expert_w):
    m, d = at.shape
    ts = _tile(seq, 512, CHUNK)
    n_groups = d // GROUP_W
    steps = (m // ts) * n_groups
    spec = pl.BlockSpec((ts, GROUP_W), lambda i, j: (i, j))
    in_specs, out_specs = [spec] * 8, [spec] * 4
    out_shape = [jax.ShapeDtypeStruct((m, d), dt) for dt in (BF16, F32, BF16, F32)]
    args = (at, bt, kt, rt, v, bh, kh, gc)
    rows_total = expert_w.shape[0] * expert_w.shape[1]
    fused = steps * EXPERT_CAST_MAX_ROWS >= rows_total and rows_total % steps == 0
    if fused:
        c_in, c_out, c_shape, c_arg = _expert_cast_specs(expert_w, steps, lambda i, j: i * n_groups + j)
        in_specs, out_specs, out_shape, args = in_specs + [c_in], out_specs + [c_out], out_shape + [c_shape], args + (c_arg,)
    outs = pl.pallas_call(
        functools.partial(_rwkv_intra_cast_kernel if fused else _rwkv_intra_kernel, mm=_mm_bf16),
        grid=(m // ts, n_groups),
        in_specs=in_specs,
        out_specs=out_specs,
        out_shape=out_shape,
        compiler_params=_cparams("arbitrary", "arbitrary"),
        name="rwkv_intra",
    )(*args)
    if fused:
        return outs[:4], outs[4]
    return outs, _expert_cast_call(expert_w)


def _rwkv_seq_kernel(ra_ref, o2_ref, p_ref, q_ref, bonus_ref, g_ref, gnw_ref, gnb_ref, seg_ref, segt_ref,
                     o_ref, h_sc, y_sc, *, mm):
    w = GROUP_W
    n_groups = RWKV_DIM // w
    n_chunks = ra_ref.shape[0] // CHUNK

    @pl.when(pl.program_id(1) == 0)
    def _():
        h_sc[...] = jnp.zeros_like(h_sc)

    def chunk(c, carry):
        r0 = pl.multiple_of(c * CHUNK, CHUNK)
        rows = pl.ds(r0, CHUNK)
        groups = [slice(gi * w, (gi + 1) * w) for gi in range(n_groups)]
        lhs = [jnp.concatenate([ra_ref[rows, cols], _stack_heads(p_ref[rows, cols])], axis=0) for cols in groups]
        res = [mm(lhs[gi], h_sc[gi]) for gi in range(n_groups)]
        for gi, cols in enumerate(groups):
            y_sc[rows, cols] = res[gi][:CHUNK] + o2_ref[rows, cols]
            h_sc[gi] = res[gi][CHUNK:] + _stack_heads(q_ref[rows, cols])
        return carry

    lax.fori_loop(0, n_chunks, chunk, 0)

    y = y_sc[...]
    seg = seg_ref[...]
    segt = segt_ref[...]
    mean = _dot_lx(_dot_lx(y, seg), segt) * (1.0 / HEAD)
    yc = y - mean
    var = _dot_lx(_dot_lx(yc * yc, seg), segt) * (1.0 / HEAD)
    yn = yc * lax.rsqrt(var + GN_EPS) * gnw_ref[...] + gnb_ref[...]
    o_ref[...] = ((yn + bonus_ref[...]) * g_ref[...]).astype(o_ref.dtype)


def _rwkv_seq_call(ra, o2, p, q, bonus, g, gn_w, gn_b, batch, seq):
    m, d = ra.shape
    ts = _tile(seq, 512, CHUNK)
    nt = seq // ts
    lane = np.arange(d)
    seg_np = (lane[:, None] // HEAD) == np.arange(LANES)[None, :]
    seg = jnp.asarray(seg_np, BF16)
    segt = jnp.asarray(seg_np.T, BF16)
    tok = pl.BlockSpec((ts, d), lambda b, i: (b * nt + i, 0))
    full = lambda shape: pl.BlockSpec(shape, lambda b, i: (0, 0))
    return pl.pallas_call(
        functools.partial(_rwkv_seq_kernel, mm=_mm_bf16),
        grid=(batch, nt),
        in_specs=[tok] * 6 + [full((1, d)), full((1, d)), full((d, LANES)), full((LANES, d))],
        out_specs=tok,
        out_shape=jax.ShapeDtypeStruct((m, d), BF16),
        scratch_shapes=[pltpu.VMEM((d // GROUP_W, GROUP_W, GROUP_W), F32), pltpu.VMEM((ts, d), F32)],
        compiler_params=_cparams("arbitrary", "arbitrary"),
        name="rwkv_seq",
    )(ra, o2, p, q, bonus, g, gn_w.reshape(1, d), gn_b.reshape(1, d), seg, segt)


def _rope_tab(pos_ref, inv_ref):
    ang = pos_ref[...].astype(F32) * inv_ref[...]
    lane = lax.broadcasted_iota(jnp.int32, ang.shape, 1)
    return jnp.where(lane < QK_ROPE, jnp.cos(ang), jnp.sin(ang))


def _mla_q_kernel(lat_ref, pos_ref, inv_ref, g_ref, w_ref, o_ref, *, scale):
    hn = _rms(lat_ref[...], g_ref[...]).astype(BF16)
    q = _dot(hn, w_ref[...]) * scale
    tab = _rope_tab(pos_ref, inv_ref)
    for h in range(MLA_HEADS):
        c0 = h * QK_CAT
        o_ref[0, h, :, 0:QK_NOPE] = q[:, c0:c0 + QK_NOPE].astype(BF16)
        o_ref[0, h, :, QK_NOPE:QK_CAT] = (q[:, c0 + QK_NOPE:c0 + QK_CAT] * tab).astype(BF16)


def _mla_kv_kernel(lat_ref, kr_ref, pos_ref, inv_ref, g_ref, w_ref, k_ref, v_ref):
    hn = _rms(lat_ref[...], g_ref[...]).astype(BF16)
    kv = _dot(hn, w_ref[...])
    t = kr_ref[...] * _rope_tab(pos_ref, inv_ref)
    k_rope = (t + pltpu.roll(t, QK_ROPE, axis=1)).astype(BF16)
    for h in range(MLA_HEADS):
        c0 = h * (QK_NOPE + V_HEAD)
        k_ref[0, h, :, 0:QK_NOPE] = kv[:, c0:c0 + QK_NOPE].astype(BF16)
        k_ref[0, h, :, QK_NOPE:QK_CAT] = k_rope
        v_ref[0, h, :, 0:V_HEAD] = kv[:, c0 + QK_NOPE:c0 + QK_NOPE + V_HEAD].astype(BF16)
        v_ref[0, h, :, V_HEAD:V_EXT] = jnp.ones((kv.shape[0], V_EXT - V_HEAD), BF16)


def _mla_proj_calls(p_mla, pos, inv_tab, q_norm, wq, kv_norm, wkv, batch, seq):
    tm = _tile(seq, 512)
    nt = seq // tm
    scale = float((QK_NOPE + QK_ROPE) ** -0.5) * LOG2_E
    lat = lambda c: pl.BlockSpec((tm, Q_LORA), lambda b, i: (b * nt + i, c))
    posspec = pl.BlockSpec((tm, 1), lambda b, i: (b * nt + i, 0))
    full = lambda shape: pl.BlockSpec(shape, lambda b, i: (0, 0))
    headed = lambda w: pl.BlockSpec((1, MLA_HEADS, tm, w), lambda b, i: (b, 0, i, 0))
    q = pl.pallas_call(
        functools.partial(_mla_q_kernel, scale=scale),
        grid=(batch, nt),
        in_specs=[lat(0), posspec, full((1, LANES)), full((1, Q_LORA)), full((Q_LORA, MLA_HEADS * QK_CAT))],
        out_specs=headed(QK_CAT),
        out_shape=jax.ShapeDtypeStruct((batch, MLA_HEADS, seq, QK_CAT), BF16),
        compiler_params=_cparams("parallel", "parallel"),
        name="mla_q",
    )(p_mla, pos, inv_tab, q_norm.reshape(1, -1), wq)
    k, v = pl.pallas_call(
        _mla_kv_kernel,
        grid=(batch, nt),
        in_specs=[lat(1), pl.BlockSpec((tm, LANES), lambda b, i: (b * nt + i, 2 * Q_LORA // LANES)),
                  posspec, full((1, LANES)), full((1, KV_LORA)),
                  full((KV_LORA, MLA_HEADS * (QK_NOPE + V_HEAD)))],
        out_specs=[headed(QK_CAT), headed(V_EXT)],
        out_shape=[jax.ShapeDtypeStruct((batch, MLA_HEADS, seq, QK_CAT), BF16),
                   jax.ShapeDtypeStruct((batch, MLA_HEADS, seq, V_EXT), BF16)],
        compiler_params=_cparams("parallel", "parallel"),
        name="mla_kv",
    )(p_mla, p_mla, pos, inv_tab, kv_norm.reshape(1, -1), wkv)
    return q, k, v


def _cast_expert_block(w_ref, wo_ref):
    for f in range(wo_ref.shape[1]):
        wo_ref[0, f] = w_ref[:, f * MOE_TF:(f + 1) * MOE_TF].astype(BF16)


def _expert_cast_specs(w, n_blocks, block_of):
    n_exp, d, ff = w.shape
    nf = ff // MOE_TF
    rows = n_exp * d // n_blocks
    per_exp = d // rows
    in_spec = pl.BlockSpec((rows, ff), lambda *a: (block_of(*a), 0))
    out_spec = pl.BlockSpec((1, nf, rows, MOE_TF),
                            lambda *a: (block_of(*a) // per_exp, 0, block_of(*a) % per_exp, 0))
    out_shape = jax.ShapeDtypeStruct((n_exp, nf, d, MOE_TF), BF16)
    return in_spec, out_spec, out_shape, w.reshape(n_exp * d, ff)


def _expert_cast_call(w):
    in_spec, out_spec, out_shape, arg = _expert_cast_specs(w, EXPERT_CAST_BLOCKS, lambda i: i)
    return pl.pallas_call(
        _cast_expert_block,
        grid=(EXPERT_CAST_BLOCKS,),
        in_specs=[in_spec],
        out_specs=out_spec,
        out_shape=out_shape,
        compiler_params=_cparams("parallel"),
        name="expert_cast",
    )(arg)


def _attn_cast_kernel(qi_ref, kj_ref, q_ref, k_ref, v_ref, w_ref, o_ref, wo_ref, m_sc, acc_sc):
    _cast_expert_block(w_ref, wo_ref)
    _attn_kernel(qi_ref, kj_ref, q_ref, k_ref, v_ref, o_ref, m_sc, acc_sc)


def _attn_kernel(qi_ref, kj_ref, q_ref, k_ref, v_ref, o_ref, m_sc, acc_sc):
    t = pl.program_id(2)
    qi = qi_ref[t]
    kj = kj_ref[t]
    n_heads = q_ref.shape[1]
    tq = q_ref.shape[2]
    tk = k_ref.shape[2]
    chunk_gap = kj * (tk // CHUNK) - qi * (tq // CHUNK)

    @pl.when(kj == 0)
    def _():
        m_sc[...] = jnp.full_like(m_sc, NEG_BIG)
        acc_sc[...] = jnp.zeros_like(acc_sc)

    def update(heads, mask):
        s = [_dot_nt(q_ref[0, h], k_ref[0, h]) for h in heads]
        if mask:
            ri = lax.broadcasted_iota(jnp.int32, s[0].shape, 0) // CHUNK
            ci = lax.broadcasted_iota(jnp.int32, s[0].shape, 1) // CHUNK
            visible = ci - ri <= -chunk_gap
            s = [jnp.where(visible, x, NEG_BIG) for x in s]
        for i, h in enumerate(heads):
            m_prev = m_sc[h]
            m_new = jnp.maximum(m_prev, jnp.max(s[i], axis=-1, keepdims=True))
            alpha = jnp.exp2(m_prev - m_new)
            p = jnp.exp2(s[i] - jnp.concatenate([m_new] * (tk // LANES), axis=1))
            pv = _dot(p.astype(BF16), v_ref[0, h])
            acc_sc[h] = jnp.concatenate([alpha] * (V_EXT // LANES), axis=1) * acc_sc[h] + pv
            m_sc[h] = m_new

    pairs = [tuple(range(h, min(h + 2, n_heads))) for h in range(0, n_heads, 2)]

    fully_visible = chunk_gap <= -(tk // CHUNK - 1)

    @pl.when(fully_visible)
    def _():
        for hp in pairs:
            update(hp, False)

    @pl.when(jnp.logical_not(fully_visible))
    def _():
        for hp in pairs:
            update(hp, True)

    @pl.when((kj + 1) * tk == (qi + 1) * tq)
    def _():
        for h in range(n_heads):
            acc = acc_sc[h]
            o_ref[:, h * V_HEAD:(h + 1) * V_HEAD] = (acc[:, :V_HEAD] / acc[:, V_HEAD:]).astype(o_ref.dtype)


def _attn_call(q, k, v, batch, seq, expert_w):
    tk = _tile(seq, 512, CHUNK)
    tq = _tile(seq, 2 * tk, tk)
    nq = seq // tq
    hg = ATTN_HEADS_PER_STEP
    n_hg = MLA_HEADS // hg
    pairs = [(i, j) for i in range(nq) for j in range((i + 1) * tq // tk)]
    qi = jnp.asarray([p[0] for p in pairs], jnp.int32)
    kj = jnp.asarray([p[1] for p in pairs], jnp.int32)
    in_specs = [
        pl.BlockSpec((1, hg, tq, QK_CAT), lambda b, h, t, qi, kj: (b, h, qi[t], 0)),
        pl.BlockSpec((1, hg, tk, QK_CAT), lambda b, h, t, qi, kj: (b, h, kj[t], 0)),
        pl.BlockSpec((1, hg, tk, V_EXT), lambda b, h, t, qi, kj: (b, h, kj[t], 0)),
    ]
    out_specs = [pl.BlockSpec((tq, hg * V_HEAD), lambda b, h, t, qi, kj: (b * nq + qi[t], h))]
    out_shape = [jax.ShapeDtypeStruct((batch * seq, MLA_HEADS * V_HEAD), BF16)]
    args = (qi, kj, q, k, v)
    steps = batch * n_hg * len(pairs)
    n_blocks = 1 << (steps.bit_length() - 1)
    rows_total = expert_w.shape[0] * expert_w.shape[1]
    fused = n_blocks * EXPERT_CAST_MAX_ROWS >= rows_total
    if fused:
        def block_of(b, h, t, qi, kj):
            return jnp.minimum((b * n_hg + h) * len(pairs) + t, n_blocks - 1)

        c_in, c_out, c_shape, c_arg = _expert_cast_specs(expert_w, n_blocks, block_of)
        in_specs, out_specs, out_shape, args = in_specs + [c_in], out_specs + [c_out], out_shape + [c_shape], args + (c_arg,)
    grid_spec = pltpu.PrefetchScalarGridSpec(
        num_scalar_prefetch=2,
        grid=(batch, n_hg, len(pairs)),
        in_specs=in_specs,
        out_specs=out_specs,
        scratch_shapes=[pltpu.VMEM((hg, tq, LANES), F32), pltpu.VMEM((hg, tq, V_EXT), F32)],
    )
    outs = pl.pallas_call(
        _attn_cast_kernel if fused else _attn_kernel,
        grid_spec=grid_spec,
        out_shape=out_shape,
        compiler_params=_cparams("arbitrary", "arbitrary", "arbitrary"),
        name="mla_attn",
    )(*args)
    if fused:
        return outs[0], outs[1]
    return outs[0], _expert_cast_call(expert_w)


def _mix_out_kernel(ya_ref, yb_ref, w_ref, x_ref, g_ref, xo_ref, hn_ref):
    half = ya_ref.shape[1]
    acc = _dot(ya_ref[...], w_ref[0:half, :]) + _dot(yb_ref[...], w_ref[half:, :])
    x = x_ref[...] + acc
    xo_ref[...] = x
    hn_ref[...] = _rms(x, g_ref[...]).astype(hn_ref.dtype)


def _mix_out_call(ya, yb, w, x, g):
    m, d = x.shape
    tm = _tile(m, 512)
    half = ya.shape[1]
    row = pl.BlockSpec((tm, d), lambda i: (i, 0))
    return pl.pallas_call(
        _mix_out_kernel,
        grid=(m // tm,),
        in_specs=[pl.BlockSpec((tm, half), lambda i: (i, 0)), pl.BlockSpec((tm, half), lambda i: (i, 0)),
                  pl.BlockSpec((2 * half, d), lambda i: (0, 0)), row, pl.BlockSpec((1, d), lambda i: (0, 0))],
        out_specs=[row, row],
        out_shape=[jax.ShapeDtypeStruct((m, d), F32), jax.ShapeDtypeStruct((m, d), BF16)],
        compiler_params=_cparams("parallel"),
        name="mix_out",
    )(ya, yb, w, x, g.reshape(1, d))


def _conv_out_kernel(a_ref, w_ref, b_ref, x_ref, g_ref, r_ref, xo_ref, hn_ref, lg_ref):
    x = x_ref[...] + _dot(a_ref[...], w_ref[...]) + b_ref[...]
    xo_ref[...] = x
    hn = _rms(x, g_ref[...])
    tm = x.shape[0]
    for j in range(SLABS):
        hn_ref[pl.ds(j, tm, stride=SLABS), :] = hn[:, j * LANES:(j + 1) * LANES]
    lg_ref[...] = _dot_hp(hn, r_ref[...])


def _conv_out_call(a, w, b, x, g, router_pad):
    m, d = x.shape
    tm = _tile(m, 256)
    row = pl.BlockSpec((tm, d), lambda i: (i, 0))
    full = lambda shape: pl.BlockSpec(shape, lambda i: (0, 0))
    return pl.pallas_call(
        _conv_out_kernel,
        grid=(m // tm,),
        in_specs=[row, full((d, d)), full((1, d)), row, full((1, d)), full((d, LANES))],
        out_specs=[row, pl.BlockSpec((tm * SLABS, LANES), lambda i: (i, 0)),
                   pl.BlockSpec((tm, LANES), lambda i: (i, 0))],
        out_shape=[jax.ShapeDtypeStruct((m, d), F32), jax.ShapeDtypeStruct((m * SLABS, LANES), F32),
                   jax.ShapeDtypeStruct((m, LANES), F32)],
        compiler_params=_cparams("parallel"),
        name="conv_out",
    )(a, w, b.reshape(1, d), x, g.reshape(1, d), router_pad)


def _ffn_kernel(h_ref, w1_ref, w3_ref, w2_ref, x_ref, g_ref, xo_ref, hn_ref):
    f = pl.program_id(1)

    @pl.when(f == 0)
    def _():
        xo_ref[...] = x_ref[...]

    h = h_ref[...]
    gate = (_silu(_dot(h, w1_ref[...])) * _dot(h, w3_ref[...])).astype(BF16)
    xo_ref[...] += _dot(gate, w2_ref[...])

    @pl.when(f == pl.num_programs(1) - 1)
    def _():
        hn_ref[...] = _rms(xo_ref[...], g_ref[...]).astype(hn_ref.dtype)


def _ffn_call(h, w1, w3, w2, x, g):
    m, d = x.shape
    ff = w1.shape[1]
    tm = _tile(m, 512)
    tf = _tile(ff, 512, LANES)
    row = pl.BlockSpec((tm, d), lambda i, f: (i, 0))
    wtile = pl.BlockSpec((d, tf), lambda i, f: (0, f))
    return pl.pallas_call(
        _ffn_kernel,
        grid=(m // tm, ff // tf),
        in_specs=[row, wtile, wtile,
                  pl.BlockSpec((tf, d), lambda i, f: (f, 0)), row, pl.BlockSpec((1, d), lambda i, f: (0, 0))],
        out_specs=[row, row],
        out_shape=[jax.ShapeDtypeStruct((m, d), F32), jax.ShapeDtypeStruct((m, d), BF16)],
        compiler_params=_cparams("parallel", "arbitrary"),
        name="ffn",
    )(h, w1, w3, w2, x, g.reshape(1, d))


def _glu_kernel(h_ref, wa_ref, wb_ref, ba_ref, bb_ref, o_ref):
    h = h_ref[...]
    a = _dot(h, wa_ref[...]) + ba_ref[...]
    b = _dot(h, wb_ref[...]) + bb_ref[...]
    o_ref[...] = a * _sigmoid(b)


def _glu_call(h, w, b):
    m, d = h.shape
    n = w.shape[1] // 2
    tm = _tile(m, 1024)
    tn = _tile(n, 512, LANES)
    nj = n // tn
    b2 = b.reshape(1, 2 * n)
    return pl.pallas_call(
        _glu_kernel,
        grid=(m // tm, nj),
        in_specs=[pl.BlockSpec((tm, d), lambda i, j: (i, 0)),
                  pl.BlockSpec((d, tn), lambda i, j: (0, j)), pl.BlockSpec((d, tn), lambda i, j: (0, j + nj)),
                  pl.BlockSpec((1, tn), lambda i, j: (0, j)), pl.BlockSpec((1, tn), lambda i, j: (0, j + nj))],
        out_specs=pl.BlockSpec((tm, tn), lambda i, j: (i, j)),
        out_shape=jax.ShapeDtypeStruct((m, n), F32),
        compiler_params=_cparams("parallel", "arbitrary"),
        name="conv_glu",
    )(h, w, w, b2, b2)


def _dwconv_kernel(u_ref, halo_ref, dw_ref, dwb_ref, lg_ref, lb_ref, o_ref, ext_sc, acc_sc, *, tiles_per_seq):
    ts, d = u_ref.shape
    first = (pl.program_id(0) % tiles_per_seq) == 0
    ext_sc[0, 0:CONV_HALO, :] = jnp.where(first, 0.0, halo_ref[...])
    ext_sc[0, CONV_HALO:, :] = u_ref[...]
    n_shift = ts + CONV_HALO - SUBLANES
    for b in range(1, SUBLANES):
        ext_sc[b, 0:n_shift, :] = ext_sc[0, b:b + n_shift, :]
    rc, cc = 64, 256
    base = CONV_HALO - (CONV_WIDTH - 1)
    for c0 in range(0, d, cc):
        for r0 in range(0, ts, rc):
            acc = jnp.zeros((rc, cc), F32) + dwb_ref[:, c0:c0 + cc]
            for j in range(CONV_WIDTH):
                a8, b = divmod(base + j, SUBLANES)
                rows = slice(r0 + a8 * SUBLANES, r0 + a8 * SUBLANES + rc)
                acc = acc + dw_ref[j:j + 1, c0:c0 + cc] * ext_sc[b, rows, c0:c0 + cc]
            acc_sc[r0:r0 + rc, c0:c0 + cc] = acc
    y = acc_sc[...]
    mu = jnp.mean(y, axis=-1, keepdims=True)
    yc = y - mu
    var = jnp.mean(yc * yc, axis=-1, keepdims=True)
    yn = yc * lax.rsqrt(var + LN_EPS) * lg_ref[...] + lb_ref[...]
    o_ref[...] = _silu(yn).astype(o_ref.dtype)


def _dwconv_call(u, seq, dw_w, dw_b, ln_g, ln_b):
    m, d = u.shape
    ts = _tile(seq, 256, CONV_HALO)
    tiles_per_seq = seq // ts
    dw_pad = jnp.concatenate([dw_w, jnp.zeros((CONV_HALO - CONV_WIDTH, d), F32)], axis=0)
    full = lambda shape: pl.BlockSpec(shape, lambda i: (0, 0))
    return pl.pallas_call(
        functools.partial(_dwconv_kernel, tiles_per_seq=tiles_per_seq),
        grid=(m // ts,),
        in_specs=[pl.BlockSpec((ts, d), lambda i: (i, 0)),
                  pl.BlockSpec((CONV_HALO, d), lambda i: (jnp.maximum(i * (ts // CONV_HALO) - 1, 0), 0)),
                  full((CONV_HALO, d)), full((1, d)), full((1, d)), full((1, d))],
        out_specs=pl.BlockSpec((ts, d), lambda i: (i, 0)),
        out_shape=jax.ShapeDtypeStruct((m, d), BF16),
        scratch_shapes=[pltpu.VMEM((SUBLANES, ts + CONV_HALO, d), F32), pltpu.VMEM((ts, d), F32)],
        compiler_params=_cparams("parallel"),
        name="dwconv",
    )(u, u, dw_pad, dw_b.reshape(1, d), ln_g.reshape(1, d), ln_b.reshape(1, d))


def _route_kernel(lg_ref, tri_ref, o_ref, cnt_ref, carry_sc):
    @pl.when(pl.program_id(0) == 0)
    def _():
        carry_sc[...] = jnp.zeros_like(carry_sc)

    lg = lg_ref[...]
    lane = lax.broadcasted_iota(jnp.int32, lg.shape, 1)
    lg = jnp.where(lane < N_EXPERTS, lg, -jnp.inf)
    m1 = jnp.max(lg, axis=-1, keepdims=True)
    e1 = jnp.min(jnp.where(lg == m1, lane, LANES), axis=-1, keepdims=True)
    lg2 = jnp.where(lane == e1, -jnp.inf, lg)
    m2 = jnp.max(lg2, axis=-1, keepdims=True)
    e2 = jnp.min(jnp.where(lg2 == m2, lane, LANES), axis=-1, keepdims=True)
    ex = jnp.exp(m2 - m1)
    g1 = 1.0 / (1.0 + ex)
    g2 = ex / (1.0 + ex)
    oh1 = (lane == e1).astype(F32)
    oh2 = (lane == e2).astype(F32)
    both = oh1 + oh2
    before = _dot(tri_ref[...], both.astype(BF16)) + carry_sc[...]
    r1 = jnp.sum(before * oh1, axis=-1, keepdims=True)
    r2 = jnp.sum(before * oh2, axis=-1, keepdims=True)
    carry_sc[...] = carry_sc[...] + jnp.sum(both, axis=0, keepdims=True)
    cnt_ref[...] = jnp.broadcast_to(carry_sc[...], cnt_ref.shape)
    out = jnp.where(lane == 0, e1.astype(F32), 0.0)
    out = jnp.where(lane == 1, e2.astype(F32), out)
    out = jnp.where(lane == 2, g1, out)
    out = jnp.where(lane == 3, g2, out)
    out = jnp.where(lane == 4, r1, out)
    out = jnp.where(lane == 5, r2, out)
    o_ref[...] = out


def _route_call(logits):
    m = logits.shape[0]
    tm = _tile(m, 512)
    idx = np.arange(tm)
    tri = jnp.asarray(idx[:, None] > idx[None, :], BF16)
    return pl.pallas_call(
        _route_kernel,
        grid=(m // tm,),
        in_specs=[pl.BlockSpec((tm, LANES), lambda i: (i, 0)), pl.BlockSpec((tm, tm), lambda i: (0, 0))],
        out_specs=[pl.BlockSpec((tm, LANES), lambda i: (i, 0)), pl.BlockSpec((8, LANES), lambda i: (0, 0))],
        out_shape=[jax.ShapeDtypeStruct((m, LANES), F32), jax.ShapeDtypeStruct((8, LANES), F32)],
        scratch_shapes=[pltpu.VMEM((1, LANES), F32)],
        compiler_params=_cparams("arbitrary"),
        name="moe_route",
    )(logits, tri)


def _slab_copy(src_ref, dst_ref, sem, src_slab, dst_row):
    src = src_ref.at[pl.ds(pl.multiple_of(src_slab, SLABS), SLABS)]
    dst = dst_ref.at[pl.ds(pl.multiple_of(dst_row * SLABS, SLABS), SLABS)]
    return pltpu.make_async_copy(src, dst, sem)


def _slab_cols(ref, j, rows):
    return ref[pl.ds(j, rows, stride=SLABS), :]


def _gather_kernel(used_ref, tok_ref, nxt_ref, src_ref, o_ref, buf_sc, sem):
    rows = o_ref.shape[0]
    i = pl.program_id(0)
    last = pl.num_programs(0) - 1

    def start_block(idx_ref, slot):
        def body(r, c):
            _slab_copy(src_ref, buf_sc.at[slot], sem.at[slot], idx_ref[0, 0, r], r).start()
            return c
        lax.fori_loop(0, rows, body, 0, unroll=8)

    def wait_block(slot):
        def body(r, c):
            _slab_copy(src_ref, buf_sc.at[slot], sem.at[slot], 0, r).wait()
            return c
        lax.fori_loop(0, rows, body, 0, unroll=8)

    @pl.when(jnp.logical_and(i == 0, used_ref[0] > 0))
    def _():
        start_block(tok_ref, 0)

    for slot in range(2):
        @pl.when(jnp.logical_and(i % 2 == slot, jnp.logical_and(i < last, used_ref[jnp.minimum(i + 1, last)] > 0)))
        def _():
            start_block(nxt_ref, 1 - slot)

        @pl.when(jnp.logical_and(i % 2 == slot, used_ref[i] > 0))
        def _():
            wait_block(slot)
            for j in range(SLABS):
                o_ref[:, j * LANES:(j + 1) * LANES] = _slab_cols(buf_sc.at[slot], j, rows).astype(o_ref.dtype)

    @pl.when(used_ref[i] == 0)
    def _():
        o_ref[...] = jnp.zeros_like(o_ref)


def _gather_call(src_slabs, slot_slab, block_used):
    cap = slot_slab.shape[0]
    d = SLABS * LANES
    rows = GATHER_ROWS
    nb = cap // rows
    grid_spec = pltpu.PrefetchScalarGridSpec(
        num_scalar_prefetch=1,
        grid=(nb,),
        in_specs=[pl.BlockSpec((1, 1, rows), lambda i, u: (i, 0, 0), memory_space=pltpu.SMEM),
                  pl.BlockSpec((1, 1, rows), lambda i, u: (jnp.minimum(i + 1, nb - 1), 0, 0),
                               memory_space=pltpu.SMEM),
                  pl.BlockSpec(memory_space=pl.ANY)],
        out_specs=pl.BlockSpec((rows, d), lambda i, u: (i, 0)),
        scratch_shapes=[pltpu.VMEM((2, rows * SLABS, LANES), F32), pltpu.SemaphoreType.DMA((2,))],
    )
    slots = slot_slab.reshape(nb, 1, rows)
    return pl.pallas_call(
        _gather_kernel,
        grid_spec=grid_spec,
        out_shape=jax.ShapeDtypeStruct((cap, d), BF16),
        compiler_params=_cparams("arbitrary"),
        name="moe_gather",
    )(block_used, slots, slots, src_slabs)


def _moe_kernel(te_ref, tr_ref, x_ref, w1_ref, w3_ref, w2_ref, o_ref, w2_sc):
    i = pl.program_id(0)
    f = pl.program_id(1)
    nrows = tr_ref[i]
    tm = x_ref.shape[0]

    @pl.when(f == 0)
    def _():
        o_ref[...] = jnp.zeros_like(o_ref)

    def swiglu(rows):
        x = x_ref[rows, :]
        gate = (_silu(_dot(x, w1_ref[0, 0])) * _dot(x, w3_ref[0, 0])).astype(BF16)
        o_ref[rows, :] += _dot(gate, w2_sc[...])

    @pl.when(nrows == tm)
    def _():
        w2_sc[...] = w2_ref[0].astype(BF16)
        swiglu(slice(0, tm))

    @pl.when(jnp.logical_and(nrows > 0, nrows < tm))
    def _():
        w2_sc[...] = w2_ref[0].astype(BF16)
        for s in range(tm // MOE_SUB):
            @pl.when(s * MOE_SUB < nrows)
            def _():
                swiglu(slice(s * MOE_SUB, (s + 1) * MOE_SUB))


def _moe_call(xs, tile_expert, tile_rows, w1t, w3t, w2):
    cap, d = xs.shape
    n_exp, nf, _, tf = w1t.shape
    tm = MOE_TM

    def fidx(i, f, te, tr):
        return jnp.where(tr[i] > 0, f, nf - 1)

    grid_spec = pltpu.PrefetchScalarGridSpec(
        num_scalar_prefetch=2,
        grid=(cap // tm, nf),
        in_specs=[
            pl.BlockSpec((tm, d), lambda i, f, te, tr: (i, 0)),
            pl.BlockSpec((1, 1, d, tf), lambda i, f, te, tr: (te[i], fidx(i, f, te, tr), 0, 0)),
            pl.BlockSpec((1, 1, d, tf), lambda i, f, te, tr: (te[i], fidx(i, f, te, tr), 0, 0)),
            pl.BlockSpec((1, tf, d), lambda i, f, te, tr: (te[i], fidx(i, f, te, tr), 0)),
        ],
        out_specs=pl.BlockSpec((tm, d), lambda i, f, te, tr: (i, 0)),
        scratch_shapes=[pltpu.VMEM((tf, d), BF16)],
    )
    return pl.pallas_call(
        _moe_kernel,
        grid_spec=grid_spec,
        out_shape=jax.ShapeDtypeStruct((cap, d), F32),
        compiler_params=_cparams("parallel", "arbitrary"),
        name="moe_experts",
    )(tile_expert, tile_rows, xs, w1t, w3t, w2)


def _row_copy(src_ref, dst_ref, sem, src_row, dst_row):
    return pltpu.make_async_copy(src_ref.at[pl.ds(src_row, 1)], dst_ref.at[pl.ds(dst_row, 1)], sem)


def _combine_kernel(pos_ref, nxt_ref, y_ref, x_ref, rt_ref, g_ref, o_ref, buf_sc, sem):
    rows = o_ref.shape[0]
    i = pl.program_id(0)
    last = pl.num_programs(0) - 1

    def start_block(idx_ref, slot):
        def body(r, c):
            for k in range(2):
                _row_copy(y_ref, buf_sc.at[slot, k], sem.at[slot], idx_ref[0, k, r], r).start()
            return c
        lax.fori_loop(0, rows, body, 0, unroll=8)

    def wait_block(slot):
        def body(r, c):
            for k in range(2):
                _row_copy(y_ref, buf_sc.at[slot, k], sem.at[slot], 0, r).wait()
            return c
        lax.fori_loop(0, rows, body, 0, unroll=8)

    @pl.when(i == 0)
    def _():
        start_block(pos_ref, 0)

    for slot in range(2):
        @pl.when(jnp.logical_and(i % 2 == slot, i < last))
        def _():
            start_block(nxt_ref, 1 - slot)

        @pl.when(i % 2 == slot)
        def _():
            wait_block(slot)
            rt = rt_ref[...]
            x = x_ref[...] + rt[:, 2:3] * buf_sc[slot, 0] + rt[:, 3:4] * buf_sc[slot, 1]
            o_ref[...] = _rms(x, g_ref[...])


def _combine_call(pos, yb, x, route, g):
    m, d = x.shape
    rows = GATHER_ROWS if m % GATHER_ROWS == 0 else m
    nb = m // rows
    slots = pos.reshape(nb, rows, 2).transpose(0, 2, 1)
    return pl.pallas_call(
        _combine_kernel,
        grid=(nb,),
        in_specs=[pl.BlockSpec((1, 2, rows), lambda i: (i, 0, 0), memory_space=pltpu.SMEM),
                  pl.BlockSpec((1, 2, rows), lambda i: (jnp.minimum(i + 1, nb - 1), 0, 0), memory_space=pltpu.SMEM),
                  pl.BlockSpec(memory_space=pl.ANY),
                  pl.BlockSpec((rows, d), lambda i: (i, 0)),
                  pl.BlockSpec((rows, LANES), lambda i: (i, 0)),
                  pl.BlockSpec((1, d), lambda i: (0, 0))],
        out_specs=pl.BlockSpec((rows, d), lambda i: (i, 0)),
        out_shape=jax.ShapeDtypeStruct((m, d), F32),
        scratch_shapes=[pltpu.VMEM((2, 2, rows, d), F32), pltpu.SemaphoreType.DMA((2,))],
        compiler_params=_cparams("arbitrary"),
        name="moe_combine",
    )(slots, slots, yb, x, route, g.reshape(1, d))


def _rot_cols(w):
    half = QK_ROPE // 2
    return jnp.concatenate([-w[..., half:], w[..., :half]], axis=-1)


def _mixer_weights(w_in, w_up, a_up, w_q_up):
    d = RWKV_DIM
    w_rwkv = w_in[:, :RWKV_IN]
    w_lat = w_in[:, RWKV_IN:RWKV_IN + Q_LORA + KV_LORA]
    w_kr = w_in[:, RWKV_IN + Q_LORA + KV_LORA:]
    w_mla = jnp.concatenate([w_lat, w_kr, _rot_cols(w_kr)], axis=1)
    lora = jnp.zeros((W_LORA + A_LORA, 2 * d), F32)
    lora = lora.at[:W_LORA, :d].set(w_up).at[W_LORA:, d:].set(a_up)
    wq = w_q_up.reshape(Q_LORA, MLA_HEADS, QK_NOPE + QK_ROPE)
    wq_rope = wq[..., QK_NOPE:]
    wq = jnp.concatenate([wq, _rot_cols(wq_rope)], axis=-1).reshape(Q_LORA, MLA_HEADS * QK_CAT)
    return w_rwkv.astype(BF16), w_mla.astype(BF16), lora, wq.astype(BF16)


def kernel(x, positions, l0_mix_norm, l0_w_in, l0_shift_mu, l0_w0, l0_w_up, l0_a0, l0_a_up, l0_g_up, l0_k_k, l0_k_a, l0_r_k, l0_gn_w, l0_gn_b, l0_q_norm, l0_w_q_up, l0_kv_norm, l0_w_kv_up, l0_w_out, l0_ffn_norm, l0_ffn_w1, l0_ffn_w3, l0_ffn_w2, l1_mix_norm, l1_pw1_w, l1_pw1_b, l1_dw_w, l1_dw_b, l1_ln_g, l1_ln_b, l1_pw2_w, l1_pw2_b, l1_ffn_norm, l1_router, l1_exp_w1, l1_exp_w3, l1_exp_w2, final_norm):
    batch, seq, d = x.shape
    m = batch * seq
    x0 = x.reshape(m, d)
    pos = positions.reshape(m, 1)
    inv = ROPE_THETA ** (-jnp.arange(0, QK_ROPE, 2, dtype=F32) / QK_ROPE)
    inv_tab = jnp.tile(inv, LANES // inv.shape[0]).reshape(1, LANES)

    w_rwkv, w_mla, lora_w, wq = _mixer_weights(l0_w_in, l0_w_up, l0_a_up, l0_w_q_up)

    hn0 = _rmsnorm_call(x0, l0_mix_norm)
    p_rwkv = _matmul_call(hn0, w_rwkv, name="in_proj_rwkv")
    p_mla = _matmul_call(hn0, w_mla, name="in_proj_mla")

    prep = _rwkv_prep_call(p_rwkv, seq, l0_shift_mu, lora_w, l0_w0, l0_a0, l0_g_up, l0_k_k, l0_k_a,
                           l0_r_k.reshape(-1))
    at, bt, kt, rt, v, bh, kh, gc, bonus, gate = prep
    (ra, o2, pm, qm), w3t = _rwkv_intra_call(at, bt, kt, rt, v, bh, kh, gc, seq, l1_exp_w3)
    y_rwkv = _rwkv_seq_call(ra, o2, pm, qm, bonus, gate, l0_gn_w, l0_gn_b, batch, seq)

    q, k, vv = _mla_proj_calls(p_mla, pos, inv_tab, l0_q_norm, wq, l0_kv_norm, l0_w_kv_up.astype(BF16),
                               batch, seq)
    y_mla, w1t = _attn_call(q, k, vv, batch, seq, l1_exp_w1)

    x1, hn1 = _mix_out_call(y_rwkv, y_mla, l0_w_out.astype(BF16), x0, l0_ffn_norm)

    x2, hn2 = _ffn_call(hn1, l0_ffn_w1.astype(BF16), l0_ffn_w3.astype(BF16), l0_ffn_w2.astype(BF16),
                        x1, l1_mix_norm)

    u = _glu_call(hn2, l1_pw1_w.astype(BF16), l1_pw1_b)
    sc = _dwconv_call(u, seq, l1_dw_w, l1_dw_b, l1_ln_g, l1_ln_b)
    router_pad = jnp.zeros((d, LANES), F32).at[:, :N_EXPERTS].set(l1_router)
    x3, hn3, logits = _conv_out_call(sc, l1_pw2_w.astype(BF16), l1_pw2_b, x2, l1_ffn_norm, router_pad)

    route, counts = _route_call(logits)
    e = route[:, 0:2].astype(jnp.int32)
    rank = route[:, 4:6].astype(jnp.int32)
    cnt = counts[0, :N_EXPERTS].astype(jnp.int32)
    padded = (cnt + MOE_TM - 1) // MOE_TM * MOE_TM
    pad_end = jnp.cumsum(padded)
    pad_start = pad_end - padded
    pos_slot = pad_start[e] + rank
    cap = (2 * m + N_EXPERTS * MOE_TM + MOE_TM - 1) // MOE_TM * MOE_TM
    tok = jnp.broadcast_to(jnp.arange(m, dtype=jnp.int32)[:, None], (m, 2))
    slot_slab = jnp.zeros((cap,), jnp.int32).at[pos_slot.reshape(-1)].set(tok.reshape(-1) * SLABS)
    tile_start = jnp.arange(cap // MOE_TM, dtype=jnp.int32) * MOE_TM
    tile_expert = jnp.minimum(jnp.sum(tile_start[:, None] >= pad_end[None, :], axis=1), N_EXPERTS - 1)
    used = tile_start < pad_end[-1]
    last_used = jnp.max(jnp.where(used, tile_expert, 0))
    tile_expert = jnp.where(used, tile_expert, last_used).astype(jnp.int32)
    tile_rows = jnp.clip(pad_start[tile_expert] + cnt[tile_expert] - tile_start, 0, MOE_TM)
    tile_rows = jnp.where(used, tile_rows, 0).astype(jnp.int32)

    per_tile = MOE_TM // GATHER_ROWS
    block_off = jnp.tile(jnp.arange(per_tile, dtype=jnp.int32) * GATHER_ROWS, cap // MOE_TM)
    block_used = (jnp.repeat(tile_rows, per_tile) > block_off).astype(jnp.int32)
    xs = _gather_call(hn3, slot_slab, block_used)
    yb = _moe_call(xs, tile_expert, tile_rows, w1t, w3t, l1_exp_w2)
    out = _combine_call(pos_slot, yb, x3, route, final_norm)
    return out.reshape(batch, seq, d)
```

```python
import functools

import numpy as np
import jax
import jax.numpy as jnp
from jax import lax
from jax.experimental import pallas as pl
from jax.experimental.pallas import tpu as pltpu

F32 = jnp.float32
BF16 = jnp.bfloat16

D_MODEL = 2048
CHUNK = 64
HEAD = 64
N_HEADS = 16
RWKV_DIM = N_HEADS * HEAD
W_LORA, A_LORA, G_LORA = 64, 64, 128
RWKV_IN = 3 * RWKV_DIM + W_LORA + A_LORA + G_LORA
MLA_HEADS = 8
Q_LORA = KV_LORA = 512
QK_NOPE, QK_ROPE, V_HEAD = 128, 64, 128
QK_CAT = 2 * QK_NOPE
V_EXT = 2 * V_HEAD
ROPE_THETA = 10000.0
CONV_WIDTH = 31
CONV_HALO = 32
N_EXPERTS = 8
NORM_EPS = 1e-6
LN_EPS = 1e-5
GN_EPS = HEAD * 1e-5
NEG_BIG = -1e30

LANES = 128
SUBLANES = 8
VMEM_LIMIT = 56 * 1024 * 1024

HEADS_PER_GROUP = 2
GROUP_W = HEADS_PER_GROUP * HEAD
MOE_TM = 1024
MOE_SUB = 512
MOE_TF = 512
EXPERT_CAST_BLOCKS = 512
EXPERT_CAST_MAX_ROWS = 64
DWCONV_CAST_MAX_ROWS = 256
GATHER_ROWS = 512
SLABS = D_MODEL // LANES
ATTN_HEADS_PER_STEP = 4
LOG2_E = 1.4426950408889634


def _cparams(*sem):
    return pltpu.CompilerParams(dimension_semantics=sem, vmem_limit_bytes=VMEM_LIMIT)


def _tile(n, pref, align=8):
    if n <= pref:
        return n
    t = (pref // align) * align
    while t > align and n % t:
        t -= align
    assert n % t == 0, (n, pref)
    return t


def _dot(a, b):
    return jnp.dot(a, b, preferred_element_type=F32)


def _dot_nt(a, b):
    return lax.dot_general(a, b, (((1,), (1,)), ((), ())), preferred_element_type=F32)


def _split2(x):
    hi = x.astype(BF16)
    lo = (x - hi.astype(F32)).astype(BF16)
    return hi, lo


def _dot_lx(a, b_exact):
    hi, lo = _split2(a)
    return _dot(hi, b_exact) + _dot(lo, b_exact)


def _dot_hp(a, b):
    ah, al = _split2(a)
    bh, bl = _split2(b)
    return _dot(ah, bh) + (_dot(ah, bl) + _dot(al, bh))


def _sigmoid(x):
    return 1.0 / (1.0 + jnp.exp(-x))


def _silu(x):
    return x * _sigmoid(x)


def _softplus(x):
    return jnp.maximum(x, 0.0) + jnp.log(1.0 + jnp.exp(-jnp.abs(x)))


def _rms(x, g):
    return x * lax.rsqrt(jnp.mean(x * x, axis=-1, keepdims=True) + NORM_EPS) * g


def _rmsnorm_kernel(x_ref, g_ref, o_ref):
    o_ref[...] = _rms(x_ref[...], g_ref[...]).astype(o_ref.dtype)


def _rmsnorm_call(x, g):
    m, d = x.shape
    tm = _tile(m, 512)
    return pl.pallas_call(
        _rmsnorm_kernel,
        grid=(m // tm,),
        in_specs=[pl.BlockSpec((tm, d), lambda i: (i, 0)), pl.BlockSpec((1, d), lambda i: (0, 0))],
        out_specs=pl.BlockSpec((tm, d), lambda i: (i, 0)),
        out_shape=jax.ShapeDtypeStruct((m, d), BF16),
        compiler_params=_cparams("parallel"),
        name="rmsnorm",
    )(x, g.reshape(1, d))


def _matmul_kernel(a_ref, w_ref, o_ref):
    o_ref[...] = _dot(a_ref[...], w_ref[...])


def _matmul_call(a, w, tm_pref=1024, tn_pref=1664, name="matmul"):
    m, k = a.shape
    n = w.shape[1]
    tm = _tile(m, tm_pref)
    tn = _tile(n, tn_pref, LANES)
    return pl.pallas_call(
        _matmul_kernel,
        grid=(m // tm, n // tn),
        in_specs=[pl.BlockSpec((tm, k), lambda i, j: (i, 0)), pl.BlockSpec((k, tn), lambda i, j: (0, j))],
        out_specs=pl.BlockSpec((tm, tn), lambda i, j: (i, j)),
        out_shape=jax.ShapeDtypeStruct((m, n), F32),
        compiler_params=_cparams("parallel", "arbitrary"),
        name=name,
    )(a, w)


def _rwkv_prep_kernel(p_ref, prev_ref, mu_ref, lw_ref, w0_ref, a0_ref, gup_ref, kk_ref, ka_ref, rk_ref,
                      tril_ref, ones_ref, seg_ref, segt_ref,
                      at_ref, bt_ref, kt_ref, rt_ref, v_ref, bh_ref, kh_ref, gc_ref, bonus_ref, g_ref,
                      *, tiles_per_seq):
    ts = p_ref.shape[0]
    first = (pl.program_id(0) % tiles_per_seq) == 0
    row0 = lax.broadcasted_iota(jnp.int32, (ts, 1), 0) == 0

    def mixed(c0, c1):
        pc = p_ref[:, c0:c1]
        prev = jnp.where(first, 0.0, prev_ref[7:8, c0:c1])
        sh = jnp.where(row0, prev, pltpu.roll(pc, 1, axis=0))
        return pc + (sh - pc) * mu_ref[:, c0:c1]

    d = RWKV_DIM
    r = mixed(0, d)
    k = mixed(d, 2 * d)
    v = mixed(2 * d, 3 * d)
    xwa = mixed(3 * d, 3 * d + W_LORA + A_LORA)
    xg = mixed(3 * d + W_LORA + A_LORA, RWKV_IN)

    lane = lax.broadcasted_iota(jnp.int32, xwa.shape, 1)
    z = jnp.where(lane < W_LORA, jnp.tanh(xwa), xwa)
    wa = _dot_hp(z, lw_ref[...])
    w = -_softplus(-(w0_ref[...] + wa[:, :d])) - 0.5
    logdecay = -jnp.exp(w)
    a = _sigmoid(a0_ref[...] + wa[:, d:])
    g_ref[...] = _dot_hp(_sigmoid(xg), gup_ref[...])

    seg = seg_ref[...]
    segt = segt_ref[...]

    def head_sum(x):
        return _dot_lx(_dot_lx(x, seg), segt)

    kkr = k * kk_ref[...]
    kk = kkr / jnp.maximum(jnp.sqrt(head_sum(kkr * kkr)), 1e-12)
    k2 = k * (1.0 + (a - 1.0) * ka_ref[...])
    bonus_ref[...] = head_sum(r * k2 * rk_ref[...]) * v

    ld_hi, ld_lo = _split2(logdecay)
    tril = tril_ref[...]
    ones = ones_ref[...]
    gcum = _dot(tril, ld_hi) + _dot(tril, ld_lo)
    gtot = _dot(ones, ld_hi) + _dot(ones, ld_lo)
    g_in = jnp.exp(gcum)
    g_ex = jnp.exp(gcum - logdecay)
    g_inv = jnp.exp(-gcum)
    g_rest = jnp.exp(gtot - gcum)
    beta = kk * a
    at_ref[...] = (-kk * g_ex).astype(at_ref.dtype)
    bt_ref[...] = (beta * g_inv).astype(bt_ref.dtype)
    kt_ref[...] = (k2 * g_inv).astype(kt_ref.dtype)
    rt_ref[...] = (r * g_in).astype(rt_ref.dtype)
    v_ref[...] = v.astype(v_ref.dtype)
    bh_ref[...] = beta * g_rest
    kh_ref[...] = k2 * g_rest
    gc_ref[...] = jnp.exp(gtot)


def _rwkv_prep_call(p, seq, mu, lora_w, w0, a0, g_up, k_k, k_a, r_k):
    m = p.shape[0]
    ts = _tile(seq, 256, CHUNK)
    tiles_per_seq = seq // ts
    d = RWKV_DIM
    idx = np.arange(ts)
    same = (idx[:, None] // CHUNK) == (idx[None, :] // CHUNK)
    tril = jnp.asarray(same & (idx[:, None] >= idx[None, :]), BF16)
    ones = jnp.asarray(same, BF16)
    lane = np.arange(d)
    seg_np = (lane[:, None] // HEAD) == np.arange(LANES)[None, :]
    seg = jnp.asarray(seg_np, BF16)
    segt = jnp.asarray(seg_np.T, BF16)

    def row(x):
        return x.reshape(1, -1).astype(F32)

    full = lambda shape: pl.BlockSpec(shape, lambda i: (0, 0))
    tok = pl.BlockSpec((ts, d), lambda i: (i, 0))
    outs = pl.pallas_call(
        functools.partial(_rwkv_prep_kernel, tiles_per_seq=tiles_per_seq),
        grid=(m // ts,),
        in_specs=[
            pl.BlockSpec((ts, RWKV_IN), lambda i: (i, 0)),
            pl.BlockSpec((8, RWKV_IN), lambda i: (jnp.maximum(i * (ts // 8) - 1, 0), 0)),
            full((1, RWKV_IN)), full((W_LORA + A_LORA, 2 * d)), full((1, d)), full((1, d)),
            full((G_LORA, d)), full((1, d)), full((1, d)), full((1, d)),
            full((ts, ts)), full((ts, ts)), full((d, LANES)), full((LANES, d)),
        ],
        out_specs=[tok] * 10,
        out_shape=[jax.ShapeDtypeStruct((m, d), BF16)] * 5 + [jax.ShapeDtypeStruct((m, d), F32)] * 5,
        compiler_params=_cparams("parallel"),
        name="rwkv_prep",
    )(p, p, row(mu), lora_w, row(w0), row(a0), g_up, row(k_k), row(k_a), row(r_k), tril, ones, seg, segt)
    return outs


def _stack_heads(x):
    lane_head = lax.broadcasted_iota(jnp.int32, x.shape, 1) // HEAD
    return jnp.concatenate([jnp.where(lane_head == h, x, 0.0) for h in range(HEADS_PER_GROUP)], axis=0)


def _unstack_heads(x):
    out = x[0:CHUNK]
    for h in range(1, HEADS_PER_GROUP):
        out = out + x[h * CHUNK:(h + 1) * CHUNK]
    return out


def _rwkv_intra_kernel(at_ref, bt_ref, kt_ref, rt_ref, v_ref, bh_ref, kh_ref, gc_ref,
                       ra_ref, o2_ref, p_ref, q_ref, *, mm):
    w = GROUP_W
    n_chunks = at_ref.shape[0] // CHUNK
    ri = lax.broadcasted_iota(jnp.int32, (w, w), 0)
    ci = lax.broadcasted_iota(jnp.int32, (w, w), 1)
    strict = (ri % CHUNK) > (ci % CHUNK)
    incl = (ri % CHUNK) >= (ci % CHUNK)
    blk16 = (ri // 16) == (ci // 16)
    eye = (ri == ci).astype(F32)

    chunks = [slice(c * CHUNK, (c + 1) * CHUNK) for c in range(n_chunks)]

    def each(f, *lists):
        return [f(*args) for args in zip(*lists)]

    def stacked(ref):
        return [_stack_heads(ref[rows, :]) for rows in chunks]

    a_s, r_s, b_s, k_s, v_s = stacked(at_ref), stacked(rt_ref), stacked(bt_ref), stacked(kt_ref), stacked(v_ref)
    s = each(lambda a, r, b, k: mm(jnp.concatenate([a, r], axis=0), jnp.concatenate([b, k], axis=0), nt=True),
             a_s, r_s, b_s, k_s)
    l_ab = [jnp.where(strict, x[:w, :w], 0.0) for x in s]
    l_ak = [jnp.where(strict, x[:w, w:], 0.0) for x in s]
    m_rb = [jnp.where(incl, x[w:, :w], 0.0) for x in s]
    m_rk = [jnp.where(incl, x[w:, w:], 0.0) for x in s]

    def mm2(a, b0, b1):
        out = mm(a, jnp.concatenate([b0, b1], axis=1))
        return out[:, :w], out[:, w:]

    dg = [jnp.where(blk16, x, 0.0) for x in l_ab]
    off = each(lambda x, y: x - y, l_ab, dg)
    td = [eye + x for x in dg]
    pw = each(mm, dg, dg)
    for _ in range(2):
        sq_ptd = each(mm2, pw, pw, td)
        td = each(lambda t_, x_: t_ + x_[1], td, sq_ptd)
        pw = [x_[0] for x_ in sq_ptd]
    td = each(lambda t_, p_: t_ + mm(p_, t_), td, pw)
    n1 = each(mm, td, off)
    sq_ntd = each(mm2, n1, n1, td)
    t1 = each(lambda t_, x_: t_ + x_[1], td, sq_ntd)
    t = each(lambda t_, x_: t_ + mm(x_[0], t_), t1, sq_ntd)

    lv = each(mm, l_ak, v_s)
    y = each(lambda tt, x, a: mm(tt, jnp.concatenate([x, a], axis=1)), t, lv, a_s)
    z = each(mm, m_rb, y)
    mv = each(mm, m_rk, v_s)
    z2 = each(lambda ref_rows, yy: mm(_stack_heads(bh_ref[ref_rows, :]).T, yy), chunks, y)
    kv = each(lambda ref_rows, vv: mm(_stack_heads(kh_ref[ref_rows, :]).T, vv), chunks, v_s)
    for c, rows in enumerate(chunks):
        ra_ref[rows, :] = _unstack_heads(r_s[c] + z[c][:, w:]).astype(ra_ref.dtype)
        o2_ref[rows, :] = _unstack_heads(z[c][:, :w] + mv[c])
        p_ref[rows, :] = _unstack_heads(eye * gc_ref[c * CHUNK:c * CHUNK + 1, :]
                                        + z2[c][:, w:]).astype(p_ref.dtype)
        q_ref[rows, :] = _unstack_heads(z2[c][:, :w] + kv[c])


def _mm_bf16(a, b, nt=False):
    a = a.astype(BF16)
    b = b.astype(BF16)
    return _dot_nt(a, b) if nt else _dot(a, b)


def _rwkv_intra_call(at, bt, kt, rt, v, bh, kh, gc, seq):
    m, d = at.shape
    ts = _tile(seq, 512, CHUNK)
    spec = pl.BlockSpec((ts, GROUP_W), lambda i, j: (i, j))
    return pl.pallas_call(
        functools.partial(_rwkv_intra_kernel, mm=_mm_bf16),
        grid=(m // ts, d // GROUP_W),
        in_specs=[spec] * 8,
        out_specs=[spec] * 4,
        out_shape=[jax.ShapeDtypeStruct((m, d), dt) for dt in (BF16, F32, BF16, F32)],
        compiler_params=_cparams("parallel", "parallel"),
        name="rwkv_intra",
    )(at, bt, kt, rt, v, bh, kh, gc)


def _rwkv_seq_kernel(ra_ref, o2_ref, p_ref, q_ref, bonus_ref, g_ref, gnw_ref, gnb_ref, seg_ref, segt_ref,
                     o_ref, h_sc, y_sc, *, mm):
    w = GROUP_W
    n_groups = RWKV_DIM // w
    n_chunks = ra_ref.shape[0] // CHUNK

    @pl.when(pl.program_id(1) == 0)
    def _():
        h_sc[...] = jnp.zeros_like(h_sc)

    def chunk(c, carry):
        r0 = pl.multiple_of(c * CHUNK, CHUNK)
        rows = pl.ds(r0, CHUNK)
        groups = [slice(gi * w, (gi + 1) * w) for gi in range(n_groups)]
        lhs = [jnp.concatenate([ra_ref[rows, cols], _stack_heads(p_ref[rows, cols])], axis=0) for cols in groups]
        res = [mm(lhs[gi], h_sc[gi]) for gi in range(n_groups)]
        for gi, cols in enumerate(groups):
            y_sc[rows, cols] = res[gi][:CHUNK] + o2_ref[rows, cols]
            h_sc[gi] = res[gi][CHUNK:] + _stack_heads(q_ref[rows, cols])
        return carry

    lax.fori_loop(0, n_chunks, chunk, 0)

    y = y_sc[...]
    seg = seg_ref[...]
    segt = segt_ref[...]
    mean = _dot_lx(_dot_lx(y, seg), segt) * (1.0 / HEAD)
    yc = y - mean
    var = _dot_lx(_dot_lx(yc * yc, seg), segt) * (1.0 / HEAD)
    yn = yc * lax.rsqrt(var + GN_EPS) * gnw_ref[...] + gnb_ref[...]
    o_ref[...] = ((yn + bonus_ref[...]) * g_ref[...]).astype(o_ref.dtype)


def _rwkv_seq_call(ra, o2, p, q, bonus, g, gn_w, gn_b, batch, seq):
    m, d = ra.shape
    ts = _tile(seq, 512, CHUNK)
    nt = seq // ts
    lane = np.arange(d)
    seg_np = (lane[:, None] // HEAD) == np.arange(LANES)[None, :]
    seg = jnp.asarray(seg_np, BF16)
    segt = jnp.asarray(seg_np.T, BF16)
    tok = pl.BlockSpec((ts, d), lambda b, i: (b * nt + i, 0))
    full = lambda shape: pl.BlockSpec(shape, lambda b, i: (0, 0))
    return pl.pallas_call(
        functools.partial(_rwkv_seq_kernel, mm=_mm_bf16),
        grid=(batch, nt),
        in_specs=[tok] * 6 + [full((1, d)), full((1, d)), full((d, LANES)), full((LANES, d))],
        out_specs=tok,
        out_shape=jax.ShapeDtypeStruct((m, d), BF16),
        scratch_shapes=[pltpu.VMEM((d // GROUP_W, GROUP_W, GROUP_W), F32), pltpu.VMEM((ts, d), F32)],
        compiler_params=_cparams("arbitrary", "arbitrary"),
        name="rwkv_seq",
    )(ra, o2, p, q, bonus, g, gn_w.reshape(1, d), gn_b.reshape(1, d), seg, segt)


def _rope_tab(pos_ref, inv_ref):
    ang = pos_ref[...].astype(F32) * inv_ref[...]
    lane = lax.broadcasted_iota(jnp.int32, ang.shape, 1)
    return jnp.where(lane < QK_ROPE, jnp.cos(ang), jnp.sin(ang))


def _mla_proj_kernel(h_ref, pos_ref, inv_ref, wm_ref, qn_ref, wq_ref, kvn_ref, wkv_ref, q_ref, k_ref, v_ref,
                     *, scale):
    p = _dot(h_ref[...], wm_ref[...])
    tab = _rope_tab(pos_ref, inv_ref)

    hn = _rms(p[:, 0:Q_LORA], qn_ref[...]).astype(BF16)
    q = _dot(hn, wq_ref[...]) * scale
    for h in range(MLA_HEADS):
        c0 = h * QK_CAT
        q_ref[0, h, :, 0:QK_NOPE] = q[:, c0:c0 + QK_NOPE].astype(BF16)
        q_ref[0, h, :, QK_NOPE:QK_CAT] = (q[:, c0 + QK_NOPE:c0 + QK_CAT] * tab).astype(BF16)

    hn = _rms(p[:, Q_LORA:Q_LORA + KV_LORA], kvn_ref[...]).astype(BF16)
    kv = _dot(hn, wkv_ref[...])
    t = p[:, Q_LORA + KV_LORA:] * tab
    k_rope = (t + pltpu.roll(t, QK_ROPE, axis=1)).astype(BF16)
    for h in range(MLA_HEADS):
        c0 = h * (QK_NOPE + V_HEAD)
        k_ref[0, h, :, 0:QK_NOPE] = kv[:, c0:c0 + QK_NOPE].astype(BF16)
        k_ref[0, h, :, QK_NOPE:QK_CAT] = k_rope
        v_ref[0, h, :, 0:V_HEAD] = kv[:, c0 + QK_NOPE:c0 + QK_NOPE + V_HEAD].astype(BF16)
        v_ref[0, h, :, V_HEAD:V_EXT] = jnp.ones((kv.shape[0], V_EXT - V_HEAD), BF16)


def _mla_proj_call(hn, pos, inv_tab, w_mla, q_norm, wq, kv_norm, wkv, batch, seq):
    d = hn.shape[1]
    tm = _tile(seq, 512)
    nt = seq // tm
    scale = float((QK_NOPE + QK_ROPE) ** -0.5) * LOG2_E
    tok = lambda w: pl.BlockSpec((tm, w), lambda b, i: (b * nt + i, 0))
    full = lambda shape: pl.BlockSpec(shape, lambda b, i: (0, 0))
    headed = lambda w: pl.BlockSpec((1, MLA_HEADS, tm, w), lambda b, i: (b, 0, i, 0))
    return pl.pallas_call(
        functools.partial(_mla_proj_kernel, scale=scale),
        grid=(batch, nt),
        in_specs=[tok(d), tok(1), full((1, LANES)), full(w_mla.shape), full((1, Q_LORA)), full(wq.shape),
                  full((1, KV_LORA)), full(wkv.shape)],
        out_specs=[headed(QK_CAT), headed(QK_CAT), headed(V_EXT)],
        out_shape=[jax.ShapeDtypeStruct((batch, MLA_HEADS, seq, QK_CAT), BF16),
                   jax.ShapeDtypeStruct((batch, MLA_HEADS, seq, QK_CAT), BF16),
                   jax.ShapeDtypeStruct((batch, MLA_HEADS, seq, V_EXT), BF16)],
        compiler_params=_cparams("parallel", "parallel"),
        name="mla_proj",
    )(hn, pos, inv_tab, w_mla, q_norm.reshape(1, -1), wq, kv_norm.reshape(1, -1), wkv)


def _cast_expert_block(w_ref, wo_ref):
    for f in range(wo_ref.shape[1]):
        wo_ref[0, f] = w_ref[:, f * MOE_TF:(f + 1) * MOE_TF].astype(BF16)


def _expert_cast_specs(w, n_blocks, block_of):
    n_exp, d, ff = w.shape
    nf = ff // MOE_TF
    rows = n_exp * d // n_blocks
    per_exp = d // rows
    in_spec = pl.BlockSpec((rows, ff), lambda *a: (block_of(*a), 0))
    out_spec = pl.BlockSpec((1, nf, rows, MOE_TF),
                            lambda *a: (block_of(*a) // per_exp, 0, block_of(*a) % per_exp, 0))
    out_shape = jax.ShapeDtypeStruct((n_exp, nf, d, MOE_TF), BF16)
    return in_spec, out_spec, out_shape, w.reshape(n_exp * d, ff)


def _expert_cast_call(w):
    in_spec, out_spec, out_shape, arg = _expert_cast_specs(w, EXPERT_CAST_BLOCKS, lambda i: i)
    return pl.pallas_call(
        _cast_expert_block,
        grid=(EXPERT_CAST_BLOCKS,),
        in_specs=[in_spec],
        out_specs=out_spec,
        out_shape=out_shape,
        compiler_params=_cparams("parallel"),
        name="expert_cast",
    )(arg)


def _attn_cast_kernel(qi_ref, kj_ref, q_ref, k_ref, v_ref, w_ref, o_ref, wo_ref, m_sc, acc_sc):
    _cast_expert_block(w_ref, wo_ref)
    _attn_kernel(qi_ref, kj_ref, q_ref, k_ref, v_ref, o_ref, m_sc, acc_sc)


def _attn_kernel(qi_ref, kj_ref, q_ref, k_ref, v_ref, o_ref, m_sc, acc_sc):
    t = pl.program_id(2)
    qi = qi_ref[t]
    kj = kj_ref[t]
    n_heads = q_ref.shape[1]
    tq = q_ref.shape[2]
    tk = k_ref.shape[2]
    chunk_gap = kj * (tk // CHUNK) - qi * (tq // CHUNK)

    @pl.when(kj == 0)
    def _():
        m_sc[...] = jnp.full_like(m_sc, NEG_BIG)
        acc_sc[...] = jnp.zeros_like(acc_sc)

    def update(heads, mask):
        s = [_dot_nt(q_ref[0, h], k_ref[0, h]) for h in heads]
        if mask:
            ri = lax.broadcasted_iota(jnp.int32, s[0].shape, 0) // CHUNK
            ci = lax.broadcasted_iota(jnp.int32, s[0].shape, 1) // CHUNK
            visible = ci - ri <= -chunk_gap
            s = [jnp.where(visible, x, NEG_BIG) for x in s]
        for i, h in enumerate(heads):
            m_prev = m_sc[h]
            m_new = jnp.maximum(m_prev, jnp.max(s[i], axis=-1, keepdims=True))
            alpha = jnp.exp2(m_prev - m_new)
            p = jnp.exp2(s[i] - jnp.concatenate([m_new] * (tk // LANES), axis=1))
            pv = _dot(p.astype(BF16), v_ref[0, h])
            acc_sc[h] = jnp.concatenate([alpha] * (V_EXT // LANES), axis=1) * acc_sc[h] + pv
            m_sc[h] = m_new

    pairs = [tuple(range(h, min(h + 2, n_heads))) for h in range(0, n_heads, 2)]

    fully_visible = chunk_gap <= -(tk // CHUNK - 1)

    @pl.when(fully_visible)
    def _():
        for hp in pairs:
            update(hp, False)

    @pl.when(jnp.logical_not(fully_visible))
    def _():
        for hp in pairs:
            update(hp, True)

    @pl.when((kj + 1) * tk == (qi + 1) * tq)
    def _():
        for h in range(n_heads):
            acc = acc_sc[h]
            o_ref[:, h * V_HEAD:(h + 1) * V_HEAD] = (acc[:, :V_HEAD] / acc[:, V_HEAD:]).astype(o_ref.dtype)


def _attn_call(q, k, v, batch, seq, expert_w):
    tk = _tile(seq, 512, CHUNK)
    tq = _tile(seq, 2 * tk, tk)
    nq = seq // tq
    hg = ATTN_HEADS_PER_STEP
    n_hg = MLA_HEADS // hg
    pairs = [(i, j) for i in range(nq) for j in range((i + 1) * tq // tk)]
    qi = jnp.asarray([p[0] for p in pairs], jnp.int32)
    kj = jnp.asarray([p[1] for p in pairs], jnp.int32)
    in_specs = [
        pl.BlockSpec((1, hg, tq, QK_CAT), lambda b, h, t, qi, kj: (b, h, qi[t], 0)),
        pl.BlockSpec((1, hg, tk, QK_CAT), lambda b, h, t, qi, kj: (b, h, kj[t], 0)),
        pl.BlockSpec((1, hg, tk, V_EXT), lambda b, h, t, qi, kj: (b, h, kj[t], 0)),
    ]
    out_specs = [pl.BlockSpec((tq, hg * V_HEAD), lambda b, h, t, qi, kj: (b * nq + qi[t], h))]
    out_shape = [jax.ShapeDtypeStruct((batch * seq, MLA_HEADS * V_HEAD), BF16)]
    args = (qi, kj, q, k, v)
    steps = batch * n_hg * len(pairs)
    n_blocks = 1 << (steps.bit_length() - 1)
    rows_total = expert_w.shape[0] * expert_w.shape[1]
    fused = n_blocks * EXPERT_CAST_MAX_ROWS >= rows_total
    if fused:
        def block_of(b, h, t, qi, kj):
            return jnp.minimum((b * n_hg + h) * len(pairs) + t, n_blocks - 1)

        c_in, c_out, c_shape, c_arg = _expert_cast_specs(expert_w, n_blocks, block_of)
        in_specs, out_specs, out_shape, args = in_specs + [c_in], out_specs + [c_out], out_shape + [c_shape], args + (c_arg,)
    grid_spec = pltpu.PrefetchScalarGridSpec(
        num_scalar_prefetch=2,
        grid=(batch, n_hg, len(pairs)),
        in_specs=in_specs,
        out_specs=out_specs,
        scratch_shapes=[pltpu.VMEM((hg, tq, LANES), F32), pltpu.VMEM((hg, tq, V_EXT), F32)],
    )
    outs = pl.pallas_call(
        _attn_cast_kernel if fused else _attn_kernel,
        grid_spec=grid_spec,
        out_shape=out_shape,
        compiler_params=_cparams("arbitrary", "arbitrary", "arbitrary"),
        name="mla_attn",
    )(*args)
    if fused:
        return outs[0], outs[1]
    return outs[0], _expert_cast_call(expert_w)


def _mix_out_kernel(ya_ref, yb_ref, w_ref, x_ref, g_ref, xo_ref, hn_ref):
    half = ya_ref.shape[1]
    acc = _dot(ya_ref[...], w_ref[0:half, :]) + _dot(yb_ref[...], w_ref[half:, :])
    x = x_ref[...] + acc
    xo_ref[...] = x
    hn_ref[...] = _rms(x, g_ref[...]).astype(hn_ref.dtype)


def _mix_out_call(ya, yb, w, x, g):
    m, d = x.shape
    tm = _tile(m, 512)
    half = ya.shape[1]
    row = pl.BlockSpec((tm, d), lambda i: (i, 0))
    return pl.pallas_call(
        _mix_out_kernel,
        grid=(m // tm,),
        in_specs=[pl.BlockSpec((tm, half), lambda i: (i, 0)), pl.BlockSpec((tm, half), lambda i: (i, 0)),
                  pl.BlockSpec((2 * half, d), lambda i: (0, 0)), row, pl.BlockSpec((1, d), lambda i: (0, 0))],
        out_specs=[row, row],
        out_shape=[jax.ShapeDtypeStruct((m, d), F32), jax.ShapeDtypeStruct((m, d), BF16)],
        compiler_params=_cparams("parallel"),
        name="mix_out",
    )(ya, yb, w, x, g.reshape(1, d))


def _conv_out_kernel(a_ref, w_ref, b_ref, x_ref, g_ref, r_ref, xo_ref, hn_ref, lg_ref):
    x = x_ref[...] + _dot(a_ref[...], w_ref[...]) + b_ref[...]
    xo_ref[...] = x
    hn = _rms(x, g_ref[...])
    tm = x.shape[0]
    for j in range(SLABS):
        hn_ref[pl.ds(j, tm, stride=SLABS), :] = hn[:, j * LANES:(j + 1) * LANES]
    lg_ref[...] = _dot_hp(hn, r_ref[...])


def _conv_out_call(a, w, b, x, g, router_pad):
    m, d = x.shape
    tm = _tile(m, 256)
    row = pl.BlockSpec((tm, d), lambda i: (i, 0))
    full = lambda shape: pl.BlockSpec(shape, lambda i: (0, 0))
    return pl.pallas_call(
        _conv_out_kernel,
        grid=(m // tm,),
        in_specs=[row, full((d, d)), full((1, d)), row, full((1, d)), full((d, LANES))],
        out_specs=[row, pl.BlockSpec((tm * SLABS, LANES), lambda i: (i, 0)),
                   pl.BlockSpec((tm, LANES), lambda i: (i, 0))],
        out_shape=[jax.ShapeDtypeStruct((m, d), F32), jax.ShapeDtypeStruct((m * SLABS, LANES), F32),
                   jax.ShapeDtypeStruct((m, LANES), F32)],
        compiler_params=_cparams("parallel"),
        name="conv_out",
    )(a, w, b.reshape(1, d), x, g.reshape(1, d), router_pad)


def _ffn_kernel(h_ref, w1_ref, w3_ref, w2_ref, x_ref, g_ref, xo_ref, hn_ref):
    f = pl.program_id(1)

    @pl.when(f == 0)
    def _():
        xo_ref[...] = x_ref[...]

    h = h_ref[...]
    gate = (_silu(_dot(h, w1_ref[...])) * _dot(h, w3_ref[...])).astype(BF16)
    xo_ref[...] += _dot(gate, w2_ref[...])

    @pl.when(f == pl.num_programs(1) - 1)
    def _():
        hn_ref[...] = _rms(xo_ref[...], g_ref[...]).astype(hn_ref.dtype)


def _ffn_call(h, w1, w3, w2, x, g):
    m, d = x.shape
    ff = w1.shape[1]
    tm = _tile(m, 512)
    tf = _tile(ff, 512, LANES)
    row = pl.BlockSpec((tm, d), lambda i, f: (i, 0))
    wtile = pl.BlockSpec((d, tf), lambda i, f: (0, f))
    return pl.pallas_call(
        _ffn_kernel,
        grid=(m // tm, ff // tf),
        in_specs=[row, wtile, wtile,
                  pl.BlockSpec((tf, d), lambda i, f: (f, 0)), row, pl.BlockSpec((1, d), lambda i, f: (0, 0))],
        out_specs=[row, row],
        out_shape=[jax.ShapeDtypeStruct((m, d), F32), jax.ShapeDtypeStruct((m, d), BF16)],
        compiler_params=_cparams("parallel", "arbitrary"),
        name="ffn",
    )(h, w1, w3, w2, x, g.reshape(1, d))


def _glu_kernel(h_ref, wa_ref, wb_ref, ba_ref, bb_ref, o_ref):
    h = h_ref[...]
    a = _dot(h, wa_ref[...]) + ba_ref[...]
    b = _dot(h, wb_ref[...]) + bb_ref[...]
    o_ref[...] = a * _sigmoid(b)


def _glu_call(h, w, b):
    m, d = h.shape
    n = w.shape[1] // 2
    tm = _tile(m, 1024)
    tn = _tile(n, 512, LANES)
    nj = n // tn
    b2 = b.reshape(1, 2 * n)
    return pl.pallas_call(
        _glu_kernel,
        grid=(m // tm, nj),
        in_specs=[pl.BlockSpec((tm, d), lambda i, j: (i, 0)),
                  pl.BlockSpec((d, tn), lambda i, j: (0, j)), pl.BlockSpec((d, tn), lambda i, j: (0, j + nj)),
                  pl.BlockSpec((1, tn), lambda i, j: (0, j)), pl.BlockSpec((1, tn), lambda i, j: (0, j + nj))],
        out_specs=pl.BlockSpec((tm, tn), lambda i, j: (i, j)),
        out_shape=jax.ShapeDtypeStruct((m, n), F32),
        compiler_params=_cparams("parallel", "arbitrary"),
        name="conv_glu",
    )(h, w, w, b2, b2)


def _dwconv_kernel(u_ref, halo_ref, dw_ref, dwb_ref, lg_ref, lb_ref, o_ref, ext_sc, acc_sc, *, tiles_per_seq):
    ts, d = u_ref.shape
    first = (pl.program_id(0) % tiles_per_seq) == 0
    ext_sc[0, 0:CONV_HALO, :] = jnp.where(first, 0.0, halo_ref[...])
    ext_sc[0, CONV_HALO:, :] = u_ref[...]
    n_shift = ts + CONV_HALO - SUBLANES
    for b in range(1, SUBLANES):
        ext_sc[b, 0:n_shift, :] = ext_sc[0, b:b + n_shift, :]
    rc, cc = 64, 256
    base = CONV_HALO - (CONV_WIDTH - 1)
    for c0 in range(0, d, cc):
        for r0 in range(0, ts, rc):
            acc = jnp.zeros((rc, cc), F32) + dwb_ref[:, c0:c0 + cc]
            for j in range(CONV_WIDTH):
                a8, b = divmod(base + j, SUBLANES)
                rows = slice(r0 + a8 * SUBLANES, r0 + a8 * SUBLANES + rc)
                acc = acc + dw_ref[j:j + 1, c0:c0 + cc] * ext_sc[b, rows, c0:c0 + cc]
            acc_sc[r0:r0 + rc, c0:c0 + cc] = acc
    y = acc_sc[...]
    mu = jnp.mean(y, axis=-1, keepdims=True)
    yc = y - mu
    var = jnp.mean(yc * yc, axis=-1, keepdims=True)
    yn = yc * lax.rsqrt(var + LN_EPS) * lg_ref[...] + lb_ref[...]
    o_ref[...] = _silu(yn).astype(o_ref.dtype)


def _dwconv_cast_kernel(u_ref, halo_ref, dw_ref, dwb_ref, lg_ref, lb_ref, w_ref, o_ref, wo_ref, ext_sc, acc_sc,
                        *, tiles_per_seq):
    _cast_expert_block(w_ref, wo_ref)
    _dwconv_kernel(u_ref, halo_ref, dw_ref, dwb_ref, lg_ref, lb_ref, o_ref, ext_sc, acc_sc,
                   tiles_per_seq=tiles_per_seq)


def _dwconv_call(u, seq, dw_w, dw_b, ln_g, ln_b, expert_w):
    m, d = u.shape
    ts = _tile(seq, 256, CONV_HALO)
    tiles_per_seq = seq // ts
    steps = m // ts
    dw_pad = jnp.concatenate([dw_w, jnp.zeros((CONV_HALO - CONV_WIDTH, d), F32)], axis=0)
    full = lambda shape: pl.BlockSpec(shape, lambda i: (0, 0))
    in_specs = [pl.BlockSpec((ts, d), lambda i: (i, 0)),
                pl.BlockSpec((CONV_HALO, d), lambda i: (jnp.maximum(i * (ts // CONV_HALO) - 1, 0), 0)),
                full((CONV_HALO, d)), full((1, d)), full((1, d)), full((1, d))]
    out_specs = [pl.BlockSpec((ts, d), lambda i: (i, 0))]
    out_shape = [jax.ShapeDtypeStruct((m, d), BF16)]
    args = (u, u, dw_pad, dw_b.reshape(1, d), ln_g.reshape(1, d), ln_b.reshape(1, d))
    rows_total = expert_w.shape[0] * expert_w.shape[1]
    fused = steps * DWCONV_CAST_MAX_ROWS >= rows_total and rows_total % steps == 0
    if fused:
        c_in, c_out, c_shape, c_arg = _expert_cast_specs(expert_w, steps, lambda i: i)
        in_specs, out_specs, out_shape, args = in_specs + [c_in], out_specs + [c_out], out_shape + [c_shape], args + (c_arg,)
    outs = pl.pallas_call(
        functools.partial(_dwconv_cast_kernel if fused else _dwconv_kernel, tiles_per_seq=tiles_per_seq),
        grid=(steps,),
        in_specs=in_specs,
        out_specs=out_specs,
        out_shape=out_shape,
        scratch_shapes=[pltpu.VMEM((SUBLANES, ts + CONV_HALO, d), F32), pltpu.VMEM((ts, d), F32)],
        compiler_params=_cparams("arbitrary"),
        name="dwconv",
    )(*args)
    if fused:
        return outs[0], outs[1]
    return outs[0], _expert_cast_call(expert_w)


def _route_kernel(lg_ref, tri_ref, o_ref, cnt_ref, carry_sc):
    @pl.when(pl.program_id(0) == 0)
    def _():
        carry_sc[...] = jnp.zeros_like(carry_sc)

    lg = lg_ref[...]
    lane = lax.broadcasted_iota(jnp.int32, lg.shape, 1)
    lg = jnp.where(lane < N_EXPERTS, lg, -jnp.inf)
    m1 = jnp.max(lg, axis=-1, keepdims=True)
    e1 = jnp.min(jnp.where(lg == m1, lane, LANES), axis=-1, keepdims=True)
    lg2 = jnp.where(lane == e1, -jnp.inf, lg)
    m2 = jnp.max(lg2, axis=-1, keepdims=True)
    e2 = jnp.min(jnp.where(lg2 == m2, lane, LANES), axis=-1, keepdims=True)
    ex = jnp.exp(m2 - m1)
    g1 = 1.0 / (1.0 + ex)
    g2 = ex / (1.0 + ex)
    oh1 = (lane == e1).astype(F32)
    oh2 = (lane == e2).astype(F32)
    both = oh1 + oh2
    before = _dot(tri_ref[...], both.astype(BF16)) + carry_sc[...]
    r1 = jnp.sum(before * oh1, axis=-1, keepdims=True)
    r2 = jnp.sum(before * oh2, axis=-1, keepdims=True)
    carry_sc[...] = carry_sc[...] + jnp.sum(both, axis=0, keepdims=True)
    cnt_ref[...] = jnp.broadcast_to(carry_sc[...], cnt_ref.shape)
    out = jnp.where(lane == 0, e1.astype(F32), 0.0)
    out = jnp.where(lane == 1, e2.astype(F32), out)
    out = jnp.where(lane == 2, g1, out)
    out = jnp.where(lane == 3, g2, out)
    out = jnp.where(lane == 4, r1, out)
    out = jnp.where(lane == 5, r2, out)
    o_ref[...] = out


def _route_call(logits):
    m = logits.shape[0]
    tm = _tile(m, 512)
    idx = np.arange(tm)
    tri = jnp.asarray(idx[:, None] > idx[None, :], BF16)
    return pl.pallas_call(
        _route_kernel,
        grid=(m // tm,),
        in_specs=[pl.BlockSpec((tm, LANES), lambda i: (i, 0)), pl.BlockSpec((tm, tm), lambda i: (0, 0))],
        out_specs=[pl.BlockSpec((tm, LANES), lambda i: (i, 0)), pl.BlockSpec((8, LANES), lambda i: (0, 0))],
        out_shape=[jax.ShapeDtypeStruct((m, LANES), F32), jax.ShapeDtypeStruct((8, LANES), F32)],
        scratch_shapes=[pltpu.VMEM((1, LANES), F32)],
        compiler_params=_cparams("arbitrary"),
        name="moe_route",
    )(logits, tri)


def _slab_copy(src_ref, dst_ref, sem, src_slab, dst_row):
    src = src_ref.at[pl.ds(pl.multiple_of(src_slab, SLABS), SLABS)]
    dst = dst_ref.at[pl.ds(pl.multiple_of(dst_row * SLABS, SLABS), SLABS)]
    return pltpu.make_async_copy(src, dst, sem)


def _slab_cols(ref, j, rows):
    return ref[pl.ds(j, rows, stride=SLABS), :]


def _gather_kernel(used_ref, tok_ref, nxt_ref, src_ref, o_ref, buf_sc, sem):
    rows = o_ref.shape[0]
    i = pl.program_id(0)
    last = pl.num_programs(0) - 1

    def start_block(idx_ref, slot):
        def body(r, c):
            _slab_copy(src_ref, buf_sc.at[slot], sem.at[slot], idx_ref[0, 0, r], r).start()
            return c
        lax.fori_loop(0, rows, body, 0, unroll=8)

    def wait_block(slot):
        def body(r, c):
            _slab_copy(src_ref, buf_sc.at[slot], sem.at[slot], 0, r).wait()
            return c
        lax.fori_loop(0, rows, body, 0, unroll=8)

    @pl.when(jnp.logical_and(i == 0, used_ref[0] > 0))
    def _():
        start_block(tok_ref, 0)

    for slot in range(2):
        @pl.when(jnp.logical_and(i % 2 == slot, jnp.logical_and(i < last, used_ref[jnp.minimum(i + 1, last)] > 0)))
        def _():
            start_block(nxt_ref, 1 - slot)

        @pl.when(jnp.logical_and(i % 2 == slot, used_ref[i] > 0))
        def _():
            wait_block(slot)
            for j in range(SLABS):
                o_ref[:, j * LANES:(j + 1) * LANES] = _slab_cols(buf_sc.at[slot], j, rows).astype(o_ref.dtype)

    @pl.when(used_ref[i] == 0)
    def _():
        o_ref[...] = jnp.zeros_like(o_ref)


def _gather_call(src_slabs, slot_slab, block_used):
    cap = slot_slab.shape[0]
    d = SLABS * LANES
    rows = GATHER_ROWS
    nb = cap // rows
    grid_spec = pltpu.PrefetchScalarGridSpec(
        num_scalar_prefetch=1,
        grid=(nb,),
        in_specs=[pl.BlockSpec((1, 1, rows), lambda i, u: (i, 0, 0), memory_space=pltpu.SMEM),
                  pl.BlockSpec((1, 1, rows), lambda i, u: (jnp.minimum(i + 1, nb - 1), 0, 0),
                               memory_space=pltpu.SMEM),
                  pl.BlockSpec(memory_space=pl.ANY)],
        out_specs=pl.BlockSpec((rows, d), lambda i, u: (i, 0)),
        scratch_shapes=[pltpu.VMEM((2, rows * SLABS, LANES), F32), pltpu.SemaphoreType.DMA((2,))],
    )
    slots = slot_slab.reshape(nb, 1, rows)
    return pl.pallas_call(
        _gather_kernel,
        grid_spec=grid_spec,
        out_shape=jax.ShapeDtypeStruct((cap, d), BF16),
        compiler_params=_cparams("arbitrary"),
        name="moe_gather",
    )(block_used, slots, slots, src_slabs)


def _moe_kernel(te_ref, tr_ref, x_ref, w1_ref, w3_ref, w2_ref, o_ref, w2_sc):
    i = pl.program_id(0)
    f = pl.program_id(1)
    nrows = tr_ref[i]
    tm = x_ref.shape[0]

    @pl.when(f == 0)
    def _():
        o_ref[...] = jnp.zeros_like(o_ref)

    def swiglu(rows):
        x = x_ref[rows, :]
        gate = (_silu(_dot(x, w1_ref[0, 0])) * _dot(x, w3_ref[0, 0])).astype(BF16)
        o_ref[rows, :] += _dot(gate, w2_sc[...])

    @pl.when(nrows == tm)
    def _():
        w2_sc[...] = w2_ref[0].astype(BF16)
        swiglu(slice(0, tm))

    @pl.when(jnp.logical_and(nrows > 0, nrows < tm))
    def _():
        w2_sc[...] = w2_ref[0].astype(BF16)
        for s in range(tm // MOE_SUB):
            @pl.when(s * MOE_SUB < nrows)
            def _():
                swiglu(slice(s * MOE_SUB, (s + 1) * MOE_SUB))


def _moe_call(xs, tile_expert, tile_rows, w1t, w3t, w2):
    cap, d = xs.shape
    n_exp, nf, _, tf = w1t.shape
    tm = MOE_TM

    def fidx(i, f, te, tr):
        return jnp.where(tr[i] > 0, f, nf - 1)

    grid_spec = pltpu.PrefetchScalarGridSpec(
        num_scalar_prefetch=2,
        grid=(cap // tm, nf),
        in_specs=[
            pl.BlockSpec((tm, d), lambda i, f, te, tr: (i, 0)),
            pl.BlockSpec((1, 1, d, tf), lambda i, f, te, tr: (te[i], fidx(i, f, te, tr), 0, 0)),
            pl.BlockSpec((1, 1, d, tf), lambda i, f, te, tr: (te[i], fidx(i, f, te, tr), 0, 0)),
            pl.BlockSpec((1, tf, d), lambda i, f, te, tr: (te[i], fidx(i, f, te, tr), 0)),
        ],
        out_specs=pl.BlockSpec((tm, d), lambda i, f, te, tr: (i, 0)),
        scratch_shapes=[pltpu.VMEM((tf, d), BF16)],
    )
    return pl.pallas_call(
        _moe_kernel,
        grid_spec=grid_spec,
        out_shape=jax.ShapeDtypeStruct((cap, d), F32),
        compiler_params=_cparams("parallel", "arbitrary"),
        name="moe_experts",
    )(tile_expert, tile_rows, xs, w1t, w3t, w2)


def _row_copy(src_ref, dst_ref, sem, src_row, dst_row):
    return pltpu.make_async_copy(src_ref.at[pl.ds(src_row, 1)], dst_ref.at[pl.ds(dst_row, 1)], sem)


def _combine_kernel(pos_ref, nxt_ref, y_ref, x_ref, rt_ref, g_ref, o_ref, buf_sc, sem):
    rows = o_ref.shape[0]
    i = pl.program_id(0)
    last = pl.num_programs(0) - 1

    def start_block(idx_ref, slot):
        def body(r, c):
            for k in range(2):
                _row_copy(y_ref, buf_sc.at[slot, k], sem.at[slot], idx_ref[0, k, r], r).start()
            return c
        lax.fori_loop(0, rows, body, 0, unroll=8)

    def wait_block(slot):
        def body(r, c):
            for k in range(2):
                _row_copy(y_ref, buf_sc.at[slot, k], sem.at[slot], 0, r).wait()
            return c
        lax.fori_loop(0, rows, body, 0, unroll=8)

    @pl.when(i == 0)
    def _():
        start_block(pos_ref, 0)

    for slot in range(2):
        @pl.when(jnp.logical_and(i % 2 == slot, i < last))
        def _():
            start_block(nxt_ref, 1 - slot)

        @pl.when(i % 2 == slot)
        def _():
            wait_block(slot)
            rt = rt_ref[...]
            x = x_ref[...] + rt[:, 2:3] * buf_sc[slot, 0] + rt[:, 3:4] * buf_sc[slot, 1]
            o_ref[...] = _rms(x, g_ref[...])


def _combine_call(pos, yb, x, route, g):
    m, d = x.shape
    rows = GATHER_ROWS if m % GATHER_ROWS == 0 else m
    nb = m // rows
    slots = pos.reshape(nb, rows, 2).transpose(0, 2, 1)
    return pl.pallas_call(
        _combine_kernel,
        grid=(nb,),
        in_specs=[pl.BlockSpec((1, 2, rows), lambda i: (i, 0, 0), memory_space=pltpu.SMEM),
                  pl.BlockSpec((1, 2, rows), lambda i: (jnp.minimum(i + 1, nb - 1), 0, 0), memory_space=pltpu.SMEM),
                  pl.BlockSpec(memory_space=pl.ANY),
                  pl.BlockSpec((rows, d), lambda i: (i, 0)),
                  pl.BlockSpec((rows, LANES), lambda i: (i, 0)),
                  pl.BlockSpec((1, d), lambda i: (0, 0))],
        out_specs=pl.BlockSpec((rows, d), lambda i: (i, 0)),
        out_shape=jax.ShapeDtypeStruct((m, d), F32),
        scratch_shapes=[pltpu.VMEM((2, 2, rows, d), F32), pltpu.SemaphoreType.DMA((2,))],
        compiler_params=_cparams("arbitrary"),
        name="moe_combine",
    )(slots, slots, yb, x, route, g.reshape(1, d))


def _rot_cols(w):
    half = QK_ROPE // 2
    return jnp.concatenate([-w[..., half:], w[..., :half]], axis=-1)


def _mixer_weights(w_in, w_up, a_up, w_q_up):
    d = RWKV_DIM
    w_rwkv = w_in[:, :RWKV_IN]
    w_lat = w_in[:, RWKV_IN:RWKV_IN + Q_LORA + KV_LORA]
    w_kr = w_in[:, RWKV_IN + Q_LORA + KV_LORA:]
    w_mla = jnp.concatenate([w_lat, w_kr, _rot_cols(w_kr)], axis=1)
    lora = jnp.zeros((W_LORA + A_LORA, 2 * d), F32)
    lora = lora.at[:W_LORA, :d].set(w_up).at[W_LORA:, d:].set(a_up)
    wq = w_q_up.reshape(Q_LORA, MLA_HEADS, QK_NOPE + QK_ROPE)
    wq_rope = wq[..., QK_NOPE:]
    wq = jnp.concatenate([wq, _rot_cols(wq_rope)], axis=-1).reshape(Q_LORA, MLA_HEADS * QK_CAT)
    return w_rwkv.astype(BF16), w_mla.astype(BF16), lora, wq.astype(BF16)


def kernel(x, positions, l0_mix_norm, l0_w_in, l0_shift_mu, l0_w0, l0_w_up, l0_a0, l0_a_up, l0_g_up, l0_k_k, l0_k_a, l0_r_k, l0_gn_w, l0_gn_b, l0_q_norm, l0_w_q_up, l0_kv_norm, l0_w_kv_up, l0_w_out, l0_ffn_norm, l0_ffn_w1, l0_ffn_w3, l0_ffn_w2, l1_mix_norm, l1_pw1_w, l1_pw1_b, l1_dw_w, l1_dw_b, l1_ln_g, l1_ln_b, l1_pw2_w, l1_pw2_b, l1_ffn_norm, l1_router, l1_exp_w1, l1_exp_w3, l1_exp_w2, final_norm):
    batch, seq, d = x.shape
    m = batch * seq
    x0 = x.reshape(m, d)
    pos = positions.reshape(m, 1)
    inv = ROPE_THETA ** (-jnp.arange(0, QK_ROPE, 2, dtype=F32) / QK_ROPE)
    inv_tab = jnp.tile(inv, LANES // inv.shape[0]).reshape(1, LANES)

    w_rwkv, w_mla, lora_w, wq = _mixer_weights(l0_w_in, l0_w_up, l0_a_up, l0_w_q_up)

    hn0 = _rmsnorm_call(x0, l0_mix_norm)
    p_rwkv = _matmul_call(hn0, w_rwkv, name="in_proj_rwkv")

    prep = _rwkv_prep_call(p_rwkv, seq, l0_shift_mu, lora_w, l0_w0, l0_a0, l0_g_up, l0_k_k, l0_k_a,
                           l0_r_k.reshape(-1))
    at, bt, kt, rt, v, bh, kh, gc, bonus, gate = prep
    ra, o2, pm, qm = _rwkv_intra_call(at, bt, kt, rt, v, bh, kh, gc, seq)
    y_rwkv = _rwkv_seq_call(ra, o2, pm, qm, bonus, gate, l0_gn_w, l0_gn_b, batch, seq)

    q, k, vv = _mla_proj_call(hn0, pos, inv_tab, w_mla, l0_q_norm, wq, l0_kv_norm, l0_w_kv_up.astype(BF16),
                              batch, seq)
    y_mla, w1t = _attn_call(q, k, vv, batch, seq, l1_exp_w1)

    x1, hn1 = _mix_out_call(y_rwkv, y_mla, l0_w_out.astype(BF16), x0, l0_ffn_norm)

    x2, hn2 = _ffn_call(hn1, l0_ffn_w1.astype(BF16), l0_ffn_w3.astype(BF16), l0_ffn_w2.astype(BF16),
                        x1, l1_mix_norm)

    u = _glu_call(hn2, l1_pw1_w.astype(BF16), l1_pw1_b)
    sc, w3t = _dwconv_call(u, seq, l1_dw_w, l1_dw_b, l1_ln_g, l1_ln_b, l1_exp_w3)
    router_pad = jnp.zeros((d, LANES), F32).at[:, :N_EXPERTS].set(l1_router)
    x3, hn3, logits = _conv_out_call(sc, l1_pw2_w.astype(BF16), l1_pw2_b, x2, l1_ffn_norm, router_pad)

    route, counts = _route_call(logits)
    e = route[:, 0:2].astype(jnp.int32)
    rank = route[:, 4:6].astype(jnp.int32)
    cnt = counts[0, :N_EXPERTS].astype(jnp.int32)
    padded = (cnt + MOE_TM - 1) // MOE_TM * MOE_TM
    pad_end = jnp.cumsum(padded)
    pad_start = pad_end - padded
    pos_slot = pad_start[e] + rank
    cap = (2 * m + N_EXPERTS * MOE_TM + MOE_TM - 1) // MOE_TM * MOE_TM
    tok = jnp.broadcast_to(jnp.arange(m, dtype=jnp.int32)[:, None], (m, 2))
    slot_slab = jnp.zeros((cap,), jnp.int32).at[pos_slot.reshape(-1)].set(tok.reshape(-1) * SLABS)
    tile_start = jnp.arange(cap // MOE_TM, dtype=jnp.int32) * MOE_TM
    tile_expert = jnp.minimum(jnp.sum(tile_start[:, None] >= pad_end[None, :], axis=1), N_EXPERTS - 1)
    used = tile_start < pad_end[-1]
    last_used = jnp.max(jnp.where(used, tile_expert, 0))
    tile_expert = jnp.where(used, tile_expert, last_used).astype(jnp.int32)
    tile_rows = jnp.clip(pad_start[tile_expert] + cnt[tile_expert] - tile_start, 0, MOE_TM)
    tile_rows = jnp.where(used, tile_rows, 0).astype(jnp.int32)

    per_tile = MOE_TM // GATHER_ROWS
    block_off = jnp.tile(jnp.arange(per_tile, dtype=jnp.int32) * GATHER_ROWS, cap // MOE_TM)
    block_used = (jnp.repeat(tile_rows, per_tile) > block_off).astype(jnp.int32)
    xs = _gather_call(hn3, slot_slab, block_used)
    yb = _moe_call(xs, tile_expert, tile_rows, w1t, w3t, l1_exp_w2)
    out = _combine_call(pos_slot, yb, x3, route, final_norm)
    return out.reshape(batch, seq, d)
```

```python
import functools

import numpy as np
import jax
import jax.numpy as jnp
from jax import lax
from jax.experimental import pallas as pl
from jax.experimental.pallas import tpu as pltpu

F32 = jnp.float32
BF16 = jnp.bfloat16

D_MODEL = 2048
CHUNK = 64
HEAD = 64
N_HEADS = 16
RWKV_DIM = N_HEADS * HEAD
W_LORA, A_LORA, G_LORA = 64, 64, 128
RWKV_IN = 3 * RWKV_DIM + W_LORA + A_LORA + G_LORA
MLA_HEADS = 8
Q_LORA = KV_LORA = 512
QK_NOPE, QK_ROPE, V_HEAD = 128, 64, 128
QK_CAT = 2 * QK_NOPE
V_EXT = 2 * V_HEAD
ROPE_THETA = 10000.0
CONV_WIDTH = 31
CONV_HALO = 32
N_EXPERTS = 8
NORM_EPS = 1e-6
LN_EPS = 1e-5
GN_EPS = HEAD * 1e-5
NEG_BIG = -1e30

LANES = 128
SUBLANES = 8
VMEM_LIMIT = 56 * 1024 * 1024

HEADS_PER_GROUP = 2
GROUP_W = HEADS_PER_GROUP * HEAD
MOE_TM = 1024
MOE_SUB = 512
MOE_TF = 512
EXPERT_CAST_BLOCKS = 512
EXPERT_CAST_MAX_ROWS = 128
DWCONV_CAST_MAX_ROWS = 256
GATHER_ROWS = 512
SLABS = D_MODEL // LANES
ATTN_HEADS_PER_STEP = 8
LOG2_E = 1.4426950408889634


def _cparams(*sem):
    return pltpu.CompilerParams(dimension_semantics=sem, vmem_limit_bytes=VMEM_LIMIT)


def _tile(n, pref, align=8):
    if n <= pref:
        return n
    t = (pref // align) * align
    while t > align and n % t:
        t -= align
    assert n % t == 0, (n, pref)
    return t


def _dot(a, b):
    return jnp.dot(a, b, preferred_element_type=F32)


def _dot_nt(a, b):
    return lax.dot_general(a, b, (((1,), (1,)), ((), ())), preferred_element_type=F32)


def _split2(x):
    hi = x.astype(BF16)
    lo = (x - hi.astype(F32)).astype(BF16)
    return hi, lo


def _dot_lx(a, b_exact):
    hi, lo = _split2(a)
    return _dot(hi, b_exact) + _dot(lo, b_exact)


def _dot_hp(a, b):
    ah, al = _split2(a)
    bh, bl = _split2(b)
    return _dot(ah, bh) + (_dot(ah, bl) + _dot(al, bh))


def _sigmoid(x):
    return 1.0 / (1.0 + jnp.exp(-x))


def _silu(x):
    return x * _sigmoid(x)


def _softplus(x):
    return jnp.maximum(x, 0.0) + jnp.log(1.0 + jnp.exp(-jnp.abs(x)))


def _rms(x, g):
    return x * lax.rsqrt(jnp.mean(x * x, axis=-1, keepdims=True) + NORM_EPS) * g


def _rmsnorm_kernel(x_ref, g_ref, o_ref):
    o_ref[...] = _rms(x_ref[...], g_ref[...]).astype(o_ref.dtype)


def _rmsnorm_call(x, g):
    m, d = x.shape
    tm = _tile(m, 512)
    return pl.pallas_call(
        _rmsnorm_kernel,
        grid=(m // tm,),
        in_specs=[pl.BlockSpec((tm, d), lambda i: (i, 0)), pl.BlockSpec((1, d), lambda i: (0, 0))],
        out_specs=pl.BlockSpec((tm, d), lambda i: (i, 0)),
        out_shape=jax.ShapeDtypeStruct((m, d), BF16),
        compiler_params=_cparams("parallel"),
        name="rmsnorm",
    )(x, g.reshape(1, d))


def _matmul_kernel(a_ref, w_ref, o_ref):
    o_ref[...] = _dot(a_ref[...], w_ref[...])


def _matmul_call(a, w, tm_pref=1024, tn_pref=1664, name="matmul"):
    m, k = a.shape
    n = w.shape[1]
    tm = _tile(m, tm_pref)
    tn = _tile(n, tn_pref, LANES)
    return pl.pallas_call(
        _matmul_kernel,
        grid=(m // tm, n // tn),
        in_specs=[pl.BlockSpec((tm, k), lambda i, j: (i, 0)), pl.BlockSpec((k, tn), lambda i, j: (0, j))],
        out_specs=pl.BlockSpec((tm, tn), lambda i, j: (i, j)),
        out_shape=jax.ShapeDtypeStruct((m, n), F32),
        compiler_params=_cparams("parallel", "arbitrary"),
        name=name,
    )(a, w)


def _rwkv_prep_kernel(p_ref, prev_ref, mu_ref, lw_ref, w0_ref, a0_ref, gup_ref, kk_ref, ka_ref, rk_ref,
                      tril_ref, ones_ref, seg_ref, segt_ref,
                      at_ref, bt_ref, kt_ref, rt_ref, v_ref, bh_ref, kh_ref, gc_ref, bonus_ref, g_ref,
                      *, tiles_per_seq):
    ts = p_ref.shape[0]
    first = (pl.program_id(0) % tiles_per_seq) == 0
    row0 = lax.broadcasted_iota(jnp.int32, (ts, 1), 0) == 0

    def mixed(c0, c1):
        pc = p_ref[:, c0:c1]
        prev = jnp.where(first, 0.0, prev_ref[7:8, c0:c1])
        sh = jnp.where(row0, prev, pltpu.roll(pc, 1, axis=0))
        return pc + (sh - pc) * mu_ref[:, c0:c1]

    d = RWKV_DIM
    r = mixed(0, d)
    k = mixed(d, 2 * d)
    v = mixed(2 * d, 3 * d)
    xwa = mixed(3 * d, 3 * d + W_LORA + A_LORA)
    xg = mixed(3 * d + W_LORA + A_LORA, RWKV_IN)

    lane = lax.broadcasted_iota(jnp.int32, xwa.shape, 1)
    z = jnp.where(lane < W_LORA, jnp.tanh(xwa), xwa)
    wa = _dot_hp(z, lw_ref[...])
    w = -_softplus(-(w0_ref[...] + wa[:, :d])) - 0.5
    logdecay = -jnp.exp(w)
    a = _sigmoid(a0_ref[...] + wa[:, d:])
    g_ref[...] = _dot_hp(_sigmoid(xg), gup_ref[...])

    seg = seg_ref[...]
    segt = segt_ref[...]

    def head_sum(x):
        return _dot_lx(_dot_lx(x, seg), segt)

    kkr = k * kk_ref[...]
    kk = kkr / jnp.maximum(jnp.sqrt(head_sum(kkr * kkr)), 1e-12)
    k2 = k * (1.0 + (a - 1.0) * ka_ref[...])
    bonus_ref[...] = head_sum(r * k2 * rk_ref[...]) * v

    ld_hi, ld_lo = _split2(logdecay)
    tril = tril_ref[...]
    ones = ones_ref[...]
    gcum = _dot(tril, ld_hi) + _dot(tril, ld_lo)
    gtot = _dot(ones, ld_hi) + _dot(ones, ld_lo)
    g_in = jnp.exp(gcum)
    g_ex = jnp.exp(gcum - logdecay)
    g_inv = jnp.exp(-gcum)
    g_rest = jnp.exp(gtot - gcum)
    beta = kk * a
    at_ref[...] = (-kk * g_ex).astype(at_ref.dtype)
    bt_ref[...] = (beta * g_inv).astype(bt_ref.dtype)
    kt_ref[...] = (k2 * g_inv).astype(kt_ref.dtype)
    rt_ref[...] = (r * g_in).astype(rt_ref.dtype)
    v_ref[...] = v.astype(v_ref.dtype)
    bh_ref[...] = beta * g_rest
    kh_ref[...] = k2 * g_rest
    gc_ref[...] = jnp.exp(gtot)


def _rwkv_prep_call(p, seq, mu, lora_w, w0, a0, g_up, k_k, k_a, r_k):
    m = p.shape[0]
    ts = _tile(seq, 256, CHUNK)
    tiles_per_seq = seq // ts
    d = RWKV_DIM
    idx = np.arange(ts)
    same = (idx[:, None] // CHUNK) == (idx[None, :] // CHUNK)
    tril = jnp.asarray(same & (idx[:, None] >= idx[None, :]), BF16)
    ones = jnp.asarray(same, BF16)
    lane = np.arange(d)
    seg_np = (lane[:, None] // HEAD) == np.arange(LANES)[None, :]
    seg = jnp.asarray(seg_np, BF16)
    segt = jnp.asarray(seg_np.T, BF16)

    def row(x):
        return x.reshape(1, -1).astype(F32)

    full = lambda shape: pl.BlockSpec(shape, lambda i: (0, 0))
    tok = pl.BlockSpec((ts, d), lambda i: (i, 0))
    outs = pl.pallas_call(
        functools.partial(_rwkv_prep_kernel, tiles_per_seq=tiles_per_seq),
        grid=(m // ts,),
        in_specs=[
            pl.BlockSpec((ts, RWKV_IN), lambda i: (i, 0)),
            pl.BlockSpec((8, RWKV_IN), lambda i: (jnp.maximum(i * (ts // 8) - 1, 0), 0)),
            full((1, RWKV_IN)), full((W_LORA + A_LORA, 2 * d)), full((1, d)), full((1, d)),
            full((G_LORA, d)), full((1, d)), full((1, d)), full((1, d)),
            full((ts, ts)), full((ts, ts)), full((d, LANES)), full((LANES, d)),
        ],
        out_specs=[tok] * 10,
        out_shape=[jax.ShapeDtypeStruct((m, d), BF16)] * 5 + [jax.ShapeDtypeStruct((m, d), F32)] * 5,
        compiler_params=_cparams("parallel"),
        name="rwkv_prep",
    )(p, p, row(mu), lora_w, row(w0), row(a0), g_up, row(k_k), row(k_a), row(r_k), tril, ones, seg, segt)
    return outs


def _stack_heads(x):
    lane_head = lax.broadcasted_iota(jnp.int32, x.shape, 1) // HEAD
    return jnp.concatenate([jnp.where(lane_head == h, x, 0.0) for h in range(HEADS_PER_GROUP)], axis=0)


def _unstack_heads(x):
    out = x[0:CHUNK]
    for h in range(1, HEADS_PER_GROUP):
        out = out + x[h * CHUNK:(h + 1) * CHUNK]
    return out


def _rwkv_intra_kernel(at_ref, bt_ref, kt_ref, rt_ref, v_ref, bh_ref, kh_ref, gc_ref,
                       ra_ref, o2_ref, p_ref, q_ref, *, mm):
    w = GROUP_W
    n_chunks = at_ref.shape[0] // CHUNK
    ri = lax.broadcasted_iota(jnp.int32, (w, w), 0)
    ci = lax.broadcasted_iota(jnp.int32, (w, w), 1)
    strict = (ri % CHUNK) > (ci % CHUNK)
    incl = (ri % CHUNK) >= (ci % CHUNK)
    blk16 = (ri // 16) == (ci // 16)
    eye = (ri == ci).astype(F32)

    chunks = [slice(c * CHUNK, (c + 1) * CHUNK) for c in range(n_chunks)]

    def each(f, *lists):
        return [f(*args) for args in zip(*lists)]

    def stacked(ref):
        return [_stack_heads(ref[rows, :]) for rows in chunks]

    a_s, r_s, b_s, k_s, v_s = stacked(at_ref), stacked(rt_ref), stacked(bt_ref), stacked(kt_ref), stacked(v_ref)
    s = each(lambda a, r, b, k: mm(jnp.concatenate([a, r], axis=0), jnp.concatenate([b, k], axis=0), nt=True),
             a_s, r_s, b_s, k_s)
    l_ab = [jnp.where(strict, x[:w, :w], 0.0) for x in s]
    l_ak = [jnp.where(strict, x[:w, w:], 0.0) for x in s]
    m_rb = [jnp.where(incl, x[w:, :w], 0.0) for x in s]
    m_rk = [jnp.where(incl, x[w:, w:], 0.0) for x in s]

    def mm2(a, b0, b1):
        out = mm(a, jnp.concatenate([b0, b1], axis=1))
        return out[:, :w], out[:, w:]

    dg = [jnp.where(blk16, x, 0.0) for x in l_ab]
    off = each(lambda x, y: x - y, l_ab, dg)
    td = [eye + x for x in dg]
    pw = each(mm, dg, dg)
    for _ in range(2):
        sq_ptd = each(mm2, pw, pw, td)
        td = each(lambda t_, x_: t_ + x_[1], td, sq_ptd)
        pw = [x_[0] for x_ in sq_ptd]
    td = each(lambda t_, p_: t_ + mm(p_, t_), td, pw)
    n1 = each(mm, td, off)
    sq_ntd = each(mm2, n1, n1, td)
    t1 = each(lambda t_, x_: t_ + x_[1], td, sq_ntd)
    t = each(lambda t_, x_: t_ + mm(x_[0], t_), t1, sq_ntd)

    lv = each(mm, l_ak, v_s)
    y = each(lambda tt, x, a: mm(tt, jnp.concatenate([x, a], axis=1)), t, lv, a_s)
    z = each(mm, m_rb, y)
    mv = each(mm, m_rk, v_s)
    z2 = each(lambda ref_rows, yy: mm(_stack_heads(bh_ref[ref_rows, :]).T, yy), chunks, y)
    kv = each(lambda ref_rows, vv: mm(_stack_heads(kh_ref[ref_rows, :]).T, vv), chunks, v_s)
    for c, rows in enumerate(chunks):
        ra_ref[rows, :] = _unstack_heads(r_s[c] + z[c][:, w:]).astype(ra_ref.dtype)
        o2_ref[rows, :] = _unstack_heads(z[c][:, :w] + mv[c])
        p_ref[rows, :] = _unstack_heads(eye * gc_ref[c * CHUNK:c * CHUNK + 1, :]
                                        + z2[c][:, w:]).astype(p_ref.dtype)
        q_ref[rows, :] = _unstack_heads(z2[c][:, :w] + kv[c])


def _mm_bf16(a, b, nt=False):
    a = a.astype(BF16)
    b = b.astype(BF16)
    return _dot_nt(a, b) if nt else _dot(a, b)


def _rwkv_intra_call(at, bt, kt, rt, v, bh, kh, gc, seq):
    m, d = at.shape
    ts = _tile(seq, 512, CHUNK)
    spec = pl.BlockSpec((ts, GROUP_W), lambda i, j: (i, j))
    return pl.pallas_call(
        functools.partial(_rwkv_intra_kernel, mm=_mm_bf16),
        grid=(m // ts, d // GROUP_W),
        in_specs=[spec] * 8,
        out_specs=[spec] * 4,
        out_shape=[jax.ShapeDtypeStruct((m, d), dt) for dt in (BF16, F32, BF16, F32)],
        compiler_params=_cparams("parallel", "parallel"),
        name="rwkv_intra",
    )(at, bt, kt, rt, v, bh, kh, gc)


def _rwkv_seq_kernel(ra_ref, o2_ref, p_ref, q_ref, bonus_ref, g_ref, gnw_ref, gnb_ref, seg_ref, segt_ref,
                     o_ref, h_sc, y_sc, *, mm):
    w = GROUP_W
    n_groups = RWKV_DIM // w
    n_chunks = ra_ref.shape[0] // CHUNK

    @pl.when(pl.program_id(1) == 0)
    def _():
        h_sc[...] = jnp.zeros_like(h_sc)

    def chunk(c, carry):
        r0 = pl.multiple_of(c * CHUNK, CHUNK)
        rows = pl.ds(r0, CHUNK)
        groups = [slice(gi * w, (gi + 1) * w) for gi in range(n_groups)]
        lhs = [jnp.concatenate([ra_ref[rows, cols], _stack_heads(p_ref[rows, cols])], axis=0) for cols in groups]
        res = [mm(lhs[gi], h_sc[gi]) for gi in range(n_groups)]
        for gi, cols in enumerate(groups):
            y_sc[rows, cols] = res[gi][:CHUNK] + o2_ref[rows, cols]
            h_sc[gi] = res[gi][CHUNK:] + _stack_heads(q_ref[rows, cols])
        return carry

    lax.fori_loop(0, n_chunks, chunk, 0)

    y = y_sc[...]
    seg = seg_ref[...]
    segt = segt_ref[...]
    mean = _dot_lx(_dot_lx(y, seg), segt) * (1.0 / HEAD)
    yc = y - mean
    var = _dot_lx(_dot_lx(yc * yc, seg), segt) * (1.0 / HEAD)
    yn = yc * lax.rsqrt(var + GN_EPS) * gnw_ref[...] + gnb_ref[...]
    o_ref[...] = ((yn + bonus_ref[...]) * g_ref[...]).astype(o_ref.dtype)


def _rwkv_seq_call(ra, o2, p, q, bonus, g, gn_w, gn_b, batch, seq):
    m, d = ra.shape
    ts = _tile(seq, 512, CHUNK)
    nt = seq // ts
    lane = np.arange(d)
    seg_np = (lane[:, None] // HEAD) == np.arange(LANES)[None, :]
    seg = jnp.asarray(seg_np, BF16)
    segt = jnp.asarray(seg_np.T, BF16)
    tok = pl.BlockSpec((ts, d), lambda b, i: (b * nt + i, 0))
    full = lambda shape: pl.BlockSpec(shape, lambda b, i: (0, 0))
    return pl.pallas_call(
        functools.partial(_rwkv_seq_kernel, mm=_mm_bf16),
        grid=(batch, nt),
        in_specs=[tok] * 6 + [full((1, d)), full((1, d)), full((d, LANES)), full((LANES, d))],
        out_specs=tok,
        out_shape=jax.ShapeDtypeStruct((m, d), BF16),
        scratch_shapes=[pltpu.VMEM((d // GROUP_W, GROUP_W, GROUP_W), F32), pltpu.VMEM((ts, d), F32)],
        compiler_params=_cparams("arbitrary", "arbitrary"),
        name="rwkv_seq",
    )(ra, o2, p, q, bonus, g, gn_w.reshape(1, d), gn_b.reshape(1, d), seg, segt)


def _rope_tab(pos_ref, inv_ref):
    ang = pos_ref[...].astype(F32) * inv_ref[...]
    lane = lax.broadcasted_iota(jnp.int32, ang.shape, 1)
    return jnp.where(lane < QK_ROPE, jnp.cos(ang), jnp.sin(ang))


def _mla_proj_kernel(h_ref, pos_ref, inv_ref, wm_ref, qn_ref, wq_ref, kvn_ref, wkv_ref, q_ref, k_ref, v_ref,
                     *, scale):
    p = _dot(h_ref[...], wm_ref[...])
    tab = _rope_tab(pos_ref, inv_ref)

    hn = _rms(p[:, 0:Q_LORA], qn_ref[...]).astype(BF16)
    q = _dot(hn, wq_ref[...]) * scale
    for h in range(MLA_HEADS):
        c0 = h * QK_CAT
        q_ref[0, h, :, 0:QK_NOPE] = q[:, c0:c0 + QK_NOPE].astype(BF16)
        q_ref[0, h, :, QK_NOPE:QK_CAT] = (q[:, c0 + QK_NOPE:c0 + QK_CAT] * tab).astype(BF16)

    hn = _rms(p[:, Q_LORA:Q_LORA + KV_LORA], kvn_ref[...]).astype(BF16)
    kv = _dot(hn, wkv_ref[...])
    t = p[:, Q_LORA + KV_LORA:] * tab
    k_rope = (t + pltpu.roll(t, QK_ROPE, axis=1)).astype(BF16)
    for h in range(MLA_HEADS):
        c0 = h * (QK_NOPE + V_HEAD)
        k_ref[0, h, :, 0:QK_NOPE] = kv[:, c0:c0 + QK_NOPE].astype(BF16)
        k_ref[0, h, :, QK_NOPE:QK_CAT] = k_rope
        v_ref[0, h, :, 0:V_HEAD] = kv[:, c0 + QK_NOPE:c0 + QK_NOPE + V_HEAD].astype(BF16)
        v_ref[0, h, :, V_HEAD:V_EXT] = jnp.ones((kv.shape[0], V_EXT - V_HEAD), BF16)


def _mla_proj_call(hn, pos, inv_tab, w_mla, q_norm, wq, kv_norm, wkv, batch, seq):
    d = hn.shape[1]
    tm = _tile(seq, 512)
    nt = seq // tm
    scale = float((QK_NOPE + QK_ROPE) ** -0.5) * LOG2_E
    tok = lambda w: pl.BlockSpec((tm, w), lambda b, i: (b * nt + i, 0))
    full = lambda shape: pl.BlockSpec(shape, lambda b, i: (0, 0))
    headed = lambda w: pl.BlockSpec((1, MLA_HEADS, tm, w), lambda b, i: (b, 0, i, 0))
    return pl.pallas_call(
        functools.partial(_mla_proj_kernel, scale=scale),
        grid=(batch, nt),
        in_specs=[tok(d), tok(1), full((1, LANES)), full(w_mla.shape), full((1, Q_LORA)), full(wq.shape),
                  full((1, KV_LORA)), full(wkv.shape)],
        out_specs=[headed(QK_CAT), headed(QK_CAT), headed(V_EXT)],
        out_shape=[jax.ShapeDtypeStruct((batch, MLA_HEADS, seq, QK_CAT), BF16),
                   jax.ShapeDtypeStruct((batch, MLA_HEADS, seq, QK_CAT), BF16),
                   jax.ShapeDtypeStruct((batch, MLA_HEADS, seq, V_EXT), BF16)],
        compiler_params=_cparams("parallel", "parallel"),
        name="mla_proj",
    )(hn, pos, inv_tab, w_mla, q_norm.reshape(1, -1), wq, kv_norm.reshape(1, -1), wkv)


def _cast_expert_block(w_ref, wo_ref):
    for f in range(wo_ref.shape[1]):
        wo_ref[0, f] = w_ref[:, f * MOE_TF:(f + 1) * MOE_TF].astype(BF16)


def _expert_cast_specs(w, n_blocks, block_of):
    n_exp, d, ff = w.shape
    nf = ff // MOE_TF
    rows = n_exp * d // n_blocks
    per_exp = d // rows
    in_spec = pl.BlockSpec((rows, ff), lambda *a: (block_of(*a), 0))
    out_spec = pl.BlockSpec((1, nf, rows, MOE_TF),
                            lambda *a: (block_of(*a) // per_exp, 0, block_of(*a) % per_exp, 0))
    out_shape = jax.ShapeDtypeStruct((n_exp, nf, d, MOE_TF), BF16)
    return in_spec, out_spec, out_shape, w.reshape(n_exp * d, ff)


def _expert_cast_call(w):
    in_spec, out_spec, out_shape, arg = _expert_cast_specs(w, EXPERT_CAST_BLOCKS, lambda i: i)
    return pl.pallas_call(
        _cast_expert_block,
        grid=(EXPERT_CAST_BLOCKS,),
        in_specs=[in_spec],
        out_specs=out_spec,
        out_shape=out_shape,
        compiler_params=_cparams("parallel"),
        name="expert_cast",
    )(arg)


def _attn_cast_kernel(qi_ref, kj_ref, q_ref, k_ref, v_ref, w_ref, o_ref, wo_ref, m_sc, acc_sc):
    _cast_expert_block(w_ref, wo_ref)
    _attn_kernel(qi_ref, kj_ref, q_ref, k_ref, v_ref, o_ref, m_sc, acc_sc)


def _attn_kernel(qi_ref, kj_ref, q_ref, k_ref, v_ref, o_ref, m_sc, acc_sc):
    t = pl.program_id(2)
    qi = qi_ref[t]
    kj = kj_ref[t]
    n_heads = q_ref.shape[1]
    tq = q_ref.shape[2]
    tk = k_ref.shape[2]
    chunk_gap = kj * (tk // CHUNK) - qi * (tq // CHUNK)

    @pl.when(kj == 0)
    def _():
        m_sc[...] = jnp.full_like(m_sc, NEG_BIG)
        acc_sc[...] = jnp.zeros_like(acc_sc)

    def update(heads, mask):
        s = [_dot_nt(q_ref[0, h], k_ref[0, h]) for h in heads]
        if mask:
            ri = lax.broadcasted_iota(jnp.int32, s[0].shape, 0) // CHUNK
            ci = lax.broadcasted_iota(jnp.int32, s[0].shape, 1) // CHUNK
            visible = ci - ri <= -chunk_gap
            s = [jnp.where(visible, x, NEG_BIG) for x in s]
        for i, h in enumerate(heads):
            m_prev = m_sc[h]
            m_new = jnp.maximum(m_prev, jnp.max(s[i], axis=-1, keepdims=True))
            alpha = jnp.exp2(m_prev - m_new)
            p = jnp.exp2(s[i] - jnp.concatenate([m_new] * (tk // LANES), axis=1))
            pv = _dot(p.astype(BF16), v_ref[0, h])
            acc_sc[h] = jnp.concatenate([alpha] * (V_EXT // LANES), axis=1) * acc_sc[h] + pv
            m_sc[h] = m_new

    pairs = [tuple(range(h, min(h + 2, n_heads))) for h in range(0, n_heads, 2)]

    fully_visible = chunk_gap <= -(tk // CHUNK - 1)

    @pl.when(fully_visible)
    def _():
        for hp in pairs:
            update(hp, False)

    @pl.when(jnp.logical_not(fully_visible))
    def _():
        for hp in pairs:
            update(hp, True)

    @pl.when((kj + 1) * tk == (qi + 1) * tq)
    def _():
        for h in range(n_heads):
            acc = acc_sc[h]
            o_ref[:, h * V_HEAD:(h + 1) * V_HEAD] = (acc[:, :V_HEAD] / acc[:, V_HEAD:]).astype(o_ref.dtype)


def _attn_call(q, k, v, batch, seq, expert_w):
    tk = _tile(seq, 512, CHUNK)
    tq = _tile(seq, 2 * tk, tk)
    nq = seq // tq
    hg = ATTN_HEADS_PER_STEP
    n_hg = MLA_HEADS // hg
    pairs = [(i, j) for i in range(nq) for j in range((i + 1) * tq // tk)]
    qi = jnp.asarray([p[0] for p in pairs], jnp.int32)
    kj = jnp.asarray([p[1] for p in pairs], jnp.int32)
    in_specs = [
        pl.BlockSpec((1, hg, tq, QK_CAT), lambda b, h, t, qi, kj: (b, h, qi[t], 0)),
        pl.BlockSpec((1, hg, tk, QK_CAT), lambda b, h, t, qi, kj: (b, h, kj[t], 0)),
        pl.BlockSpec((1, hg, tk, V_EXT), lambda b, h, t, qi, kj: (b, h, kj[t], 0)),
    ]
    out_specs = [pl.BlockSpec((tq, hg * V_HEAD), lambda b, h, t, qi, kj: (b * nq + qi[t], h))]
    out_shape = [jax.ShapeDtypeStruct((batch * seq, MLA_HEADS * V_HEAD), BF16)]
    args = (qi, kj, q, k, v)
    steps = batch * n_hg * len(pairs)
    n_blocks = 1 << (steps.bit_length() - 1)
    rows_total = expert_w.shape[0] * expert_w.shape[1]
    fused = n_blocks * EXPERT_CAST_MAX_ROWS >= rows_total
    if fused:
        def block_of(b, h, t, qi, kj):
            return jnp.minimum((b * n_hg + h) * len(pairs) + t, n_blocks - 1)

        c_in, c_out, c_shape, c_arg = _expert_cast_specs(expert_w, n_blocks, block_of)
        in_specs, out_specs, out_shape, args = in_specs + [c_in], out_specs + [c_out], out_shape + [c_shape], args + (c_arg,)
    grid_spec = pltpu.PrefetchScalarGridSpec(
        num_scalar_prefetch=2,
        grid=(batch, n_hg, len(pairs)),
        in_specs=in_specs,
        out_specs=out_specs,
        scratch_shapes=[pltpu.VMEM((hg, tq, LANES), F32), pltpu.VMEM((hg, tq, V_EXT), F32)],
    )
    outs = pl.pallas_call(
        _attn_cast_kernel if fused else _attn_kernel,
        grid_spec=grid_spec,
        out_shape=out_shape,
        compiler_params=_cparams("arbitrary", "arbitrary", "arbitrary"),
        name="mla_attn",
    )(*args)
    if fused:
        return outs[0], outs[1]
    return outs[0], _expert_cast_call(expert_w)


def _mix_out_kernel(ya_ref, yb_ref, w_ref, x_ref, g_ref, xo_ref, hn_ref):
    half = ya_ref.shape[1]
    acc = _dot(ya_ref[...], w_ref[0:half, :]) + _dot(yb_ref[...], w_ref[half:, :])
    x = x_ref[...] + acc
    xo_ref[...] = x
    hn_ref[...] = _rms(x, g_ref[...]).astype(hn_ref.dtype)


def _mix_out_call(ya, yb, w, x, g):
    m, d = x.shape
    tm = _tile(m, 512)
    half = ya.shape[1]
    row = pl.BlockSpec((tm, d), lambda i: (i, 0))
    return pl.pallas_call(
        _mix_out_kernel,
        grid=(m // tm,),
        in_specs=[pl.BlockSpec((tm, half), lambda i: (i, 0)), pl.BlockSpec((tm, half), lambda i: (i, 0)),
                  pl.BlockSpec((2 * half, d), lambda i: (0, 0)), row, pl.BlockSpec((1, d), lambda i: (0, 0))],
        out_specs=[row, row],
        out_shape=[jax.ShapeDtypeStruct((m, d), F32), jax.ShapeDtypeStruct((m, d), BF16)],
        compiler_params=_cparams("parallel"),
        name="mix_out",
    )(ya, yb, w, x, g.reshape(1, d))


def _conv_out_kernel(a_ref, w_ref, b_ref, x_ref, g_ref, r_ref, xo_ref, hn_ref, lg_ref):
    tm = x_ref.shape[0]
    xo_ref[...] = _dot(a_ref[...], w_ref[...])
    half = tm // 2
    for r0 in range(0, tm, half):
        rows = slice(r0, r0 + half)
        x = x_ref[rows, :] + xo_ref[rows, :] + b_ref[...]
        xo_ref[rows, :] = x
        hn = _rms(x, g_ref[...])
        for j in range(SLABS):
            hn_ref[pl.ds(r0 * SLABS + j, half, stride=SLABS), :] = hn[:, j * LANES:(j + 1) * LANES]
        lg_ref[rows, :] = _dot_hp(hn, r_ref[...])


def _conv_out_call(a, w, b, x, g, router_pad):
    m, d = x.shape
    tm = _tile(m, 512)
    row = pl.BlockSpec((tm, d), lambda i: (i, 0))
    full = lambda shape: pl.BlockSpec(shape, lambda i: (0, 0))
    return pl.pallas_call(
        _conv_out_kernel,
        grid=(m // tm,),
        in_specs=[row, full((d, d)), full((1, d)), row, full((1, d)), full((d, LANES))],
        out_specs=[row, pl.BlockSpec((tm * SLABS, LANES), lambda i: (i, 0)),
                   pl.BlockSpec((tm, LANES), lambda i: (i, 0))],
        out_shape=[jax.ShapeDtypeStruct((m, d), F32), jax.ShapeDtypeStruct((m * SLABS, LANES), F32),
                   jax.ShapeDtypeStruct((m, LANES), F32)],
        compiler_params=_cparams("parallel"),
        name="conv_out",
    )(a, w, b.reshape(1, d), x, g.reshape(1, d), router_pad)


def _ffn_kernel(h_ref, w1_ref, w3_ref, w2_ref, x_ref, g_ref, xo_ref, hn_ref):
    f = pl.program_id(1)

    @pl.when(f == 0)
    def _():
        xo_ref[...] = x_ref[...]

    h = h_ref[...]
    gate = (_silu(_dot(h, w1_ref[...])) * _dot(h, w3_ref[...])).astype(BF16)
    xo_ref[...] += _dot(gate, w2_ref[...])

    @pl.when(f == pl.num_programs(1) - 1)
    def _():
        hn_ref[...] = _rms(xo_ref[...], g_ref[...]).astype(hn_ref.dtype)


def _ffn_call(h, w1, w3, w2, x, g):
    m, d = x.shape
    ff = w1.shape[1]
    tm = _tile(m, 512)
    tf = _tile(ff, 512, LANES)
    row = pl.BlockSpec((tm, d), lambda i, f: (i, 0))
    wtile = pl.BlockSpec((d, tf), lambda i, f: (0, f))
    return pl.pallas_call(
        _ffn_kernel,
        grid=(m // tm, ff // tf),
        in_specs=[row, wtile, wtile,
                  pl.BlockSpec((tf, d), lambda i, f: (f, 0)), row, pl.BlockSpec((1, d), lambda i, f: (0, 0))],
        out_specs=[row, row],
        out_shape=[jax.ShapeDtypeStruct((m, d), F32), jax.ShapeDtypeStruct((m, d), BF16)],
        compiler_params=_cparams("parallel", "arbitrary"),
        name="ffn",
    )(h, w1, w3, w2, x, g.reshape(1, d))


def _glu_kernel(h_ref, wa_ref, wb_ref, ba_ref, bb_ref, o_ref):
    h = h_ref[...]
    a = _dot(h, wa_ref[...]) + ba_ref[...]
    b = _dot(h, wb_ref[...]) + bb_ref[...]
    o_ref[...] = a * _sigmoid(b)


def _glu_call(h, w, b):
    m, d = h.shape
    n = w.shape[1] // 2
    tm = _tile(m, 1024)
    tn = _tile(n, 512, LANES)
    nj = n // tn
    b2 = b.reshape(1, 2 * n)
    return pl.pallas_call(
        _glu_kernel,
        grid=(m // tm, nj),
        in_specs=[pl.BlockSpec((tm, d), lambda i, j: (i, 0)),
                  pl.BlockSpec((d, tn), lambda i, j: (0, j)), pl.BlockSpec((d, tn), lambda i, j: (0, j + nj)),
                  pl.BlockSpec((1, tn), lambda i, j: (0, j)), pl.BlockSpec((1, tn), lambda i, j: (0, j + nj))],
        out_specs=pl.BlockSpec((tm, tn), lambda i, j: (i, j)),
        out_shape=jax.ShapeDtypeStruct((m, n), F32),
        compiler_params=_cparams("parallel", "arbitrary"),
        name="conv_glu",
    )(h, w, w, b2, b2)


def _dwconv_kernel(u_ref, halo_ref, dw_ref, dwb_ref, lg_ref, lb_ref, o_ref, ext_sc, acc_sc, *, tiles_per_seq):
    ts, d = u_ref.shape
    first = (pl.program_id(0) % tiles_per_seq) == 0
    ext_sc[0, 0:CONV_HALO, :] = jnp.where(first, 0.0, halo_ref[...])
    ext_sc[0, CONV_HALO:, :] = u_ref[...]
    n_shift = ts + CONV_HALO - SUBLANES
    for b in range(1, SUBLANES):
        ext_sc[b, 0:n_shift, :] = ext_sc[0, b:b + n_shift, :]
    rc, cc = 64, 256
    base = CONV_HALO - (CONV_WIDTH - 1)
    for c0 in range(0, d, cc):
        for r0 in range(0, ts, rc):
            acc = jnp.zeros((rc, cc), F32) + dwb_ref[:, c0:c0 + cc]
            for j in range(CONV_WIDTH):
                a8, b = divmod(base + j, SUBLANES)
                rows = slice(r0 + a8 * SUBLANES, r0 + a8 * SUBLANES + rc)
                acc = acc + dw_ref[j:j + 1, c0:c0 + cc] * ext_sc[b, rows, c0:c0 + cc]
            acc_sc[r0:r0 + rc, c0:c0 + cc] = acc
    y = acc_sc[...]
    mu = jnp.mean(y, axis=-1, keepdims=True)
    yc = y - mu
    var = jnp.mean(yc * yc, axis=-1, keepdims=True)
    yn = yc * lax.rsqrt(var + LN_EPS) * lg_ref[...] + lb_ref[...]
    o_ref[...] = _silu(yn).astype(o_ref.dtype)


def _dwconv_cast_kernel(u_ref, halo_ref, dw_ref, dwb_ref, lg_ref, lb_ref, w_ref, o_ref, wo_ref, ext_sc, acc_sc,
                        *, tiles_per_seq):
    _cast_expert_block(w_ref, wo_ref)
    _dwconv_kernel(u_ref, halo_ref, dw_ref, dwb_ref, lg_ref, lb_ref, o_ref, ext_sc, acc_sc,
                   tiles_per_seq=tiles_per_seq)


def _dwconv_call(u, seq, dw_w, dw_b, ln_g, ln_b, expert_w):
    m, d = u.shape
    ts = _tile(seq, 256, CONV_HALO)
    tiles_per_seq = seq // ts
    steps = m // ts
    dw_pad = jnp.concatenate([dw_w, jnp.zeros((CONV_HALO - CONV_WIDTH, d), F32)], axis=0)
    full = lambda shape: pl.BlockSpec(shape, lambda i: (0, 0))
    in_specs = [pl.BlockSpec((ts, d), lambda i: (i, 0)),
                pl.BlockSpec((CONV_HALO, d), lambda i: (jnp.maximum(i * (ts // CONV_HALO) - 1, 0), 0)),
                full((CONV_HALO, d)), full((1, d)), full((1, d)), full((1, d))]
    out_specs = [pl.BlockSpec((ts, d), lambda i: (i, 0))]
    out_shape = [jax.ShapeDtypeStruct((m, d), BF16)]
    args = (u, u, dw_pad, dw_b.reshape(1, d), ln_g.reshape(1, d), ln_b.reshape(1, d))
    rows_total = expert_w.shape[0] * expert_w.shape[1]
    fused = steps * DWCONV_CAST_MAX_ROWS >= rows_total and rows_total % steps == 0
    if fused:
        c_in, c_out, c_shape, c_arg = _expert_cast_specs(expert_w, steps, lambda i: i)
        in_specs, out_specs, out_shape, args = in_specs + [c_in], out_specs + [c_out], out_shape + [c_shape], args + (c_arg,)
    outs = pl.pallas_call(
        functools.partial(_dwconv_cast_kernel if fused else _dwconv_kernel, tiles_per_seq=tiles_per_seq),
        grid=(steps,),
        in_specs=in_specs,
        out_specs=out_specs,
        out_shape=out_shape,
        scratch_shapes=[pltpu.VMEM((SUBLANES, ts + CONV_HALO, d), F32), pltpu.VMEM((ts, d), F32)],
        compiler_params=_cparams("arbitrary"),
        name="dwconv",
    )(*args)
    if fused:
        return outs[0], outs[1]
    return outs[0], _expert_cast_call(expert_w)


def _route_kernel(lg_ref, tri_ref, o_ref, cnt_ref, carry_sc):
    @pl.when(pl.program_id(0) == 0)
    def _():
        carry_sc[...] = jnp.zeros_like(carry_sc)

    lg = lg_ref[...]
    lane = lax.broadcasted_iota(jnp.int32, lg.shape, 1)
    lg = jnp.where(lane < N_EXPERTS, lg, -jnp.inf)
    m1 = jnp.max(lg, axis=-1, keepdims=True)
    e1 = jnp.min(jnp.where(lg == m1, lane, LANES), axis=-1, keepdims=True)
    lg2 = jnp.where(lane == e1, -jnp.inf, lg)
    m2 = jnp.max(lg2, axis=-1, keepdims=True)
    e2 = jnp.min(jnp.where(lg2 == m2, lane, LANES), axis=-1, keepdims=True)
    ex = jnp.exp(m2 - m1)
    g1 = 1.0 / (1.0 + ex)
    g2 = ex / (1.0 + ex)
    oh1 = (lane == e1).astype(F32)
    oh2 = (lane == e2).astype(F32)
    both = oh1 + oh2
    before = _dot(tri_ref[...], both.astype(BF16)) + carry_sc[...]
    r1 = jnp.sum(before * oh1, axis=-1, keepdims=True)
    r2 = jnp.sum(before * oh2, axis=-1, keepdims=True)
    carry_sc[...] = carry_sc[...] + jnp.sum(both, axis=0, keepdims=True)
    cnt_ref[...] = jnp.broadcast_to(carry_sc[...], cnt_ref.shape)
    out = jnp.where(lane == 0, e1.astype(F32), 0.0)
    out = jnp.where(lane == 1, e2.astype(F32), out)
    out = jnp.where(lane == 2, g1, out)
    out = jnp.where(lane == 3, g2, out)
    out = jnp.where(lane == 4, r1, out)
    out = jnp.where(lane == 5, r2, out)
    o_ref[...] = out


def _route_call(logits):
    m = logits.shape[0]
    tm = _tile(m, 512)
    idx = np.arange(tm)
    tri = jnp.asarray(idx[:, None] > idx[None, :], BF16)
    return pl.pallas_call(
        _route_kernel,
        grid=(m // tm,),
        in_specs=[pl.BlockSpec((tm, LANES), lambda i: (i, 0)), pl.BlockSpec((tm, tm), lambda i: (0, 0))],
        out_specs=[pl.BlockSpec((tm, LANES), lambda i: (i, 0)), pl.BlockSpec((8, LANES), lambda i: (0, 0))],
        out_shape=[jax.ShapeDtypeStruct((m, LANES), F32), jax.ShapeDtypeStruct((8, LANES), F32)],
        scratch_shapes=[pltpu.VMEM((1, LANES), F32)],
        compiler_params=_cparams("arbitrary"),
        name="moe_route",
    )(logits, tri)


def _slab_copy(src_ref, dst_ref, sem, src_slab, dst_row):
    src = src_ref.at[pl.ds(pl.multiple_of(src_slab, SLABS), SLABS)]
    dst = dst_ref.at[pl.ds(pl.multiple_of(dst_row * SLABS, SLABS), SLABS)]
    return pltpu.make_async_copy(src, dst, sem)


def _slab_cols(ref, j, rows):
    return ref[pl.ds(j, rows, stride=SLABS), :]


def _gather_kernel(used_ref, tok_ref, nxt_ref, src_ref, o_ref, buf_sc, sem):
    rows = o_ref.shape[0]
    i = pl.program_id(0)
    last = pl.num_programs(0) - 1

    def start_block(idx_ref, slot):
        def body(r, c):
            _slab_copy(src_ref, buf_sc.at[slot], sem.at[slot], idx_ref[0, 0, r], r).start()
            return c
        lax.fori_loop(0, rows, body, 0, unroll=8)

    def wait_block(slot):
        def body(r, c):
            _slab_copy(src_ref, buf_sc.at[slot], sem.at[slot], 0, r).wait()
            return c
        lax.fori_loop(0, rows, body, 0, unroll=8)

    @pl.when(jnp.logical_and(i == 0, used_ref[0] > 0))
    def _():
        start_block(tok_ref, 0)

    for slot in range(2):
        @pl.when(jnp.logical_and(i % 2 == slot, jnp.logical_and(i < last, used_ref[jnp.minimum(i + 1, last)] > 0)))
        def _():
            start_block(nxt_ref, 1 - slot)

        @pl.when(jnp.logical_and(i % 2 == slot, used_ref[i] > 0))
        def _():
            wait_block(slot)
            for j in range(SLABS):
                o_ref[:, j * LANES:(j + 1) * LANES] = _slab_cols(buf_sc.at[slot], j, rows).astype(o_ref.dtype)

    @pl.when(used_ref[i] == 0)
    def _():
        o_ref[...] = jnp.zeros_like(o_ref)


def _gather_call(src_slabs, slot_slab, block_used):
    cap = slot_slab.shape[0]
    d = SLABS * LANES
    rows = GATHER_ROWS
    nb = cap // rows
    grid_spec = pltpu.PrefetchScalarGridSpec(
        num_scalar_prefetch=1,
        grid=(nb,),
        in_specs=[pl.BlockSpec((1, 1, rows), lambda i, u: (i, 0, 0), memory_space=pltpu.SMEM),
                  pl.BlockSpec((1, 1, rows), lambda i, u: (jnp.minimum(i + 1, nb - 1), 0, 0),
                               memory_space=pltpu.SMEM),
                  pl.BlockSpec(memory_space=pl.ANY)],
        out_specs=pl.BlockSpec((rows, d), lambda i, u: (i, 0)),
        scratch_shapes=[pltpu.VMEM((2, rows * SLABS, LANES), F32), pltpu.SemaphoreType.DMA((2,))],
    )
    slots = slot_slab.reshape(nb, 1, rows)
    return pl.pallas_call(
        _gather_kernel,
        grid_spec=grid_spec,
        out_shape=jax.ShapeDtypeStruct((cap, d), BF16),
        compiler_params=_cparams("arbitrary"),
        name="moe_gather",
    )(block_used, slots, slots, src_slabs)


def _moe_kernel(te_ref, tr_ref, x_ref, w1_ref, w3_ref, w2_ref, o_ref, w2_sc):
    i = pl.program_id(0)
    f = pl.program_id(1)
    nrows = tr_ref[i]
    tm = x_ref.shape[0]

    @pl.when(f == 0)
    def _():
        o_ref[...] = jnp.zeros_like(o_ref)

    def swiglu(rows):
        x = x_ref[rows, :]
        gate = (_silu(_dot(x, w1_ref[0, 0])) * _dot(x, w3_ref[0, 0])).astype(BF16)
        o_ref[rows, :] += _dot(gate, w2_sc[...])

    @pl.when(nrows == tm)
    def _():
        w2_sc[...] = w2_ref[0].astype(BF16)
        swiglu(slice(0, tm))

    @pl.when(jnp.logical_and(nrows > 0, nrows < tm))
    def _():
        w2_sc[...] = w2_ref[0].astype(BF16)
        for s in range(tm // MOE_SUB):
            @pl.when(s * MOE_SUB < nrows)
            def _():
                swiglu(slice(s * MOE_SUB, (s + 1) * MOE_SUB))


def _moe_call(xs, tile_expert, tile_rows, w1t, w3t, w2):
    cap, d = xs.shape
    n_exp, nf, _, tf = w1t.shape
    tm = MOE_TM

    def fidx(i, f, te, tr):
        return jnp.where(tr[i] > 0, f, nf - 1)

    grid_spec = pltpu.PrefetchScalarGridSpec(
        num_scalar_prefetch=2,
        grid=(cap // tm, nf),
        in_specs=[
            pl.BlockSpec((tm, d), lambda i, f, te, tr: (i, 0)),
            pl.BlockSpec((1, 1, d, tf), lambda i, f, te, tr: (te[i], fidx(i, f, te, tr), 0, 0)),
            pl.BlockSpec((1, 1, d, tf), lambda i, f, te, tr: (te[i], fidx(i, f, te, tr), 0, 0)),
            pl.BlockSpec((1, tf, d), lambda i, f, te, tr: (te[i], fidx(i, f, te, tr), 0)),
        ],
        out_specs=pl.BlockSpec((tm, d), lambda i, f, te, tr: (i, 0)),
        scratch_shapes=[pltpu.VMEM((tf, d), BF16)],
    )
    return pl.pallas_call(
        _moe_kernel,
        grid_spec=grid_spec,
        out_shape=jax.ShapeDtypeStruct((cap, d), F32),
        compiler_params=_cparams("parallel", "arbitrary"),
        name="moe_experts",
    )(tile_expert, tile_rows, xs, w1t, w3t, w2)


def _row_copy(src_ref, dst_ref, sem, src_row, dst_row):
    return pltpu.make_async_copy(src_ref.at[pl.ds(src_row, 1)], dst_ref.at[pl.ds(dst_row, 1)], sem)


def _combine_kernel(pos_ref, nxt_ref, y_ref, x_ref, rt_ref, g_ref, o_ref, buf_sc, sem):
    rows = o_ref.shape[0]
    i = pl.program_id(0)
    last = pl.num_programs(0) - 1

    def start_block(idx_ref, slot):
        def body(r, c):
            for k in range(2):
                _row_copy(y_ref, buf_sc.at[slot, k], sem.at[slot], idx_ref[0, k, r], r).start()
            return c
        lax.fori_loop(0, rows, body, 0, unroll=8)

    def wait_block(slot):
        def body(r, c):
            for k in range(2):
                _row_copy(y_ref, buf_sc.at[slot, k], sem.at[slot], 0, r).wait()
            return c
        lax.fori_loop(0, rows, body, 0, unroll=8)

    @pl.when(i == 0)
    def _():
        start_block(pos_ref, 0)

    for slot in range(2):
        @pl.when(jnp.logical_and(i % 2 == slot, i < last))
        def _():
            start_block(nxt_ref, 1 - slot)

        @pl.when(i % 2 == slot)
        def _():
            wait_block(slot)
            rt = rt_ref[...]
            x = x_ref[...] + rt[:, 2:3] * buf_sc[slot, 0] + rt[:, 3:4] * buf_sc[slot, 1]
            o_ref[...] = _rms(x, g_ref[...])


def _combine_call(pos, yb, x, route, g):
    m, d = x.shape
    rows = GATHER_ROWS if m % GATHER_ROWS == 0 else m
    nb = m // rows
    slots = pos.reshape(nb, rows, 2).transpose(0, 2, 1)
    return pl.pallas_call(
        _combine_kernel,
        grid=(nb,),
        in_specs=[pl.BlockSpec((1, 2, rows), lambda i: (i, 0, 0), memory_space=pltpu.SMEM),
                  pl.BlockSpec((1, 2, rows), lambda i: (jnp.minimum(i + 1, nb - 1), 0, 0), memory_space=pltpu.SMEM),
                  pl.BlockSpec(memory_space=pl.ANY),
                  pl.BlockSpec((rows, d), lambda i: (i, 0)),
                  pl.BlockSpec((rows, LANES), lambda i: (i, 0)),
                  pl.BlockSpec((1, d), lambda i: (0, 0))],
        out_specs=pl.BlockSpec((rows, d), lambda i: (i, 0)),
        out_shape=jax.ShapeDtypeStruct((m, d), F32),
        scratch_shapes=[pltpu.VMEM((2, 2, rows, d), F32), pltpu.SemaphoreType.DMA((2,))],
        compiler_params=_cparams("arbitrary"),
        name="moe_combine",
    )(slots, slots, yb, x, route, g.reshape(1, d))


def _rot_cols(w):
    half = QK_ROPE // 2
    return jnp.concatenate([-w[..., half:], w[..., :half]], axis=-1)


def _mixer_weights(w_in, w_up, a_up, w_q_up):
    d = RWKV_DIM
    w_rwkv = w_in[:, :RWKV_IN]
    w_lat = w_in[:, RWKV_IN:RWKV_IN + Q_LORA + KV_LORA]
    w_kr = w_in[:, RWKV_IN + Q_LORA + KV_LORA:]
    w_mla = jnp.concatenate([w_lat, w_kr, _rot_cols(w_kr)], axis=1)
    lora = jnp.zeros((W_LORA + A_LORA, 2 * d), F32)
    lora = lora.at[:W_LORA, :d].set(w_up).at[W_LORA:, d:].set(a_up)
    wq = w_q_up.reshape(Q_LORA, MLA_HEADS, QK_NOPE + QK_ROPE)
    wq_rope = wq[..., QK_NOPE:]
    wq = jnp.concatenate([wq, _rot_cols(wq_rope)], axis=-1).reshape(Q_LORA, MLA_HEADS * QK_CAT)
    return w_rwkv.astype(BF16), w_mla.astype(BF16), lora, wq.astype(BF16)


def kernel(x, positions, l0_mix_norm, l0_w_in, l0_shift_mu, l0_w0, l0_w_up, l0_a0, l0_a_up, l0_g_up, l0_k_k, l0_k_a, l0_r_k, l0_gn_w, l0_gn_b, l0_q_norm, l0_w_q_up, l0_kv_norm, l0_w_kv_up, l0_w_out, l0_ffn_norm, l0_ffn_w1, l0_ffn_w3, l0_ffn_w2, l1_mix_norm, l1_pw1_w, l1_pw1_b, l1_dw_w, l1_dw_b, l1_ln_g, l1_ln_b, l1_pw2_w, l1_pw2_b, l1_ffn_norm, l1_router, l1_exp_w1, l1_exp_w3, l1_exp_w2, final_norm):
    batch, seq, d = x.shape
    m = batch * seq
    x0 = x.reshape(m, d)
    pos = positions.reshape(m, 1)
    inv = ROPE_THETA ** (-jnp.arange(0, QK_ROPE, 2, dtype=F32) / QK_ROPE)
    inv_tab = jnp.tile(inv, LANES // inv.shape[0]).reshape(1, LANES)

    w_rwkv, w_mla, lora_w, wq = _mixer_weights(l0_w_in, l0_w_up, l0_a_up, l0_w_q_up)

    hn0 = _rmsnorm_call(x0, l0_mix_norm)
    p_rwkv = _matmul_call(hn0, w_rwkv, name="in_proj_rwkv")

    prep = _rwkv_prep_call(p_rwkv, seq, l0_shift_mu, lora_w, l0_w0, l0_a0, l0_g_up, l0_k_k, l0_k_a,
                           l0_r_k.reshape(-1))
    at, bt, kt, rt, v, bh, kh, gc, bonus, gate = prep
    ra, o2, pm, qm = _rwkv_intra_call(at, bt, kt, rt, v, bh, kh, gc, seq)
    y_rwkv = _rwkv_seq_call(ra, o2, pm, qm, bonus, gate, l0_gn_w, l0_gn_b, batch, seq)

    q, k, vv = _mla_proj_call(hn0, pos, inv_tab, w_mla, l0_q_norm, wq, l0_kv_norm, l0_w_kv_up.astype(BF16),
                              batch, seq)
    y_mla, w1t = _attn_call(q, k, vv, batch, seq, l1_exp_w1)

    x1, hn1 = _mix_out_call(y_rwkv, y_mla, l0_w_out.astype(BF16), x0, l0_ffn_norm)

    x2, hn2 = _ffn_call(hn1, l0_ffn_w1.astype(BF16), l0_ffn_w3.astype(BF16), l0_ffn_w2.astype(BF16),
                        x1, l1_mix_norm)

    u = _glu_call(hn2, l1_pw1_w.astype(BF16), l1_pw1_b)
    sc, w3t = _dwconv_call(u, seq, l1_dw_w, l1_dw_b, l1_ln_g, l1_ln_b, l1_exp_w3)
    router_pad = jnp.zeros((d, LANES), F32).at[:, :N_EXPERTS].set(l1_router)
    x3, hn3, logits = _conv_out_call(sc, l1_pw2_w.astype(BF16), l1_pw2_b, x2, l1_ffn_norm, router_pad)

    route, counts = _route_call(logits)
    e = route[:, 0:2].astype(jnp.int32)
    rank = route[:, 4:6].astype(jnp.int32)
    cnt = counts[0, :N_EXPERTS].astype(jnp.int32)
    padded = (cnt + MOE_TM - 1) // MOE_TM * MOE_TM
    pad_end = jnp.cumsum(padded)
    pad_start = pad_end - padded
    pos_slot = pad_start[e] + rank
    cap = (2 * m + N_EXPERTS * MOE_TM + MOE_TM - 1) // MOE_TM * MOE_TM
    tok = jnp.broadcast_to(jnp.arange(m, dtype=jnp.int32)[:, None], (m, 2))
    slot_slab = jnp.zeros((cap,), jnp.int32).at[pos_slot.reshape(-1)].set(tok.reshape(-1) * SLABS)
    tile_start = jnp.arange(cap // MOE_TM, dtype=jnp.int32) * MOE_TM
    tile_expert = jnp.minimum(jnp.sum(tile_start[:, None] >= pad_end[None, :], axis=1), N_EXPERTS - 1)
    used = tile_start < pad_end[-1]
    last_used = jnp.max(jnp.where(used, tile_expert, 0))
    tile_expert = jnp.where(used, tile_expert, last_used).astype(jnp.int32)
    tile_rows = jnp.clip(pad_start[tile_expert] + cnt[tile_expert] - tile_start, 0, MOE_TM)
    tile_rows = jnp.where(used, tile_rows, 0).astype(jnp.int32)

    per_tile = MOE_TM // GATHER_ROWS
    block_off = jnp.tile(jnp.arange(per_tile, dtype=jnp.int32) * GATHER_ROWS, cap // MOE_TM)
    block_used = (jnp.repeat(tile_rows, per_tile) > block_off).astype(jnp.int32)
    xs = _gather_call(hn3, slot_slab, block_used)
    yb = _moe_call(xs, tile_expert, tile_rows, w1t, w3t, l1_exp_w2)
    out = _combine_call(pos_slot, yb, x3, route, final_norm)
    return out.reshape(batch, seq, d)
```

```python
import functools

import numpy as np
import jax
import jax.numpy as jnp
from jax import lax
from jax.experimental import pallas as pl
from jax.experimental.pallas import tpu as pltpu

F32 = jnp.float32
BF16 = jnp.bfloat16

D_MODEL = 2048
CHUNK = 64
HEAD = 64
N_HEADS = 16
RWKV_DIM = N_HEADS * HEAD
W_LORA, A_LORA, G_LORA = 64, 64, 128
RWKV_IN = 3 * RWKV_DIM + W_LORA + A_LORA + G_LORA
MLA_HEADS = 8
Q_LORA = KV_LORA = 512
QK_NOPE, QK_ROPE, V_HEAD = 128, 64, 128
QK_CAT = 2 * QK_NOPE
V_EXT = 2 * V_HEAD
ROPE_THETA = 10000.0
CONV_WIDTH = 31
CONV_HALO = 32
N_EXPERTS = 8
NORM_EPS = 1e-6
LN_EPS = 1e-5
GN_EPS = HEAD * 1e-5
NEG_BIG = -1e30

LANES = 128
SUBLANES = 8
VMEM_LIMIT = 56 * 1024 * 1024

HEADS_PER_GROUP = 2
GROUP_W = HEADS_PER_GROUP * HEAD
MOE_TM = 1024
MOE_SUB = 512
MOE_TF = 512
EXPERT_CAST_BLOCKS = 512
EXPERT_CAST_MAX_ROWS = 128
DWCONV_CAST_MAX_ROWS = 256
GATHER_ROWS = 512
SLABS = D_MODEL // LANES
ATTN_HEADS_PER_STEP = 8
LOG2_E = 1.4426950408889634


def _cparams(*sem):
    return pltpu.CompilerParams(dimension_semantics=sem, vmem_limit_bytes=VMEM_LIMIT)


def _tile(n, pref, align=8):
    if n <= pref:
        return n
    t = (pref // align) * align
    while t > align and n % t:
        t -= align
    assert n % t == 0, (n, pref)
    return t


def _dot(a, b):
    return jnp.dot(a, b, preferred_element_type=F32)


def _dot_nt(a, b):
    return lax.dot_general(a, b, (((1,), (1,)), ((), ())), preferred_element_type=F32)


def _split2(x):
    hi = x.astype(BF16)
    lo = (x - hi.astype(F32)).astype(BF16)
    return hi, lo


def _dot_lx(a, b_exact):
    hi, lo = _split2(a)
    return _dot(hi, b_exact) + _dot(lo, b_exact)


def _dot_hp(a, b):
    ah, al = _split2(a)
    bh, bl = _split2(b)
    return _dot(ah, bh) + (_dot(ah, bl) + _dot(al, bh))


def _sigmoid(x):
    return 1.0 / (1.0 + jnp.exp(-x))


def _silu(x):
    return x * _sigmoid(x)


def _softplus(x):
    return jnp.maximum(x, 0.0) + jnp.log(1.0 + jnp.exp(-jnp.abs(x)))


def _rms(x, g):
    return x * lax.rsqrt(jnp.mean(x * x, axis=-1, keepdims=True) + NORM_EPS) * g


def _rmsnorm_kernel(x_ref, g_ref, o_ref):
    o_ref[...] = _rms(x_ref[...], g_ref[...]).astype(o_ref.dtype)


def _rmsnorm_call(x, g):
    m, d = x.shape
    tm = _tile(m, 512)
    return pl.pallas_call(
        _rmsnorm_kernel,
        grid=(m // tm,),
        in_specs=[pl.BlockSpec((tm, d), lambda i: (i, 0)), pl.BlockSpec((1, d), lambda i: (0, 0))],
        out_specs=pl.BlockSpec((tm, d), lambda i: (i, 0)),
        out_shape=jax.ShapeDtypeStruct((m, d), BF16),
        compiler_params=_cparams("parallel"),
        name="rmsnorm",
    )(x, g.reshape(1, d))


def _matmul_kernel(a_ref, w_ref, o_ref):
    o_ref[...] = _dot(a_ref[...], w_ref[...])


def _matmul_call(a, w, tm_pref=1024, tn_pref=1664, name="matmul"):
    m, k = a.shape
    n = w.shape[1]
    tm = _tile(m, tm_pref)
    tn = _tile(n, tn_pref, LANES)
    return pl.pallas_call(
        _matmul_kernel,
        grid=(m // tm, n // tn),
        in_specs=[pl.BlockSpec((tm, k), lambda i, j: (i, 0)), pl.BlockSpec((k, tn), lambda i, j: (0, j))],
        out_specs=pl.BlockSpec((tm, tn), lambda i, j: (i, j)),
        out_shape=jax.ShapeDtypeStruct((m, n), F32),
        compiler_params=_cparams("parallel", "arbitrary"),
        name=name,
    )(a, w)


def _rwkv_prep_kernel(p_ref, prev_ref, mu_ref, lw_ref, w0_ref, a0_ref, gup_ref, kk_ref, ka_ref, rk_ref,
                      tril_ref, ones_ref, seg_ref, segt_ref,
                      at_ref, bt_ref, kt_ref, rt_ref, v_ref, bh_ref, kh_ref, gc_ref, bonus_ref, g_ref,
                      *, tiles_per_seq):
    ts = p_ref.shape[0]
    first = (pl.program_id(0) % tiles_per_seq) == 0
    row0 = lax.broadcasted_iota(jnp.int32, (ts, 1), 0) == 0

    def mixed(c0, c1):
        pc = p_ref[:, c0:c1]
        prev = jnp.where(first, 0.0, prev_ref[7:8, c0:c1])
        sh = jnp.where(row0, prev, pltpu.roll(pc, 1, axis=0))
        return pc + (sh - pc) * mu_ref[:, c0:c1]

    d = RWKV_DIM
    r = mixed(0, d)
    k = mixed(d, 2 * d)
    v = mixed(2 * d, 3 * d)
    xwa = mixed(3 * d, 3 * d + W_LORA + A_LORA)
    xg = mixed(3 * d + W_LORA + A_LORA, RWKV_IN)

    lane = lax.broadcasted_iota(jnp.int32, xwa.shape, 1)
    z = jnp.where(lane < W_LORA, jnp.tanh(xwa), xwa)
    wa = _dot_hp(z, lw_ref[...])
    w = -_softplus(-(w0_ref[...] + wa[:, :d])) - 0.5
    logdecay = -jnp.exp(w)
    a = _sigmoid(a0_ref[...] + wa[:, d:])
    g_ref[...] = _dot_hp(_sigmoid(xg), gup_ref[...])

    seg = seg_ref[...]
    segt = segt_ref[...]

    def head_sum(x):
        return _dot_lx(_dot_lx(x, seg), segt)

    kkr = k * kk_ref[...]
    kk = kkr / jnp.maximum(jnp.sqrt(head_sum(kkr * kkr)), 1e-12)
    k2 = k * (1.0 + (a - 1.0) * ka_ref[...])
    bonus_ref[...] = head_sum(r * k2 * rk_ref[...]) * v

    ld_hi, ld_lo = _split2(logdecay)
    tril = tril_ref[...]
    ones = ones_ref[...]
    gcum = _dot(tril, ld_hi) + _dot(tril, ld_lo)
    gtot = _dot(ones, ld_hi) + _dot(ones, ld_lo)
    g_in = jnp.exp(gcum)
    g_ex = jnp.exp(gcum - logdecay)
    g_inv = jnp.exp(-gcum)
    g_rest = jnp.exp(gtot - gcum)
    beta = kk * a
    at_ref[...] = (-kk * g_ex).astype(at_ref.dtype)
    bt_ref[...] = (beta * g_inv).astype(bt_ref.dtype)
    kt_ref[...] = (k2 * g_inv).astype(kt_ref.dtype)
    rt_ref[...] = (r * g_in).astype(rt_ref.dtype)
    v_ref[...] = v.astype(v_ref.dtype)
    bh_ref[...] = beta * g_rest
    kh_ref[...] = k2 * g_rest
    gc_ref[...] = jnp.exp(gtot)


def _rwkv_prep_call(p, seq, mu, lora_w, w0, a0, g_up, k_k, k_a, r_k):
    m = p.shape[0]
    ts = _tile(seq, 256, CHUNK)
    tiles_per_seq = seq // ts
    d = RWKV_DIM
    idx = np.arange(ts)
    same = (idx[:, None] // CHUNK) == (idx[None, :] // CHUNK)
    tril = jnp.asarray(same & (idx[:, None] >= idx[None, :]), BF16)
    ones = jnp.asarray(same, BF16)
    lane = np.arange(d)
    seg_np = (lane[:, None] // HEAD) == np.arange(LANES)[None, :]
    seg = jnp.asarray(seg_np, BF16)
    segt = jnp.asarray(seg_np.T, BF16)

    def row(x):
        return x.reshape(1, -1).astype(F32)

    full = lambda shape: pl.BlockSpec(shape, lambda i: (0, 0))
    tok = pl.BlockSpec((ts, d), lambda i: (i, 0))
    outs = pl.pallas_call(
        functools.partial(_rwkv_prep_kernel, tiles_per_seq=tiles_per_seq),
        grid=(m // ts,),
        in_specs=[
            pl.BlockSpec((ts, RWKV_IN), lambda i: (i, 0)),
            pl.BlockSpec((8, RWKV_IN), lambda i: (jnp.maximum(i * (ts // 8) - 1, 0), 0)),
            full((1, RWKV_IN)), full((W_LORA + A_LORA, 2 * d)), full((1, d)), full((1, d)),
            full((G_LORA, d)), full((1, d)), full((1, d)), full((1, d)),
            full((ts, ts)), full((ts, ts)), full((d, LANES)), full((LANES, d)),
        ],
        out_specs=[tok] * 10,
        out_shape=[jax.ShapeDtypeStruct((m, d), BF16)] * 5 + [jax.ShapeDtypeStruct((m, d), F32)] * 5,
        compiler_params=_cparams("parallel"),
        name="rwkv_prep",
    )(p, p, row(mu), lora_w, row(w0), row(a0), g_up, row(k_k), row(k_a), row(r_k), tril, ones, seg, segt)
    return outs


def _stack_heads(x):
    lane_head = lax.broadcasted_iota(jnp.int32, x.shape, 1) // HEAD
    return jnp.concatenate([jnp.where(lane_head == h, x, 0.0) for h in range(HEADS_PER_GROUP)], axis=0)


def _unstack_heads(x):
    out = x[0:CHUNK]
    for h in range(1, HEADS_PER_GROUP):
        out = out + x[h * CHUNK:(h + 1) * CHUNK]
    return out


def _rwkv_intra_kernel(at_ref, bt_ref, kt_ref, rt_ref, v_ref, bh_ref, kh_ref, gc_ref,
                       ra_ref, o2_ref, p_ref, q_ref, *, mm):
    w = GROUP_W
    n_chunks = at_ref.shape[0] // CHUNK
    ri = lax.broadcasted_iota(jnp.int32, (w, w), 0)
    ci = lax.broadcasted_iota(jnp.int32, (w, w), 1)
    strict = (ri % CHUNK) > (ci % CHUNK)
    incl = (ri % CHUNK) >= (ci % CHUNK)
    blk16 = (ri // 16) == (ci // 16)
    eye = (ri == ci).astype(F32)

    chunks = [slice(c * CHUNK, (c + 1) * CHUNK) for c in range(n_chunks)]

    def each(f, *lists):
        return [f(*args) for args in zip(*lists)]

    def stacked(ref):
        return [_stack_heads(ref[rows, :]) for rows in chunks]

    a_s, r_s, b_s, k_s, v_s = stacked(at_ref), stacked(rt_ref), stacked(bt_ref), stacked(kt_ref), stacked(v_ref)
    s = each(lambda a, r, b, k: mm(jnp.concatenate([a, r], axis=0), jnp.concatenate([b, k], axis=0), nt=True),
             a_s, r_s, b_s, k_s)
    l_ab = [jnp.where(strict, x[:w, :w], 0.0) for x in s]
    l_ak = [jnp.where(strict, x[:w, w:], 0.0) for x in s]
    m_rb = [jnp.where(incl, x[w:, :w], 0.0) for x in s]
    m_rk = [jnp.where(incl, x[w:, w:], 0.0) for x in s]

    def mm2(a, b0, b1):
        out = mm(a, jnp.concatenate([b0, b1], axis=1))
        return out[:, :w], out[:, w:]

    dg = [jnp.where(blk16, x, 0.0) for x in l_ab]
    off = each(lambda x, y: x - y, l_ab, dg)
    td = [eye + x for x in dg]
    pw = each(mm, dg, dg)
    for _ in range(2):
        sq_ptd = each(mm2, pw, pw, td)
        td = each(lambda t_, x_: t_ + x_[1], td, sq_ptd)
        pw = [x_[0] for x_ in sq_ptd]
    td = each(lambda t_, p_: t_ + mm(p_, t_), td, pw)
    n1 = each(mm, td, off)
    sq_ntd = each(mm2, n1, n1, td)
    t1 = each(lambda t_, x_: t_ + x_[1], td, sq_ntd)
    t = each(lambda t_, x_: t_ + mm(x_[0], t_), t1, sq_ntd)

    lv = each(mm, l_ak, v_s)
    y = each(lambda tt, x, a: mm(tt, jnp.concatenate([x, a], axis=1)), t, lv, a_s)
    z = each(mm, m_rb, y)
    mv = each(mm, m_rk, v_s)
    z2 = each(lambda ref_rows, yy: mm(_stack_heads(bh_ref[ref_rows, :]).T, yy), chunks, y)
    kv = each(lambda ref_rows, vv: mm(_stack_heads(kh_ref[ref_rows, :]).T, vv), chunks, v_s)
    for c, rows in enumerate(chunks):
        ra_ref[rows, :] = _unstack_heads(r_s[c] + z[c][:, w:]).astype(ra_ref.dtype)
        o2_ref[rows, :] = _unstack_heads(z[c][:, :w] + mv[c])
        p_ref[rows, :] = _unstack_heads(eye * gc_ref[c * CHUNK:c * CHUNK + 1, :]
                                        + z2[c][:, w:]).astype(p_ref.dtype)
        q_ref[rows, :] = _unstack_heads(z2[c][:, :w] + kv[c])


def _mm_bf16(a, b, nt=False):
    a = a.astype(BF16)
    b = b.astype(BF16)
    return _dot_nt(a, b) if nt else _dot(a, b)


def _rwkv_intra_call(at, bt, kt, rt, v, bh, kh, gc, seq):
    m, d = at.shape
    ts = _tile(seq, 512, CHUNK)
    spec = pl.BlockSpec((ts, GROUP_W), lambda i, j: (i, j))
    return pl.pallas_call(
        functools.partial(_rwkv_intra_kernel, mm=_mm_bf16),
        grid=(m // ts, d // GROUP_W),
        in_specs=[spec] * 8,
        out_specs=[spec] * 4,
        out_shape=[jax.ShapeDtypeStruct((m, d), dt) for dt in (BF16, F32, BF16, F32)],
        compiler_params=_cparams("parallel", "parallel"),
        name="rwkv_intra",
    )(at, bt, kt, rt, v, bh, kh, gc)


def _rwkv_seq_kernel(ra_ref, o2_ref, p_ref, q_ref, bonus_ref, g_ref, gnw_ref, gnb_ref, seg_ref, segt_ref,
                     o_ref, h_sc, y_sc, *, mm):
    w = GROUP_W
    n_groups = RWKV_DIM // w
    n_batch = ra_ref.shape[0]
    n_chunks = ra_ref.shape[1] // CHUNK

    @pl.when(pl.program_id(0) == 0)
    def _():
        h_sc[...] = jnp.zeros_like(h_sc)

    def chunk(c, carry):
        r0 = pl.multiple_of(c * CHUNK, CHUNK)
        rows = pl.ds(r0, CHUNK)
        units = [(b, gi, slice(gi * w, (gi + 1) * w)) for b in range(n_batch) for gi in range(n_groups)]
        lhs = [jnp.concatenate([ra_ref[b, rows, cols], _stack_heads(p_ref[b, rows, cols])], axis=0)
               for b, gi, cols in units]
        res = [mm(x, h_sc[b, gi]) for x, (b, gi, cols) in zip(lhs, units)]
        for x, (b, gi, cols) in zip(res, units):
            y_sc[b, rows, cols] = x[:CHUNK] + o2_ref[b, rows, cols]
            h_sc[b, gi] = x[CHUNK:] + _stack_heads(q_ref[b, rows, cols])
        return carry

    lax.fori_loop(0, n_chunks, chunk, 0)

    seg = seg_ref[...]
    segt = segt_ref[...]
    for b in range(n_batch):
        y = y_sc[b]
        mean = _dot_lx(_dot_lx(y, seg), segt) * (1.0 / HEAD)
        yc = y - mean
        var = _dot_lx(_dot_lx(yc * yc, seg), segt) * (1.0 / HEAD)
        yn = yc * lax.rsqrt(var + GN_EPS) * gnw_ref[...] + gnb_ref[...]
        o_ref[b] = ((yn + bonus_ref[b]) * g_ref[b]).astype(o_ref.dtype)


def _rwkv_seq_call(ra, o2, p, q, bonus, g, gn_w, gn_b, batch, seq):
    m, d = ra.shape
    ts = _tile(seq, 256, CHUNK)
    lane = np.arange(d)
    seg_np = (lane[:, None] // HEAD) == np.arange(LANES)[None, :]
    seg = jnp.asarray(seg_np, BF16)
    segt = jnp.asarray(seg_np.T, BF16)
    tok = pl.BlockSpec((batch, ts, d), lambda i: (0, i, 0))
    full = lambda shape: pl.BlockSpec(shape, lambda i: (0, 0))
    seqs = [x.reshape(batch, seq, d) for x in (ra, o2, p, q, bonus, g)]
    out = pl.pallas_call(
        functools.partial(_rwkv_seq_kernel, mm=_mm_bf16),
        grid=(seq // ts,),
        in_specs=[tok] * 6 + [full((1, d)), full((1, d)), full((d, LANES)), full((LANES, d))],
        out_specs=tok,
        out_shape=jax.ShapeDtypeStruct((batch, seq, d), BF16),
        scratch_shapes=[pltpu.VMEM((batch, d // GROUP_W, GROUP_W, GROUP_W), F32), pltpu.VMEM((batch, ts, d), F32)],
        compiler_params=_cparams("arbitrary"),
        name="rwkv_seq",
    )(*seqs, gn_w.reshape(1, d), gn_b.reshape(1, d), seg, segt)
    return out.reshape(m, d)


def _rope_tab(pos_ref, inv_ref):
    ang = pos_ref[...].astype(F32) * inv_ref[...]
    lane = lax.broadcasted_iota(jnp.int32, ang.shape, 1)
    return jnp.where(lane < QK_ROPE, jnp.cos(ang), jnp.sin(ang))


def _mla_proj_kernel(h_ref, pos_ref, inv_ref, wm_ref, qn_ref, wq_ref, kvn_ref, wkv_ref, q_ref, k_ref, v_ref,
                     *, scale):
    p = _dot(h_ref[...], wm_ref[...])
    tab = _rope_tab(pos_ref, inv_ref)

    hn = _rms(p[:, 0:Q_LORA], qn_ref[...]).astype(BF16)
    q = _dot(hn, wq_ref[...]) * scale
    for h in range(MLA_HEADS):
        c0 = h * QK_CAT
        q_ref[0, h, :, 0:QK_NOPE] = q[:, c0:c0 + QK_NOPE].astype(BF16)
        q_ref[0, h, :, QK_NOPE:QK_CAT] = (q[:, c0 + QK_NOPE:c0 + QK_CAT] * tab).astype(BF16)

    hn = _rms(p[:, Q_LORA:Q_LORA + KV_LORA], kvn_ref[...]).astype(BF16)
    kv = _dot(hn, wkv_ref[...])
    t = p[:, Q_LORA + KV_LORA:] * tab
    k_rope = (t + pltpu.roll(t, QK_ROPE, axis=1)).astype(BF16)
    for h in range(MLA_HEADS):
        c0 = h * (QK_NOPE + V_HEAD)
        k_ref[0, h, :, 0:QK_NOPE] = kv[:, c0:c0 + QK_NOPE].astype(BF16)
        k_ref[0, h, :, QK_NOPE:QK_CAT] = k_rope
        v_ref[0, h, :, 0:V_HEAD] = kv[:, c0 + QK_NOPE:c0 + QK_NOPE + V_HEAD].astype(BF16)
        v_ref[0, h, :, V_HEAD:V_EXT] = jnp.ones((kv.shape[0], V_EXT - V_HEAD), BF16)


def _mla_proj_call(hn, pos, inv_tab, w_mla, q_norm, wq, kv_norm, wkv, batch, seq):
    d = hn.shape[1]
    tm = _tile(seq, 512)
    nt = seq // tm
    scale = float((QK_NOPE + QK_ROPE) ** -0.5) * LOG2_E
    tok = lambda w: pl.BlockSpec((tm, w), lambda b, i: (b * nt + i, 0))
    full = lambda shape: pl.BlockSpec(shape, lambda b, i: (0, 0))
    headed = lambda w: pl.BlockSpec((1, MLA_HEADS, tm, w), lambda b, i: (b, 0, i, 0))
    return pl.pallas_call(
        functools.partial(_mla_proj_kernel, scale=scale),
        grid=(batch, nt),
        in_specs=[tok(d), tok(1), full((1, LANES)), full(w_mla.shape), full((1, Q_LORA)), full(wq.shape),
                  full((1, KV_LORA)), full(wkv.shape)],
        out_specs=[headed(QK_CAT), headed(QK_CAT), headed(V_EXT)],
        out_shape=[jax.ShapeDtypeStruct((batch, MLA_HEADS, seq, QK_CAT), BF16),
                   jax.ShapeDtypeStruct((batch, MLA_HEADS, seq, QK_CAT), BF16),
                   jax.ShapeDtypeStruct((batch, MLA_HEADS, seq, V_EXT), BF16)],
        compiler_params=_cparams("parallel", "parallel"),
        name="mla_proj",
    )(hn, pos, inv_tab, w_mla, q_norm.reshape(1, -1), wq, kv_norm.reshape(1, -1), wkv)


def _cast_expert_block(w_ref, wo_ref):
    for f in range(wo_ref.shape[1]):
        wo_ref[0, f] = w_ref[:, f * MOE_TF:(f + 1) * MOE_TF].astype(BF16)


def _expert_cast_specs(w, n_blocks, block_of):
    n_exp, d, ff = w.shape
    nf = ff // MOE_TF
    rows = n_exp * d // n_blocks
    per_exp = d // rows
    in_spec = pl.BlockSpec((rows, ff), lambda *a: (block_of(*a), 0))
    out_spec = pl.BlockSpec((1, nf, rows, MOE_TF),
                            lambda *a: (block_of(*a) // per_exp, 0, block_of(*a) % per_exp, 0))
    out_shape = jax.ShapeDtypeStruct((n_exp, nf, d, MOE_TF), BF16)
    return in_spec, out_spec, out_shape, w.reshape(n_exp * d, ff)


def _expert_cast_call(w):
    in_spec, out_spec, out_shape, arg = _expert_cast_specs(w, EXPERT_CAST_BLOCKS, lambda i: i)
    return pl.pallas_call(
        _cast_expert_block,
        grid=(EXPERT_CAST_BLOCKS,),
        in_specs=[in_spec],
        out_specs=out_spec,
        out_shape=out_shape,
        compiler_params=_cparams("parallel"),
        name="expert_cast",
    )(arg)


def _attn_cast_kernel(qi_ref, kj_ref, q_ref, k_ref, v_ref, w_ref, o_ref, wo_ref, m_sc, acc_sc):
    _cast_expert_block(w_ref, wo_ref)
    _attn_kernel(qi_ref, kj_ref, q_ref, k_ref, v_ref, o_ref, m_sc, acc_sc)


def _attn_kernel(qi_ref, kj_ref, q_ref, k_ref, v_ref, o_ref, m_sc, acc_sc):
    t = pl.program_id(2)
    qi = qi_ref[t]
    kj = kj_ref[t]
    n_heads = q_ref.shape[1]
    tq = q_ref.shape[2]
    tk = k_ref.shape[2]
    chunk_gap = kj * (tk // CHUNK) - qi * (tq // CHUNK)

    @pl.when(kj == 0)
    def _():
        m_sc[...] = jnp.full_like(m_sc, NEG_BIG)
        acc_sc[...] = jnp.zeros_like(acc_sc)

    def update(heads, mask):
        s = [_dot_nt(q_ref[0, h], k_ref[0, h]) for h in heads]
        if mask:
            ri = lax.broadcasted_iota(jnp.int32, s[0].shape, 0) // CHUNK
            ci = lax.broadcasted_iota(jnp.int32, s[0].shape, 1) // CHUNK
            visible = ci - ri <= -chunk_gap
            s = [jnp.where(visible, x, NEG_BIG) for x in s]
        for i, h in enumerate(heads):
            m_prev = m_sc[h]
            m_new = jnp.maximum(m_prev, jnp.max(s[i], axis=-1, keepdims=True))
            alpha = jnp.exp2(m_prev - m_new)
            p = jnp.exp2(s[i] - jnp.concatenate([m_new] * (tk // LANES), axis=1))
            pv = _dot(p.astype(BF16), v_ref[0, h])
            acc_sc[h] = jnp.concatenate([alpha] * (V_EXT // LANES), axis=1) * acc_sc[h] + pv
            m_sc[h] = m_new

    pairs = [tuple(range(h, min(h + 2, n_heads))) for h in range(0, n_heads, 2)]

    fully_visible = chunk_gap <= -(tk // CHUNK - 1)

    @pl.when(fully_visible)
    def _():
        for hp in pairs:
            update(hp, False)

    @pl.when(jnp.logical_not(fully_visible))
    def _():
        for hp in pairs:
            update(hp, True)

    @pl.when((kj + 1) * tk == (qi + 1) * tq)
    def _():
        for h in range(n_heads):
            acc = acc_sc[h]
            o_ref[:, h * V_HEAD:(h + 1) * V_HEAD] = (acc[:, :V_HEAD] / acc[:, V_HEAD:]).astype(o_ref.dtype)


def _attn_call(q, k, v, batch, seq, expert_w):
    tk = _tile(seq, 512, CHUNK)
    tq = _tile(seq, 2 * tk, tk)
    nq = seq // tq
    hg = ATTN_HEADS_PER_STEP
    n_hg = MLA_HEADS // hg
    pairs = [(i, j) for i in range(nq) for j in range((i + 1) * tq // tk)]
    qi = jnp.asarray([p[0] for p in pairs], jnp.int32)
    kj = jnp.asarray([p[1] for p in pairs], jnp.int32)
    in_specs = [
        pl.BlockSpec((1, hg, tq, QK_CAT), lambda b, h, t, qi, kj: (b, h, qi[t], 0)),
        pl.BlockSpec((1, hg, tk, QK_CAT), lambda b, h, t, qi, kj: (b, h, kj[t], 0)),
        pl.BlockSpec((1, hg, tk, V_EXT), lambda b, h, t, qi, kj: (b, h, kj[t], 0)),
    ]
    out_specs = [pl.BlockSpec((tq, hg * V_HEAD), lambda b, h, t, qi, kj: (b * nq + qi[t], h))]
    out_shape = [jax.ShapeDtypeStruct((batch * seq, MLA_HEADS * V_HEAD), BF16)]
    args = (qi, kj, q, k, v)
    steps = batch * n_hg * len(pairs)
    n_blocks = 1 << (steps.bit_length() - 1)
    rows_total = expert_w.shape[0] * expert_w.shape[1]
    fused = n_blocks * EXPERT_CAST_MAX_ROWS >= rows_total
    if fused:
        def block_of(b, h, t, qi, kj):
            return jnp.minimum((b * n_hg + h) * len(pairs) + t, n_blocks - 1)

        c_in, c_out, c_shape, c_arg = _expert_cast_specs(expert_w, n_blocks, block_of)
        in_specs, out_specs, out_shape, args = in_specs + [c_in], out_specs + [c_out], out_shape + [c_shape], args + (c_arg,)
    grid_spec = pltpu.PrefetchScalarGridSpec(
        num_scalar_prefetch=2,
        grid=(batch, n_hg, len(pairs)),
        in_specs=in_specs,
        out_specs=out_specs,
        scratch_shapes=[pltpu.VMEM((hg, tq, LANES), F32), pltpu.VMEM((hg, tq, V_EXT), F32)],
    )
    outs = pl.pallas_call(
        _attn_cast_kernel if fused else _attn_kernel,
        grid_spec=grid_spec,
        out_shape=out_shape,
        compiler_params=_cparams("arbitrary", "arbitrary", "arbitrary"),
        name="mla_attn",
    )(*args)
    if fused:
        return outs[0], outs[1]
    return outs[0], _expert_cast_call(expert_w)


def _mix_out_kernel(ya_ref, yb_ref, w_ref, x_ref, g_ref, xo_ref, hn_ref):
    half = ya_ref.shape[1]
    acc = _dot(ya_ref[...], w_ref[0:half, :]) + _dot(yb_ref[...], w_ref[half:, :])
    x = x_ref[...] + acc
    xo_ref[...] = x
    hn_ref[...] = _rms(x, g_ref[...]).astype(hn_ref.dtype)


def _mix_out_call(ya, yb, w, x, g):
    m, d = x.shape
    tm = _tile(m, 512)
    half = ya.shape[1]
    row = pl.BlockSpec((tm, d), lambda i: (i, 0))
    return pl.pallas_call(
        _mix_out_kernel,
        grid=(m // tm,),
        in_specs=[pl.BlockSpec((tm, half), lambda i: (i, 0)), pl.BlockSpec((tm, half), lambda i: (i, 0)),
                  pl.BlockSpec((2 * half, d), lambda i: (0, 0)), row, pl.BlockSpec((1, d), lambda i: (0, 0))],
        out_specs=[row, row],
        out_shape=[jax.ShapeDtypeStruct((m, d), F32), jax.ShapeDtypeStruct((m, d), BF16)],
        compiler_params=_cparams("parallel"),
        name="mix_out",
    )(ya, yb, w, x, g.reshape(1, d))


def _conv_out_kernel(a_ref, w_ref, b_ref, x_ref, g_ref, r_ref, xo_ref, hn_ref, lg_ref):
    tm = x_ref.shape[0]
    xo_ref[...] = _dot(a_ref[...], w_ref[...])
    half = tm // 2
    for r0 in range(0, tm, half):
        rows = slice(r0, r0 + half)
        x = x_ref[rows, :] + xo_ref[rows, :] + b_ref[...]
        xo_ref[rows, :] = x
        hn = _rms(x, g_ref[...])
        for j in range(SLABS):
            hn_ref[pl.ds(r0 * SLABS + j, half, stride=SLABS), :] = hn[:, j * LANES:(j + 1) * LANES]
        lg_ref[rows, :] = _dot_hp(hn, r_ref[...])


def _conv_out_call(a, w, b, x, g, router_pad):
    m, d = x.shape
    tm = _tile(m, 512)
    row = pl.BlockSpec((tm, d), lambda i: (i, 0))
    full = lambda shape: pl.BlockSpec(shape, lambda i: (0, 0))
    return pl.pallas_call(
        _conv_out_kernel,
        grid=(m // tm,),
        in_specs=[row, full((d, d)), full((1, d)), row, full((1, d)), full((d, LANES))],
        out_specs=[row, pl.BlockSpec((tm * SLABS, LANES), lambda i: (i, 0)),
                   pl.BlockSpec((tm, LANES), lambda i: (i, 0))],
        out_shape=[jax.ShapeDtypeStruct((m, d), F32), jax.ShapeDtypeStruct((m * SLABS, LANES), F32),
                   jax.ShapeDtypeStruct((m, LANES), F32)],
        compiler_params=_cparams("parallel"),
        name="conv_out",
    )(a, w, b.reshape(1, d), x, g.reshape(1, d), router_pad)


def _ffn_kernel(h_ref, w1_ref, w3_ref, w2_ref, x_ref, g_ref, xo_ref, hn_ref):
    f = pl.program_id(1)

    @pl.when(f == 0)
    def _():
        xo_ref[...] = x_ref[...]

    h = h_ref[...]
    gate = (_silu(_dot(h, w1_ref[...])) * _dot(h, w3_ref[...])).astype(BF16)
    xo_ref[...] += _dot(gate, w2_ref[...])

    @pl.when(f == pl.num_programs(1) - 1)
    def _():
        hn_ref[...] = _rms(xo_ref[...], g_ref[...]).astype(hn_ref.dtype)


def _ffn_call(h, w1, w3, w2, x, g):
    m, d = x.shape
    ff = w1.shape[1]
    tm = _tile(m, 512)
    tf = _tile(ff, 512, LANES)
    row = pl.BlockSpec((tm, d), lambda i, f: (i, 0))
    wtile = pl.BlockSpec((d, tf), lambda i, f: (0, f))
    return pl.pallas_call(
        _ffn_kernel,
        grid=(m // tm, ff // tf),
        in_specs=[row, wtile, wtile,
                  pl.BlockSpec((tf, d), lambda i, f: (f, 0)), row, pl.BlockSpec((1, d), lambda i, f: (0, 0))],
        out_specs=[row, row],
        out_shape=[jax.ShapeDtypeStruct((m, d), F32), jax.ShapeDtypeStruct((m, d), BF16)],
        compiler_params=_cparams("parallel", "arbitrary"),
        name="ffn",
    )(h, w1, w3, w2, x, g.reshape(1, d))


def _glu_kernel(h_ref, wa_ref, wb_ref, ba_ref, bb_ref, o_ref):
    h = h_ref[...]
    a = _dot(h, wa_ref[...]) + ba_ref[...]
    b = _dot(h, wb_ref[...]) + bb_ref[...]
    o_ref[...] = a * _sigmoid(b)


def _glu_call(h, w, b):
    m, d = h.shape
    n = w.shape[1] // 2
    tm = _tile(m, 1024)
    tn = _tile(n, 512, LANES)
    nj = n // tn
    b2 = b.reshape(1, 2 * n)
    return pl.pallas_call(
        _glu_kernel,
        grid=(m // tm, nj),
        in_specs=[pl.BlockSpec((tm, d), lambda i, j: (i, 0)),
                  pl.BlockSpec((d, tn), lambda i, j: (0, j)), pl.BlockSpec((d, tn), lambda i, j: (0, j + nj)),
                  pl.BlockSpec((1, tn), lambda i, j: (0, j)), pl.BlockSpec((1, tn), lambda i, j: (0, j + nj))],
        out_specs=pl.BlockSpec((tm, tn), lambda i, j: (i, j)),
        out_shape=jax.ShapeDtypeStruct((m, n), F32),
        compiler_params=_cparams("parallel", "arbitrary"),
        name="conv_glu",
    )(h, w, w, b2, b2)


def _dwconv_kernel(u_ref, halo_ref, dw_ref, dwb_ref, lg_ref, lb_ref, o_ref, ext_sc, acc_sc, *, tiles_per_seq):
    ts, d = u_ref.shape
    first = (pl.program_id(0) % tiles_per_seq) == 0
    ext_sc[0, 0:CONV_HALO, :] = jnp.where(first, 0.0, halo_ref[...])
    ext_sc[0, CONV_HALO:, :] = u_ref[...]
    n_shift = ts + CONV_HALO - SUBLANES
    for b in range(1, SUBLANES):
        ext_sc[b, 0:n_shift, :] = ext_sc[0, b:b + n_shift, :]
    rc, cc = 64, 256
    base = CONV_HALO - (CONV_WIDTH - 1)
    for c0 in range(0, d, cc):
        for r0 in range(0, ts, rc):
            acc = jnp.zeros((rc, cc), F32) + dwb_ref[:, c0:c0 + cc]
            for j in range(CONV_WIDTH):
                a8, b = divmod(base + j, SUBLANES)
                rows = slice(r0 + a8 * SUBLANES, r0 + a8 * SUBLANES + rc)
                acc = acc + dw_ref[j:j + 1, c0:c0 + cc] * ext_sc[b, rows, c0:c0 + cc]
            acc_sc[r0:r0 + rc, c0:c0 + cc] = acc
    y = acc_sc[...]
    mu = jnp.mean(y, axis=-1, keepdims=True)
    yc = y - mu
    var = jnp.mean(yc * yc, axis=-1, keepdims=True)
    yn = yc * lax.rsqrt(var + LN_EPS) * lg_ref[...] + lb_ref[...]
    o_ref[...] = _silu(yn).astype(o_ref.dtype)


def _dwconv_cast_kernel(u_ref, halo_ref, dw_ref, dwb_ref, lg_ref, lb_ref, w_ref, o_ref, wo_ref, ext_sc, acc_sc,
                        *, tiles_per_seq):
    _cast_expert_block(w_ref, wo_ref)
    _dwconv_kernel(u_ref, halo_ref, dw_ref, dwb_ref, lg_ref, lb_ref, o_ref, ext_sc, acc_sc,
                   tiles_per_seq=tiles_per_seq)


def _dwconv_call(u, seq, dw_w, dw_b, ln_g, ln_b, expert_w):
    m, d = u.shape
    ts = _tile(seq, 256, CONV_HALO)
    tiles_per_seq = seq // ts
    steps = m // ts
    dw_pad = jnp.concatenate([dw_w, jnp.zeros((CONV_HALO - CONV_WIDTH, d), F32)], axis=0)
    full = lambda shape: pl.BlockSpec(shape, lambda i: (0, 0))
    in_specs = [pl.BlockSpec((ts, d), lambda i: (i, 0)),
                pl.BlockSpec((CONV_HALO, d), lambda i: (jnp.maximum(i * (ts // CONV_HALO) - 1, 0), 0)),
                full((CONV_HALO, d)), full((1, d)), full((1, d)), full((1, d))]
    out_specs = [pl.BlockSpec((ts, d), lambda i: (i, 0))]
    out_shape = [jax.ShapeDtypeStruct((m, d), BF16)]
    args = (u, u, dw_pad, dw_b.reshape(1, d), ln_g.reshape(1, d), ln_b.reshape(1, d))
    rows_total = expert_w.shape[0] * expert_w.shape[1]
    fused = steps * DWCONV_CAST_MAX_ROWS >= rows_total and rows_total % steps == 0
    if fused:
        c_in, c_out, c_shape, c_arg = _expert_cast_specs(expert_w, steps, lambda i: i)
        in_specs, out_specs, out_shape, args = in_specs + [c_in], out_specs + [c_out], out_shape + [c_shape], args + (c_arg,)
    outs = pl.pallas_call(
        functools.partial(_dwconv_cast_kernel if fused else _dwconv_kernel, tiles_per_seq=tiles_per_seq),
        grid=(steps,),
        in_specs=in_specs,
        out_specs=out_specs,
        out_shape=out_shape,
        scratch_shapes=[pltpu.VMEM((SUBLANES, ts + CONV_HALO, d), F32), pltpu.VMEM((ts, d), F32)],
        compiler_params=_cparams("arbitrary"),
        name="dwconv",
    )(*args)
    if fused:
        return outs[0], outs[1]
    return outs[0], _expert_cast_call(expert_w)


def _route_kernel(lg_ref, tri_ref, o_ref, cnt_ref, carry_sc):
    @pl.when(pl.program_id(0) == 0)
    def _():
        carry_sc[...] = jnp.zeros_like(carry_sc)

    lg = lg_ref[...]
    lane = lax.broadcasted_iota(jnp.int32, lg.shape, 1)
    lg = jnp.where(lane < N_EXPERTS, lg, -jnp.inf)
    m1 = jnp.max(lg, axis=-1, keepdims=True)
    e1 = jnp.min(jnp.where(lg == m1, lane, LANES), axis=-1, keepdims=True)
    lg2 = jnp.where(lane == e1, -jnp.inf, lg)
    m2 = jnp.max(lg2, axis=-1, keepdims=True)
    e2 = jnp.min(jnp.where(lg2 == m2, lane, LANES), axis=-1, keepdims=True)
    ex = jnp.exp(m2 - m1)
    g1 = 1.0 / (1.0 + ex)
    g2 = ex / (1.0 + ex)
    oh1 = (lane == e1).astype(F32)
    oh2 = (lane == e2).astype(F32)
    both = oh1 + oh2
    before = _dot(tri_ref[...], both.astype(BF16)) + carry_sc[...]
    r1 = jnp.sum(before * oh1, axis=-1, keepdims=True)
    r2 = jnp.sum(before * oh2, axis=-1, keepdims=True)
    carry_sc[...] = carry_sc[...] + jnp.sum(both, axis=0, keepdims=True)
    cnt_ref[...] = jnp.broadcast_to(carry_sc[...], cnt_ref.shape)
    out = jnp.where(lane == 0, e1.astype(F32), 0.0)
    out = jnp.where(lane == 1, e2.astype(F32), out)
    out = jnp.where(lane == 2, g1, out)
    out = jnp.where(lane == 3, g2, out)
    out = jnp.where(lane == 4, r1, out)
    out = jnp.where(lane == 5, r2, out)
    o_ref[...] = out


def _route_call(logits):
    m = logits.shape[0]
    tm = _tile(m, 512)
    idx = np.arange(tm)
    tri = jnp.asarray(idx[:, None] > idx[None, :], BF16)
    return pl.pallas_call(
        _route_kernel,
        grid=(m // tm,),
        in_specs=[pl.BlockSpec((tm, LANES), lambda i: (i, 0)), pl.BlockSpec((tm, tm), lambda i: (0, 0))],
        out_specs=[pl.BlockSpec((tm, LANES), lambda i: (i, 0)), pl.BlockSpec((8, LANES), lambda i: (0, 0))],
        out_shape=[jax.ShapeDtypeStruct((m, LANES), F32), jax.ShapeDtypeStruct((8, LANES), F32)],
        scratch_shapes=[pltpu.VMEM((1, LANES), F32)],
        compiler_params=_cparams("arbitrary"),
        name="moe_route",
    )(logits, tri)


def _slab_copy(src_ref, dst_ref, sem, src_slab, dst_row):
    src = src_ref.at[pl.ds(pl.multiple_of(src_slab, SLABS), SLABS)]
    dst = dst_ref.at[pl.ds(pl.multiple_of(dst_row * SLABS, SLABS), SLABS)]
    return pltpu.make_async_copy(src, dst, sem)


def _slab_cols(ref, j, rows):
    return ref[pl.ds(j, rows, stride=SLABS), :]


def _gather_kernel(used_ref, tok_ref, nxt_ref, src_ref, o_ref, buf_sc, sem):
    rows = o_ref.shape[0]
    i = pl.program_id(0)
    last = pl.num_programs(0) - 1

    def start_block(idx_ref, slot):
        def body(r, c):
            _slab_copy(src_ref, buf_sc.at[slot], sem.at[slot], idx_ref[0, 0, r], r).start()
            return c
        lax.fori_loop(0, rows, body, 0, unroll=8)

    def wait_block(slot):
        def body(r, c):
            _slab_copy(src_ref, buf_sc.at[slot], sem.at[slot], 0, r).wait()
            return c
        lax.fori_loop(0, rows, body, 0, unroll=8)

    @pl.when(jnp.logical_and(i == 0, used_ref[0] > 0))
    def _():
        start_block(tok_ref, 0)

    for slot in range(2):
        @pl.when(jnp.logical_and(i % 2 == slot, jnp.logical_and(i < last, used_ref[jnp.minimum(i + 1, last)] > 0)))
        def _():
            start_block(nxt_ref, 1 - slot)

        @pl.when(jnp.logical_and(i % 2 == slot, used_ref[i] > 0))
        def _():
            wait_block(slot)
            for j in range(SLABS):
                o_ref[:, j * LANES:(j + 1) * LANES] = _slab_cols(buf_sc.at[slot], j, rows).astype(o_ref.dtype)

    @pl.when(used_ref[i] == 0)
    def _():
        o_ref[...] = jnp.zeros_like(o_ref)


def _gather_call(src_slabs, slot_slab, block_used):
    cap = slot_slab.shape[0]
    d = SLABS * LANES
    rows = GATHER_ROWS
    nb = cap // rows
    grid_spec = pltpu.PrefetchScalarGridSpec(
        num_scalar_prefetch=1,
        grid=(nb,),
        in_specs=[pl.BlockSpec((1, 1, rows), lambda i, u: (i, 0, 0), memory_space=pltpu.SMEM),
                  pl.BlockSpec((1, 1, rows), lambda i, u: (jnp.minimum(i + 1, nb - 1), 0, 0),
                               memory_space=pltpu.SMEM),
                  pl.BlockSpec(memory_space=pl.ANY)],
        out_specs=pl.BlockSpec((rows, d), lambda i, u: (i, 0)),
        scratch_shapes=[pltpu.VMEM((2, rows * SLABS, LANES), F32), pltpu.SemaphoreType.DMA((2,))],
    )
    slots = slot_slab.reshape(nb, 1, rows)
    return pl.pallas_call(
        _gather_kernel,
        grid_spec=grid_spec,
        out_shape=jax.ShapeDtypeStruct((cap, d), BF16),
        compiler_params=_cparams("arbitrary"),
        name="moe_gather",
    )(block_used, slots, slots, src_slabs)


def _moe_kernel(te_ref, tr_ref, x_ref, w1_ref, w3_ref, w2_ref, o_ref, w2_sc):
    i = pl.program_id(0)
    f = pl.program_id(1)
    nrows = tr_ref[i]
    tm = x_ref.shape[0]

    @pl.when(f == 0)
    def _():
        o_ref[...] = jnp.zeros_like(o_ref)

    def swiglu(rows):
        x = x_ref[rows, :]
        gate = (_silu(_dot(x, w1_ref[0, 0])) * _dot(x, w3_ref[0, 0])).astype(BF16)
        o_ref[rows, :] += _dot(gate, w2_sc[...])

    @pl.when(nrows == tm)
    def _():
        w2_sc[...] = w2_ref[0].astype(BF16)
        swiglu(slice(0, tm))

    @pl.when(jnp.logical_and(nrows > 0, nrows < tm))
    def _():
        w2_sc[...] = w2_ref[0].astype(BF16)
        for s in range(tm // MOE_SUB):
            @pl.when(s * MOE_SUB < nrows)
            def _():
                swiglu(slice(s * MOE_SUB, (s + 1) * MOE_SUB))


def _moe_call(xs, tile_expert, tile_rows, w1t, w3t, w2):
    cap, d = xs.shape
    n_exp, nf, _, tf = w1t.shape
    tm = MOE_TM

    def fidx(i, f, te, tr):
        return jnp.where(tr[i] > 0, f, nf - 1)

    grid_spec = pltpu.PrefetchScalarGridSpec(
        num_scalar_prefetch=2,
        grid=(cap // tm, nf),
        in_specs=[
            pl.BlockSpec((tm, d), lambda i, f, te, tr: (i, 0)),
            pl.BlockSpec((1, 1, d, tf), lambda i, f, te, tr: (te[i], fidx(i, f, te, tr), 0, 0)),
            pl.BlockSpec((1, 1, d, tf), lambda i, f, te, tr: (te[i], fidx(i, f, te, tr), 0, 0)),
            pl.BlockSpec((1, tf, d), lambda i, f, te, tr: (te[i], fidx(i, f, te, tr), 0)),
        ],
        out_specs=pl.BlockSpec((tm, d), lambda i, f, te, tr: (i, 0)),
        scratch_shapes=[pltpu.VMEM((tf, d), BF16)],
    )
    return pl.pallas_call(
        _moe_kernel,
        grid_spec=grid_spec,
        out_shape=jax.ShapeDtypeStruct((cap, d), F32),
        compiler_params=_cparams("parallel", "arbitrary"),
        name="moe_experts",
    )(tile_expert, tile_rows, xs, w1t, w3t, w2)


def _row_copy(src_ref, dst_ref, sem, src_row, dst_row):
    return pltpu.make_async_copy(src_ref.at[pl.ds(src_row, 1)], dst_ref.at[pl.ds(dst_row, 1)], sem)


def _combine_kernel(pos_ref, nxt_ref, y_ref, x_ref, rt_ref, g_ref, o_ref, buf_sc, sem):
    rows = o_ref.shape[0]
    i = pl.program_id(0)
    last = pl.num_programs(0) - 1

    def start_block(idx_ref, slot):
        def body(r, c):
            for k in range(2):
                _row_copy(y_ref, buf_sc.at[slot, k], sem.at[slot], idx_ref[0, k, r], r).start()
            return c
        lax.fori_loop(0, rows, body, 0, unroll=8)

    def wait_block(slot):
        def body(r, c):
            for k in range(2):
                _row_copy(y_ref, buf_sc.at[slot, k], sem.at[slot], 0, r).wait()
            return c
        lax.fori_loop(0, rows, body, 0, unroll=8)

    @pl.when(i == 0)
    def _():
        start_block(pos_ref, 0)

    for slot in range(2):
        @pl.when(jnp.logical_and(i % 2 == slot, i < last))
        def _():
            start_block(nxt_ref, 1 - slot)

        @pl.when(i % 2 == slot)
        def _():
            wait_block(slot)
            rt = rt_ref[...]
            x = x_ref[...] + rt[:, 2:3] * buf_sc[slot, 0] + rt[:, 3:4] * buf_sc[slot, 1]
            o_ref[...] = _rms(x, g_ref[...])


def _combine_call(pos, yb, x, route, g):
    m, d = x.shape
    rows = GATHER_ROWS if m % GATHER_ROWS == 0 else m
    nb = m // rows
    slots = pos.reshape(nb, rows, 2).transpose(0, 2, 1)
    return pl.pallas_call(
        _combine_kernel,
        grid=(nb,),
        in_specs=[pl.BlockSpec((1, 2, rows), lambda i: (i, 0, 0), memory_space=pltpu.SMEM),
                  pl.BlockSpec((1, 2, rows), lambda i: (jnp.minimum(i + 1, nb - 1), 0, 0), memory_space=pltpu.SMEM),
                  pl.BlockSpec(memory_space=pl.ANY),
                  pl.BlockSpec((rows, d), lambda i: (i, 0)),
                  pl.BlockSpec((rows, LANES), lambda i: (i, 0)),
                  pl.BlockSpec((1, d), lambda i: (0, 0))],
        out_specs=pl.BlockSpec((rows, d), lambda i: (i, 0)),
        out_shape=jax.ShapeDtypeStruct((m, d), F32),
        scratch_shapes=[pltpu.VMEM((2, 2, rows, d), F32), pltpu.SemaphoreType.DMA((2,))],
        compiler_params=_cparams("arbitrary"),
        name="moe_combine",
    )(slots, slots, yb, x, route, g.reshape(1, d))


def _rot_cols(w):
    half = QK_ROPE // 2
    return jnp.concatenate([-w[..., half:], w[..., :half]], axis=-1)


def _mixer_weights(w_in, w_up, a_up, w_q_up):
    d = RWKV_DIM
    w_rwkv = w_in[:, :RWKV_IN]
    w_lat = w_in[:, RWKV_IN:RWKV_IN + Q_LORA + KV_LORA]
    w_kr = w_in[:, RWKV_IN + Q_LORA + KV_LORA:]
    w_mla = jnp.concatenate([w_lat, w_kr, _rot_cols(w_kr)], axis=1)
    lora = jnp.zeros((W_LORA + A_LORA, 2 * d), F32)
    lora = lora.at[:W_LORA, :d].set(w_up).at[W_LORA:, d:].set(a_up)
    wq = w_q_up.reshape(Q_LORA, MLA_HEADS, QK_NOPE + QK_ROPE)
    wq_rope = wq[..., QK_NOPE:]
    wq = jnp.concatenate([wq, _rot_cols(wq_rope)], axis=-1).reshape(Q_LORA, MLA_HEADS * QK_CAT)
    return w_rwkv.astype(BF16), w_mla.astype(BF16), lora, wq.astype(BF16)


def kernel(x, positions, l0_mix_norm, l0_w_in, l0_shift_mu, l0_w0, l0_w_up, l0_a0, l0_a_up, l0_g_up, l0_k_k, l0_k_a, l0_r_k, l0_gn_w, l0_gn_b, l0_q_norm, l0_w_q_up, l0_kv_norm, l0_w_kv_up, l0_w_out, l0_ffn_norm, l0_ffn_w1, l0_ffn_w3, l0_ffn_w2, l1_mix_norm, l1_pw1_w, l1_pw1_b, l1_dw_w, l1_dw_b, l1_ln_g, l1_ln_b, l1_pw2_w, l1_pw2_b, l1_ffn_norm, l1_router, l1_exp_w1, l1_exp_w3, l1_exp_w2, final_norm):
    batch, seq, d = x.shape
    m = batch * seq
    x0 = x.reshape(m, d)
    pos = positions.reshape(m, 1)
    inv = ROPE_THETA ** (-jnp.arange(0, QK_ROPE, 2, dtype=F32) / QK_ROPE)
    inv_tab = jnp.tile(inv, LANES // inv.shape[0]).reshape(1, LANES)

    w_rwkv, w_mla, lora_w, wq = _mixer_weights(l0_w_in, l0_w_up, l0_a_up, l0_w_q_up)

    hn0 = _rmsnorm_call(x0, l0_mix_norm)
    p_rwkv = _matmul_call(hn0, w_rwkv, name="in_proj_rwkv")

    prep = _rwkv_prep_call(p_rwkv, seq, l0_shift_mu, lora_w, l0_w0, l0_a0, l0_g_up, l0_k_k, l0_k_a,
                           l0_r_k.reshape(-1))
    at, bt, kt, rt, v, bh, kh, gc, bonus, gate = prep
    ra, o2, pm, qm = _rwkv_intra_call(at, bt, kt, rt, v, bh, kh, gc, seq)
    y_rwkv = _rwkv_seq_call(ra, o2, pm, qm, bonus, gate, l0_gn_w, l0_gn_b, batch, seq)

    q, k, vv = _mla_proj_call(hn0, pos, inv_tab, w_mla, l0_q_norm, wq, l0_kv_norm, l0_w_kv_up.astype(BF16),
                              batch, seq)
    y_mla, w1t = _attn_call(q, k, vv, batch, seq, l1_exp_w1)

    x1, hn1 = _mix_out_call(y_rwkv, y_mla, l0_w_out.astype(BF16), x0, l0_ffn_norm)

    x2, hn2 = _ffn_call(hn1, l0_ffn_w1.astype(BF16), l0_ffn_w3.astype(BF16), l0_ffn_w2.astype(BF16),
                        x1, l1_mix_norm)

    u = _glu_call(hn2, l1_pw1_w.astype(BF16), l1_pw1_b)
    sc, w3t = _dwconv_call(u, seq, l1_dw_w, l1_dw_b, l1_ln_g, l1_ln_b, l1_exp_w3)
    router_pad = jnp.zeros((d, LANES), F32).at[:, :N_EXPERTS].set(l1_router)
    x3, hn3, logits = _conv_out_call(sc, l1_pw2_w.astype(BF16), l1_pw2_b, x2, l1_ffn_norm, router_pad)

    route, counts = _route_call(logits)
    e = route[:, 0:2].astype(jnp.int32)
    rank = route[:, 4:6].astype(jnp.int32)
    cnt = counts[0, :N_EXPERTS].astype(jnp.int32)
    padded = (cnt + MOE_TM - 1) // MOE_TM * MOE_TM
    pad_end = jnp.cumsum(padded)
    pad_start = pad_end - padded
    pos_slot = pad_start[e] + rank
    cap = (2 * m + N_EXPERTS * MOE_TM + MOE_TM - 1) // MOE_TM * MOE_TM
    tok = jnp.broadcast_to(jnp.arange(m, dtype=jnp.int32)[:, None], (m, 2))
    slot_slab = jnp.zeros((cap,), jnp.int32).at[pos_slot.reshape(-1)].set(tok.reshape(-1) * SLABS)
    tile_start = jnp.arange(cap // MOE_TM, dtype=jnp.int32) * MOE_TM
    tile_expert = jnp.minimum(jnp.sum(tile_start[:, None] >= pad_end[None, :], axis=1), N_EXPERTS - 1)
    used = tile_start < pad_end[-1]
    last_used = jnp.max(jnp.where(used, tile_expert, 0))
    tile_expert = jnp.where(used, tile_expert, last_used).astype(jnp.int32)
    tile_rows = jnp.clip(pad_start[tile_expert] + cnt[tile_expert] - tile_start, 0, MOE_TM)
    tile_rows = jnp.where(used, tile_rows, 0).astype(jnp.int32)

    per_tile = MOE_TM // GATHER_ROWS
    block_off = jnp.tile(jnp.arange(per_tile, dtype=jnp.int32) * GATHER_ROWS, cap // MOE_TM)
    block_used = (jnp.repeat(tile_rows, per_tile) > block_off).astype(jnp.int32)
    xs = _gather_call(hn3, slot_slab, block_used)
    yb = _moe_call(xs, tile_expert, tile_rows, w1t, w3t, l1_exp_w2)
    out = _combine_call(pos_slot, yb, x3, route, final_norm)
    return out.reshape(batch, seq, d)
```
